```python
import math
import jax
import jax.numpy as jnp
from jax import lax
import numpy as np

D_MODEL = 1024
BATCH = 4
SEQ = 8192
DEPTH = 2

N_MIXERS = 2
N_CONV_LAYERS = (DEPTH + 1) // 2
N_ATTN_LAYERS = DEPTH // 2
CONV_WIDTH = 3
N_HEADS = 16
HEAD_DIM = D_MODEL // N_HEADS
MOBA_BLOCK = 256
MOBA_TOPK = 3
Q_CHUNK = 64
ROPE_THETA = 10000.0
PEER_HEADS = 8
PEER_NKEYS = 128
PEER_EXPERTS = PEER_NKEYS * PEER_NKEYS
PEER_QDIM = 128
PEER_HALF = PEER_QDIM // 2
PEER_TOPK = 16
PEER_CHUNK = 128
RMS_EPS = 1e-6

kernel_name = "hybrid_shortconv_moba_peer"


def rms_norm(x, g):
    xf = x.astype(jnp.float32)
    y = xf * lax.rsqrt(jnp.mean(xf * xf, axis=-1, keepdims=True) + RMS_EPS)
    return (y * g.astype(jnp.float32)).astype(x.dtype)


def short_conv_mixer(h, w_in, conv_w, w_out):
    bcz = h @ w_in
    b_gate, c_gate, z = jnp.split(bcz, 3, axis=-1)
    u = c_gate * z
    u_conv = lax.conv_general_dilated(
        u, conv_w[:, None, :].astype(u.dtype), window_strides=(1,),
        padding=[(CONV_WIDTH - 1, 0)], dimension_numbers=('NWC', 'WIO', 'NWC'),
        feature_group_count=D_MODEL)
    return (b_gate * u_conv) @ w_out


def rope(x, positions):
    half = HEAD_DIM // 2
    inv = ROPE_THETA ** (-jnp.arange(half, dtype=jnp.float32) / half)
    ang = positions.astype(jnp.float32)[:, None] * inv[None, :]
    cos, sin = jnp.cos(ang), jnp.sin(ang)
    xf = x.astype(jnp.float32)
    x1, x2 = xf[..., :half], xf[..., half:]
    return jnp.concatenate([x1 * cos - x2 * sin, x2 * cos + x1 * sin], axis=-1).astype(x.dtype)


def moba_mixer(h, w_qkv, w_o):
    B, S, _ = h.shape
    qkv = (h @ w_qkv).reshape(B, S, 3, N_HEADS, HEAD_DIM)
    q = jnp.transpose(qkv[:, :, 0], (0, 2, 1, 3))
    k = jnp.transpose(qkv[:, :, 1], (0, 2, 1, 3))
    v = jnp.transpose(qkv[:, :, 2], (0, 2, 1, 3))
    pos = jnp.arange(S)
    q = rope(q, pos)
    k = rope(k, pos)
    nb = -(-S // MOBA_BLOCK)
    pad = nb * MOBA_BLOCK - S
    k_blk = jnp.pad(k, ((0, 0), (0, 0), (0, pad), (0, 0))).reshape(B, N_HEADS, nb, MOBA_BLOCK, HEAD_DIM)
    v_blk = jnp.pad(v, ((0, 0), (0, 0), (0, pad), (0, 0))).reshape(B, N_HEADS, nb, MOBA_BLOCK, HEAD_DIM)
    k_mean = jnp.mean(k_blk.astype(jnp.float32), axis=3)
    n_sel = min(MOBA_TOPK, nb)
    nc = S // Q_CHUNK
    q_c = q.reshape(B, N_HEADS, nc, Q_CHUNK, HEAD_DIM)
    scale = HEAD_DIM ** -0.5
    head_ix = jnp.arange(N_HEADS)[:, None, None]
    blk_ids = jnp.arange(nb)

    def chunk(idx):
        b = idx // nc
        c = idx % nc
        qb = q_c[b, :, c]
        kb = k_blk[b]
        vb = v_blk[b]
        j = (c * Q_CHUNK) // MOBA_BLOCK
        q_pos = c * Q_CHUNK + jnp.arange(Q_CHUNK)
        gate = jnp.einsum('hqd,hnd->hqn', qb.astype(jnp.float32), k_mean[b])
        gate = jnp.where((blk_ids < j)[None, None, :], gate, -jnp.inf)
        _, sel = lax.top_k(gate, n_sel)
        sel_valid = sel < j
        k_sel = kb[head_ix, sel]
        v_sel = vb[head_ix, sel]
        s_sel = jnp.einsum('hqd,hqntd->hqnt', qb, k_sel, preferred_element_type=jnp.float32) * scale
        s_sel = jnp.where(sel_valid[..., None], s_sel, -jnp.inf).reshape(N_HEADS, Q_CHUNK, n_sel * MOBA_BLOCK)
        k_own = lax.dynamic_index_in_dim(kb, j, axis=1, keepdims=False)
        v_own = lax.dynamic_index_in_dim(vb, j, axis=1, keepdims=False)
        s_own = jnp.einsum('hqd,htd->hqt', qb, k_own, preferred_element_type=jnp.float32) * scale
        k_pos = j * MOBA_BLOCK + jnp.arange(MOBA_BLOCK)
        s_own = jnp.where((k_pos[None, :] <= q_pos[:, None])[None], s_own, -jnp.inf)
        p = jax.nn.softmax(jnp.concatenate([s_sel, s_own], axis=-1), axis=-1).astype(vb.dtype)
        p_sel = p[..., :n_sel * MOBA_BLOCK].reshape(N_HEADS, Q_CHUNK, n_sel, MOBA_BLOCK)
        p_own = p[..., n_sel * MOBA_BLOCK:]
        return (jnp.einsum('hqnt,hqntd->hqd', p_sel, v_sel)
                + jnp.einsum('hqt,htd->hqd', p_own, v_own))

    out = lax.map(chunk, jnp.arange(B * nc))
    out = out.reshape(B, nc, N_HEADS, Q_CHUNK, HEAD_DIM).transpose(0, 1, 3, 2, 4)
    out = out.reshape(B, S, N_HEADS * HEAD_DIM)
    return out @ w_o


def peer_ffn(h, w_q, sub_k1, sub_k2, u_emb, v_emb):
    B, S, D = h.shape
    T = B * S
    xs = h.reshape(T // PEER_CHUNK, PEER_CHUNK, D)

    def chunk(xc):
        q = (xc @ w_q).reshape(PEER_CHUNK, PEER_HEADS, 2, PEER_HALF).astype(jnp.float32)
        s1 = jnp.einsum('thd,nd->thn', q[:, :, 0], sub_k1.astype(jnp.float32))
        s2 = jnp.einsum('thd,nd->thn', q[:, :, 1], sub_k2.astype(jnp.float32))
        v1, i1 = lax.top_k(s1, PEER_TOPK)
        v2, i2 = lax.top_k(s2, PEER_TOPK)
        cand = (v1[..., :, None] + v2[..., None, :]).reshape(PEER_CHUNK, PEER_HEADS, PEER_TOPK * PEER_TOPK)
        cidx = (i1[..., :, None] * PEER_NKEYS + i2[..., None, :]).reshape(PEER_CHUNK, PEER_HEADS, PEER_TOPK * PEER_TOPK)
        top_s, top_pos = lax.top_k(cand, PEER_TOPK)
        e_idx = jnp.take_along_axis(cidx, top_pos, axis=-1)
        g = jax.nn.softmax(top_s, axis=-1)
        u = u_emb[e_idx]
        a = jax.nn.gelu(jnp.einsum('td,thkd->thk', xc, u, preferred_element_type=jnp.float32), approximate=False)
        w = (g * a).astype(xc.dtype)
        return jnp.einsum('thk,thkd->td', w, v_emb[e_idx])

    return lax.map(chunk, xs).reshape(B, S, D)


def setup_inputs(seed: int = 0) -> dict:
    key = jax.random.key(seed)
    ks = jax.random.split(key, 16)
    D = D_MODEL
    f32 = jnp.float32
    x = jax.random.normal(ks[0], (BATCH, SEQ, D), f32)
    norm_mix = 1.0 + 0.01 * jax.random.normal(ks[1], (DEPTH, D), f32)
    norm_ffn = 1.0 + 0.01 * jax.random.normal(ks[2], (DEPTH, D), f32)
    conv_w_in = jax.random.normal(ks[3], (N_CONV_LAYERS, D, 3 * D), f32) * D ** -0.5
    conv_w = jax.random.normal(ks[4], (N_CONV_LAYERS, CONV_WIDTH, D), f32) * 0.5
    conv_w_out = jax.random.normal(ks[5], (N_CONV_LAYERS, D, D), f32) * D ** -0.5
    attn_w_qkv = jax.random.normal(ks[6], (N_ATTN_LAYERS, D, 3 * N_HEADS * HEAD_DIM), f32) * D ** -0.5
    attn_w_o = jax.random.normal(ks[7], (N_ATTN_LAYERS, N_HEADS * HEAD_DIM, D), f32) * (N_HEADS * HEAD_DIM) ** -0.5
    peer_w_q = jax.random.normal(ks[8], (DEPTH, D, PEER_HEADS * PEER_QDIM), f32) * D ** -0.5
    peer_k1 = jax.random.normal(ks[9], (DEPTH, PEER_NKEYS, PEER_HALF), f32) * PEER_HALF ** -0.5
    peer_k2 = jax.random.normal(ks[10], (DEPTH, PEER_NKEYS, PEER_HALF), f32) * PEER_HALF ** -0.5
    peer_u = jax.random.normal(ks[11], (DEPTH, PEER_EXPERTS, D), f32) * D ** -0.5
    peer_v = jax.random.normal(ks[12], (DEPTH, PEER_EXPERTS, D), f32) * D ** -0.5
    norm_final = 1.0 + 0.01 * jax.random.normal(ks[13], (D,), f32)
    return {"x": x, "norm_mix": norm_mix, "norm_ffn": norm_ffn,
            "conv_w_in": conv_w_in, "conv_w": conv_w, "conv_w_out": conv_w_out,
            "attn_w_qkv": attn_w_qkv, "attn_w_o": attn_w_o,
            "peer_w_q": peer_w_q, "peer_k1": peer_k1, "peer_k2": peer_k2,
            "peer_u": peer_u, "peer_v": peer_v, "norm_final": norm_final}


def reference(x, norm_mix, norm_ffn, conv_w_in, conv_w, conv_w_out, attn_w_qkv, attn_w_o,
              peer_w_q, peer_k1, peer_k2, peer_u, peer_v, norm_final):
    for i in range(DEPTH):
        hn = rms_norm(x, norm_mix[i])
        j = i // N_MIXERS
        if i % N_MIXERS == 0:
            x = x + short_conv_mixer(hn, conv_w_in[j], conv_w[j], conv_w_out[j])
        else:
            x = x + moba_mixer(hn, attn_w_qkv[j], attn_w_o[j])
        x = x + peer_ffn(rms_norm(x, norm_ffn[i]), peer_w_q[i], peer_k1[i], peer_k2[i], peer_u[i], peer_v[i])
    return rms_norm(x, norm_final)
```

```python
import functools

import jax
import jax.numpy as jnp
from jax import lax
from jax.experimental import pallas as pl
from jax.experimental.pallas import tpu as pltpu

F32 = jnp.float32
BF16 = jnp.bfloat16
I32 = jnp.int32

RMS_EPS = 1e-6
N_HEADS = 16
HEAD_DIM = 64
MOBA_BLOCK = 256
MOBA_TOPK = 3
ROPE_THETA = 10000.0
PEER_HEADS = 8
PEER_NKEYS = 128
PEER_HALF = 64
PEER_TOPK = 16

LANES = 128
SUBLANES = 8
VMEM_LIMIT = 56 * 1024 * 1024
MASK_NEG = -1e9

TOKEN_TILE = 512
PEER_TILE = 128
SLOT_GROUP = 16


def _cparams(sem):
    return pltpu.CompilerParams(dimension_semantics=sem, vmem_limit_bytes=VMEM_LIMIT)


def _rms(x, g):
    ms = jnp.mean(x * x, axis=-1, keepdims=True)
    return x * lax.rsqrt(ms + RMS_EPS) * g


def _split(a):
    hi = a.astype(BF16)
    lo = (a - hi.astype(F32)).astype(BF16)
    return hi, lo


def _dot(a, b):
    return lax.dot_general(a, b, (((1,), (0,)), ((), ())), preferred_element_type=F32)


def _dot_t(a, b):
    return lax.dot_general(a, b, (((1,), (1,)), ((), ())), preferred_element_type=F32)


def _dot3(a_hi, a_lo, b_hi, b_lo):
    return _dot(a_hi, b_hi) + _dot(a_lo, b_hi) + _dot(a_hi, b_lo)


def _dot3_t(a_hi, a_lo, b_hi, b_lo):
    return _dot_t(a_hi, b_hi) + _dot_t(a_lo, b_hi) + _dot_t(a_hi, b_lo)


def _conv_mixer_kernel(x_ref, g_ref, win_ref, cw_ref, wout_ref, o_ref, ubuf_ref):
    tm, d = x_ref.shape[1], x_ref.shape[2]

    @pl.when(pl.program_id(1) == 0)
    def _():
        ubuf_ref[0:SUBLANES, :] = jnp.zeros((SUBLANES, d), F32)

    x = x_ref[0]
    hn = _rms(x, g_ref[...]).astype(BF16)
    bcz = _dot(hn, win_ref[...])
    b_gate, c_gate, z = bcz[:, :d], bcz[:, d:2 * d], bcz[:, 2 * d:]
    u = c_gate * z
    ubuf_ref[SUBLANES:SUBLANES + tm, :] = u
    u1 = ubuf_ref[SUBLANES - 1:SUBLANES - 1 + tm, :]
    u2 = ubuf_ref[SUBLANES - 2:SUBLANES - 2 + tm, :]
    cw = cw_ref[...]
    u_conv = cw[0:1, :] * u2 + cw[1:2, :] * u1 + cw[2:3, :] * u
    ubuf_ref[0:SUBLANES, :] = u[tm - SUBLANES:tm, :]
    y = (b_gate * u_conv).astype(BF16)
    o_ref[0] = x + _dot(y, wout_ref[...])


def _conv_mixer(x, g, w_in, conv_w, w_out, tm):
    b, s, d = x.shape
    return pl.pallas_call(
        _conv_mixer_kernel,
        grid=(b, s // tm),
        in_specs=[
            pl.BlockSpec((1, tm, d), lambda i, j: (i, j, 0)),
            pl.BlockSpec((1, d), lambda i, j: (0, 0)),
            pl.BlockSpec((d, 3 * d), lambda i, j: (0, 0)),
            pl.BlockSpec((3, d), lambda i, j: (0, 0)),
            pl.BlockSpec((d, d), lambda i, j: (0, 0)),
        ],
        out_specs=pl.BlockSpec((1, tm, d), lambda i, j: (i, j, 0)),
        out_shape=jax.ShapeDtypeStruct((b, s, d), F32),
        scratch_shapes=[pltpu.VMEM((tm + SUBLANES, d), F32)],
        compiler_params=_cparams(("arbitrary", "arbitrary")),
        name="conv_mixer",
    )(x, g.reshape(1, d), w_in.astype(BF16), conv_w, w_out.astype(BF16))


def _topk_rows(s, k, payload=None):
    n = s.shape[0]
    row = lax.broadcasted_iota(I32, s.shape, 0)
    vals, outs = [], []
    for _ in range(k):
        m = jnp.max(s, axis=0, keepdims=True)
        i = jnp.min(jnp.where(s == m, row, n), axis=0, keepdims=True)
        pick = row == i
        vals.append(m)
        if payload is None:
            outs.append(i)
        else:
            outs.append(jnp.max(jnp.where(pick, payload, -1), axis=0, keepdims=True))
        s = jnp.where(pick, -jnp.inf, s)
    return jnp.concatenate(vals, axis=0), jnp.concatenate(outs, axis=0)


def _router_kernel(x_ref, g_ref, wq_hi_ref, wq_lo_ref, k1_hi_ref, k1_lo_ref, k2_hi_ref, k2_lo_ref,
                   e_ref, gate_ref):
    t = x_ref.shape[0]
    xn = _rms(x_ref[...], g_ref[...])
    xh, xl = _split(xn)
    q = _dot3(xh, xl, wq_hi_ref[...], wq_lo_ref[...])
    nk = k1_hi_ref.shape[0]
    for h in range(PEER_HEADS):
        qh, ql = _split(q[:, h * LANES:(h + 1) * LANES])
        s1 = _dot3_t(k1_hi_ref[...], k1_lo_ref[...], qh, ql)
        s2 = _dot3_t(k2_hi_ref[...], k2_lo_ref[...], qh, ql)
        v1, i1 = _topk_rows(s1, PEER_TOPK)
        v2, i2 = _topk_rows(s2, PEER_TOPK)
        kk = PEER_TOPK
        cand = (v1[:, None, :] + v2[None, :, :]).reshape(kk * kk, t)
        cidx = (i1[:, None, :] * nk + i2[None, :, :]).reshape(kk * kk, t)
        top_s, e_idx = _topk_rows(cand, PEER_TOPK, payload=cidx)
        p = jnp.exp(top_s - top_s[0:1, :])
        gate = p / jnp.sum(p, axis=0, keepdims=True)
        e_ref[0, h * kk:(h + 1) * kk, :] = e_idx
        gate_ref[0, h * kk:(h + 1) * kk, :] = gate


def _pad_keys(k, lo):
    nk, half = k.shape
    out = jnp.zeros((nk, LANES), F32)
    return out.at[:, lo:lo + half].set(k)


def _peer_router(x2d, g, w_q, k1, k2):
    t, d = x2d.shape
    nt = t // PEER_TILE
    slots = PEER_HEADS * PEER_TOPK
    wq_hi, wq_lo = _split(w_q)
    k1_hi, k1_lo = _split(_pad_keys(k1, 0))
    k2_hi, k2_lo = _split(_pad_keys(k2, PEER_HALF))
    const = lambda i: (0, 0)
    kspec = pl.BlockSpec(k1_hi.shape, const)
    return pl.pallas_call(
        _router_kernel,
        grid=(nt,),
        in_specs=[
            pl.BlockSpec((PEER_TILE, d), lambda i: (i, 0)),
            pl.BlockSpec((1, d), const),
            pl.BlockSpec(wq_hi.shape, const),
            pl.BlockSpec(wq_lo.shape, const),
            kspec, kspec, kspec, kspec,
        ],
        out_specs=[
            pl.BlockSpec((1, slots, PEER_TILE), lambda i: (i, 0, 0)),
            pl.BlockSpec((1, slots, PEER_TILE), lambda i: (i, 0, 0)),
        ],
        out_shape=[
            jax.ShapeDtypeStruct((nt, slots, PEER_TILE), I32),
            jax.ShapeDtypeStruct((nt, slots, PEER_TILE), F32),
        ],
        compiler_params=_cparams(("arbitrary",)),
        name="peer_router",
    )(x2d, g.reshape(1, d), wq_hi, wq_lo, k1_hi, k1_lo, k2_hi, k2_lo)


def _score_kernel(e_ref, x_ref, g_ref, gate_ref, tab_ref, w_ref, stage_ref):
    t, d = x_ref.shape
    nchunk = d // (2 * LANES)
    slots = e_ref.shape[1]
    rows = SLOT_GROUP * t

    xn = _rms(x_ref[...], g_ref[...]).astype(BF16)
    x_even = jnp.concatenate([xn[:, (2 * c) * LANES:(2 * c + 1) * LANES] for c in range(nchunk)], axis=1)
    x_odd = jnp.concatenate([xn[:, (2 * c + 1) * LANES:(2 * c + 2) * LANES] for c in range(nchunk)], axis=1)
    rhs = jnp.concatenate([x_even, x_odd], axis=0)

    ri = lax.broadcasted_iota(I32, (2 * t, 2 * t), 0)
    ci = lax.broadcasted_iota(I32, (2 * t, 2 * t), 1)
    diag = (ci == (ri % 2) * t + ri // 2).astype(F32)

    for sg in range(slots // SLOT_GROUP):
        def gather(i, carry, sg=sg):
            s = i // (t // SUBLANES)
            t8 = i % (t // SUBLANES)
            for tt in range(SUBLANES):
                idx = e_ref[0, sg * SLOT_GROUP + s, t8 * SUBLANES + tt]
                slab = tab_ref[idx]
                stage_ref[i, pl.ds(tt, nchunk, stride=SUBLANES), :] = slab
            return carry
        lax.fori_loop(0, rows // SUBLANES, gather, 0)

        planes = []
        for c in range(nchunk):
            plane = stage_ref[:, c * SUBLANES:(c + 1) * SUBLANES, :].reshape(rows, LANES)
            planes.append(pltpu.bitcast(plane, BF16))
        lhs = jnp.concatenate(planes, axis=1)
        prod = _dot_t(lhs, rhs)
        prod = prod.reshape(SLOT_GROUP, 2 * t, 2 * t) * diag[None]
        a2 = jnp.sum(prod, axis=1)
        a = a2[:, :t] + a2[:, t:]
        gelu = 0.5 * a * (1.0 + lax.erf(a * (2.0 ** -0.5)))
        sl = slice(sg * SLOT_GROUP, (sg + 1) * SLOT_GROUP)
        w_ref[0, sl, :] = gate_ref[0, sl, :] * gelu


def _peer_scores(e_t, gate_t, x2d, g, table):
    nt, slots, t = e_t.shape
    d = x2d.shape[1]
    nchunk = d // (2 * LANES)
    rows = SLOT_GROUP * t
    return pl.pallas_call(
        _score_kernel,
        grid=(nt,),
        in_specs=[
            pl.BlockSpec((1, slots, t), lambda i: (i, 0, 0), memory_space=pltpu.SMEM),
            pl.BlockSpec((t, d), lambda i: (i, 0)),
            pl.BlockSpec((1, d), lambda i: (0, 0)),
            pl.BlockSpec((1, slots, t), lambda i: (i, 0, 0)),
            pl.BlockSpec(table.shape, lambda i: (0, 0, 0), pipeline_mode=pl.Buffered(1)),
        ],
        out_specs=pl.BlockSpec((1, slots, t), lambda i: (i, 0, 0)),
        out_shape=jax.ShapeDtypeStruct((nt, slots, t), F32),
        scratch_shapes=[pltpu.VMEM((rows // SUBLANES, nchunk * SUBLANES, LANES), I32)],
        compiler_params=_cparams(("arbitrary",)),
        name="peer_scores",
    )(e_t, x2d, g.reshape(1, d), gate_t, table)


def _value_kernel(e_ref, w_ref, x_ref, gf_ref, tab_ref, o_ref, acc_ref, *, final_norm):
    t, d = x_ref.shape
    nblk = d // LANES
    slots = e_ref.shape[1]
    unroll = 16

    def token(tok, carry):
        def group(gi, acc):
            for k in range(unroll):
                s = gi * unroll + k
                idx = e_ref[0, s, tok]
                wt = w_ref[0, s, tok]
                acc = acc + wt * tab_ref[idx].astype(F32)
            return acc
        acc = lax.fori_loop(0, slots // unroll, group, jnp.zeros((nblk, LANES), F32))
        acc_ref[pl.ds(pl.multiple_of(tok * nblk, nblk), nblk), :] = acc
        return carry
    lax.fori_loop(0, t, token, 0)

    y = jnp.concatenate([acc_ref[pl.ds(b, t, stride=nblk), :] for b in range(nblk)], axis=1)
    y = x_ref[...] + y
    if final_norm:
        y = _rms(y, gf_ref[...])
    o_ref[...] = y


def _peer_values(e_t, w_t, x2d, table, g_final, final_norm):
    nt, slots, t = e_t.shape
    d = x2d.shape[1]
    smem = functools.partial(pl.BlockSpec, (1, slots, t), lambda i: (i, 0, 0), memory_space=pltpu.SMEM)
    return pl.pallas_call(
        functools.partial(_value_kernel, final_norm=final_norm),
        grid=(nt,),
        in_specs=[
            smem(), smem(),
            pl.BlockSpec((t, d), lambda i: (i, 0)),
            pl.BlockSpec((1, d), lambda i: (0, 0)),
            pl.BlockSpec(table.shape, lambda i: (0, 0, 0), pipeline_mode=pl.Buffered(1)),
        ],
        out_specs=pl.BlockSpec((t, d), lambda i: (i, 0)),
        out_shape=jax.ShapeDtypeStruct(x2d.shape, F32),
        scratch_shapes=[pltpu.VMEM((t * d // LANES, LANES), F32)],
        compiler_params=_cparams(("arbitrary",)),
        name="peer_values",
    )(e_t, w_t, x2d, g_final.reshape(1, d), table)


def _pack_pairs(w):
    e, d = w.shape
    bits = lax.bitcast_convert_type(w.astype(BF16), jnp.uint16).astype(jnp.uint32)
    bits = bits.reshape(e, d // (2 * LANES), 2, LANES)
    packed = bits[:, :, 0, :] | (bits[:, :, 1, :] << 16)
    return lax.bitcast_convert_type(packed, I32)


def _peer_ffn(x, g, w_q, k1, k2, u_emb, v_emb, g_final, final_norm):
    b, s, d = x.shape
    x2d = x.reshape(b * s, d)
    e_t, gate_t = _peer_router(x2d, g, w_q, k1, k2)
    w_t = _peer_scores(e_t, gate_t, x2d, g, _pack_pairs(u_emb))
    v_tab = v_emb.astype(BF16).reshape(v_emb.shape[0], d // LANES, LANES)
    out = _peer_values(e_t, w_t, x2d, v_tab, g_final, final_norm)
    return out.reshape(b, s, d)


def _qkv_kernel(x_ref, g_ref, wqk_hi_ref, wqk_lo_ref, wv_ref, cos_ref, sin_ref,
                qt_ref, k_ref, vt_ref, km_ref):
    tm, d = x_ref.shape[1], x_ref.shape[2]
    hn = _rms(x_ref[0], g_ref[...])
    hh, hl = _split(hn)
    qk = _dot3(hh, hl, wqk_hi_ref[...], wqk_lo_ref[...])
    v = _dot(hh, wv_ref[...])
    cos = jnp.concatenate([cos_ref[...]] * (d // LANES), axis=1)
    sin = jnp.concatenate([sin_ref[...]] * (d // LANES), axis=1)
    lane = lax.broadcasted_iota(I32, (tm, d), 1)
    first_half = (lane % HEAD_DIM) < (HEAD_DIM // 2)

    def rope(a):
        rot = jnp.where(first_half, pltpu.roll(a, d - HEAD_DIM // 2, 1), pltpu.roll(a, HEAD_DIM // 2, 1))
        return a * cos + rot * sin

    q = rope(qk[:, :d])
    k = rope(qk[:, d:])
    qt_ref[0] = q.T
    k_ref[0] = k.astype(BF16)
    vt_ref[0] = v.T.astype(BF16)
    nb = tm // MOBA_BLOCK
    km_ref[0, 0] = jnp.mean(k.reshape(nb, MOBA_BLOCK, d), axis=1)


def _qkv_rope(x, g, w_qkv, tm):
    b, s, d = x.shape
    half = HEAD_DIM // 2
    inv = ROPE_THETA ** (-jnp.arange(half, dtype=F32) / half)
    ang = jnp.arange(s).astype(F32)[:, None] * inv[None, :]
    cos, sin = jnp.cos(ang), jnp.sin(ang)
    cos128 = jnp.tile(jnp.concatenate([cos, cos], axis=1), (1, LANES // HEAD_DIM))
    sin128 = jnp.tile(jnp.concatenate([-sin, sin], axis=1), (1, LANES // HEAD_DIM))
    wqk_hi, wqk_lo = _split(w_qkv[:, :2 * d])
    wv = w_qkv[:, 2 * d:].astype(BF16)
    nb = tm // MOBA_BLOCK
    const = lambda i, j: (0, 0)
    qt, k, vt, km = pl.pallas_call(
        _qkv_kernel,
        grid=(b, s // tm),
        in_specs=[
            pl.BlockSpec((1, tm, d), lambda i, j: (i, j, 0)),
            pl.BlockSpec((1, d), const),
            pl.BlockSpec((d, 2 * d), const),
            pl.BlockSpec((d, 2 * d), const),
            pl.BlockSpec((d, d), const),
            pl.BlockSpec((tm, LANES), lambda i, j: (j, 0)),
            pl.BlockSpec((tm, LANES), lambda i, j: (j, 0)),
        ],
        out_specs=[
            pl.BlockSpec((1, d, tm), lambda i, j: (i, 0, j)),
            pl.BlockSpec((1, tm, d), lambda i, j: (i, j, 0)),
            pl.BlockSpec((1, d, tm), lambda i, j: (i, 0, j)),
            pl.BlockSpec((1, 1, nb, d), lambda i, j: (i, j, 0, 0)),
        ],
        out_shape=[
            jax.ShapeDtypeStruct((b, d, s), F32),
            jax.ShapeDtypeStruct((b, s, d), BF16),
            jax.ShapeDtypeStruct((b, d, s), BF16),
            jax.ShapeDtypeStruct((b, s // tm, nb, d), F32),
        ],
        compiler_params=_cparams(("arbitrary", "arbitrary")),
        name="qkv_rope",
    )(x, g.reshape(1, d), wqk_hi, wqk_lo, wv, cos128, sin128)
    return qt, k, vt, km.reshape(b, s // MOBA_BLOCK, d)


def _moba_kernel(qt_ref, k_ref, vt_ref, km_ref, o_ref):
    bs = MOBA_BLOCK
    nb = km_ref.shape[1]
    j = pl.program_id(2)
    qt = qt_ref[0]
    km = km_ref[0]
    scale = HEAD_DIM ** -0.5
    lane_k = lax.broadcasted_iota(I32, (bs, LANES), 1)
    lane_km = lax.broadcasted_iota(I32, (nb, LANES), 1)
    blk = lax.broadcasted_iota(I32, (nb, bs), 0)
    krow = lax.broadcasted_iota(I32, (bs, bs), 0)
    qcol = lax.broadcasted_iota(I32, (bs, bs), 1)
    zeros_pad = jnp.zeros((LANES - HEAD_DIM - nb, bs), F32)
    halves = []
    for hh in range(LANES // HEAD_DIM):
        head_lo = hh * HEAD_DIM
        bias_lo = HEAD_DIM - head_lo
        in_head_k = (lane_k >= head_lo) & (lane_k < head_lo + HEAD_DIM)
        kmh, kml = _split(jnp.where((lane_km >= head_lo) & (lane_km < head_lo + HEAD_DIM), km, 0.0))
        qh, ql = _split(qt)
        gate = _dot3(kmh, kml, qh, ql)
        valid = blk < j
        gate = jnp.where(valid, gate, -jnp.inf)
        sel = jnp.zeros((nb, bs), F32)
        for _ in range(MOBA_TOPK):
            m = jnp.max(gate, axis=0, keepdims=True)
            i = jnp.min(jnp.where(gate == m, blk, nb), axis=0, keepdims=True)
            pick = blk == i
            sel = jnp.where(pick, 1.0, sel)
            gate = jnp.where(pick, -jnp.inf, gate)
        bias_t = jnp.where((sel > 0.0) & valid, 0.0, MASK_NEG)
        q_head = qt[head_lo:head_lo + HEAD_DIM, :] * scale
        if hh == 0:
            q_aug = jnp.concatenate([q_head, bias_t, zeros_pad], axis=0)
        else:
            q_aug = jnp.concatenate([bias_t, zeros_pad, q_head], axis=0)
        q_aug = q_aug.astype(BF16)

        def scores(n, onehot_lane):
            kn = k_ref[0, pl.ds(pl.multiple_of(n * bs, bs), bs), :]
            marker = jnp.where(lane_k == onehot_lane, 1.0, 0.0).astype(BF16)
            k_aug = jnp.where(in_head_k, kn, marker)
            return _dot(k_aug, q_aug)

        def values(n):
            return vt_ref[0, :, pl.ds(pl.multiple_of(n * bs, bs), bs)]

        s_own = jnp.where(krow <= qcol, scores(j, -1), -1e30)
        m0 = jnp.max(s_own, axis=0, keepdims=True)
        p0 = jnp.exp(s_own - m0)
        l0 = jnp.sum(p0, axis=0, keepdims=True)
        acc0 = _dot(values(j), p0.astype(BF16))

        def body(n, carry, bias_lo=bias_lo, scores=scores):
            m, l, acc = carry
            s = scores(n, bias_lo + n)
            m_new = jnp.maximum(m, jnp.max(s, axis=0, keepdims=True))
            alpha = jnp.exp(m - m_new)
            p = jnp.exp(s - m_new)
            l = alpha * l + jnp.sum(p, axis=0, keepdims=True)
            acc = alpha * acc + _dot(values(n), p.astype(BF16))
            return m_new, l, acc

        _, l, acc = lax.fori_loop(0, j, body, (m0, l0, acc0))
        halves.append((acc / l)[head_lo:head_lo + HEAD_DIM, :])
    o_ref[0] = jnp.concatenate(halves, axis=0).T


def _moba_attention(qt, k, vt, km):
    b, d, s = qt.shape
    nb = s // MOBA_BLOCK
    return pl.pallas_call(
        _moba_kernel,
        grid=(b, d // LANES, nb),
        in_specs=[
            pl.BlockSpec((1, LANES, MOBA_BLOCK), lambda i, h, j: (i, h, j)),
            pl.BlockSpec((1, s, LANES), lambda i, h, j: (i, 0, h)),
            pl.BlockSpec((1, LANES, s), lambda i, h, j: (i, h, 0)),
            pl.BlockSpec((1, nb, LANES), lambda i, h, j: (i, 0, h)),
        ],
        out_specs=pl.BlockSpec((1, MOBA_BLOCK, LANES), lambda i, h, j: (i, j, h)),
        out_shape=jax.ShapeDtypeStruct((b, s, d), F32),
        compiler_params=_cparams(("arbitrary", "arbitrary", "arbitrary")),
        name="moba_attention",
    )(qt, k, vt, km)


def _proj_residual_kernel(x_ref, a_ref, w_ref, o_ref):
    o_ref[...] = x_ref[...] + _dot(a_ref[...].astype(BF16), w_ref[...])


def _proj_residual(x2d, a2d, w, tm):
    t, d = x2d.shape
    return pl.pallas_call(
        _proj_residual_kernel,
        grid=(t // tm,),
        in_specs=[
            pl.BlockSpec((tm, d), lambda i: (i, 0)),
            pl.BlockSpec((tm, a2d.shape[1]), lambda i: (i, 0)),
            pl.BlockSpec(w.shape, lambda i: (0, 0)),
        ],
        out_specs=pl.BlockSpec((tm, d), lambda i: (i, 0)),
        out_shape=jax.ShapeDtypeStruct((t, d), F32),
        compiler_params=_cparams(("arbitrary",)),
        name="attn_out_proj",
    )(x2d, a2d, w.astype(BF16))


def _moba_mixer(x, g, w_qkv, w_o):
    b, s, d = x.shape
    tm = min(TOKEN_TILE, s)
    qt, k, vt, km = _qkv_rope(x, g, w_qkv, tm)
    attn = _moba_attention(qt, k, vt, km)
    return _proj_residual(x.reshape(b * s, d), attn.reshape(b * s, d), w_o, tm).reshape(b, s, d)


def kernel(x, norm_mix, norm_ffn, conv_w_in, conv_w, conv_w_out, attn_w_qkv, attn_w_o,
           peer_w_q, peer_k1, peer_k2, peer_u, peer_v, norm_final):
    depth = norm_mix.shape[0]
    tm = min(TOKEN_TILE, x.shape[1])
    for i in range(depth):
        j = i // 2
        if i % 2 == 0:
            x = _conv_mixer(x, norm_mix[i], conv_w_in[j], conv_w[j], conv_w_out[j], tm)
        else:
            x = _moba_mixer(x, norm_mix[i], attn_w_qkv[j], attn_w_o[j])
        x = _peer_ffn(x, norm_ffn[i], peer_w_q[i], peer_k1[i], peer_k2[i], peer_u[i], peer_v[i],
                      norm_final, final_norm=(i == depth - 1))
    return x
```

```python
import functools

import jax
import jax.numpy as jnp
from jax import lax
from jax.experimental import pallas as pl
from jax.experimental.pallas import tpu as pltpu

F32 = jnp.float32
BF16 = jnp.bfloat16
I32 = jnp.int32

RMS_EPS = 1e-6
N_HEADS = 16
HEAD_DIM = 64
MOBA_BLOCK = 256
MOBA_TOPK = 3
ROPE_THETA = 10000.0
PEER_HEADS = 8
PEER_NKEYS = 128
PEER_HALF = 64
PEER_TOPK = 16

LANES = 128
SUBLANES = 8
VMEM_LIMIT = 56 * 1024 * 1024
MASK_NEG = -1e9

TOKEN_TILE = 512
PEER_TILE = 128
OCTET = 8
KV_CHUNK = 2


def _cparams(sem):
    return pltpu.CompilerParams(dimension_semantics=sem, vmem_limit_bytes=VMEM_LIMIT)


def _rms(x, g):
    ms = jnp.mean(x * x, axis=-1, keepdims=True)
    return x * lax.rsqrt(ms + RMS_EPS) * g


def _split(a):
    hi = a.astype(BF16)
    lo = (a - hi.astype(F32)).astype(BF16)
    return hi, lo


def _dot(a, b):
    return lax.dot_general(a, b, (((1,), (0,)), ((), ())), preferred_element_type=F32)


def _dot_t(a, b):
    return lax.dot_general(a, b, (((1,), (1,)), ((), ())), preferred_element_type=F32)


def _dot3(a_hi, a_lo, b_hi, b_lo):
    return _dot(a_hi, b_hi) + _dot(a_lo, b_hi) + _dot(a_hi, b_lo)


def _dot3_t(a_hi, a_lo, b_hi, b_lo):
    return _dot_t(a_hi, b_hi) + _dot_t(a_lo, b_hi) + _dot_t(a_hi, b_lo)


def _conv_mixer_kernel(x_ref, g_ref, win_ref, cw_ref, wout_ref, o_ref, ubuf_ref):
    tm, d = x_ref.shape[1], x_ref.shape[2]

    @pl.when(pl.program_id(1) == 0)
    def _():
        ubuf_ref[0:SUBLANES, :] = jnp.zeros((SUBLANES, d), F32)

    x = x_ref[0]
    hn = _rms(x, g_ref[...]).astype(BF16)
    bcz = _dot(hn, win_ref[...])
    b_gate, c_gate, z = bcz[:, :d], bcz[:, d:2 * d], bcz[:, 2 * d:]
    u = c_gate * z
    ubuf_ref[SUBLANES:SUBLANES + tm, :] = u
    u1 = ubuf_ref[SUBLANES - 1:SUBLANES - 1 + tm, :]
    u2 = ubuf_ref[SUBLANES - 2:SUBLANES - 2 + tm, :]
    cw = cw_ref[...]
    u_conv = cw[0:1, :] * u2 + cw[1:2, :] * u1 + cw[2:3, :] * u
    ubuf_ref[0:SUBLANES, :] = u[tm - SUBLANES:tm, :]
    y = (b_gate * u_conv).astype(BF16)
    o_ref[0] = x + _dot(y, wout_ref[...])


def _conv_mixer(x, g, w_in, conv_w, w_out, tm):
    b, s, d = x.shape
    return pl.pallas_call(
        _conv_mixer_kernel,
        grid=(b, s // tm),
        in_specs=[
            pl.BlockSpec((1, tm, d), lambda i, j: (i, j, 0)),
            pl.BlockSpec((1, d), lambda i, j: (0, 0)),
            pl.BlockSpec((d, 3 * d), lambda i, j: (0, 0)),
            pl.BlockSpec((3, d), lambda i, j: (0, 0)),
            pl.BlockSpec((d, d), lambda i, j: (0, 0)),
        ],
        out_specs=pl.BlockSpec((1, tm, d), lambda i, j: (i, j, 0)),
        out_shape=jax.ShapeDtypeStruct((b, s, d), F32),
        scratch_shapes=[pltpu.VMEM((tm + SUBLANES, d), F32)],
        compiler_params=_cparams(("arbitrary", "arbitrary")),
        name="conv_mixer",
    )(x, g.reshape(1, d), w_in.astype(BF16), conv_w, w_out.astype(BF16))


def _topk_rows(s, k, payload=None):
    n = s.shape[0]
    row = lax.broadcasted_iota(I32, s.shape, 0)
    vals, outs = [], []
    for _ in range(k):
        m = jnp.max(s, axis=0, keepdims=True)
        i = jnp.min(jnp.where(s == m, row, n), axis=0, keepdims=True)
        pick = row == i
        vals.append(m)
        if payload is None:
            outs.append(i)
        else:
            outs.append(jnp.max(jnp.where(pick, payload, -1), axis=0, keepdims=True))
        s = jnp.where(pick, -jnp.inf, s)
    return jnp.concatenate(vals, axis=0), jnp.concatenate(outs, axis=0)


def _router_kernel(x_ref, g_ref, wq_hi_ref, wq_lo_ref, k1_hi_ref, k1_lo_ref, k2_hi_ref, k2_lo_ref,
                   e_ref, gate_ref):
    t = x_ref.shape[0]
    xn = _rms(x_ref[...], g_ref[...])
    xh, xl = _split(xn)
    q = _dot3(xh, xl, wq_hi_ref[...], wq_lo_ref[...])
    nk = k1_hi_ref.shape[0]
    for h in range(PEER_HEADS):
        qh, ql = _split(q[:, h * LANES:(h + 1) * LANES])
        s1 = _dot3_t(k1_hi_ref[...], k1_lo_ref[...], qh, ql)
        s2 = _dot3_t(k2_hi_ref[...], k2_lo_ref[...], qh, ql)
        v1, i1 = _topk_rows(s1, PEER_TOPK)
        v2, i2 = _topk_rows(s2, PEER_TOPK)
        kk = PEER_TOPK
        cand = (v1[:, None, :] + v2[None, :, :]).reshape(kk * kk, t)
        cidx = (i1[:, None, :] * nk + i2[None, :, :]).reshape(kk * kk, t)
        top_s, e_idx = _topk_rows(cand, PEER_TOPK, payload=cidx)
        p = jnp.exp(top_s - top_s[0:1, :])
        gate = p / jnp.sum(p, axis=0, keepdims=True)
        e_ref[0, h * kk:(h + 1) * kk, :] = e_idx
        gate_ref[0, h * kk:(h + 1) * kk, :] = gate


def _pad_keys(k, lo):
    nk, half = k.shape
    out = jnp.zeros((nk, LANES), F32)
    return out.at[:, lo:lo + half].set(k)


def _peer_router(x2d, g, w_q, k1, k2):
    t, d = x2d.shape
    nt = t // PEER_TILE
    slots = PEER_HEADS * PEER_TOPK
    wq_hi, wq_lo = _split(w_q)
    k1_hi, k1_lo = _split(_pad_keys(k1, 0))
    k2_hi, k2_lo = _split(_pad_keys(k2, PEER_HALF))
    const = lambda i: (0, 0)
    kspec = pl.BlockSpec(k1_hi.shape, const)
    return pl.pallas_call(
        _router_kernel,
        grid=(nt,),
        in_specs=[
            pl.BlockSpec((PEER_TILE, d), lambda i: (i, 0)),
            pl.BlockSpec((1, d), const),
            pl.BlockSpec(wq_hi.shape, const),
            pl.BlockSpec(wq_lo.shape, const),
            kspec, kspec, kspec, kspec,
        ],
        out_specs=[
            pl.BlockSpec((1, slots, PEER_TILE), lambda i: (i, 0, 0)),
            pl.BlockSpec((1, slots, PEER_TILE), lambda i: (i, 0, 0)),
        ],
        out_shape=[
            jax.ShapeDtypeStruct((nt, slots, PEER_TILE), I32),
            jax.ShapeDtypeStruct((nt, slots, PEER_TILE), F32),
        ],
        compiler_params=_cparams(("arbitrary",)),
        name="peer_router",
    )(x2d, g.reshape(1, d), wq_hi, wq_lo, k1_hi, k1_lo, k2_hi, k2_lo)


def _gather_octet(e_ref, tab_ref, stage_ref, octet, convert):
    t = e_ref.shape[2]
    nsub = tab_ref.shape[1]
    for k in range(OCTET):
        for tok in range(t):
            idx = e_ref[0, octet * OCTET + k, tok]
            stage_ref[k * (t // SUBLANES) + tok // SUBLANES,
                      pl.ds(tok % SUBLANES, nsub, stride=SUBLANES), :] = convert(tab_ref[idx])


def _octet_pipeline(n_octets, gather, consume, stage_a, stage_b):
    gather(0, stage_a)

    def pair(p, carry):
        gather(2 * p + 1, stage_b)
        consume(2 * p, stage_a)
        gather(jnp.minimum(2 * p + 2, n_octets - 1), stage_a)
        consume(2 * p + 1, stage_b)
        return carry
    lax.fori_loop(0, n_octets // 2, pair, 0)


def _score_kernel(e_ref, x_ref, g_ref, gate_ref, tab_ref, w_ref, stage_a, stage_b, a_ref):
    t, d = x_ref.shape
    nchunk = d // (2 * LANES)
    slots = e_ref.shape[1]
    rows = OCTET * t

    xn = _rms(x_ref[...], g_ref[...]).astype(BF16)
    x_even = jnp.concatenate([xn[:, (2 * c) * LANES:(2 * c + 1) * LANES] for c in range(nchunk)], axis=1)
    x_odd = jnp.concatenate([xn[:, (2 * c + 1) * LANES:(2 * c + 2) * LANES] for c in range(nchunk)], axis=1)
    rhs = jnp.concatenate([x_even, x_odd], axis=0)

    ri = lax.broadcasted_iota(I32, (2 * t, 2 * t), 0)
    ci = lax.broadcasted_iota(I32, (2 * t, 2 * t), 1)
    diag = (ci == (ri % 2) * t + ri // 2).astype(F32)

    def consume(o, stage_ref):
        planes = []
        for c in range(nchunk):
            plane = stage_ref[:, c * SUBLANES:(c + 1) * SUBLANES, :].reshape(rows, LANES)
            planes.append(pltpu.bitcast(plane, BF16))
        lhs = jnp.concatenate(planes, axis=1)
        prod = _dot_t(lhs, rhs)
        prod = prod.reshape(OCTET, 2 * t, 2 * t) * diag[None]
        a2 = jnp.sum(prod, axis=1)
        a_ref[pl.ds(pl.multiple_of(o * OCTET, OCTET), OCTET), :] = a2[:, :t] + a2[:, t:]

    gather = functools.partial(_gather_octet, e_ref, tab_ref, convert=lambda v: v)
    _octet_pipeline(slots // OCTET, lambda o, st: gather(st, o), consume, stage_a, stage_b)

    a = a_ref[...]
    gelu = 0.5 * a * (1.0 + lax.erf(a * (2.0 ** -0.5)))
    w_ref[0] = (gate_ref[0] * gelu).T


def _peer_scores(e_t, gate_t, x2d, g, table):
    nt, slots, t = e_t.shape
    d = x2d.shape[1]
    stage = pltpu.VMEM((OCTET * t // SUBLANES, table.shape[1] * SUBLANES, LANES), I32)
    return pl.pallas_call(
        _score_kernel,
        grid=(nt,),
        in_specs=[
            pl.BlockSpec((1, slots, t), lambda i: (i, 0, 0), memory_space=pltpu.SMEM),
            pl.BlockSpec((t, d), lambda i: (i, 0)),
            pl.BlockSpec((1, d), lambda i: (0, 0)),
            pl.BlockSpec((1, slots, t), lambda i: (i, 0, 0)),
            pl.BlockSpec(table.shape, lambda i: (0, 0, 0), pipeline_mode=pl.Buffered(1)),
        ],
        out_specs=pl.BlockSpec((1, t, slots), lambda i: (i, 0, 0)),
        out_shape=jax.ShapeDtypeStruct((nt, t, slots), F32),
        scratch_shapes=[stage, stage, pltpu.VMEM((slots, t), F32)],
        compiler_params=_cparams(("arbitrary",)),
        name="peer_scores",
    )(e_t, x2d, g.reshape(1, d), gate_t, table)


def _value_kernel(e_ref, w_ref, x_ref, gf_ref, tab_ref, o_ref, stage_a, stage_b, wb_ref, acc_ref, *, final_norm):
    t, d = x_ref.shape
    nblk = d // LANES
    slots = e_ref.shape[1]
    groups = t // SUBLANES

    acc_ref[...] = x_ref[...]

    def consume(o, stage_ref):
        w_oct = pltpu.roll(w_ref[0], lax.rem(slots - o * OCTET, slots), 1)
        for k in range(OCTET):
            wb_ref[k] = jnp.broadcast_to(w_oct[:, k:k + 1], (t, LANES))
        for b in range(nblk):
            acc = acc_ref[:, b * LANES:(b + 1) * LANES]
            for k in range(OCTET):
                rows = stage_ref[k * groups:(k + 1) * groups, b * SUBLANES:(b + 1) * SUBLANES, :]
                acc = acc + wb_ref[k] * rows.reshape(t, LANES)
            acc_ref[:, b * LANES:(b + 1) * LANES] = acc

    gather = functools.partial(_gather_octet, e_ref, tab_ref, convert=lambda v: v.astype(F32))
    _octet_pipeline(slots // OCTET, lambda o, st: gather(st, o), consume, stage_a, stage_b)

    y = acc_ref[...]
    if final_norm:
        y = _rms(y, gf_ref[...])
    o_ref[...] = y


def _peer_values(e_t, w_t, x2d, table, g_final, final_norm):
    nt, slots, t = e_t.shape
    d = x2d.shape[1]
    stage = pltpu.VMEM((OCTET * t // SUBLANES, table.shape[1] * SUBLANES, LANES), F32)
    return pl.pallas_call(
        functools.partial(_value_kernel, final_norm=final_norm),
        grid=(nt,),
        in_specs=[
            pl.BlockSpec((1, slots, t), lambda i: (i, 0, 0), memory_space=pltpu.SMEM),
            pl.BlockSpec((1, t, slots), lambda i: (i, 0, 0)),
            pl.BlockSpec((t, d), lambda i: (i, 0)),
            pl.BlockSpec((1, d), lambda i: (0, 0)),
            pl.BlockSpec(table.shape, lambda i: (0, 0, 0), pipeline_mode=pl.Buffered(1)),
        ],
        out_specs=pl.BlockSpec((t, d), lambda i: (i, 0)),
        out_shape=jax.ShapeDtypeStruct(x2d.shape, F32),
        scratch_shapes=[stage, stage, pltpu.VMEM((OCTET, t, LANES), F32), pltpu.VMEM((t, d), F32)],
        compiler_params=_cparams(("arbitrary",)),
        name="peer_values",
    )(e_t, w_t, x2d, g_final.reshape(1, d), table)


def _pack_pairs(w):
    e, d = w.shape
    bits = lax.bitcast_convert_type(w.astype(BF16), jnp.uint16).astype(jnp.uint32)
    bits = bits.reshape(e, d // (2 * LANES), 2, LANES)
    packed = bits[:, :, 0, :] | (bits[:, :, 1, :] << 16)
    return lax.bitcast_convert_type(packed, I32)


def _peer_ffn(x, g, w_q, k1, k2, u_emb, v_emb, g_final, final_norm):
    b, s, d = x.shape
    x2d = x.reshape(b * s, d)
    e_slot, gate_slot = _peer_router(x2d, g, w_q, k1, k2)
    w_tok = _peer_scores(e_slot, gate_slot, x2d, g, _pack_pairs(u_emb))
    v_tab = v_emb.astype(BF16).reshape(v_emb.shape[0], d // LANES, LANES)
    out = _peer_values(e_slot, w_tok, x2d, v_tab, g_final, final_norm)
    return out.reshape(b, s, d)


def _qkv_kernel(x_ref, g_ref, wqk_hi_ref, wqk_lo_ref, wv_ref, cos_ref, sin_ref,
                qt_ref, k_ref, vt_ref, km_ref):
    tm, d = x_ref.shape[1], x_ref.shape[2]
    hn = _rms(x_ref[0], g_ref[...])
    hh, hl = _split(hn)
    qk = _dot3(hh, hl, wqk_hi_ref[...], wqk_lo_ref[...])
    v = _dot(hh, wv_ref[...])
    cos = jnp.concatenate([cos_ref[...]] * (d // LANES), axis=1)
    sin = jnp.concatenate([sin_ref[...]] * (d // LANES), axis=1)
    lane = lax.broadcasted_iota(I32, (tm, d), 1)
    first_half = (lane % HEAD_DIM) < (HEAD_DIM // 2)

    def rope(a):
        rot = jnp.where(first_half, pltpu.roll(a, d - HEAD_DIM // 2, 1), pltpu.roll(a, HEAD_DIM // 2, 1))
        return a * cos + rot * sin

    q = rope(qk[:, :d])
    k = rope(qk[:, d:])
    qt_ref[0] = q.T
    k_ref[0] = k.astype(BF16)
    vt_ref[0] = v.T.astype(BF16)
    nb = tm // MOBA_BLOCK
    km_ref[0, 0] = jnp.mean(k.reshape(nb, MOBA_BLOCK, d), axis=1)


def _qkv_rope(x, g, w_qkv, tm):
    b, s, d = x.shape
    half = HEAD_DIM // 2
    inv = ROPE_THETA ** (-jnp.arange(half, dtype=F32) / half)
    ang = jnp.arange(s).astype(F32)[:, None] * inv[None, :]
    cos, sin = jnp.cos(ang), jnp.sin(ang)
    cos128 = jnp.tile(jnp.concatenate([cos, cos], axis=1), (1, LANES // HEAD_DIM))
    sin128 = jnp.tile(jnp.concatenate([-sin, sin], axis=1), (1, LANES // HEAD_DIM))
    wqk_hi, wqk_lo = _split(w_qkv[:, :2 * d])
    wv = w_qkv[:, 2 * d:].astype(BF16)
    nb = tm // MOBA_BLOCK
    const = lambda i, j: (0, 0)
    qt, k, vt, km = pl.pallas_call(
        _qkv_kernel,
        grid=(b, s // tm),
        in_specs=[
            pl.BlockSpec((1, tm, d), lambda i, j: (i, j, 0)),
            pl.BlockSpec((1, d), const),
            pl.BlockSpec((d, 2 * d), const),
            pl.BlockSpec((d, 2 * d), const),
            pl.BlockSpec((d, d), const),
            pl.BlockSpec((tm, LANES), lambda i, j: (j, 0)),
            pl.BlockSpec((tm, LANES), lambda i, j: (j, 0)),
        ],
        out_specs=[
            pl.BlockSpec((1, d, tm), lambda i, j: (i, 0, j)),
            pl.BlockSpec((1, tm, d), lambda i, j: (i, j, 0)),
            pl.BlockSpec((1, d, tm), lambda i, j: (i, 0, j)),
            pl.BlockSpec((1, 1, nb, d), lambda i, j: (i, j, 0, 0)),
        ],
        out_shape=[
            jax.ShapeDtypeStruct((b, d, s), F32),
            jax.ShapeDtypeStruct((b, s, d), BF16),
            jax.ShapeDtypeStruct((b, d, s), BF16),
            jax.ShapeDtypeStruct((b, s // tm, nb, d), F32),
        ],
        compiler_params=_cparams(("arbitrary", "arbitrary")),
        name="qkv_rope",
    )(x, g.reshape(1, d), wqk_hi, wqk_lo, wv, cos128, sin128)
    return qt, k, vt, km.reshape(b, s // MOBA_BLOCK, d)


def _moba_kernel(qt_ref, k_ref, vt_ref, km_ref, o_ref):
    bs = MOBA_BLOCK
    nb = km_ref.shape[1]
    n_heads = LANES // HEAD_DIM
    j = pl.program_id(2)
    qt = qt_ref[0]
    km = km_ref[0]
    scale = HEAD_DIM ** -0.5
    lane_km = lax.broadcasted_iota(I32, (nb, LANES), 1)
    blk = lax.broadcasted_iota(I32, (nb, bs), 0)
    zeros_pad = jnp.zeros((LANES - HEAD_DIM - nb, bs), F32)
    qh, ql = _split(qt)

    def k_aug(kn, head_lo, marker_lane):
        lane = lax.broadcasted_iota(I32, kn.shape, 1)
        in_head = (lane >= head_lo) & (lane < head_lo + HEAD_DIM)
        marker = jnp.where(lane == marker_lane, 1.0, 0.0).astype(BF16)
        return jnp.where(in_head, kn, marker)

    k_own = k_ref[0, pl.ds(pl.multiple_of(j * bs, bs), bs), :]
    v_own = vt_ref[0, :, pl.ds(pl.multiple_of(j * bs, bs), bs)]
    krow = lax.broadcasted_iota(I32, (bs, bs), 0)
    qcol = lax.broadcasted_iota(I32, (bs, bs), 1)

    q_augs, state = [], []
    for hh in range(n_heads):
        head_lo = hh * HEAD_DIM
        in_head_km = (lane_km >= head_lo) & (lane_km < head_lo + HEAD_DIM)
        kmh, kml = _split(jnp.where(in_head_km, km, 0.0))
        gate = _dot3(kmh, kml, qh, ql)
        valid = blk < j
        gate = jnp.where(valid, gate, -jnp.inf)
        sel = jnp.zeros((nb, bs), F32)
        for _ in range(MOBA_TOPK):
            m = jnp.max(gate, axis=0, keepdims=True)
            i = jnp.min(jnp.where(gate == m, blk, nb), axis=0, keepdims=True)
            pick = blk == i
            sel = jnp.where(pick, 1.0, sel)
            gate = jnp.where(pick, -jnp.inf, gate)
        bias_t = jnp.where((sel > 0.0) & valid, 0.0, MASK_NEG)
        q_head = qt[head_lo:head_lo + HEAD_DIM, :] * scale
        if hh == 0:
            q_aug = jnp.concatenate([q_head, bias_t, zeros_pad], axis=0)
        else:
            q_aug = jnp.concatenate([bias_t, zeros_pad, q_head], axis=0)
        q_aug = q_aug.astype(BF16)
        q_augs.append(q_aug)

        s_own = jnp.where(krow <= qcol, _dot(k_aug(k_own, head_lo, -1), q_aug), -1e30)
        m0 = jnp.max(s_own, axis=0, keepdims=True)
        p0 = jnp.exp(s_own - m0)
        state += [m0, jnp.sum(p0, axis=0, keepdims=True), _dot(v_own, p0.astype(BF16))]

    chunk = KV_CHUNK * bs
    row_blk = lax.broadcasted_iota(I32, (chunk, LANES), 0) // bs

    def body(c, state):
        kn = k_ref[0, pl.ds(pl.multiple_of(c * chunk, chunk), chunk), :]
        vn = vt_ref[0, :, pl.ds(pl.multiple_of(c * chunk, chunk), chunk)]
        new_state = []
        for hh in range(n_heads):
            m, l, acc = state[3 * hh:3 * hh + 3]
            head_lo = hh * HEAD_DIM
            bias_lo = HEAD_DIM - head_lo
            s = _dot(k_aug(kn, head_lo, bias_lo + c * KV_CHUNK + row_blk), q_augs[hh])
            m_new = jnp.maximum(m, jnp.max(s, axis=0, keepdims=True))
            alpha = jnp.exp(m - m_new)
            p = jnp.exp(s - m_new)
            l = alpha * l + jnp.sum(p, axis=0, keepdims=True)
            acc = alpha * acc + _dot(vn, p.astype(BF16))
            new_state += [m_new, l, acc]
        return tuple(new_state)

    state = lax.fori_loop(0, (j + KV_CHUNK - 1) // KV_CHUNK, body, tuple(state))
    halves = []
    for hh in range(n_heads):
        _, l, acc = state[3 * hh:3 * hh + 3]
        halves.append((acc / l)[hh * HEAD_DIM:(hh + 1) * HEAD_DIM, :])
    o_ref[0] = jnp.concatenate(halves, axis=0).T


def _moba_attention(qt, k, vt, km):
    b, d, s = qt.shape
    nb = s // MOBA_BLOCK
    assert nb % KV_CHUNK == 0
    return pl.pallas_call(
        _moba_kernel,
        grid=(b, d // LANES, nb),
        in_specs=[
            pl.BlockSpec((1, LANES, MOBA_BLOCK), lambda i, h, j: (i, h, j)),
            pl.BlockSpec((1, s, LANES), lambda i, h, j: (i, 0, h)),
            pl.BlockSpec((1, LANES, s), lambda i, h, j: (i, h, 0)),
            pl.BlockSpec((1, nb, LANES), lambda i, h, j: (i, 0, h)),
        ],
        out_specs=pl.BlockSpec((1, MOBA_BLOCK, LANES), lambda i, h, j: (i, j, h)),
        out_shape=jax.ShapeDtypeStruct((b, s, d), F32),
        compiler_params=_cparams(("arbitrary", "arbitrary", "arbitrary")),
        name="moba_attention",
    )(qt, k, vt, km)


def _proj_residual_kernel(x_ref, a_ref, w_ref, o_ref):
    o_ref[...] = x_ref[...] + _dot(a_ref[...].astype(BF16), w_ref[...])


def _proj_residual(x2d, a2d, w, tm):
    t, d = x2d.shape
    return pl.pallas_call(
        _proj_residual_kernel,
        grid=(t // tm,),
        in_specs=[
            pl.BlockSpec((tm, d), lambda i: (i, 0)),
            pl.BlockSpec((tm, a2d.shape[1]), lambda i: (i, 0)),
            pl.BlockSpec(w.shape, lambda i: (0, 0)),
        ],
        out_specs=pl.BlockSpec((tm, d), lambda i: (i, 0)),
        out_shape=jax.ShapeDtypeStruct((t, d), F32),
        compiler_params=_cparams(("arbitrary",)),
        name="attn_out_proj",
    )(x2d, a2d, w.astype(BF16))


def _moba_mixer(x, g, w_qkv, w_o):
    b, s, d = x.shape
    tm = min(TOKEN_TILE, s)
    qt, k, vt, km = _qkv_rope(x, g, w_qkv, tm)
    attn = _moba_attention(qt, k, vt, km)
    return _proj_residual(x.reshape(b * s, d), attn.reshape(b * s, d), w_o, tm).reshape(b, s, d)


def kernel(x, norm_mix, norm_ffn, conv_w_in, conv_w, conv_w_out, attn_w_qkv, attn_w_o,
           peer_w_q, peer_k1, peer_k2, peer_u, peer_v, norm_final):
    depth = norm_mix.shape[0]
    tm = min(TOKEN_TILE, x.shape[1])
    for i in range(depth):
        j = i // 2
        if i % 2 == 0:
            x = _conv_mixer(x, norm_mix[i], conv_w_in[j], conv_w[j], conv_w_out[j], tm)
        else:
            x = _moba_mixer(x, norm_mix[i], attn_w_qkv[j], attn_w_o[j])
        x = _peer_ffn(x, norm_ffn[i], peer_w_q[i], peer_k1[i], peer_k2[i], peer_u[i], peer_v[i],
                      norm_final, final_norm=(i == depth - 1))
    return x
```

```python
import functools

import jax
import jax.numpy as jnp
from jax import lax
from jax.experimental import pallas as pl
from jax.experimental.pallas import tpu as pltpu

F32 = jnp.float32
BF16 = jnp.bfloat16
I32 = jnp.int32

RMS_EPS = 1e-6
N_HEADS = 16
HEAD_DIM = 64
MOBA_BLOCK = 256
MOBA_TOPK = 3
ROPE_THETA = 10000.0
PEER_HEADS = 8
PEER_NKEYS = 128
PEER_HALF = 64
PEER_TOPK = 16

LANES = 128
SUBLANES = 8
VMEM_LIMIT = 56 * 1024 * 1024
MASK_NEG = -1e9

TOKEN_TILE = 512
PEER_TILE = 128
OCTET = 8
KV_CHUNK = 2


def _cparams(sem):
    return pltpu.CompilerParams(dimension_semantics=sem, vmem_limit_bytes=VMEM_LIMIT)


def _rms(x, g):
    ms = jnp.mean(x * x, axis=-1, keepdims=True)
    return x * lax.rsqrt(ms + RMS_EPS) * g


def _split(a):
    hi = a.astype(BF16)
    lo = (a - hi.astype(F32)).astype(BF16)
    return hi, lo


def _dot(a, b):
    return lax.dot_general(a, b, (((1,), (0,)), ((), ())), preferred_element_type=F32)


def _dot_t(a, b):
    return lax.dot_general(a, b, (((1,), (1,)), ((), ())), preferred_element_type=F32)


def _dot3(a_hi, a_lo, b_hi, b_lo):
    return _dot(a_hi, b_hi) + _dot(a_lo, b_hi) + _dot(a_hi, b_lo)


def _dot3_t(a_hi, a_lo, b_hi, b_lo):
    return _dot_t(a_hi, b_hi) + _dot_t(a_lo, b_hi) + _dot_t(a_hi, b_lo)


def _conv_mixer_kernel(x_ref, g_ref, win_ref, cw_ref, wout_ref, o_ref, ubuf_ref):
    tm, d = x_ref.shape[1], x_ref.shape[2]

    @pl.when(pl.program_id(1) == 0)
    def _():
        ubuf_ref[0:SUBLANES, :] = jnp.zeros((SUBLANES, d), F32)

    x = x_ref[0]
    hn = _rms(x, g_ref[...]).astype(BF16)
    bcz = _dot(hn, win_ref[...])
    b_gate, c_gate, z = bcz[:, :d], bcz[:, d:2 * d], bcz[:, 2 * d:]
    u = c_gate * z
    ubuf_ref[SUBLANES:SUBLANES + tm, :] = u
    u1 = ubuf_ref[SUBLANES - 1:SUBLANES - 1 + tm, :]
    u2 = ubuf_ref[SUBLANES - 2:SUBLANES - 2 + tm, :]
    cw = cw_ref[...]
    u_conv = cw[0:1, :] * u2 + cw[1:2, :] * u1 + cw[2:3, :] * u
    ubuf_ref[0:SUBLANES, :] = u[tm - SUBLANES:tm, :]
    y = (b_gate * u_conv).astype(BF16)
    o_ref[0] = x + _dot(y, wout_ref[...])


def _conv_mixer(x, g, w_in, conv_w, w_out, tm):
    b, s, d = x.shape
    return pl.pallas_call(
        _conv_mixer_kernel,
        grid=(b, s // tm),
        in_specs=[
            pl.BlockSpec((1, tm, d), lambda i, j: (i, j, 0)),
            pl.BlockSpec((1, d), lambda i, j: (0, 0)),
            pl.BlockSpec((d, 3 * d), lambda i, j: (0, 0)),
            pl.BlockSpec((3, d), lambda i, j: (0, 0)),
            pl.BlockSpec((d, d), lambda i, j: (0, 0)),
        ],
        out_specs=pl.BlockSpec((1, tm, d), lambda i, j: (i, j, 0)),
        out_shape=jax.ShapeDtypeStruct((b, s, d), F32),
        scratch_shapes=[pltpu.VMEM((tm + SUBLANES, d), F32)],
        compiler_params=_cparams(("arbitrary", "arbitrary")),
        name="conv_mixer",
    )(x, g.reshape(1, d), w_in.astype(BF16), conv_w, w_out.astype(BF16))


def _topk_rows(s, k, order=None, payload=None):
    if order is None:
        order = lax.broadcasted_iota(I32, s.shape, 0)
    big = jnp.iinfo(jnp.int32).max
    vals, outs = [], []
    for _ in range(k):
        m = jnp.max(s, axis=0, keepdims=True)
        i = jnp.min(jnp.where(s == m, order, big), axis=0, keepdims=True)
        pick = order == i
        vals.append(m)
        if payload is None:
            outs.append(i)
        else:
            outs.append(jnp.max(jnp.where(pick, payload, -1), axis=0, keepdims=True))
        s = jnp.where(pick, -jnp.inf, s)
    return jnp.concatenate(vals, axis=0), jnp.concatenate(outs, axis=0)


def _staircase(kk):
    groups = []
    for a in range(2):
        for b0 in range(0, kk // (a + 1), SUBLANES):
            groups.append((a, 0, b0, 1, lambda j, a=a, b0=b0: (a + 1) * (b0 + j + 1) <= kk))
    for b in range(kk // 3):
        for a0 in range(0, kk // (b + 1), SUBLANES):
            groups.append((a0, 1, b, 0, lambda j, a0=a0, b=b: (a0 + j >= 2) & ((a0 + j + 1) * (b + 1) <= kk)))
    return groups


def _staircase_topk(v1, i1, v2, i2, nk):
    kk, t = v1.shape
    j = lax.broadcasted_iota(I32, (SUBLANES, t), 0)

    def rows(x, x0, step):
        if step == 0:
            return jnp.broadcast_to(x[x0:x0 + 1, :], (SUBLANES, t))
        return x[x0:x0 + SUBLANES, :]

    cand, order, cidx = [], [], []
    for a0, a_step, b0, b_step, valid in _staircase(kk):
        ok = valid(j)
        cand.append(jnp.where(ok, rows(v1, a0, a_step) + rows(v2, b0, b_step), -jnp.inf))
        order.append(jnp.where(ok, (a0 + j * a_step) * kk + (b0 + j * b_step), jnp.iinfo(jnp.int32).max - 1))
        cidx.append(rows(i1, a0, a_step) * nk + rows(i2, b0, b_step))
    cat = lambda xs: jnp.concatenate(xs, axis=0)
    return _topk_rows(cat(cand), kk, order=cat(order), payload=cat(cidx))


def _router_kernel(x_ref, g_ref, wq_hi_ref, wq_lo_ref, k1_hi_ref, k1_lo_ref, k2_hi_ref, k2_lo_ref,
                   e_ref, gate_ref):
    t = x_ref.shape[0]
    xn = _rms(x_ref[...], g_ref[...])
    xh, xl = _split(xn)
    q = _dot3(xh, xl, wq_hi_ref[...], wq_lo_ref[...])
    nk = k1_hi_ref.shape[0]
    for h in range(PEER_HEADS):
        qh, ql = _split(q[:, h * LANES:(h + 1) * LANES])
        s1 = _dot3_t(k1_hi_ref[...], k1_lo_ref[...], qh, ql)
        s2 = _dot3_t(k2_hi_ref[...], k2_lo_ref[...], qh, ql)
        v1, i1 = _topk_rows(s1, PEER_TOPK)
        v2, i2 = _topk_rows(s2, PEER_TOPK)
        kk = PEER_TOPK
        top_s, e_idx = _staircase_topk(v1, i1, v2, i2, nk)
        p = jnp.exp(top_s - top_s[0:1, :])
        gate = p / jnp.sum(p, axis=0, keepdims=True)
        e_ref[0, h * kk:(h + 1) * kk, :] = e_idx
        gate_ref[0, h * kk:(h + 1) * kk, :] = gate


def _pad_keys(k, lo):
    nk, half = k.shape
    out = jnp.zeros((nk, LANES), F32)
    return out.at[:, lo:lo + half].set(k)


def _peer_router(x2d, g, w_q, k1, k2):
    t, d = x2d.shape
    nt = t // PEER_TILE
    slots = PEER_HEADS * PEER_TOPK
    wq_hi, wq_lo = _split(w_q)
    k1_hi, k1_lo = _split(_pad_keys(k1, 0))
    k2_hi, k2_lo = _split(_pad_keys(k2, PEER_HALF))
    const = lambda i: (0, 0)
    kspec = pl.BlockSpec(k1_hi.shape, const)
    return pl.pallas_call(
        _router_kernel,
        grid=(nt,),
        in_specs=[
            pl.BlockSpec((PEER_TILE, d), lambda i: (i, 0)),
            pl.BlockSpec((1, d), const),
            pl.BlockSpec(wq_hi.shape, const),
            pl.BlockSpec(wq_lo.shape, const),
            kspec, kspec, kspec, kspec,
        ],
        out_specs=[
            pl.BlockSpec((1, slots, PEER_TILE), lambda i: (i, 0, 0)),
            pl.BlockSpec((1, slots, PEER_TILE), lambda i: (i, 0, 0)),
        ],
        out_shape=[
            jax.ShapeDtypeStruct((nt, slots, PEER_TILE), I32),
            jax.ShapeDtypeStruct((nt, slots, PEER_TILE), F32),
        ],
        compiler_params=_cparams(("arbitrary",)),
        name="peer_router",
    )(x2d, g.reshape(1, d), wq_hi, wq_lo, k1_hi, k1_lo, k2_hi, k2_lo)


def _gather_octet(e_ref, tab_ref, stage_ref, octet):
    t = e_ref.shape[2]
    nsub = tab_ref.shape[1]
    for k in range(OCTET):
        for tok in range(t):
            idx = e_ref[0, octet * OCTET + k, tok]
            stage_ref[k * (t // SUBLANES) + tok // SUBLANES,
                      pl.ds(tok % SUBLANES, nsub, stride=SUBLANES), :] = tab_ref[idx]


def _octet_pipeline(n_octets, gather, consume, stage_a, stage_b):
    gather(0, stage_a)

    def pair(p, carry):
        gather(2 * p + 1, stage_b)
        consume(2 * p, stage_a)
        gather(jnp.minimum(2 * p + 2, n_octets - 1), stage_a)
        consume(2 * p + 1, stage_b)
        return carry
    lax.fori_loop(0, n_octets // 2, pair, 0)


def _score_kernel(e_ref, x_ref, g_ref, gate_ref, tab_ref, w_ref, stage_a, stage_b, a_ref):
    t, d = x_ref.shape
    nchunk = d // (2 * LANES)
    slots = e_ref.shape[1]
    rows = OCTET * t

    xn = _rms(x_ref[...], g_ref[...]).astype(BF16)
    x_even = jnp.concatenate([xn[:, (2 * c) * LANES:(2 * c + 1) * LANES] for c in range(nchunk)], axis=1)
    x_odd = jnp.concatenate([xn[:, (2 * c + 1) * LANES:(2 * c + 2) * LANES] for c in range(nchunk)], axis=1)
    rhs = jnp.concatenate([x_even, x_odd], axis=0)

    ri = lax.broadcasted_iota(I32, (2 * t, 2 * t), 0)
    ci = lax.broadcasted_iota(I32, (2 * t, 2 * t), 1)
    diag = (ci == (ri % 2) * t + ri // 2).astype(F32)

    def consume(o, stage_ref):
        planes = []
        for c in range(nchunk):
            plane = stage_ref[:, c * SUBLANES:(c + 1) * SUBLANES, :].reshape(rows, LANES)
            planes.append(pltpu.bitcast(plane, BF16))
        lhs = jnp.concatenate(planes, axis=1)
        prod = _dot_t(lhs, rhs)
        prod = prod.reshape(OCTET, 2 * t, 2 * t) * diag[None]
        a2 = jnp.sum(prod, axis=1)
        a_ref[pl.ds(pl.multiple_of(o * OCTET, OCTET), OCTET), :] = a2[:, :t] + a2[:, t:]

    gather = lambda o, stage_ref: _gather_octet(e_ref, tab_ref, stage_ref, o)
    _octet_pipeline(slots // OCTET, gather, consume, stage_a, stage_b)

    a = a_ref[...]
    gelu = 0.5 * a * (1.0 + lax.erf(a * (2.0 ** -0.5)))
    w_ref[0] = (gate_ref[0] * gelu).T


def _peer_scores(e_t, gate_t, x2d, g, table):
    nt, slots, t = e_t.shape
    d = x2d.shape[1]
    stage = pltpu.VMEM((OCTET * t // SUBLANES, table.shape[1] * SUBLANES, LANES), I32)
    return pl.pallas_call(
        _score_kernel,
        grid=(nt,),
        in_specs=[
            pl.BlockSpec((1, slots, t), lambda i: (i, 0, 0), memory_space=pltpu.SMEM),
            pl.BlockSpec((t, d), lambda i: (i, 0)),
            pl.BlockSpec((1, d), lambda i: (0, 0)),
            pl.BlockSpec((1, slots, t), lambda i: (i, 0, 0)),
            pl.BlockSpec(table.shape, lambda i: (0, 0, 0), pipeline_mode=pl.Buffered(1)),
        ],
        out_specs=pl.BlockSpec((1, t, slots), lambda i: (i, 0, 0)),
        out_shape=jax.ShapeDtypeStruct((nt, t, slots), F32),
        scratch_shapes=[stage, stage, pltpu.VMEM((slots, t), F32)],
        compiler_params=_cparams(("arbitrary",)),
        name="peer_scores",
    )(e_t, x2d, g.reshape(1, d), gate_t, table)


def _value_kernel(e_ref, w_ref, x_ref, gf_ref, tab_ref, o_ref, stage_a, stage_b, wb_ref, acc_ref, *, final_norm):
    t, d = x_ref.shape
    nchunk = d // (2 * LANES)
    slots = e_ref.shape[1]
    groups = t // SUBLANES

    acc_ref[...] = x_ref[...]

    def consume(o, stage_ref):
        w_oct = pltpu.roll(w_ref[0], lax.rem(slots - o * OCTET, slots), 1)
        for k in range(OCTET):
            wb_ref[k] = jnp.broadcast_to(w_oct[:, k:k + 1], (t, LANES))
        for c in range(nchunk):
            lo_cols = slice(2 * c * LANES, (2 * c + 1) * LANES)
            hi_cols = slice((2 * c + 1) * LANES, (2 * c + 2) * LANES)
            acc_lo, acc_hi = acc_ref[:, lo_cols], acc_ref[:, hi_cols]
            for k in range(OCTET):
                words = stage_ref[k * groups:(k + 1) * groups, c * SUBLANES:(c + 1) * SUBLANES, :]
                words = words.reshape(t, LANES)
                lo = pltpu.bitcast(words << 16, F32)
                hi = pltpu.bitcast(words & jnp.int32(-65536), F32)
                acc_lo = acc_lo + wb_ref[k] * lo
                acc_hi = acc_hi + wb_ref[k] * hi
            acc_ref[:, lo_cols] = acc_lo
            acc_ref[:, hi_cols] = acc_hi

    gather = lambda o, stage_ref: _gather_octet(e_ref, tab_ref, stage_ref, o)
    _octet_pipeline(slots // OCTET, gather, consume, stage_a, stage_b)

    y = acc_ref[...]
    if final_norm:
        y = _rms(y, gf_ref[...])
    o_ref[...] = y


def _peer_values(e_t, w_t, x2d, table, g_final, final_norm):
    nt, slots, t = e_t.shape
    d = x2d.shape[1]
    stage = pltpu.VMEM((OCTET * t // SUBLANES, table.shape[1] * SUBLANES, LANES), I32)
    return pl.pallas_call(
        functools.partial(_value_kernel, final_norm=final_norm),
        grid=(nt,),
        in_specs=[
            pl.BlockSpec((1, slots, t), lambda i: (i, 0, 0), memory_space=pltpu.SMEM),
            pl.BlockSpec((1, t, slots), lambda i: (i, 0, 0)),
            pl.BlockSpec((t, d), lambda i: (i, 0)),
            pl.BlockSpec((1, d), lambda i: (0, 0)),
            pl.BlockSpec(table.shape, lambda i: (0, 0, 0), pipeline_mode=pl.Buffered(1)),
        ],
        out_specs=pl.BlockSpec((t, d), lambda i: (i, 0)),
        out_shape=jax.ShapeDtypeStruct(x2d.shape, F32),
        scratch_shapes=[stage, stage, pltpu.VMEM((OCTET, t, LANES), F32), pltpu.VMEM((t, d), F32)],
        compiler_params=_cparams(("arbitrary",)),
        name="peer_values",
    )(e_t, w_t, x2d, g_final.reshape(1, d), table)


def _pack_pairs(w):
    e, d = w.shape
    bits = lax.bitcast_convert_type(w.astype(BF16), jnp.uint16).astype(jnp.uint32)
    bits = bits.reshape(e, d // (2 * LANES), 2, LANES)
    packed = bits[:, :, 0, :] | (bits[:, :, 1, :] << 16)
    return lax.bitcast_convert_type(packed, I32)


def _peer_ffn(x, g, w_q, k1, k2, u_emb, v_emb, g_final, final_norm):
    b, s, d = x.shape
    x2d = x.reshape(b * s, d)
    e_slot, gate_slot = _peer_router(x2d, g, w_q, k1, k2)
    w_tok = _peer_scores(e_slot, gate_slot, x2d, g, _pack_pairs(u_emb))
    out = _peer_values(e_slot, w_tok, x2d, _pack_pairs(v_emb), g_final, final_norm)
    return out.reshape(b, s, d)


def _qkv_kernel(x_ref, g_ref, wqk_hi_ref, wqk_lo_ref, wv_ref, cos_ref, sin_ref,
                qt_ref, k_ref, vt_ref, km_ref):
    tm, d = x_ref.shape[1], x_ref.shape[2]
    hn = _rms(x_ref[0], g_ref[...])
    hh, hl = _split(hn)
    qk = _dot3(hh, hl, wqk_hi_ref[...], wqk_lo_ref[...])
    v = _dot(hh, wv_ref[...])
    cos = jnp.concatenate([cos_ref[...]] * (d // LANES), axis=1)
    sin = jnp.concatenate([sin_ref[...]] * (d // LANES), axis=1)
    lane = lax.broadcasted_iota(I32, (tm, d), 1)
    first_half = (lane % HEAD_DIM) < (HEAD_DIM // 2)

    def rope(a):
        rot = jnp.where(first_half, pltpu.roll(a, d - HEAD_DIM // 2, 1), pltpu.roll(a, HEAD_DIM // 2, 1))
        return a * cos + rot * sin

    q = rope(qk[:, :d])
    k = rope(qk[:, d:])
    qt_ref[0] = q.T
    k_ref[0] = k.astype(BF16)
    vt_ref[0] = v.T.astype(BF16)
    nb = tm // MOBA_BLOCK
    km_ref[0, 0] = jnp.mean(k.reshape(nb, MOBA_BLOCK, d), axis=1)


def _qkv_rope(x, g, w_qkv, tm):
    b, s, d = x.shape
    half = HEAD_DIM // 2
    inv = ROPE_THETA ** (-jnp.arange(half, dtype=F32) / half)
    ang = jnp.arange(s).astype(F32)[:, None] * inv[None, :]
    cos, sin = jnp.cos(ang), jnp.sin(ang)
    cos128 = jnp.tile(jnp.concatenate([cos, cos], axis=1), (1, LANES // HEAD_DIM))
    sin128 = jnp.tile(jnp.concatenate([-sin, sin], axis=1), (1, LANES // HEAD_DIM))
    wqk_hi, wqk_lo = _split(w_qkv[:, :2 * d])
    wv = w_qkv[:, 2 * d:].astype(BF16)
    nb = tm // MOBA_BLOCK
    const = lambda i, j: (0, 0)
    qt, k, vt, km = pl.pallas_call(
        _qkv_kernel,
        grid=(b, s // tm),
        in_specs=[
            pl.BlockSpec((1, tm, d), lambda i, j: (i, j, 0)),
            pl.BlockSpec((1, d), const),
            pl.BlockSpec((d, 2 * d), const),
            pl.BlockSpec((d, 2 * d), const),
            pl.BlockSpec((d, d), const),
            pl.BlockSpec((tm, LANES), lambda i, j: (j, 0)),
            pl.BlockSpec((tm, LANES), lambda i, j: (j, 0)),
        ],
        out_specs=[
            pl.BlockSpec((1, d, tm), lambda i, j: (i, 0, j)),
            pl.BlockSpec((1, tm, d), lambda i, j: (i, j, 0)),
            pl.BlockSpec((1, d, tm), lambda i, j: (i, 0, j)),
            pl.BlockSpec((1, 1, nb, d), lambda i, j: (i, j, 0, 0)),
        ],
        out_shape=[
            jax.ShapeDtypeStruct((b, d, s), F32),
            jax.ShapeDtypeStruct((b, s, d), BF16),
            jax.ShapeDtypeStruct((b, d, s), BF16),
            jax.ShapeDtypeStruct((b, s // tm, nb, d), F32),
        ],
        compiler_params=_cparams(("arbitrary", "arbitrary")),
        name="qkv_rope",
    )(x, g.reshape(1, d), wqk_hi, wqk_lo, wv, cos128, sin128)
    return qt, k, vt, km.reshape(b, s // MOBA_BLOCK, d)


def _moba_kernel(qt_ref, k_ref, vt_ref, km_ref, o_ref, s_ref):
    bs = MOBA_BLOCK
    nb = km_ref.shape[1]
    n_heads = LANES // HEAD_DIM
    j = pl.program_id(2)
    qt = qt_ref[0]
    km = km_ref[0]
    scale = HEAD_DIM ** -0.5
    lane_km = lax.broadcasted_iota(I32, (nb, LANES), 1)
    blk = lax.broadcasted_iota(I32, (nb, bs), 0)
    zeros_pad = jnp.zeros((LANES - HEAD_DIM - nb, bs), F32)
    qh, ql = _split(qt)

    def k_aug(kn, head_lo, marker_lane):
        lane = lax.broadcasted_iota(I32, kn.shape, 1)
        in_head = (lane >= head_lo) & (lane < head_lo + HEAD_DIM)
        marker = jnp.where(lane == marker_lane, 1.0, 0.0).astype(BF16)
        return jnp.where(in_head, kn, marker)

    k_own = k_ref[0, pl.ds(pl.multiple_of(j * bs, bs), bs), :]
    v_own = vt_ref[0, :, pl.ds(pl.multiple_of(j * bs, bs), bs)]
    krow = lax.broadcasted_iota(I32, (bs, bs), 0)
    qcol = lax.broadcasted_iota(I32, (bs, bs), 1)

    q_augs, state = [], []
    for hh in range(n_heads):
        head_lo = hh * HEAD_DIM
        in_head_km = (lane_km >= head_lo) & (lane_km < head_lo + HEAD_DIM)
        kmh, kml = _split(jnp.where(in_head_km, km, 0.0))
        gate = _dot3(kmh, kml, qh, ql)
        valid = blk < j
        gate = jnp.where(valid, gate, -jnp.inf)
        sel = jnp.zeros((nb, bs), F32)
        for _ in range(MOBA_TOPK):
            m = jnp.max(gate, axis=0, keepdims=True)
            i = jnp.min(jnp.where(gate == m, blk, nb), axis=0, keepdims=True)
            pick = blk == i
            sel = jnp.where(pick, 1.0, sel)
            gate = jnp.where(pick, -jnp.inf, gate)
        bias_t = jnp.where((sel > 0.0) & valid, 0.0, MASK_NEG)
        q_head = qt[head_lo:head_lo + HEAD_DIM, :] * scale
        if hh == 0:
            q_aug = jnp.concatenate([q_head, bias_t, zeros_pad], axis=0)
        else:
            q_aug = jnp.concatenate([bias_t, zeros_pad, q_head], axis=0)
        q_aug = q_aug.astype(BF16)
        q_augs.append(q_aug)

        s_own = jnp.where(krow <= qcol, _dot(k_aug(k_own, head_lo, -1), q_aug), -1e30)
        m0 = jnp.max(s_own, axis=0, keepdims=True)
        p0 = jnp.exp(s_own - m0)
        state += [m0, jnp.sum(p0, axis=0, keepdims=True), _dot(v_own, p0.astype(BF16))]

    chunk = KV_CHUNK * bs
    row_blk = lax.broadcasted_iota(I32, (chunk, LANES), 0) // bs

    def score_chunk(c, slot):
        kn = k_ref[0, pl.ds(pl.multiple_of(c * chunk, chunk), chunk), :]
        for hh in range(n_heads):
            head_lo = hh * HEAD_DIM
            bias_lo = HEAD_DIM - head_lo
            s_ref[slot, hh] = _dot(k_aug(kn, head_lo, bias_lo + c * KV_CHUNK + row_blk), q_augs[hh])

    score_chunk(0, 0)

    def body(c, state):
        s_cur = [s_ref[c % 2, hh] for hh in range(n_heads)]
        score_chunk(jnp.minimum(c + 1, nb // KV_CHUNK - 1), (c + 1) % 2)
        vn = vt_ref[0, :, pl.ds(pl.multiple_of(c * chunk, chunk), chunk)]
        new_state = []
        for hh in range(n_heads):
            m, l, acc = state[3 * hh:3 * hh + 3]
            s = s_cur[hh]
            m_new = jnp.maximum(m, jnp.max(s, axis=0, keepdims=True))
            alpha = jnp.exp(m - m_new)
            p = jnp.exp(s - m_new)
            l = alpha * l + jnp.sum(p, axis=0, keepdims=True)
            acc = alpha * acc + _dot(vn, p.astype(BF16))
            new_state += [m_new, l, acc]
        return tuple(new_state)

    state = lax.fori_loop(0, (j + KV_CHUNK - 1) // KV_CHUNK, body, tuple(state))
    halves = []
    for hh in range(n_heads):
        _, l, acc = state[3 * hh:3 * hh + 3]
        halves.append((acc / l)[hh * HEAD_DIM:(hh + 1) * HEAD_DIM, :])
    o_ref[0] = jnp.concatenate(halves, axis=0).T


def _moba_attention(qt, k, vt, km):
    b, d, s = qt.shape
    nb = s // MOBA_BLOCK
    assert nb % KV_CHUNK == 0
    return pl.pallas_call(
        _moba_kernel,
        grid=(b, d // LANES, nb),
        in_specs=[
            pl.BlockSpec((1, LANES, MOBA_BLOCK), lambda i, h, j: (i, h, j)),
            pl.BlockSpec((1, s, LANES), lambda i, h, j: (i, 0, h)),
            pl.BlockSpec((1, LANES, s), lambda i, h, j: (i, h, 0)),
            pl.BlockSpec((1, nb, LANES), lambda i, h, j: (i, 0, h)),
        ],
        out_specs=pl.BlockSpec((1, MOBA_BLOCK, LANES), lambda i, h, j: (i, j, h)),
        out_shape=jax.ShapeDtypeStruct((b, s, d), F32),
        scratch_shapes=[pltpu.VMEM((2, LANES // HEAD_DIM, KV_CHUNK * MOBA_BLOCK, MOBA_BLOCK), F32)],
        compiler_params=_cparams(("arbitrary", "arbitrary", "arbitrary")),
        name="moba_attention",
    )(qt, k, vt, km)


def _proj_residual_kernel(x_ref, a_ref, w_ref, o_ref):
    o_ref[...] = x_ref[...] + _dot(a_ref[...].astype(BF16), w_ref[...])


def _proj_residual(x2d, a2d, w, tm):
    t, d = x2d.shape
    return pl.pallas_call(
        _proj_residual_kernel,
        grid=(t // tm,),
        in_specs=[
            pl.BlockSpec((tm, d), lambda i: (i, 0)),
            pl.BlockSpec((tm, a2d.shape[1]), lambda i: (i, 0)),
            pl.BlockSpec(w.shape, lambda i: (0, 0)),
        ],
        out_specs=pl.BlockSpec((tm, d), lambda i: (i, 0)),
        out_shape=jax.ShapeDtypeStruct((t, d), F32),
        compiler_params=_cparams(("arbitrary",)),
        name="attn_out_proj",
    )(x2d, a2d, w.astype(BF16))


def _moba_mixer(x, g, w_qkv, w_o):
    b, s, d = x.shape
    tm = min(TOKEN_TILE, s)
    qt, k, vt, km = _qkv_rope(x, g, w_qkv, tm)
    attn = _moba_attention(qt, k, vt, km)
    return _proj_residual(x.reshape(b * s, d), attn.reshape(b * s, d), w_o, tm).reshape(b, s, d)


def kernel(x, norm_mix, norm_ffn, conv_w_in, conv_w, conv_w_out, attn_w_qkv, attn_w_o,
           peer_w_q, peer_k1, peer_k2, peer_u, peer_v, norm_final):
    depth = norm_mix.shape[0]
    tm = min(TOKEN_TILE, x.shape[1])
    for i in range(depth):
        j = i // 2
        if i % 2 == 0:
            x = _conv_mixer(x, norm_mix[i], conv_w_in[j], conv_w[j], conv_w_out[j], tm)
        else:
            x = _moba_mixer(x, norm_mix[i], attn_w_qkv[j], attn_w_o[j])
        x = _peer_ffn(x, norm_ffn[i], peer_w_q[i], peer_k1[i], peer_k2[i], peer_u[i], peer_v[i],
                      norm_final, final_norm=(i == depth - 1))
    return x
```

```python
import functools

import jax
import jax.numpy as jnp
from jax import lax
from jax.experimental import pallas as pl
from jax.experimental.pallas import tpu as pltpu

F32 = jnp.float32
BF16 = jnp.bfloat16
I32 = jnp.int32

RMS_EPS = 1e-6
N_HEADS = 16
HEAD_DIM = 64
MOBA_BLOCK = 256
MOBA_TOPK = 3
ROPE_THETA = 10000.0
PEER_HEADS = 8
PEER_NKEYS = 128
PEER_HALF = 64
PEER_TOPK = 16

LANES = 128
SUBLANES = 8
VMEM_LIMIT = 56 * 1024 * 1024
MASK_NEG = -1e9

TOKEN_TILE = 512
PEER_TILE = 128
OCTET = 8
KV_CHUNK = 2


def _cparams(sem):
    return pltpu.CompilerParams(dimension_semantics=sem, vmem_limit_bytes=VMEM_LIMIT)


def _rms(x, g):
    ms = jnp.mean(x * x, axis=-1, keepdims=True)
    return x * lax.rsqrt(ms + RMS_EPS) * g


def _split(a):
    hi = a.astype(BF16)
    lo = (a - hi.astype(F32)).astype(BF16)
    return hi, lo


def _dot(a, b):
    return lax.dot_general(a, b, (((1,), (0,)), ((), ())), preferred_element_type=F32)


def _dot_t(a, b):
    return lax.dot_general(a, b, (((1,), (1,)), ((), ())), preferred_element_type=F32)


def _dot3(a_hi, a_lo, b_hi, b_lo):
    return _dot(a_hi, b_hi) + _dot(a_lo, b_hi) + _dot(a_hi, b_lo)


def _dot3_t(a_hi, a_lo, b_hi, b_lo):
    return _dot_t(a_hi, b_hi) + _dot_t(a_lo, b_hi) + _dot_t(a_hi, b_lo)


def _conv_mixer_kernel(x_ref, g_ref, win_ref, cw_ref, wout_ref, o_ref, ubuf_ref):
    tm, d = x_ref.shape[1], x_ref.shape[2]

    @pl.when(pl.program_id(1) == 0)
    def _():
        ubuf_ref[0:SUBLANES, :] = jnp.zeros((SUBLANES, d), F32)

    x = x_ref[0]
    hn = _rms(x, g_ref[...]).astype(BF16)
    bcz = _dot(hn, win_ref[...])
    b_gate, c_gate, z = bcz[:, :d], bcz[:, d:2 * d], bcz[:, 2 * d:]
    u = c_gate * z
    ubuf_ref[SUBLANES:SUBLANES + tm, :] = u
    u1 = ubuf_ref[SUBLANES - 1:SUBLANES - 1 + tm, :]
    u2 = ubuf_ref[SUBLANES - 2:SUBLANES - 2 + tm, :]
    cw = cw_ref[...]
    u_conv = cw[0:1, :] * u2 + cw[1:2, :] * u1 + cw[2:3, :] * u
    ubuf_ref[0:SUBLANES, :] = u[tm - SUBLANES:tm, :]
    y = (b_gate * u_conv).astype(BF16)
    o_ref[0] = x + _dot(y, wout_ref[...])


def _conv_mixer(x, g, w_in, conv_w, w_out, tm):
    b, s, d = x.shape
    return pl.pallas_call(
        _conv_mixer_kernel,
        grid=(b, s // tm),
        in_specs=[
            pl.BlockSpec((1, tm, d), lambda i, j: (i, j, 0)),
            pl.BlockSpec((1, d), lambda i, j: (0, 0)),
            pl.BlockSpec((d, 3 * d), lambda i, j: (0, 0)),
            pl.BlockSpec((3, d), lambda i, j: (0, 0)),
            pl.BlockSpec((d, d), lambda i, j: (0, 0)),
        ],
        out_specs=pl.BlockSpec((1, tm, d), lambda i, j: (i, j, 0)),
        out_shape=jax.ShapeDtypeStruct((b, s, d), F32),
        scratch_shapes=[pltpu.VMEM((tm + SUBLANES, d), F32)],
        compiler_params=_cparams(("arbitrary", "arbitrary")),
        name="conv_mixer",
    )(x, g.reshape(1, d), w_in.astype(BF16), conv_w, w_out.astype(BF16))


def _topk_rows(s, k, order=None, payload=None):
    if order is None:
        order = lax.broadcasted_iota(I32, s.shape, 0)
    big = jnp.iinfo(jnp.int32).max
    vals, outs = [], []
    for _ in range(k):
        m = jnp.max(s, axis=0, keepdims=True)
        i = jnp.min(jnp.where(s == m, order, big), axis=0, keepdims=True)
        pick = order == i
        vals.append(m)
        if payload is None:
            outs.append(i)
        else:
            outs.append(jnp.max(jnp.where(pick, payload, -1), axis=0, keepdims=True))
        s = jnp.where(pick, -jnp.inf, s)
    return jnp.concatenate(vals, axis=0), jnp.concatenate(outs, axis=0)


def _staircase(kk):
    groups = []
    for a in range(2):
        for b0 in range(0, kk // (a + 1), SUBLANES):
            groups.append((a, 0, b0, 1, lambda j, a=a, b0=b0: (a + 1) * (b0 + j + 1) <= kk))
    for b in range(kk // 3):
        for a0 in range(0, kk // (b + 1), SUBLANES):
            groups.append((a0, 1, b, 0, lambda j, a0=a0, b=b: (a0 + j >= 2) & ((a0 + j + 1) * (b + 1) <= kk)))
    return groups


def _staircase_topk(v1, i1, v2, i2, nk):
    kk, t = v1.shape
    j = lax.broadcasted_iota(I32, (SUBLANES, t), 0)

    def rows(x, x0, step):
        if step == 0:
            return jnp.broadcast_to(x[x0:x0 + 1, :], (SUBLANES, t))
        return x[x0:x0 + SUBLANES, :]

    cand, order, cidx = [], [], []
    for a0, a_step, b0, b_step, valid in _staircase(kk):
        ok = valid(j)
        cand.append(jnp.where(ok, rows(v1, a0, a_step) + rows(v2, b0, b_step), -jnp.inf))
        order.append(jnp.where(ok, (a0 + j * a_step) * kk + (b0 + j * b_step), jnp.iinfo(jnp.int32).max - 1))
        cidx.append(rows(i1, a0, a_step) * nk + rows(i2, b0, b_step))
    cat = lambda xs: jnp.concatenate(xs, axis=0)
    return _topk_rows(cat(cand), kk, order=cat(order), payload=cat(cidx))


def _router_kernel(x_ref, g_ref, wq_hi_ref, wq_lo_ref, k1_hi_ref, k1_lo_ref, k2_hi_ref, k2_lo_ref,
                   e_ref, gate_ref):
    table_rows = x_ref.shape[1] // (2 * LANES)
    xn = _rms(x_ref[...], g_ref[...])
    xh, xl = _split(xn)
    q = _dot3(xh, xl, wq_hi_ref[...], wq_lo_ref[...])
    nk = k1_hi_ref.shape[0]
    for h in range(PEER_HEADS):
        qh, ql = _split(q[:, h * LANES:(h + 1) * LANES])
        s1 = _dot3_t(k1_hi_ref[...], k1_lo_ref[...], qh, ql)
        s2 = _dot3_t(k2_hi_ref[...], k2_lo_ref[...], qh, ql)
        v1, i1 = _topk_rows(s1, PEER_TOPK)
        v2, i2 = _topk_rows(s2, PEER_TOPK)
        kk = PEER_TOPK
        top_s, e_idx = _staircase_topk(v1, i1, v2, i2, nk)
        p = jnp.exp(top_s - top_s[0:1, :])
        gate = p / jnp.sum(p, axis=0, keepdims=True)
        e_ref[0, h * kk:(h + 1) * kk, :] = e_idx * table_rows
        gate_ref[0, h * kk:(h + 1) * kk, :] = gate


def _pad_keys(k, lo):
    nk, half = k.shape
    out = jnp.zeros((nk, LANES), F32)
    return out.at[:, lo:lo + half].set(k)


def _peer_router(x2d, g, w_q, k1, k2):
    t, d = x2d.shape
    nt = t // PEER_TILE
    slots = PEER_HEADS * PEER_TOPK
    wq_hi, wq_lo = _split(w_q)
    k1_hi, k1_lo = _split(_pad_keys(k1, 0))
    k2_hi, k2_lo = _split(_pad_keys(k2, PEER_HALF))
    const = lambda i: (0, 0)
    kspec = pl.BlockSpec(k1_hi.shape, const)
    return pl.pallas_call(
        _router_kernel,
        grid=(nt,),
        in_specs=[
            pl.BlockSpec((PEER_TILE, d), lambda i: (i, 0)),
            pl.BlockSpec((1, d), const),
            pl.BlockSpec(wq_hi.shape, const),
            pl.BlockSpec(wq_lo.shape, const),
            kspec, kspec, kspec, kspec,
        ],
        out_specs=[
            pl.BlockSpec((1, slots, PEER_TILE), lambda i: (i, 0, 0)),
            pl.BlockSpec((1, slots, PEER_TILE), lambda i: (i, 0, 0)),
        ],
        out_shape=[
            jax.ShapeDtypeStruct((nt, slots, PEER_TILE), I32),
            jax.ShapeDtypeStruct((nt, slots, PEER_TILE), F32),
        ],
        compiler_params=_cparams(("arbitrary",)),
        name="peer_router",
    )(x2d, g.reshape(1, d), wq_hi, wq_lo, k1_hi, k1_lo, k2_hi, k2_lo)


def _gather_octet(e_ref, tab_ref, stage_ref, octet, nsub):
    t = e_ref.shape[2]
    for k in range(OCTET):
        for tok in range(t):
            off = e_ref[0, octet * OCTET + k, tok]
            stage_ref[k * (t // SUBLANES) + tok // SUBLANES,
                      pl.ds(tok % SUBLANES, nsub, stride=SUBLANES), :] = tab_ref[pl.ds(off, nsub), :]


def _octet_pipeline(n_octets, gather, consume, stage_a, stage_b):
    gather(0, stage_a)

    def pair(p, carry):
        gather(2 * p + 1, stage_b)
        consume(2 * p, stage_a)
        gather(jnp.minimum(2 * p + 2, n_octets - 1), stage_a)
        consume(2 * p + 1, stage_b)
        return carry
    lax.fori_loop(0, n_octets // 2, pair, 0)


def _score_kernel(e_ref, x_ref, g_ref, gate_ref, tab_ref, w_ref, stage_a, stage_b, a_ref):
    t, d = x_ref.shape
    nchunk = d // (2 * LANES)
    slots = e_ref.shape[1]
    rows = OCTET * t

    xn = _rms(x_ref[...], g_ref[...]).astype(BF16)
    x_even = jnp.concatenate([xn[:, (2 * c) * LANES:(2 * c + 1) * LANES] for c in range(nchunk)], axis=1)
    x_odd = jnp.concatenate([xn[:, (2 * c + 1) * LANES:(2 * c + 2) * LANES] for c in range(nchunk)], axis=1)
    rhs = jnp.concatenate([x_even, x_odd], axis=0)

    ri = lax.broadcasted_iota(I32, (2 * t, 2 * t), 0)
    ci = lax.broadcasted_iota(I32, (2 * t, 2 * t), 1)
    diag = (ci == (ri % 2) * t + ri // 2).astype(F32)

    def consume(o, stage_ref):
        planes = []
        for c in range(nchunk):
            plane = stage_ref[:, c * SUBLANES:(c + 1) * SUBLANES, :].reshape(rows, LANES)
            planes.append(pltpu.bitcast(plane, BF16))
        lhs = jnp.concatenate(planes, axis=1)
        prod = _dot_t(lhs, rhs)
        prod = prod.reshape(OCTET, 2 * t, 2 * t) * diag[None]
        a2 = jnp.sum(prod, axis=1)
        a_ref[pl.ds(pl.multiple_of(o * OCTET, OCTET), OCTET), :] = a2[:, :t] + a2[:, t:]

    gather = lambda o, stage_ref: _gather_octet(e_ref, tab_ref, stage_ref, o, nchunk)
    _octet_pipeline(slots // OCTET, gather, consume, stage_a, stage_b)

    a = a_ref[...]
    gelu = 0.5 * a * (1.0 + lax.erf(a * (2.0 ** -0.5)))
    w_ref[0] = (gate_ref[0] * gelu).T


def _peer_scores(e_t, gate_t, x2d, g, table):
    nt, slots, t = e_t.shape
    d = x2d.shape[1]
    stage = pltpu.VMEM((OCTET * t // SUBLANES, d // (2 * LANES) * SUBLANES, LANES), I32)
    return pl.pallas_call(
        _score_kernel,
        grid=(nt,),
        in_specs=[
            pl.BlockSpec((1, slots, t), lambda i: (i, 0, 0), memory_space=pltpu.SMEM),
            pl.BlockSpec((t, d), lambda i: (i, 0)),
            pl.BlockSpec((1, d), lambda i: (0, 0)),
            pl.BlockSpec((1, slots, t), lambda i: (i, 0, 0)),
            pl.BlockSpec(table.shape, lambda i: (0, 0), pipeline_mode=pl.Buffered(1)),
        ],
        out_specs=pl.BlockSpec((1, t, slots), lambda i: (i, 0, 0)),
        out_shape=jax.ShapeDtypeStruct((nt, t, slots), F32),
        scratch_shapes=[stage, stage, pltpu.VMEM((slots, t), F32)],
        compiler_params=_cparams(("arbitrary",)),
        name="peer_scores",
    )(e_t, x2d, g.reshape(1, d), gate_t, table)


def _value_kernel(e_ref, w_ref, x_ref, gf_ref, tab_ref, o_ref, stage_a, stage_b, wb_ref, acc_ref, *, final_norm):
    t, d = x_ref.shape
    nchunk = d // (2 * LANES)
    slots = e_ref.shape[1]
    groups = t // SUBLANES

    acc_ref[...] = x_ref[...]

    def consume(o, stage_ref):
        w_oct = pltpu.roll(w_ref[0], lax.rem(slots - o * OCTET, slots), 1)
        for k in range(OCTET):
            wb_ref[k] = jnp.broadcast_to(w_oct[:, k:k + 1], (t, LANES))
        for c in range(nchunk):
            lo_cols = slice(2 * c * LANES, (2 * c + 1) * LANES)
            hi_cols = slice((2 * c + 1) * LANES, (2 * c + 2) * LANES)
            acc_lo, acc_hi = acc_ref[:, lo_cols], acc_ref[:, hi_cols]
            for k in range(OCTET):
                words = stage_ref[k * groups:(k + 1) * groups, c * SUBLANES:(c + 1) * SUBLANES, :]
                words = words.reshape(t, LANES)
                lo = pltpu.bitcast(words << 16, F32)
                hi = pltpu.bitcast(words & jnp.int32(-65536), F32)
                acc_lo = acc_lo + wb_ref[k] * lo
                acc_hi = acc_hi + wb_ref[k] * hi
            acc_ref[:, lo_cols] = acc_lo
            acc_ref[:, hi_cols] = acc_hi

    gather = lambda o, stage_ref: _gather_octet(e_ref, tab_ref, stage_ref, o, nchunk)
    _octet_pipeline(slots // OCTET, gather, consume, stage_a, stage_b)

    y = acc_ref[...]
    if final_norm:
        y = _rms(y, gf_ref[...])
    o_ref[...] = y


def _peer_values(e_t, w_t, x2d, table, g_final, final_norm):
    nt, slots, t = e_t.shape
    d = x2d.shape[1]
    stage = pltpu.VMEM((OCTET * t // SUBLANES, d // (2 * LANES) * SUBLANES, LANES), I32)
    return pl.pallas_call(
        functools.partial(_value_kernel, final_norm=final_norm),
        grid=(nt,),
        in_specs=[
            pl.BlockSpec((1, slots, t), lambda i: (i, 0, 0), memory_space=pltpu.SMEM),
            pl.BlockSpec((1, t, slots), lambda i: (i, 0, 0)),
            pl.BlockSpec((t, d), lambda i: (i, 0)),
            pl.BlockSpec((1, d), lambda i: (0, 0)),
            pl.BlockSpec(table.shape, lambda i: (0, 0), pipeline_mode=pl.Buffered(1)),
        ],
        out_specs=pl.BlockSpec((t, d), lambda i: (i, 0)),
        out_shape=jax.ShapeDtypeStruct(x2d.shape, F32),
        scratch_shapes=[stage, stage, pltpu.VMEM((OCTET, t, LANES), F32), pltpu.VMEM((t, d), F32)],
        compiler_params=_cparams(("arbitrary",)),
        name="peer_values",
    )(e_t, w_t, x2d, g_final.reshape(1, d), table)


def _pack_pairs(w):
    e, d = w.shape
    bits = lax.bitcast_convert_type(w.astype(BF16), jnp.uint16).astype(jnp.uint32)
    bits = bits.reshape(e, d // (2 * LANES), 2, LANES)
    packed = bits[:, :, 0, :] | (bits[:, :, 1, :] << 16)
    return lax.bitcast_convert_type(packed, I32).reshape(e * d // (2 * LANES), LANES)


def _peer_ffn(x, g, w_q, k1, k2, u_emb, v_emb, g_final, final_norm):
    b, s, d = x.shape
    x2d = x.reshape(b * s, d)
    e_slot, gate_slot = _peer_router(x2d, g, w_q, k1, k2)
    w_tok = _peer_scores(e_slot, gate_slot, x2d, g, _pack_pairs(u_emb))
    out = _peer_values(e_slot, w_tok, x2d, _pack_pairs(v_emb), g_final, final_norm)
    return out.reshape(b, s, d)


def _qkv_kernel(x_ref, g_ref, wqk_hi_ref, wqk_lo_ref, wv_ref, cos_ref, sin_ref,
                qt_ref, k_ref, vt_ref, km_ref):
    tm, d = x_ref.shape[1], x_ref.shape[2]
    hn = _rms(x_ref[0], g_ref[...])
    hh, hl = _split(hn)
    qk = _dot3(hh, hl, wqk_hi_ref[...], wqk_lo_ref[...])
    v = _dot(hh, wv_ref[...])
    cos = jnp.concatenate([cos_ref[...]] * (d // LANES), axis=1)
    sin = jnp.concatenate([sin_ref[...]] * (d // LANES), axis=1)
    lane = lax.broadcasted_iota(I32, (tm, d), 1)
    first_half = (lane % HEAD_DIM) < (HEAD_DIM // 2)

    def rope(a):
        rot = jnp.where(first_half, pltpu.roll(a, d - HEAD_DIM // 2, 1), pltpu.roll(a, HEAD_DIM // 2, 1))
        return a * cos + rot * sin

    q = rope(qk[:, :d])
    k = rope(qk[:, d:])
    qt_ref[0] = q.T
    k_ref[0] = k.astype(BF16)
    vt_ref[0] = v.T.astype(BF16)
    nb = tm // MOBA_BLOCK
    km_ref[0, 0] = jnp.mean(k.reshape(nb, MOBA_BLOCK, d), axis=1)


def _qkv_rope(x, g, w_qkv, tm):
    b, s, d = x.shape
    half = HEAD_DIM // 2
    inv = ROPE_THETA ** (-jnp.arange(half, dtype=F32) / half)
    ang = jnp.arange(s).astype(F32)[:, None] * inv[None, :]
    cos, sin = jnp.cos(ang), jnp.sin(ang)
    cos128 = jnp.tile(jnp.concatenate([cos, cos], axis=1), (1, LANES // HEAD_DIM))
    sin128 = jnp.tile(jnp.concatenate([-sin, sin], axis=1), (1, LANES // HEAD_DIM))
    wqk_hi, wqk_lo = _split(w_qkv[:, :2 * d])
    wv = w_qkv[:, 2 * d:].astype(BF16)
    nb = tm // MOBA_BLOCK
    const = lambda i, j: (0, 0)
    qt, k, vt, km = pl.pallas_call(
        _qkv_kernel,
        grid=(b, s // tm),
        in_specs=[
            pl.BlockSpec((1, tm, d), lambda i, j: (i, j, 0)),
            pl.BlockSpec((1, d), const),
            pl.BlockSpec((d, 2 * d), const),
            pl.BlockSpec((d, 2 * d), const),
            pl.BlockSpec((d, d), const),
            pl.BlockSpec((tm, LANES), lambda i, j: (j, 0)),
            pl.BlockSpec((tm, LANES), lambda i, j: (j, 0)),
        ],
        out_specs=[
            pl.BlockSpec((1, d, tm), lambda i, j: (i, 0, j)),
            pl.BlockSpec((1, tm, d), lambda i, j: (i, j, 0)),
            pl.BlockSpec((1, d, tm), lambda i, j: (i, 0, j)),
            pl.BlockSpec((1, 1, nb, d), lambda i, j: (i, j, 0, 0)),
        ],
        out_shape=[
            jax.ShapeDtypeStruct((b, d, s), F32),
            jax.ShapeDtypeStruct((b, s, d), BF16),
            jax.ShapeDtypeStruct((b, d, s), BF16),
            jax.ShapeDtypeStruct((b, s // tm, nb, d), F32),
        ],
        compiler_params=_cparams(("arbitrary", "arbitrary")),
        name="qkv_rope",
    )(x, g.reshape(1, d), wqk_hi, wqk_lo, wv, cos128, sin128)
    return qt, k, vt, km.reshape(b, s // MOBA_BLOCK, d)


def _moba_kernel(qt_ref, k_ref, vt_ref, km_ref, o_ref, s_ref):
    bs = MOBA_BLOCK
    nb = km_ref.shape[1]
    n_heads = LANES // HEAD_DIM
    j = pl.program_id(2)
    qt = qt_ref[0]
    km = km_ref[0]
    scale = HEAD_DIM ** -0.5
    lane_km = lax.broadcasted_iota(I32, (nb, LANES), 1)
    blk = lax.broadcasted_iota(I32, (nb, bs), 0)
    zeros_pad = jnp.zeros((LANES - HEAD_DIM - nb, bs), F32)
    qh, ql = _split(qt)

    def k_aug(kn, head_lo, marker_lane):
        lane = lax.broadcasted_iota(I32, kn.shape, 1)
        in_head = (lane >= head_lo) & (lane < head_lo + HEAD_DIM)
        marker = jnp.where(lane == marker_lane, 1.0, 0.0).astype(BF16)
        return jnp.where(in_head, kn, marker)

    k_own = k_ref[0, pl.ds(pl.multiple_of(j * bs, bs), bs), :]
    v_own = vt_ref[0, :, pl.ds(pl.multiple_of(j * bs, bs), bs)]
    krow = lax.broadcasted_iota(I32, (bs, bs), 0)
    qcol = lax.broadcasted_iota(I32, (bs, bs), 1)

    q_augs, state = [], []
    for hh in range(n_heads):
        head_lo = hh * HEAD_DIM
        in_head_km = (lane_km >= head_lo) & (lane_km < head_lo + HEAD_DIM)
        kmh, kml = _split(jnp.where(in_head_km, km, 0.0))
        gate = _dot3(kmh, kml, qh, ql)
        valid = blk < j
        gate = jnp.where(valid, gate, -jnp.inf)
        sel = jnp.zeros((nb, bs), F32)
        for _ in range(MOBA_TOPK):
            m = jnp.max(gate, axis=0, keepdims=True)
            i = jnp.min(jnp.where(gate == m, blk, nb), axis=0, keepdims=True)
            pick = blk == i
            sel = jnp.where(pick, 1.0, sel)
            gate = jnp.where(pick, -jnp.inf, gate)
        bias_t = jnp.where((sel > 0.0) & valid, 0.0, MASK_NEG)
        q_head = qt[head_lo:head_lo + HEAD_DIM, :] * scale
        if hh == 0:
            q_aug = jnp.concatenate([q_head, bias_t, zeros_pad], axis=0)
        else:
            q_aug = jnp.concatenate([bias_t, zeros_pad, q_head], axis=0)
        q_aug = q_aug.astype(BF16)
        q_augs.append(q_aug)

        s_own = jnp.where(krow <= qcol, _dot(k_aug(k_own, head_lo, -1), q_aug), -1e30)
        m0 = jnp.max(s_own, axis=0, keepdims=True)
        p0 = jnp.exp(s_own - m0)
        state += [m0, jnp.sum(p0, axis=0, keepdims=True), _dot(v_own, p0.astype(BF16))]

    chunk = KV_CHUNK * bs
    row_blk = lax.broadcasted_iota(I32, (chunk, LANES), 0) // bs

    def score_chunk(c, slot):
        kn = k_ref[0, pl.ds(pl.multiple_of(c * chunk, chunk), chunk), :]
        for hh in range(n_heads):
            head_lo = hh * HEAD_DIM
            bias_lo = HEAD_DIM - head_lo
            s_ref[slot, hh] = _dot(k_aug(kn, head_lo, bias_lo + c * KV_CHUNK + row_blk), q_augs[hh])

    score_chunk(0, 0)

    def body(c, state):
        s_cur = [s_ref[c % 2, hh] for hh in range(n_heads)]
        score_chunk(jnp.minimum(c + 1, nb // KV_CHUNK - 1), (c + 1) % 2)
        vn = vt_ref[0, :, pl.ds(pl.multiple_of(c * chunk, chunk), chunk)]
        new_state = []
        for hh in range(n_heads):
            m, l, acc = state[3 * hh:3 * hh + 3]
            s = s_cur[hh]
            m_new = jnp.maximum(m, jnp.max(s, axis=0, keepdims=True))
            alpha = jnp.exp(m - m_new)
            p = jnp.exp(s - m_new)
            l = alpha * l + jnp.sum(p, axis=0, keepdims=True)
            acc = alpha * acc + _dot(vn, p.astype(BF16))
            new_state += [m_new, l, acc]
        return tuple(new_state)

    state = lax.fori_loop(0, (j + KV_CHUNK - 1) // KV_CHUNK, body, tuple(state))
    halves = []
    for hh in range(n_heads):
        _, l, acc = state[3 * hh:3 * hh + 3]
        halves.append((acc / l)[hh * HEAD_DIM:(hh + 1) * HEAD_DIM, :])
    o_ref[0] = jnp.concatenate(halves, axis=0).T


def _moba_attention(qt, k, vt, km):
    b, d, s = qt.shape
    nb = s // MOBA_BLOCK
    assert nb % KV_CHUNK == 0
    return pl.pallas_call(
        _moba_kernel,
        grid=(b, d // LANES, nb),
        in_specs=[
            pl.BlockSpec((1, LANES, MOBA_BLOCK), lambda i, h, j: (i, h, j)),
            pl.BlockSpec((1, s, LANES), lambda i, h, j: (i, 0, h)),
            pl.BlockSpec((1, LANES, s), lambda i, h, j: (i, h, 0)),
            pl.BlockSpec((1, nb, LANES), lambda i, h, j: (i, 0, h)),
        ],
        out_specs=pl.BlockSpec((1, MOBA_BLOCK, LANES), lambda i, h, j: (i, j, h)),
        out_shape=jax.ShapeDtypeStruct((b, s, d), F32),
        scratch_shapes=[pltpu.VMEM((2, LANES // HEAD_DIM, KV_CHUNK * MOBA_BLOCK, MOBA_BLOCK), F32)],
        compiler_params=_cparams(("arbitrary", "arbitrary", "arbitrary")),
        name="moba_attention",
    )(qt, k, vt, km)


def _proj_residual_kernel(x_ref, a_ref, w_ref, o_ref):
    o_ref[...] = x_ref[...] + _dot(a_ref[...].astype(BF16), w_ref[...])


def _proj_residual(x2d, a2d, w, tm):
    t, d = x2d.shape
    return pl.pallas_call(
        _proj_residual_kernel,
        grid=(t // tm,),
        in_specs=[
            pl.BlockSpec((tm, d), lambda i: (i, 0)),
            pl.BlockSpec((tm, a2d.shape[1]), lambda i: (i, 0)),
            pl.BlockSpec(w.shape, lambda i: (0, 0)),
        ],
        out_specs=pl.BlockSpec((tm, d), lambda i: (i, 0)),
        out_shape=jax.ShapeDtypeStruct((t, d), F32),
        compiler_params=_cparams(("arbitrary",)),
        name="attn_out_proj",
    )(x2d, a2d, w.astype(BF16))


def _moba_mixer(x, g, w_qkv, w_o):
    b, s, d = x.shape
    tm = min(TOKEN_TILE, s)
    qt, k, vt, km = _qkv_rope(x, g, w_qkv, tm)
    attn = _moba_attention(qt, k, vt, km)
    return _proj_residual(x.reshape(b * s, d), attn.reshape(b * s, d), w_o, tm).reshape(b, s, d)


def kernel(x, norm_mix, norm_ffn, conv_w_in, conv_w, conv_w_out, attn_w_qkv, attn_w_o,
           peer_w_q, peer_k1, peer_k2, peer_u, peer_v, norm_final):
    depth = norm_mix.shape[0]
    tm = min(TOKEN_TILE, x.shape[1])
    for i in range(depth):
        j = i // 2
        if i % 2 == 0:
            x = _conv_mixer(x, norm_mix[i], conv_w_in[j], conv_w[j], conv_w_out[j], tm)
        else:
            x = _moba_mixer(x, norm_mix[i], attn_w_qkv[j], attn_w_o[j])
        x = _peer_ffn(x, norm_ffn[i], peer_w_q[i], peer_k1[i], peer_k2[i], peer_u[i], peer_v[i],
                      norm_final, final_norm=(i == depth - 1))
    return x
```

```python
import functools

import jax
import jax.numpy as jnp
from jax import lax
from jax.experimental import pallas as pl
from jax.experimental.pallas import tpu as pltpu

F32 = jnp.float32
BF16 = jnp.bfloat16
I32 = jnp.int32

RMS_EPS = 1e-6
N_HEADS = 16
HEAD_DIM = 64
MOBA_BLOCK = 256
MOBA_TOPK = 3
ROPE_THETA = 10000.0
PEER_HEADS = 8
PEER_NKEYS = 128
PEER_HALF = 64
PEER_TOPK = 16

LANES = 128
SUBLANES = 8
VMEM_LIMIT = 56 * 1024 * 1024
MASK_NEG = -1e9

TOKEN_TILE = 512
PEER_TILE = 128
OCTET = 8
KV_CHUNK = 2


def _cparams(sem):
    return pltpu.CompilerParams(dimension_semantics=sem, vmem_limit_bytes=VMEM_LIMIT)


def _rms(x, g):
    ms = jnp.mean(x * x, axis=-1, keepdims=True)
    return x * lax.rsqrt(ms + RMS_EPS) * g


def _split(a):
    hi = a.astype(BF16)
    lo = (a - hi.astype(F32)).astype(BF16)
    return hi, lo


def _dot(a, b):
    return lax.dot_general(a, b, (((1,), (0,)), ((), ())), preferred_element_type=F32)


def _dot_t(a, b):
    return lax.dot_general(a, b, (((1,), (1,)), ((), ())), preferred_element_type=F32)


def _dot3(a_hi, a_lo, b_hi, b_lo):
    return _dot(a_hi, b_hi) + _dot(a_lo, b_hi) + _dot(a_hi, b_lo)


def _dot3_t(a_hi, a_lo, b_hi, b_lo):
    return _dot_t(a_hi, b_hi) + _dot_t(a_lo, b_hi) + _dot_t(a_hi, b_lo)


def _conv_mixer_kernel(x_ref, g_ref, win_ref, cw_ref, wout_ref, o_ref, ubuf_ref):
    tm, d = x_ref.shape[1], x_ref.shape[2]

    @pl.when(pl.program_id(1) == 0)
    def _():
        ubuf_ref[0:SUBLANES, :] = jnp.zeros((SUBLANES, d), F32)

    x = x_ref[0]
    hn = _rms(x, g_ref[...]).astype(BF16)
    bcz = _dot(hn, win_ref[...])
    b_gate, c_gate, z = bcz[:, :d], bcz[:, d:2 * d], bcz[:, 2 * d:]
    u = c_gate * z
    ubuf_ref[SUBLANES:SUBLANES + tm, :] = u
    u1 = ubuf_ref[SUBLANES - 1:SUBLANES - 1 + tm, :]
    u2 = ubuf_ref[SUBLANES - 2:SUBLANES - 2 + tm, :]
    cw = cw_ref[...]
    u_conv = cw[0:1, :] * u2 + cw[1:2, :] * u1 + cw[2:3, :] * u
    ubuf_ref[0:SUBLANES, :] = u[tm - SUBLANES:tm, :]
    y = (b_gate * u_conv).astype(BF16)
    o_ref[0] = x + _dot(y, wout_ref[...])


def _conv_mixer(x, g, w_in, conv_w, w_out, tm):
    b, s, d = x.shape
    return pl.pallas_call(
        _conv_mixer_kernel,
        grid=(b, s // tm),
        in_specs=[
            pl.BlockSpec((1, tm, d), lambda i, j: (i, j, 0)),
            pl.BlockSpec((1, d), lambda i, j: (0, 0)),
            pl.BlockSpec((d, 3 * d), lambda i, j: (0, 0)),
            pl.BlockSpec((3, d), lambda i, j: (0, 0)),
            pl.BlockSpec((d, d), lambda i, j: (0, 0)),
        ],
        out_specs=pl.BlockSpec((1, tm, d), lambda i, j: (i, j, 0)),
        out_shape=jax.ShapeDtypeStruct((b, s, d), F32),
        scratch_shapes=[pltpu.VMEM((tm + SUBLANES, d), F32)],
        compiler_params=_cparams(("arbitrary", "arbitrary")),
        name="conv_mixer",
    )(x, g.reshape(1, d), w_in.astype(BF16), conv_w, w_out.astype(BF16))


def _topk_rows(s, k, order=None, payload=None):
    if order is None:
        order = lax.broadcasted_iota(I32, s.shape, 0)
    big = jnp.iinfo(jnp.int32).max
    vals, outs = [], []
    for _ in range(k):
        m = jnp.max(s, axis=0, keepdims=True)
        i = jnp.min(jnp.where(s == m, order, big), axis=0, keepdims=True)
        pick = order == i
        vals.append(m)
        if payload is None:
            outs.append(i)
        else:
            outs.append(jnp.max(jnp.where(pick, payload, -1), axis=0, keepdims=True))
        s = jnp.where(pick, -jnp.inf, s)
    return jnp.concatenate(vals, axis=0), jnp.concatenate(outs, axis=0)


def _staircase(kk):
    groups = []
    for a in range(2):
        for b0 in range(0, kk // (a + 1), SUBLANES):
            groups.append((a, 0, b0, 1, lambda j, a=a, b0=b0: (a + 1) * (b0 + j + 1) <= kk))
    for b in range(kk // 3):
        for a0 in range(0, kk // (b + 1), SUBLANES):
            groups.append((a0, 1, b, 0, lambda j, a0=a0, b=b: (a0 + j >= 2) & ((a0 + j + 1) * (b + 1) <= kk)))
    return groups


def _staircase_topk(v1, i1, v2, i2, nk):
    kk, t = v1.shape
    j = lax.broadcasted_iota(I32, (SUBLANES, t), 0)

    def rows(x, x0, step):
        if step == 0:
            return jnp.broadcast_to(x[x0:x0 + 1, :], (SUBLANES, t))
        return x[x0:x0 + SUBLANES, :]

    cand, order, cidx = [], [], []
    for a0, a_step, b0, b_step, valid in _staircase(kk):
        ok = valid(j)
        cand.append(jnp.where(ok, rows(v1, a0, a_step) + rows(v2, b0, b_step), -jnp.inf))
        order.append(jnp.where(ok, (a0 + j * a_step) * kk + (b0 + j * b_step), jnp.iinfo(jnp.int32).max - 1))
        cidx.append(rows(i1, a0, a_step) * nk + rows(i2, b0, b_step))
    cat = lambda xs: jnp.concatenate(xs, axis=0)
    return _topk_rows(cat(cand), kk, order=cat(order), payload=cat(cidx))


def _router_kernel(x_ref, g_ref, wq_hi_ref, wq_lo_ref, k1_hi_ref, k1_lo_ref, k2_hi_ref, k2_lo_ref,
                   e_ref, gate_ref):
    table_rows = x_ref.shape[1] // (2 * LANES)
    xn = _rms(x_ref[...], g_ref[...])
    xh, xl = _split(xn)
    q = _dot3(xh, xl, wq_hi_ref[...], wq_lo_ref[...])
    nk = k1_hi_ref.shape[0]
    for h in range(PEER_HEADS):
        qh, ql = _split(q[:, h * LANES:(h + 1) * LANES])
        s1 = _dot3_t(k1_hi_ref[...], k1_lo_ref[...], qh, ql)
        s2 = _dot3_t(k2_hi_ref[...], k2_lo_ref[...], qh, ql)
        v1, i1 = _topk_rows(s1, PEER_TOPK)
        v2, i2 = _topk_rows(s2, PEER_TOPK)
        kk = PEER_TOPK
        top_s, e_idx = _staircase_topk(v1, i1, v2, i2, nk)
        p = jnp.exp(top_s - top_s[0:1, :])
        gate = p / jnp.sum(p, axis=0, keepdims=True)
        e_ref[0, h * kk:(h + 1) * kk, :] = e_idx * table_rows
        gate_ref[0, h * kk:(h + 1) * kk, :] = gate


def _pad_keys(k, lo):
    nk, half = k.shape
    out = jnp.zeros((nk, LANES), F32)
    return out.at[:, lo:lo + half].set(k)


def _peer_router(x2d, g, w_q, k1, k2):
    t, d = x2d.shape
    nt = t // PEER_TILE
    slots = PEER_HEADS * PEER_TOPK
    wq_hi, wq_lo = _split(w_q)
    k1_hi, k1_lo = _split(_pad_keys(k1, 0))
    k2_hi, k2_lo = _split(_pad_keys(k2, PEER_HALF))
    const = lambda i: (0, 0)
    kspec = pl.BlockSpec(k1_hi.shape, const)
    return pl.pallas_call(
        _router_kernel,
        grid=(nt,),
        in_specs=[
            pl.BlockSpec((PEER_TILE, d), lambda i: (i, 0)),
            pl.BlockSpec((1, d), const),
            pl.BlockSpec(wq_hi.shape, const),
            pl.BlockSpec(wq_lo.shape, const),
            kspec, kspec, kspec, kspec,
        ],
        out_specs=[
            pl.BlockSpec((1, slots, PEER_TILE), lambda i: (i, 0, 0)),
            pl.BlockSpec((1, slots, PEER_TILE), lambda i: (i, 0, 0)),
        ],
        out_shape=[
            jax.ShapeDtypeStruct((nt, slots, PEER_TILE), I32),
            jax.ShapeDtypeStruct((nt, slots, PEER_TILE), F32),
        ],
        compiler_params=_cparams(("arbitrary",)),
        name="peer_router",
    )(x2d, g.reshape(1, d), wq_hi, wq_lo, k1_hi, k1_lo, k2_hi, k2_lo)


def _gather_octet(e_ref, tab_ref, stage_ref, octet, nsub):
    t = e_ref.shape[2]
    for k in range(OCTET):
        for tok in range(t):
            off = e_ref[0, octet * OCTET + k, tok]
            stage_ref[k * (t // SUBLANES) + tok // SUBLANES,
                      pl.ds(tok % SUBLANES, nsub, stride=SUBLANES), :] = tab_ref[pl.ds(off, nsub), :]


def _octet_pipeline(n_octets, gather, consume, stage_a, stage_b):
    gather(0, stage_a)

    def pair(p, carry):
        gather(2 * p + 1, stage_b)
        consume(2 * p, stage_a)
        gather(jnp.minimum(2 * p + 2, n_octets - 1), stage_a)
        consume(2 * p + 1, stage_b)
        return carry
    lax.fori_loop(0, n_octets // 2, pair, 0)


def _score_kernel(e_ref, x_ref, g_ref, gate_ref, tab_ref, w_ref, stage_a, stage_b, a_ref):
    t, d = x_ref.shape
    nchunk = d // (2 * LANES)
    slots = e_ref.shape[1]
    rows = OCTET * t

    xn = _rms(x_ref[...], g_ref[...]).astype(BF16)
    x_even = jnp.concatenate([xn[:, (2 * c) * LANES:(2 * c + 1) * LANES] for c in range(nchunk)], axis=1)
    x_odd = jnp.concatenate([xn[:, (2 * c + 1) * LANES:(2 * c + 2) * LANES] for c in range(nchunk)], axis=1)
    rhs = jnp.concatenate([x_even, x_odd], axis=0)

    ri = lax.broadcasted_iota(I32, (2 * t, 2 * t), 0)
    ci = lax.broadcasted_iota(I32, (2 * t, 2 * t), 1)
    diag = (ci == (ri % 2) * t + ri // 2).astype(F32)

    def consume(o, stage_ref):
        planes = []
        for c in range(nchunk):
            plane = stage_ref[:, c * SUBLANES:(c + 1) * SUBLANES, :].reshape(rows, LANES)
            planes.append(pltpu.bitcast(plane, BF16))
        lhs = jnp.concatenate(planes, axis=1)
        prod = _dot_t(lhs, rhs)
        prod = prod.reshape(OCTET, 2 * t, 2 * t) * diag[None]
        a2 = jnp.sum(prod, axis=1)
        a_ref[pl.ds(pl.multiple_of(o * OCTET, OCTET), OCTET), :] = a2[:, :t] + a2[:, t:]

    gather = lambda o, stage_ref: _gather_octet(e_ref, tab_ref, stage_ref, o, nchunk)
    _octet_pipeline(slots // OCTET, gather, consume, stage_a, stage_b)

    a = a_ref[...]
    gelu = 0.5 * a * (1.0 + lax.erf(a * (2.0 ** -0.5)))
    w_ref[0] = (gate_ref[0] * gelu).T


def _peer_scores(e_t, gate_t, x2d, g, table):
    nt, slots, t = e_t.shape
    d = x2d.shape[1]
    stage = pltpu.VMEM((OCTET * t // SUBLANES, d // (2 * LANES) * SUBLANES, LANES), I32)
    return pl.pallas_call(
        _score_kernel,
        grid=(nt,),
        in_specs=[
            pl.BlockSpec((1, slots, t), lambda i: (i, 0, 0), memory_space=pltpu.SMEM),
            pl.BlockSpec((t, d), lambda i: (i, 0)),
            pl.BlockSpec((1, d), lambda i: (0, 0)),
            pl.BlockSpec((1, slots, t), lambda i: (i, 0, 0)),
            pl.BlockSpec(table.shape, lambda i: (0, 0), pipeline_mode=pl.Buffered(1)),
        ],
        out_specs=pl.BlockSpec((1, t, slots), lambda i: (i, 0, 0)),
        out_shape=jax.ShapeDtypeStruct((nt, t, slots), F32),
        scratch_shapes=[stage, stage, pltpu.VMEM((slots, t), F32)],
        compiler_params=_cparams(("arbitrary",)),
        name="peer_scores",
    )(e_t, x2d, g.reshape(1, d), gate_t, table)


def _value_kernel(e_ref, w_ref, x_ref, gf_ref, tab_ref, o_ref, stage_a, stage_b, wb_ref, acc_ref, *, final_norm):
    t, d = x_ref.shape
    nchunk = d // (2 * LANES)
    slots = e_ref.shape[1]
    groups = t // SUBLANES

    acc_ref[...] = x_ref[...]

    def consume(o, stage_ref):
        w_oct = pltpu.roll(w_ref[0], lax.rem(slots - o * OCTET, slots), 1)
        for k in range(OCTET):
            wb_ref[k] = jnp.broadcast_to(w_oct[:, k:k + 1], (t, LANES))
        for c in range(nchunk):
            lo_cols = slice(2 * c * LANES, (2 * c + 1) * LANES)
            hi_cols = slice((2 * c + 1) * LANES, (2 * c + 2) * LANES)
            acc_lo, acc_hi = acc_ref[:, lo_cols], acc_ref[:, hi_cols]
            for k in range(OCTET):
                words = stage_ref[k * groups:(k + 1) * groups, c * SUBLANES:(c + 1) * SUBLANES, :]
                words = words.reshape(t, LANES)
                lo = pltpu.bitcast(words << 16, F32)
                hi = pltpu.bitcast(words & jnp.int32(-65536), F32)
                acc_lo = acc_lo + wb_ref[k] * lo
                acc_hi = acc_hi + wb_ref[k] * hi
            acc_ref[:, lo_cols] = acc_lo
            acc_ref[:, hi_cols] = acc_hi

    gather = lambda o, stage_ref: _gather_octet(e_ref, tab_ref, stage_ref, o, nchunk)
    _octet_pipeline(slots // OCTET, gather, consume, stage_a, stage_b)

    y = acc_ref[...]
    if final_norm:
        y = _rms(y, gf_ref[...])
    o_ref[...] = y


def _peer_values(e_t, w_t, x2d, table, g_final, final_norm):
    nt, slots, t = e_t.shape
    d = x2d.shape[1]
    stage = pltpu.VMEM((OCTET * t // SUBLANES, d // (2 * LANES) * SUBLANES, LANES), I32)
    return pl.pallas_call(
        functools.partial(_value_kernel, final_norm=final_norm),
        grid=(nt,),
        in_specs=[
            pl.BlockSpec((1, slots, t), lambda i: (i, 0, 0), memory_space=pltpu.SMEM),
            pl.BlockSpec((1, t, slots), lambda i: (i, 0, 0)),
            pl.BlockSpec((t, d), lambda i: (i, 0)),
            pl.BlockSpec((1, d), lambda i: (0, 0)),
            pl.BlockSpec(table.shape, lambda i: (0, 0), pipeline_mode=pl.Buffered(1)),
        ],
        out_specs=pl.BlockSpec((t, d), lambda i: (i, 0)),
        out_shape=jax.ShapeDtypeStruct(x2d.shape, F32),
        scratch_shapes=[stage, stage, pltpu.VMEM((OCTET, t, LANES), F32), pltpu.VMEM((t, d), F32)],
        compiler_params=_cparams(("arbitrary",)),
        name="peer_values",
    )(e_t, w_t, x2d, g_final.reshape(1, d), table)


def _pack_pairs(w):
    e, d = w.shape
    bits = lax.bitcast_convert_type(w.astype(BF16), jnp.uint16).astype(jnp.uint32)
    bits = bits.reshape(e, d // (2 * LANES), 2, LANES)
    packed = bits[:, :, 0, :] | (bits[:, :, 1, :] << 16)
    return lax.bitcast_convert_type(packed, I32).reshape(e * d // (2 * LANES), LANES)


def _peer_ffn(x, g, w_q, k1, k2, u_emb, v_emb, g_final, final_norm):
    b, s, d = x.shape
    x2d = x.reshape(b * s, d)
    e_slot, gate_slot = _peer_router(x2d, g, w_q, k1, k2)
    w_tok = _peer_scores(e_slot, gate_slot, x2d, g, _pack_pairs(u_emb))
    out = _peer_values(e_slot, w_tok, x2d, _pack_pairs(v_emb), g_final, final_norm)
    return out.reshape(b, s, d)


def _qkv_kernel(x_ref, g_ref, wqk_hi_ref, wqk_lo_ref, wv_ref, cos_ref, sin_ref,
                qt_ref, k0_ref, k1_ref, vt_ref, km_ref):
    tm, d = x_ref.shape[1], x_ref.shape[2]
    hn = _rms(x_ref[0], g_ref[...])
    hh, hl = _split(hn)
    qk = _dot3(hh, hl, wqk_hi_ref[...], wqk_lo_ref[...])
    v = _dot(hh, wv_ref[...])
    cos = jnp.concatenate([cos_ref[...]] * (d // LANES), axis=1)
    sin = jnp.concatenate([sin_ref[...]] * (d // LANES), axis=1)
    lane = lax.broadcasted_iota(I32, (tm, d), 1)
    first_half = (lane % HEAD_DIM) < (HEAD_DIM // 2)

    def rope(a):
        rot = jnp.where(first_half, pltpu.roll(a, d - HEAD_DIM // 2, 1), pltpu.roll(a, HEAD_DIM // 2, 1))
        return a * cos + rot * sin

    q = rope(qk[:, :d])
    k = rope(qk[:, d:])
    qt_ref[0] = q.T
    vt_ref[0] = v.T.astype(BF16)
    nb = tm // MOBA_BLOCK
    km_ref[0, 0] = jnp.mean(k.reshape(nb, MOBA_BLOCK, d), axis=1)
    row = lax.broadcasted_iota(I32, (tm, d), 0)
    block = (pl.program_id(1) * tm + row) // MOBA_BLOCK
    pair_lane = lane % LANES
    kb = k.astype(BF16)
    k0_ref[0] = jnp.where(pair_lane < HEAD_DIM, kb, jnp.where(pair_lane - HEAD_DIM == block, 1.0, 0.0).astype(BF16))
    k1_ref[0] = jnp.where(pair_lane >= HEAD_DIM, kb, jnp.where(pair_lane == block, 1.0, 0.0).astype(BF16))


def _qkv_rope(x, g, w_qkv, tm):
    b, s, d = x.shape
    half = HEAD_DIM // 2
    inv = ROPE_THETA ** (-jnp.arange(half, dtype=F32) / half)
    ang = jnp.arange(s).astype(F32)[:, None] * inv[None, :]
    cos, sin = jnp.cos(ang), jnp.sin(ang)
    cos128 = jnp.tile(jnp.concatenate([cos, cos], axis=1), (1, LANES // HEAD_DIM))
    sin128 = jnp.tile(jnp.concatenate([-sin, sin], axis=1), (1, LANES // HEAD_DIM))
    wqk_hi, wqk_lo = _split(w_qkv[:, :2 * d])
    wv = w_qkv[:, 2 * d:].astype(BF16)
    nb = tm // MOBA_BLOCK
    const = lambda i, j: (0, 0)
    qt, k0, k1, vt, km = pl.pallas_call(
        _qkv_kernel,
        grid=(b, s // tm),
        in_specs=[
            pl.BlockSpec((1, tm, d), lambda i, j: (i, j, 0)),
            pl.BlockSpec((1, d), const),
            pl.BlockSpec((d, 2 * d), const),
            pl.BlockSpec((d, 2 * d), const),
            pl.BlockSpec((d, d), const),
            pl.BlockSpec((tm, LANES), lambda i, j: (j, 0)),
            pl.BlockSpec((tm, LANES), lambda i, j: (j, 0)),
        ],
        out_specs=[
            pl.BlockSpec((1, d, tm), lambda i, j: (i, 0, j)),
            pl.BlockSpec((1, tm, d), lambda i, j: (i, j, 0)),
            pl.BlockSpec((1, tm, d), lambda i, j: (i, j, 0)),
            pl.BlockSpec((1, d, tm), lambda i, j: (i, 0, j)),
            pl.BlockSpec((1, 1, nb, d), lambda i, j: (i, j, 0, 0)),
        ],
        out_shape=[
            jax.ShapeDtypeStruct((b, d, s), F32),
            jax.ShapeDtypeStruct((b, s, d), BF16),
            jax.ShapeDtypeStruct((b, s, d), BF16),
            jax.ShapeDtypeStruct((b, d, s), BF16),
            jax.ShapeDtypeStruct((b, s // tm, nb, d), F32),
        ],
        compiler_params=_cparams(("arbitrary", "arbitrary")),
        name="qkv_rope",
    )(x, g.reshape(1, d), wqk_hi, wqk_lo, wv, cos128, sin128)
    return qt, (k0, k1), vt, km.reshape(b, s // MOBA_BLOCK, d)


def _moba_kernel(qt_ref, k0_ref, k1_ref, vt_ref, km_ref, o_ref, sa_ref, sb_ref):
    bs = MOBA_BLOCK
    nb = km_ref.shape[1]
    n_heads = LANES // HEAD_DIM
    k_refs = (k0_ref, k1_ref)
    j = pl.program_id(2)
    qt = qt_ref[0]
    km = km_ref[0]
    scale = HEAD_DIM ** -0.5 * 1.4426950408889634
    lane_km = lax.broadcasted_iota(I32, (nb, LANES), 1)
    blk = lax.broadcasted_iota(I32, (nb, bs), 0)
    zeros_pad = jnp.zeros((LANES - HEAD_DIM - nb, bs), F32)
    qh, ql = _split(qt)

    own = pl.ds(pl.multiple_of(j * bs, bs), bs)
    v_own = vt_ref[0, :, own]
    krow = lax.broadcasted_iota(I32, (bs, bs), 0)
    qcol = lax.broadcasted_iota(I32, (bs, bs), 1)

    q_augs, state = [], []
    for hh in range(n_heads):
        head_lo = hh * HEAD_DIM
        in_head_km = (lane_km >= head_lo) & (lane_km < head_lo + HEAD_DIM)
        kmh, kml = _split(jnp.where(in_head_km, km, 0.0))
        gate = _dot3(kmh, kml, qh, ql)
        valid = blk < j
        gate = jnp.where(valid, gate, -jnp.inf)
        sel = jnp.zeros((nb, bs), F32)
        for _ in range(MOBA_TOPK):
            m = jnp.max(gate, axis=0, keepdims=True)
            i = jnp.min(jnp.where(gate == m, blk, nb), axis=0, keepdims=True)
            pick = blk == i
            sel = jnp.where(pick, 1.0, sel)
            gate = jnp.where(pick, -jnp.inf, gate)
        bias_t = jnp.where((sel > 0.0) & valid, 0.0, MASK_NEG)
        q_head = qt[head_lo:head_lo + HEAD_DIM, :] * scale
        no_bias = jnp.zeros((LANES - HEAD_DIM, bs), F32)
        if hh == 0:
            q_aug = jnp.concatenate([q_head, bias_t, zeros_pad], axis=0)
            q_own = jnp.concatenate([q_head, no_bias], axis=0)
        else:
            q_aug = jnp.concatenate([bias_t, zeros_pad, q_head], axis=0)
            q_own = jnp.concatenate([no_bias, q_head], axis=0)
        q_augs.append(q_aug.astype(BF16))

        s_own = jnp.where(krow <= qcol, _dot(k_refs[hh][0, own, :], q_own.astype(BF16)), -1e30)
        m0 = jnp.max(s_own, axis=0, keepdims=True)
        p0 = jnp.exp2(s_own - m0)
        state += [m0, jnp.sum(p0, axis=0, keepdims=True), _dot(v_own, p0.astype(BF16))]

    chunk = KV_CHUNK * bs
    last_chunk = nb // KV_CHUNK - 1

    def score_chunk(c, s_ref):
        rows = pl.ds(pl.multiple_of(jnp.minimum(c, last_chunk) * chunk, chunk), chunk)
        for hh in range(n_heads):
            s_ref[hh] = _dot(k_refs[hh][0, rows, :], q_augs[hh])

    def attend(c, s_ref, state):
        vn = vt_ref[0, :, pl.ds(pl.multiple_of(c * chunk, chunk), chunk)]
        new_state = []
        for hh in range(n_heads):
            m, l, acc = state[3 * hh:3 * hh + 3]
            s = s_ref[hh]
            m_new = jnp.maximum(m, jnp.max(s, axis=0, keepdims=True))
            alpha = jnp.exp2(m - m_new)
            p = jnp.exp2(s - m_new)
            l = alpha * l + jnp.sum(p, axis=0, keepdims=True)
            acc = alpha * acc + _dot(vn, p.astype(BF16))
            new_state += [m_new, l, acc]
        return tuple(new_state)

    score_chunk(0, sa_ref)

    def body(i, state):
        score_chunk(2 * i + 1, sb_ref)
        state = attend(2 * i, sa_ref, state)
        score_chunk(2 * i + 2, sa_ref)
        return attend(2 * i + 1, sb_ref, state)

    state = lax.fori_loop(0, (j + 2 * KV_CHUNK - 1) // (2 * KV_CHUNK), body, tuple(state))
    halves = []
    for hh in range(n_heads):
        _, l, acc = state[3 * hh:3 * hh + 3]
        halves.append((acc / l)[hh * HEAD_DIM:(hh + 1) * HEAD_DIM, :])
    o_ref[0] = jnp.concatenate(halves, axis=0).T


def _moba_attention(qt, k01, vt, km):
    b, d, s = qt.shape
    nb = s // MOBA_BLOCK
    assert nb % (2 * KV_CHUNK) == 0
    return pl.pallas_call(
        _moba_kernel,
        grid=(b, d // LANES, nb),
        in_specs=[
            pl.BlockSpec((1, LANES, MOBA_BLOCK), lambda i, h, j: (i, h, j)),
            pl.BlockSpec((1, s, LANES), lambda i, h, j: (i, 0, h)),
            pl.BlockSpec((1, s, LANES), lambda i, h, j: (i, 0, h)),
            pl.BlockSpec((1, LANES, s), lambda i, h, j: (i, h, 0)),
            pl.BlockSpec((1, nb, LANES), lambda i, h, j: (i, 0, h)),
        ],
        out_specs=pl.BlockSpec((1, MOBA_BLOCK, LANES), lambda i, h, j: (i, j, h)),
        out_shape=jax.ShapeDtypeStruct((b, s, d), F32),
        scratch_shapes=[pltpu.VMEM((LANES // HEAD_DIM, KV_CHUNK * MOBA_BLOCK, MOBA_BLOCK), F32)] * 2,
        compiler_params=_cparams(("arbitrary", "arbitrary", "arbitrary")),
        name="moba_attention",
    )(qt, *k01, vt, km)


def _proj_residual_kernel(x_ref, a_ref, w_ref, o_ref):
    o_ref[...] = x_ref[...] + _dot(a_ref[...].astype(BF16), w_ref[...])


def _proj_residual(x2d, a2d, w, tm):
    t, d = x2d.shape
    return pl.pallas_call(
        _proj_residual_kernel,
        grid=(t // tm,),
        in_specs=[
            pl.BlockSpec((tm, d), lambda i: (i, 0)),
            pl.BlockSpec((tm, a2d.shape[1]), lambda i: (i, 0)),
            pl.BlockSpec(w.shape, lambda i: (0, 0)),
        ],
        out_specs=pl.BlockSpec((tm, d), lambda i: (i, 0)),
        out_shape=jax.ShapeDtypeStruct((t, d), F32),
        compiler_params=_cparams(("arbitrary",)),
        name="attn_out_proj",
    )(x2d, a2d, w.astype(BF16))


def _moba_mixer(x, g, w_qkv, w_o):
    b, s, d = x.shape
    tm = min(TOKEN_TILE, s)
    qt, k, vt, km = _qkv_rope(x, g, w_qkv, tm)
    attn = _moba_attention(qt, k, vt, km)
    return _proj_residual(x.reshape(b * s, d), attn.reshape(b * s, d), w_o, tm).reshape(b, s, d)


def kernel(x, norm_mix, norm_ffn, conv_w_in, conv_w, conv_w_out, attn_w_qkv, attn_w_o,
           peer_w_q, peer_k1, peer_k2, peer_u, peer_v, norm_final):
    depth = norm_mix.shape[0]
    tm = min(TOKEN_TILE, x.shape[1])
    for i in range(depth):
        j = i // 2
        if i % 2 == 0:
            x = _conv_mixer(x, norm_mix[i], conv_w_in[j], conv_w[j], conv_w_out[j], tm)
        else:
            x = _moba_mixer(x, norm_mix[i], attn_w_qkv[j], attn_w_o[j])
        x = _peer_ffn(x, norm_ffn[i], peer_w_q[i], peer_k1[i], peer_k2[i], peer_u[i], peer_v[i],
                      norm_final, final_norm=(i == depth - 1))
    return x
```

```python
import functools

import jax
import jax.numpy as jnp
from jax import lax
from jax.experimental import pallas as pl
from jax.experimental.pallas import tpu as pltpu
from jax.experimental.pallas import tpu_sc as plsc

F32 = jnp.float32
BF16 = jnp.bfloat16
I32 = jnp.int32

RMS_EPS = 1e-6
N_HEADS = 16
HEAD_DIM = 64
MOBA_BLOCK = 256
MOBA_TOPK = 3
ROPE_THETA = 10000.0
PEER_HEADS = 8
PEER_NKEYS = 128
PEER_HALF = 64
PEER_TOPK = 16

LANES = 128
SUBLANES = 8
VMEM_LIMIT = 56 * 1024 * 1024
MASK_NEG = -1e9

TOKEN_TILE = 512
PEER_TILE = 128
OCTET = 8
SC_CORES = 2
SC_SUBCORES = 16
SC_LANES = 16
SC_WINDOW = 32
SC_TOKEN_PERCENT = 25
KV_CHUNK = 2


def _cparams(sem):
    return pltpu.CompilerParams(dimension_semantics=sem, vmem_limit_bytes=VMEM_LIMIT)


def _rms(x, g):
    ms = jnp.mean(x * x, axis=-1, keepdims=True)
    return x * lax.rsqrt(ms + RMS_EPS) * g


def _split(a):
    hi = a.astype(BF16)
    lo = (a - hi.astype(F32)).astype(BF16)
    return hi, lo


def _dot(a, b):
    return lax.dot_general(a, b, (((1,), (0,)), ((), ())), preferred_element_type=F32)


def _dot_t(a, b):
    return lax.dot_general(a, b, (((1,), (1,)), ((), ())), preferred_element_type=F32)


def _dot3(a_hi, a_lo, b_hi, b_lo):
    return _dot(a_hi, b_hi) + _dot(a_lo, b_hi) + _dot(a_hi, b_lo)


def _dot3_t(a_hi, a_lo, b_hi, b_lo):
    return _dot_t(a_hi, b_hi) + _dot_t(a_lo, b_hi) + _dot_t(a_hi, b_lo)


def _conv_mixer_kernel(x_ref, g_ref, win_ref, cw_ref, wout_ref, o_ref, ubuf_ref):
    tm, d = x_ref.shape[1], x_ref.shape[2]

    @pl.when(pl.program_id(1) == 0)
    def _():
        ubuf_ref[0:SUBLANES, :] = jnp.zeros((SUBLANES, d), F32)

    x = x_ref[0]
    hn = _rms(x, g_ref[...]).astype(BF16)
    bcz = _dot(hn, win_ref[...])
    b_gate, c_gate, z = bcz[:, :d], bcz[:, d:2 * d], bcz[:, 2 * d:]
    u = c_gate * z
    ubuf_ref[SUBLANES:SUBLANES + tm, :] = u
    u1 = ubuf_ref[SUBLANES - 1:SUBLANES - 1 + tm, :]
    u2 = ubuf_ref[SUBLANES - 2:SUBLANES - 2 + tm, :]
    cw = cw_ref[...]
    u_conv = cw[0:1, :] * u2 + cw[1:2, :] * u1 + cw[2:3, :] * u
    ubuf_ref[0:SUBLANES, :] = u[tm - SUBLANES:tm, :]
    y = (b_gate * u_conv).astype(BF16)
    o_ref[0] = x + _dot(y, wout_ref[...])


def _conv_mixer(x, g, w_in, conv_w, w_out, tm):
    b, s, d = x.shape
    return pl.pallas_call(
        _conv_mixer_kernel,
        grid=(b, s // tm),
        in_specs=[
            pl.BlockSpec((1, tm, d), lambda i, j: (i, j, 0)),
            pl.BlockSpec((1, d), lambda i, j: (0, 0)),
            pl.BlockSpec((d, 3 * d), lambda i, j: (0, 0)),
            pl.BlockSpec((3, d), lambda i, j: (0, 0)),
            pl.BlockSpec((d, d), lambda i, j: (0, 0)),
        ],
        out_specs=pl.BlockSpec((1, tm, d), lambda i, j: (i, j, 0)),
        out_shape=jax.ShapeDtypeStruct((b, s, d), F32),
        scratch_shapes=[pltpu.VMEM((tm + SUBLANES, d), F32)],
        compiler_params=_cparams(("arbitrary", "arbitrary")),
        name="conv_mixer",
    )(x, g.reshape(1, d), w_in.astype(BF16), conv_w, w_out.astype(BF16))


def _topk_rows(s, k, order=None, payload=None):
    if order is None:
        order = lax.broadcasted_iota(I32, s.shape, 0)
    big = jnp.iinfo(jnp.int32).max
    vals, outs = [], []
    for _ in range(k):
        m = jnp.max(s, axis=0, keepdims=True)
        i = jnp.min(jnp.where(s == m, order, big), axis=0, keepdims=True)
        pick = order == i
        vals.append(m)
        if payload is None:
            outs.append(i)
        else:
            outs.append(jnp.max(jnp.where(pick, payload, -1), axis=0, keepdims=True))
        s = jnp.where(pick, -jnp.inf, s)
    return jnp.concatenate(vals, axis=0), jnp.concatenate(outs, axis=0)


def _staircase(kk):
    groups = []
    for a in range(2):
        for b0 in range(0, kk // (a + 1), SUBLANES):
            groups.append((a, 0, b0, 1, lambda j, a=a, b0=b0: (a + 1) * (b0 + j + 1) <= kk))
    for b in range(kk // 3):
        for a0 in range(0, kk // (b + 1), SUBLANES):
            groups.append((a0, 1, b, 0, lambda j, a0=a0, b=b: (a0 + j >= 2) & ((a0 + j + 1) * (b + 1) <= kk)))
    return groups


def _staircase_topk(v1, i1, v2, i2, nk):
    kk, t = v1.shape
    j = lax.broadcasted_iota(I32, (SUBLANES, t), 0)

    def rows(x, x0, step):
        if step == 0:
            return jnp.broadcast_to(x[x0:x0 + 1, :], (SUBLANES, t))
        return x[x0:x0 + SUBLANES, :]

    cand, order, cidx = [], [], []
    for a0, a_step, b0, b_step, valid in _staircase(kk):
        ok = valid(j)
        cand.append(jnp.where(ok, rows(v1, a0, a_step) + rows(v2, b0, b_step), -jnp.inf))
        order.append(jnp.where(ok, (a0 + j * a_step) * kk + (b0 + j * b_step), jnp.iinfo(jnp.int32).max - 1))
        cidx.append(rows(i1, a0, a_step) * nk + rows(i2, b0, b_step))
    cat = lambda xs: jnp.concatenate(xs, axis=0)
    return _topk_rows(cat(cand), kk, order=cat(order), payload=cat(cidx))


def _router_kernel(x_ref, g_ref, wq_hi_ref, wq_lo_ref, k1_hi_ref, k1_lo_ref, k2_hi_ref, k2_lo_ref,
                   e_ref, gate_ref):
    table_rows = x_ref.shape[1] // (2 * LANES)
    xn = _rms(x_ref[...], g_ref[...])
    xh, xl = _split(xn)
    q = _dot3(xh, xl, wq_hi_ref[...], wq_lo_ref[...])
    nk = k1_hi_ref.shape[0]
    for h in range(PEER_HEADS):
        qh, ql = _split(q[:, h * LANES:(h + 1) * LANES])
        s1 = _dot3_t(k1_hi_ref[...], k1_lo_ref[...], qh, ql)
        s2 = _dot3_t(k2_hi_ref[...], k2_lo_ref[...], qh, ql)
        v1, i1 = _topk_rows(s1, PEER_TOPK)
        v2, i2 = _topk_rows(s2, PEER_TOPK)
        kk = PEER_TOPK
        top_s, e_idx = _staircase_topk(v1, i1, v2, i2, nk)
        p = jnp.exp(top_s - top_s[0:1, :])
        gate = p / jnp.sum(p, axis=0, keepdims=True)
        e_ref[0, h * kk:(h + 1) * kk, :] = e_idx * table_rows
        gate_ref[0, h * kk:(h + 1) * kk, :] = gate


def _pad_keys(k, lo):
    nk, half = k.shape
    out = jnp.zeros((nk, LANES), F32)
    return out.at[:, lo:lo + half].set(k)


def _peer_router(x2d, g, w_q, k1, k2):
    t, d = x2d.shape
    nt = t // PEER_TILE
    slots = PEER_HEADS * PEER_TOPK
    wq_hi, wq_lo = _split(w_q)
    k1_hi, k1_lo = _split(_pad_keys(k1, 0))
    k2_hi, k2_lo = _split(_pad_keys(k2, PEER_HALF))
    const = lambda i: (0, 0)
    kspec = pl.BlockSpec(k1_hi.shape, const)
    return pl.pallas_call(
        _router_kernel,
        grid=(nt,),
        in_specs=[
            pl.BlockSpec((PEER_TILE, d), lambda i: (i, 0)),
            pl.BlockSpec((1, d), const),
            pl.BlockSpec(wq_hi.shape, const),
            pl.BlockSpec(wq_lo.shape, const),
            kspec, kspec, kspec, kspec,
        ],
        out_specs=[
            pl.BlockSpec((1, slots, PEER_TILE), lambda i: (i, 0, 0)),
            pl.BlockSpec((1, slots, PEER_TILE), lambda i: (i, 0, 0)),
        ],
        out_shape=[
            jax.ShapeDtypeStruct((nt, slots, PEER_TILE), I32),
            jax.ShapeDtypeStruct((nt, slots, PEER_TILE), F32),
        ],
        compiler_params=_cparams(("arbitrary",)),
        name="peer_router",
    )(x2d, g.reshape(1, d), wq_hi, wq_lo, k1_hi, k1_lo, k2_hi, k2_lo)


def _gather_octet(e_ref, tab_ref, stage_ref, octet, nsub):
    t = e_ref.shape[2]
    for k in range(OCTET):
        for tok in range(t):
            off = e_ref[0, octet * OCTET + k, tok]
            stage_ref[k * (t // SUBLANES) + tok // SUBLANES,
                      pl.ds(tok % SUBLANES, nsub, stride=SUBLANES), :] = tab_ref[pl.ds(off, nsub), :]


def _octet_pipeline(n_octets, gather, consume, stage_a, stage_b):
    gather(0, stage_a)

    def pair(p, carry):
        gather(2 * p + 1, stage_b)
        consume(2 * p, stage_a)
        gather(jnp.minimum(2 * p + 2, n_octets - 1), stage_a)
        consume(2 * p + 1, stage_b)
        return carry
    lax.fori_loop(0, n_octets // 2, pair, 0)


def _score_kernel(e_ref, x_ref, g_ref, gate_ref, tab_ref, w_ref, stage_a, stage_b, a_ref):
    t, d = x_ref.shape
    nchunk = d // (2 * LANES)
    slots = e_ref.shape[1]
    rows = OCTET * t

    xn = _rms(x_ref[...], g_ref[...]).astype(BF16)
    x_even = jnp.concatenate([xn[:, (2 * c) * LANES:(2 * c + 1) * LANES] for c in range(nchunk)], axis=1)
    x_odd = jnp.concatenate([xn[:, (2 * c + 1) * LANES:(2 * c + 2) * LANES] for c in range(nchunk)], axis=1)
    rhs = jnp.concatenate([x_even, x_odd], axis=0)

    ri = lax.broadcasted_iota(I32, (2 * t, 2 * t), 0)
    ci = lax.broadcasted_iota(I32, (2 * t, 2 * t), 1)
    diag = (ci == (ri % 2) * t + ri // 2).astype(F32)

    def consume(o, stage_ref):
        planes = []
        for c in range(nchunk):
            plane = stage_ref[:, c * SUBLANES:(c + 1) * SUBLANES, :].reshape(rows, LANES)
            planes.append(pltpu.bitcast(plane, BF16))
        lhs = jnp.concatenate(planes, axis=1)
        prod = _dot_t(lhs, rhs)
        prod = prod.reshape(OCTET, 2 * t, 2 * t) * diag[None]
        a2 = jnp.sum(prod, axis=1)
        a_ref[pl.ds(pl.multiple_of(o * OCTET, OCTET), OCTET), :] = a2[:, :t] + a2[:, t:]

    gather = lambda o, stage_ref: _gather_octet(e_ref, tab_ref, stage_ref, o, nchunk)
    _octet_pipeline(slots // OCTET, gather, consume, stage_a, stage_b)

    a = a_ref[...]
    gelu = 0.5 * a * (1.0 + lax.erf(a * (2.0 ** -0.5)))
    w_ref[0] = (gate_ref[0] * gelu).T


def _peer_scores(e_t, gate_t, x2d, g, table):
    nt, slots, t = e_t.shape
    d = x2d.shape[1]
    stage = pltpu.VMEM((OCTET * t // SUBLANES, d // (2 * LANES) * SUBLANES, LANES), I32)
    return pl.pallas_call(
        _score_kernel,
        grid=(nt,),
        in_specs=[
            pl.BlockSpec((1, slots, t), lambda i: (i, 0, 0), memory_space=pltpu.SMEM),
            pl.BlockSpec((t, d), lambda i: (i, 0)),
            pl.BlockSpec((1, d), lambda i: (0, 0)),
            pl.BlockSpec((1, slots, t), lambda i: (i, 0, 0)),
            pl.BlockSpec(table.shape, lambda i: (0, 0), pipeline_mode=pl.Buffered(1)),
        ],
        out_specs=pl.BlockSpec((1, t, slots), lambda i: (i, 0, 0)),
        out_shape=jax.ShapeDtypeStruct((nt, t, slots), F32),
        scratch_shapes=[stage, stage, pltpu.VMEM((slots, t), F32)],
        compiler_params=_cparams(("arbitrary",)),
        name="peer_scores",
    )(e_t, x2d, g.reshape(1, d), gate_t, table)


def _value_kernel(e_ref, w_ref, x_ref, gf_ref, tab_ref, o_ref, stage_a, stage_b, wb_ref, acc_ref, *, final_norm):
    t, d = x_ref.shape
    nchunk = d // (2 * LANES)
    slots = e_ref.shape[1]
    groups = t // SUBLANES

    acc_ref[...] = x_ref[...]

    def consume(o, stage_ref):
        w_oct = pltpu.roll(w_ref[0], lax.rem(slots - o * OCTET, slots), 1)
        for k in range(OCTET):
            wb_ref[k] = jnp.broadcast_to(w_oct[:, k:k + 1], (t, LANES))
        for c in range(nchunk):
            lo_cols = slice(2 * c * LANES, (2 * c + 1) * LANES)
            hi_cols = slice((2 * c + 1) * LANES, (2 * c + 2) * LANES)
            acc_lo, acc_hi = acc_ref[:, lo_cols], acc_ref[:, hi_cols]
            for k in range(OCTET):
                words = stage_ref[k * groups:(k + 1) * groups, c * SUBLANES:(c + 1) * SUBLANES, :]
                words = words.reshape(t, LANES)
                lo = pltpu.bitcast(words << 16, F32)
                hi = pltpu.bitcast(words & jnp.int32(-65536), F32)
                acc_lo = acc_lo + wb_ref[k] * lo
                acc_hi = acc_hi + wb_ref[k] * hi
            acc_ref[:, lo_cols] = acc_lo
            acc_ref[:, hi_cols] = acc_hi

    gather = lambda o, stage_ref: _gather_octet(e_ref, tab_ref, stage_ref, o, nchunk)
    _octet_pipeline(slots // OCTET, gather, consume, stage_a, stage_b)

    y = acc_ref[...]
    if final_norm:
        y = _rms(y, gf_ref[...])
    o_ref[...] = y


def _peer_values(e_t, w_t, x2d, table, g_final, final_norm):
    nt, slots, t = e_t.shape
    d = x2d.shape[1]
    stage = pltpu.VMEM((OCTET * t // SUBLANES, d // (2 * LANES) * SUBLANES, LANES), I32)
    return pl.pallas_call(
        functools.partial(_value_kernel, final_norm=final_norm),
        grid=(nt,),
        in_specs=[
            pl.BlockSpec((1, slots, t), lambda i: (i, 0, 0), memory_space=pltpu.SMEM),
            pl.BlockSpec((1, t, slots), lambda i: (i, 0, 0)),
            pl.BlockSpec((t, d), lambda i: (i, 0)),
            pl.BlockSpec((1, d), lambda i: (0, 0)),
            pl.BlockSpec(table.shape, lambda i: (0, 0), pipeline_mode=pl.Buffered(1)),
        ],
        out_specs=pl.BlockSpec((t, d), lambda i: (i, 0)),
        out_shape=jax.ShapeDtypeStruct(x2d.shape, F32),
        scratch_shapes=[stage, stage, pltpu.VMEM((OCTET, t, LANES), F32), pltpu.VMEM((t, d), F32)],
        compiler_params=_cparams(("arbitrary",)),
        name="peer_values",
    )(e_t, w_t, x2d, g_final.reshape(1, d), table)


def _pack_pairs(w):
    e, d = w.shape
    bits = lax.bitcast_convert_type(w.astype(BF16), jnp.uint16).astype(jnp.uint32)
    bits = bits.reshape(e * d // (2 * LANES), 2 * LANES)
    packed = bits[:, :LANES] | (bits[:, LANES:] << 16)
    return lax.bitcast_convert_type(packed, I32)


def _sc_value_kernel(tab_hbm, idx_hbm, w_hbm, out_hbm, idx_v, w_v, rows_v, acc_v):
    n_tok = out_hbm.shape[0] // (SC_CORES * SC_SUBCORES)
    d = out_hbm.shape[1]
    slots = w_v.shape[0]
    words = d // 2
    wid = lax.axis_index("s") * SC_CORES + lax.axis_index("c")

    @pl.loop(0, n_tok)
    def _(i):
        tok = wid * n_tok + i
        for k in range(d // SC_LANES):
            acc_v[pl.ds(k * SC_LANES, SC_LANES)] = jnp.zeros((SC_LANES,), F32)
        pltpu.sync_copy(w_hbm.at[pl.ds(tok * slots, slots)], w_v)

        @pl.loop(0, slots // SC_WINDOW)
        def _(win):
            pltpu.sync_copy(idx_hbm.at[pl.ds(tok * slots + win * SC_WINDOW, SC_WINDOW)], idx_v)
            pltpu.sync_copy(tab_hbm.at[idx_v], rows_v)

            @pl.loop(0, SC_WINDOW)
            def _(r):
                slot = jnp.full((SC_LANES,), win * SC_WINDOW + r, I32)
                wv = plsc.load_gather(w_v, [slot])
                for k in range(words // SC_LANES):
                    x = rows_v[r, pl.ds(k * SC_LANES, SC_LANES)]
                    lo = lax.bitcast_convert_type(x << 16, F32)
                    hi = lax.bitcast_convert_type(x & jnp.int32(-65536), F32)
                    chunk, lane = divmod(k * SC_LANES, LANES)
                    plsc.addupdate(acc_v.at[pl.ds(2 * chunk * LANES + lane, SC_LANES)], wv * lo)
                    plsc.addupdate(acc_v.at[pl.ds((2 * chunk + 1) * LANES + lane, SC_LANES)], wv * hi)

        pltpu.sync_copy(acc_v, out_hbm.at[tok])


def _peer_values_sc(idx, w, table, n_tok, d):
    slots = idx.shape[0] // n_tok
    mesh = plsc.VectorSubcoreMesh(core_axis_name="c", subcore_axis_name="s",
                                  num_cores=SC_CORES, num_subcores=SC_SUBCORES)
    return pl.kernel(
        _sc_value_kernel,
        out_type=jax.ShapeDtypeStruct((n_tok, d), F32),
        mesh=mesh,
        scratch_types=[
            pltpu.VMEM((SC_WINDOW,), I32),
            pltpu.VMEM((slots,), F32),
            pltpu.VMEM((SC_WINDOW, d // 2), I32),
            pltpu.VMEM((d,), F32),
        ],
        compiler_params=pltpu.CompilerParams(needs_layout_passes=False),
        name="peer_values_sc",
    )(table, idx, w)


def _residual_norm_kernel(x_ref, p_ref, gf_ref, o_ref, *, final_norm):
    y = x_ref[...] + p_ref[...]
    if final_norm:
        y = _rms(y, gf_ref[...])
    o_ref[...] = y


def _residual_norm(x2d, p2d, g_final, final_norm, tm):
    t, d = x2d.shape
    return pl.pallas_call(
        functools.partial(_residual_norm_kernel, final_norm=final_norm),
        grid=(t // tm,),
        in_specs=[
            pl.BlockSpec((tm, d), lambda i: (i, 0)),
            pl.BlockSpec((tm, d), lambda i: (i, 0)),
            pl.BlockSpec((1, d), lambda i: (0, 0)),
        ],
        out_specs=pl.BlockSpec((tm, d), lambda i: (i, 0)),
        out_shape=jax.ShapeDtypeStruct((t, d), F32),
        compiler_params=_cparams(("arbitrary",)),
        name="peer_residual",
    )(x2d, p2d, g_final.reshape(1, d))


def _peer_ffn(x, g, w_q, k1, k2, u_emb, v_emb, g_final, final_norm):
    b, s, d = x.shape
    x2d = x.reshape(b * s, d)
    e_slot, gate_slot = _peer_router(x2d, g, w_q, k1, k2)
    w_tok = _peer_scores(e_slot, gate_slot, x2d, g, _pack_pairs(u_emb))
    v_tab = _pack_pairs(v_emb)
    nt, slots, t = e_slot.shape
    nt_sc = nt * SC_TOKEN_PERCENT // 100
    nt_tc = nt - nt_sc
    out_tc = _peer_values(e_slot[:nt_tc], w_tok[:nt_tc], x2d[:nt_tc * t], v_tab, g_final, final_norm)
    if nt_sc == 0:
        return out_tc.reshape(b, s, d)
    rows_per_expert = d // (2 * LANES)
    idx_sc = (e_slot[nt_tc:] // rows_per_expert).transpose(0, 2, 1).reshape(-1)
    peer_sc = _peer_values_sc(idx_sc, w_tok[nt_tc:].reshape(-1), v_tab.reshape(-1, d // 2), nt_sc * t, d)
    out_sc = _residual_norm(x2d[nt_tc * t:], peer_sc, g_final, final_norm, min(TOKEN_TILE, nt_sc * t))
    return jnp.concatenate([out_tc, out_sc], axis=0).reshape(b, s, d)


def _qkv_kernel(x_ref, g_ref, wqk_hi_ref, wqk_lo_ref, wv_ref, cos_ref, sin_ref,
                qt_ref, k0_ref, k1_ref, vt_ref, km_ref):
    tm, d = x_ref.shape[1], x_ref.shape[2]
    hn = _rms(x_ref[0], g_ref[...])
    hh, hl = _split(hn)
    qk = _dot3(hh, hl, wqk_hi_ref[...], wqk_lo_ref[...])
    v = _dot(hh, wv_ref[...])
    cos = jnp.concatenate([cos_ref[...]] * (d // LANES), axis=1)
    sin = jnp.concatenate([sin_ref[...]] * (d // LANES), axis=1)
    lane = lax.broadcasted_iota(I32, (tm, d), 1)
    first_half = (lane % HEAD_DIM) < (HEAD_DIM // 2)

    def rope(a):
        rot = jnp.where(first_half, pltpu.roll(a, d - HEAD_DIM // 2, 1), pltpu.roll(a, HEAD_DIM // 2, 1))
        return a * cos + rot * sin

    q = rope(qk[:, :d])
    k = rope(qk[:, d:])
    qt_ref[0] = q.T
    vt_ref[0] = v.T.astype(BF16)
    nb = tm // MOBA_BLOCK
    km_ref[0, 0] = jnp.mean(k.reshape(nb, MOBA_BLOCK, d), axis=1)
    row = lax.broadcasted_iota(I32, (tm, d), 0)
    block = (pl.program_id(1) * tm + row) // MOBA_BLOCK
    pair_lane = lane % LANES
    kb = k.astype(BF16)
    k0_ref[0] = jnp.where(pair_lane < HEAD_DIM, kb, jnp.where(pair_lane - HEAD_DIM == block, 1.0, 0.0).astype(BF16))
    k1_ref[0] = jnp.where(pair_lane >= HEAD_DIM, kb, jnp.where(pair_lane == block, 1.0, 0.0).astype(BF16))


def _qkv_rope(x, g, w_qkv, tm):
    b, s, d = x.shape
    half = HEAD_DIM // 2
    inv = ROPE_THETA ** (-jnp.arange(half, dtype=F32) / half)
    ang = jnp.arange(s).astype(F32)[:, None] * inv[None, :]
    cos, sin = jnp.cos(ang), jnp.sin(ang)
    cos128 = jnp.tile(jnp.concatenate([cos, cos], axis=1), (1, LANES // HEAD_DIM))
    sin128 = jnp.tile(jnp.concatenate([-sin, sin], axis=1), (1, LANES // HEAD_DIM))
    wqk_hi, wqk_lo = _split(w_qkv[:, :2 * d])
    wv = w_qkv[:, 2 * d:].astype(BF16)
    nb = tm // MOBA_BLOCK
    const = lambda i, j: (0, 0)
    qt, k0, k1, vt, km = pl.pallas_call(
        _qkv_kernel,
        grid=(b, s // tm),
        in_specs=[
            pl.BlockSpec((1, tm, d), lambda i, j: (i, j, 0)),
            pl.BlockSpec((1, d), const),
            pl.BlockSpec((d, 2 * d), const),
            pl.BlockSpec((d, 2 * d), const),
            pl.BlockSpec((d, d), const),
            pl.BlockSpec((tm, LANES), lambda i, j: (j, 0)),
            pl.BlockSpec((tm, LANES), lambda i, j: (j, 0)),
        ],
        out_specs=[
            pl.BlockSpec((1, d, tm), lambda i, j: (i, 0, j)),
            pl.BlockSpec((1, tm, d), lambda i, j: (i, j, 0)),
            pl.BlockSpec((1, tm, d), lambda i, j: (i, j, 0)),
            pl.BlockSpec((1, d, tm), lambda i, j: (i, 0, j)),
            pl.BlockSpec((1, 1, nb, d), lambda i, j: (i, j, 0, 0)),
        ],
        out_shape=[
            jax.ShapeDtypeStruct((b, d, s), F32),
            jax.ShapeDtypeStruct((b, s, d), BF16),
            jax.ShapeDtypeStruct((b, s, d), BF16),
            jax.ShapeDtypeStruct((b, d, s), BF16),
            jax.ShapeDtypeStruct((b, s // tm, nb, d), F32),
        ],
        compiler_params=_cparams(("arbitrary", "arbitrary")),
        name="qkv_rope",
    )(x, g.reshape(1, d), wqk_hi, wqk_lo, wv, cos128, sin128)
    return qt, (k0, k1), vt, km.reshape(b, s // MOBA_BLOCK, d)


def _moba_kernel(qt_ref, k0_ref, k1_ref, vt_ref, km_ref, o_ref, sa_ref, sb_ref):
    bs = MOBA_BLOCK
    nb = km_ref.shape[1]
    n_heads = LANES // HEAD_DIM
    k_refs = (k0_ref, k1_ref)
    j = pl.program_id(2)
    qt = qt_ref[0]
    km = km_ref[0]
    scale = HEAD_DIM ** -0.5 * 1.4426950408889634
    lane_km = lax.broadcasted_iota(I32, (nb, LANES), 1)
    blk = lax.broadcasted_iota(I32, (nb, bs), 0)
    zeros_pad = jnp.zeros((LANES - HEAD_DIM - nb, bs), F32)
    qh, ql = _split(qt)

    own = pl.ds(pl.multiple_of(j * bs, bs), bs)
    v_own = vt_ref[0, :, own]
    krow = lax.broadcasted_iota(I32, (bs, bs), 0)
    qcol = lax.broadcasted_iota(I32, (bs, bs), 1)

    q_augs, state = [], []
    for hh in range(n_heads):
        head_lo = hh * HEAD_DIM
        in_head_km = (lane_km >= head_lo) & (lane_km < head_lo + HEAD_DIM)
        kmh, kml = _split(jnp.where(in_head_km, km, 0.0))
        gate = _dot3(kmh, kml, qh, ql)
        valid = blk < j
        gate = jnp.where(valid, gate, -jnp.inf)
        sel = jnp.zeros((nb, bs), F32)
        for _ in range(MOBA_TOPK):
            m = jnp.max(gate, axis=0, keepdims=True)
            i = jnp.min(jnp.where(gate == m, blk, nb), axis=0, keepdims=True)
            pick = blk == i
            sel = jnp.where(pick, 1.0, sel)
            gate = jnp.where(pick, -jnp.inf, gate)
        bias_t = jnp.where((sel > 0.0) & valid, 0.0, MASK_NEG)
        q_head = qt[head_lo:head_lo + HEAD_DIM, :] * scale
        no_bias = jnp.zeros((LANES - HEAD_DIM, bs), F32)
        if hh == 0:
            q_aug = jnp.concatenate([q_head, bias_t, zeros_pad], axis=0)
            q_own = jnp.concatenate([q_head, no_bias], axis=0)
        else:
            q_aug = jnp.concatenate([bias_t, zeros_pad, q_head], axis=0)
            q_own = jnp.concatenate([no_bias, q_head], axis=0)
        q_augs.append(q_aug.astype(BF16))

        s_own = jnp.where(krow <= qcol, _dot(k_refs[hh][0, own, :], q_own.astype(BF16)), -1e30)
        m0 = jnp.max(s_own, axis=0, keepdims=True)
        p0 = jnp.exp2(s_own - m0)
        state += [m0, jnp.sum(p0, axis=0, keepdims=True), _dot(v_own, p0.astype(BF16))]

    chunk = KV_CHUNK * bs
    last_chunk = nb // KV_CHUNK - 1

    def score_chunk(c, s_ref):
        rows = pl.ds(pl.multiple_of(jnp.minimum(c, last_chunk) * chunk, chunk), chunk)
        for hh in range(n_heads):
            s_ref[hh] = _dot(k_refs[hh][0, rows, :], q_augs[hh])

    def attend(c, s_ref, state):
        vn = vt_ref[0, :, pl.ds(pl.multiple_of(c * chunk, chunk), chunk)]
        new_state = []
        for hh in range(n_heads):
            m, l, acc = state[3 * hh:3 * hh + 3]
            s = s_ref[hh]
            m_new = jnp.maximum(m, jnp.max(s, axis=0, keepdims=True))
            alpha = jnp.exp2(m - m_new)
            p = jnp.exp2(s - m_new)
            l = alpha * l + jnp.sum(p, axis=0, keepdims=True)
            acc = alpha * acc + _dot(vn, p.astype(BF16))
            new_state += [m_new, l, acc]
        return tuple(new_state)

    score_chunk(0, sa_ref)

    def body(i, state):
        score_chunk(2 * i + 1, sb_ref)
        state = attend(2 * i, sa_ref, state)
        score_chunk(2 * i + 2, sa_ref)
        return attend(2 * i + 1, sb_ref, state)

    state = lax.fori_loop(0, (j + 2 * KV_CHUNK - 1) // (2 * KV_CHUNK), body, tuple(state))
    halves = []
    for hh in range(n_heads):
        _, l, acc = state[3 * hh:3 * hh + 3]
        halves.append((acc / l)[hh * HEAD_DIM:(hh + 1) * HEAD_DIM, :])
    o_ref[0] = jnp.concatenate(halves, axis=0).T


def _moba_attention(qt, k01, vt, km):
    b, d, s = qt.shape
    nb = s // MOBA_BLOCK
    assert nb % (2 * KV_CHUNK) == 0
    return pl.pallas_call(
        _moba_kernel,
        grid=(b, d // LANES, nb),
        in_specs=[
            pl.BlockSpec((1, LANES, MOBA_BLOCK), lambda i, h, j: (i, h, j)),
            pl.BlockSpec((1, s, LANES), lambda i, h, j: (i, 0, h)),
            pl.BlockSpec((1, s, LANES), lambda i, h, j: (i, 0, h)),
            pl.BlockSpec((1, LANES, s), lambda i, h, j: (i, h, 0)),
            pl.BlockSpec((1, nb, LANES), lambda i, h, j: (i, 0, h)),
        ],
        out_specs=pl.BlockSpec((1, MOBA_BLOCK, LANES), lambda i, h, j: (i, j, h)),
        out_shape=jax.ShapeDtypeStruct((b, s, d), F32),
        scratch_shapes=[pltpu.VMEM((LANES // HEAD_DIM, KV_CHUNK * MOBA_BLOCK, MOBA_BLOCK), F32)] * 2,
        compiler_params=_cparams(("arbitrary", "arbitrary", "arbitrary")),
        name="moba_attention",
    )(qt, *k01, vt, km)


def _proj_residual_kernel(x_ref, a_ref, w_ref, o_ref):
    o_ref[...] = x_ref[...] + _dot(a_ref[...].astype(BF16), w_ref[...])


def _proj_residual(x2d, a2d, w, tm):
    t, d = x2d.shape
    return pl.pallas_call(
        _proj_residual_kernel,
        grid=(t // tm,),
        in_specs=[
            pl.BlockSpec((tm, d), lambda i: (i, 0)),
            pl.BlockSpec((tm, a2d.shape[1]), lambda i: (i, 0)),
            pl.BlockSpec(w.shape, lambda i: (0, 0)),
        ],
        out_specs=pl.BlockSpec((tm, d), lambda i: (i, 0)),
        out_shape=jax.ShapeDtypeStruct((t, d), F32),
        compiler_params=_cparams(("arbitrary",)),
        name="attn_out_proj",
    )(x2d, a2d, w.astype(BF16))


def _moba_mixer(x, g, w_qkv, w_o):
    b, s, d = x.shape
    tm = min(TOKEN_TILE, s)
    qt, k, vt, km = _qkv_rope(x, g, w_qkv, tm)
    attn = _moba_attention(qt, k, vt, km)
    return _proj_residual(x.reshape(b * s, d), attn.reshape(b * s, d), w_o, tm).reshape(b, s, d)


def kernel(x, norm_mix, norm_ffn, conv_w_in, conv_w, conv_w_out, attn_w_qkv, attn_w_o,
           peer_w_q, peer_k1, peer_k2, peer_u, peer_v, norm_final):
    depth = norm_mix.shape[0]
    tm = min(TOKEN_TILE, x.shape[1])
    for i in range(depth):
        j = i // 2
        if i % 2 == 0:
            x = _conv_mixer(x, norm_mix[i], conv_w_in[j], conv_w[j], conv_w_out[j], tm)
        else:
            x = _moba_mixer(x, norm_mix[i], attn_w_qkv[j], attn_w_o[j])
        x = _peer_ffn(x, norm_ffn[i], peer_w_q[i], peer_k1[i], peer_k2[i], peer_u[i], peer_v[i],
                      norm_final, final_norm=(i == depth - 1))
    return x
```

```python
import functools

import jax
import jax.numpy as jnp
from jax import lax
from jax.experimental import pallas as pl
from jax.experimental.pallas import tpu as pltpu
from jax.experimental.pallas import tpu_sc as plsc

F32 = jnp.float32
BF16 = jnp.bfloat16
I32 = jnp.int32

RMS_EPS = 1e-6
N_HEADS = 16
HEAD_DIM = 64
MOBA_BLOCK = 256
MOBA_TOPK = 3
ROPE_THETA = 10000.0
PEER_HEADS = 8
PEER_NKEYS = 128
PEER_HALF = 64
PEER_TOPK = 16

LANES = 128
SUBLANES = 8
VMEM_LIMIT = 56 * 1024 * 1024
MASK_NEG = -1e9

TOKEN_TILE = 512
PEER_TILE = 128
OCTET = 8
SC_CORES = 2
SC_SUBCORES = 16
SC_LANES = 16
SC_BATCH = 16
SC_TOKEN_PERCENT = 25
KV_CHUNK = 2


def _cparams(sem):
    return pltpu.CompilerParams(dimension_semantics=sem, vmem_limit_bytes=VMEM_LIMIT)


def _rms(x, g):
    ms = jnp.mean(x * x, axis=-1, keepdims=True)
    return x * lax.rsqrt(ms + RMS_EPS) * g


def _split(a):
    hi = a.astype(BF16)
    lo = (a - hi.astype(F32)).astype(BF16)
    return hi, lo


def _dot(a, b):
    return lax.dot_general(a, b, (((1,), (0,)), ((), ())), preferred_element_type=F32)


def _dot_t(a, b):
    return lax.dot_general(a, b, (((1,), (1,)), ((), ())), preferred_element_type=F32)


def _dot3(a_hi, a_lo, b_hi, b_lo):
    return _dot(a_hi, b_hi) + _dot(a_lo, b_hi) + _dot(a_hi, b_lo)


def _dot3_t(a_hi, a_lo, b_hi, b_lo):
    return _dot_t(a_hi, b_hi) + _dot_t(a_lo, b_hi) + _dot_t(a_hi, b_lo)


def _conv_mixer_kernel(x_ref, g_ref, win_ref, cw_ref, wout_ref, o_ref, ubuf_ref):
    tm, d = x_ref.shape[1], x_ref.shape[2]

    @pl.when(pl.program_id(1) == 0)
    def _():
        ubuf_ref[0:SUBLANES, :] = jnp.zeros((SUBLANES, d), F32)

    x = x_ref[0]
    hn = _rms(x, g_ref[...]).astype(BF16)
    bcz = _dot(hn, win_ref[...])
    b_gate, c_gate, z = bcz[:, :d], bcz[:, d:2 * d], bcz[:, 2 * d:]
    u = c_gate * z
    ubuf_ref[SUBLANES:SUBLANES + tm, :] = u
    u1 = ubuf_ref[SUBLANES - 1:SUBLANES - 1 + tm, :]
    u2 = ubuf_ref[SUBLANES - 2:SUBLANES - 2 + tm, :]
    cw = cw_ref[...]
    u_conv = cw[0:1, :] * u2 + cw[1:2, :] * u1 + cw[2:3, :] * u
    ubuf_ref[0:SUBLANES, :] = u[tm - SUBLANES:tm, :]
    y = (b_gate * u_conv).astype(BF16)
    o_ref[0] = x + _dot(y, wout_ref[...])


def _conv_mixer(x, g, w_in, conv_w, w_out, tm):
    b, s, d = x.shape
    return pl.pallas_call(
        _conv_mixer_kernel,
        grid=(b, s // tm),
        in_specs=[
            pl.BlockSpec((1, tm, d), lambda i, j: (i, j, 0)),
            pl.BlockSpec((1, d), lambda i, j: (0, 0)),
            pl.BlockSpec((d, 3 * d), lambda i, j: (0, 0)),
            pl.BlockSpec((3, d), lambda i, j: (0, 0)),
            pl.BlockSpec((d, d), lambda i, j: (0, 0)),
        ],
        out_specs=pl.BlockSpec((1, tm, d), lambda i, j: (i, j, 0)),
        out_shape=jax.ShapeDtypeStruct((b, s, d), F32),
        scratch_shapes=[pltpu.VMEM((tm + SUBLANES, d), F32)],
        compiler_params=_cparams(("arbitrary", "arbitrary")),
        name="conv_mixer",
    )(x, g.reshape(1, d), w_in.astype(BF16), conv_w, w_out.astype(BF16))


def _topk_rows(s, k, order=None, payload=None):
    if order is None:
        order = lax.broadcasted_iota(I32, s.shape, 0)
    big = jnp.iinfo(jnp.int32).max
    vals, outs = [], []
    for _ in range(k):
        m = jnp.max(s, axis=0, keepdims=True)
        i = jnp.min(jnp.where(s == m, order, big), axis=0, keepdims=True)
        pick = order == i
        vals.append(m)
        if payload is None:
            outs.append(i)
        else:
            outs.append(jnp.max(jnp.where(pick, payload, -1), axis=0, keepdims=True))
        s = jnp.where(pick, -jnp.inf, s)
    return jnp.concatenate(vals, axis=0), jnp.concatenate(outs, axis=0)


def _staircase(kk):
    groups = []
    for a in range(2):
        for b0 in range(0, kk // (a + 1), SUBLANES):
            groups.append((a, 0, b0, 1, lambda j, a=a, b0=b0: (a + 1) * (b0 + j + 1) <= kk))
    for b in range(kk // 3):
        for a0 in range(0, kk // (b + 1), SUBLANES):
            groups.append((a0, 1, b, 0, lambda j, a0=a0, b=b: (a0 + j >= 2) & ((a0 + j + 1) * (b + 1) <= kk)))
    return groups


def _staircase_topk(v1, i1, v2, i2, nk):
    kk, t = v1.shape
    j = lax.broadcasted_iota(I32, (SUBLANES, t), 0)

    def rows(x, x0, step):
        if step == 0:
            return jnp.broadcast_to(x[x0:x0 + 1, :], (SUBLANES, t))
        return x[x0:x0 + SUBLANES, :]

    cand, order, cidx = [], [], []
    for a0, a_step, b0, b_step, valid in _staircase(kk):
        ok = valid(j)
        cand.append(jnp.where(ok, rows(v1, a0, a_step) + rows(v2, b0, b_step), -jnp.inf))
        order.append(jnp.where(ok, (a0 + j * a_step) * kk + (b0 + j * b_step), jnp.iinfo(jnp.int32).max - 1))
        cidx.append(rows(i1, a0, a_step) * nk + rows(i2, b0, b_step))
    cat = lambda xs: jnp.concatenate(xs, axis=0)
    return _topk_rows(cat(cand), kk, order=cat(order), payload=cat(cidx))


def _router_kernel(x_ref, g_ref, wq_hi_ref, wq_lo_ref, k1_hi_ref, k1_lo_ref, k2_hi_ref, k2_lo_ref,
                   e_ref, gate_ref):
    table_rows = x_ref.shape[1] // (2 * LANES)
    xn = _rms(x_ref[...], g_ref[...])
    xh, xl = _split(xn)
    q = _dot3(xh, xl, wq_hi_ref[...], wq_lo_ref[...])
    nk = k1_hi_ref.shape[0]
    for h in range(PEER_HEADS):
        qh, ql = _split(q[:, h * LANES:(h + 1) * LANES])
        s1 = _dot3_t(k1_hi_ref[...], k1_lo_ref[...], qh, ql)
        s2 = _dot3_t(k2_hi_ref[...], k2_lo_ref[...], qh, ql)
        v1, i1 = _topk_rows(s1, PEER_TOPK)
        v2, i2 = _topk_rows(s2, PEER_TOPK)
        kk = PEER_TOPK
        top_s, e_idx = _staircase_topk(v1, i1, v2, i2, nk)
        p = jnp.exp(top_s - top_s[0:1, :])
        gate = p / jnp.sum(p, axis=0, keepdims=True)
        e_ref[0, h * kk:(h + 1) * kk, :] = e_idx * table_rows
        gate_ref[0, h * kk:(h + 1) * kk, :] = gate


def _pad_keys(k, lo):
    nk, half = k.shape
    out = jnp.zeros((nk, LANES), F32)
    return out.at[:, lo:lo + half].set(k)


def _peer_router(x2d, g, w_q, k1, k2):
    t, d = x2d.shape
    nt = t // PEER_TILE
    slots = PEER_HEADS * PEER_TOPK
    wq_hi, wq_lo = _split(w_q)
    k1_hi, k1_lo = _split(_pad_keys(k1, 0))
    k2_hi, k2_lo = _split(_pad_keys(k2, PEER_HALF))
    const = lambda i: (0, 0)
    kspec = pl.BlockSpec(k1_hi.shape, const)
    return pl.pallas_call(
        _router_kernel,
        grid=(nt,),
        in_specs=[
            pl.BlockSpec((PEER_TILE, d), lambda i: (i, 0)),
            pl.BlockSpec((1, d), const),
            pl.BlockSpec(wq_hi.shape, const),
            pl.BlockSpec(wq_lo.shape, const),
            kspec, kspec, kspec, kspec,
        ],
        out_specs=[
            pl.BlockSpec((1, slots, PEER_TILE), lambda i: (i, 0, 0)),
            pl.BlockSpec((1, slots, PEER_TILE), lambda i: (i, 0, 0)),
        ],
        out_shape=[
            jax.ShapeDtypeStruct((nt, slots, PEER_TILE), I32),
            jax.ShapeDtypeStruct((nt, slots, PEER_TILE), F32),
        ],
        compiler_params=_cparams(("arbitrary",)),
        name="peer_router",
    )(x2d, g.reshape(1, d), wq_hi, wq_lo, k1_hi, k1_lo, k2_hi, k2_lo)


def _gather_octet(e_ref, tab_ref, stage_ref, octet, nsub):
    t = e_ref.shape[2]
    for k in range(OCTET):
        for tok in range(t):
            off = e_ref[0, octet * OCTET + k, tok]
            stage_ref[k * (t // SUBLANES) + tok // SUBLANES,
                      pl.ds(tok % SUBLANES, nsub, stride=SUBLANES), :] = tab_ref[pl.ds(off, nsub), :]


def _octet_pipeline(n_octets, gather, consume, stage_a, stage_b):
    gather(0, stage_a)

    def pair(p, carry):
        gather(2 * p + 1, stage_b)
        consume(2 * p, stage_a)
        gather(jnp.minimum(2 * p + 2, n_octets - 1), stage_a)
        consume(2 * p + 1, stage_b)
        return carry
    lax.fori_loop(0, n_octets // 2, pair, 0)


def _score_kernel(e_ref, x_ref, g_ref, gate_ref, tab_ref, w_ref, stage_a, stage_b, a_ref):
    t, d = x_ref.shape
    nchunk = d // (2 * LANES)
    slots = e_ref.shape[1]
    rows = OCTET * t

    xn = _rms(x_ref[...], g_ref[...]).astype(BF16)
    x_even = jnp.concatenate([xn[:, (2 * c) * LANES:(2 * c + 1) * LANES] for c in range(nchunk)], axis=1)
    x_odd = jnp.concatenate([xn[:, (2 * c + 1) * LANES:(2 * c + 2) * LANES] for c in range(nchunk)], axis=1)
    rhs = jnp.concatenate([x_even, x_odd], axis=0)

    ri = lax.broadcasted_iota(I32, (2 * t, 2 * t), 0)
    ci = lax.broadcasted_iota(I32, (2 * t, 2 * t), 1)
    diag = (ci == (ri % 2) * t + ri // 2).astype(F32)

    def consume(o, stage_ref):
        planes = []
        for c in range(nchunk):
            plane = stage_ref[:, c * SUBLANES:(c + 1) * SUBLANES, :].reshape(rows, LANES)
            planes.append(pltpu.bitcast(plane, BF16))
        lhs = jnp.concatenate(planes, axis=1)
        prod = _dot_t(lhs, rhs)
        prod = prod.reshape(OCTET, 2 * t, 2 * t) * diag[None]
        a2 = jnp.sum(prod, axis=1)
        a_ref[pl.ds(pl.multiple_of(o * OCTET, OCTET), OCTET), :] = a2[:, :t] + a2[:, t:]

    gather = lambda o, stage_ref: _gather_octet(e_ref, tab_ref, stage_ref, o, nchunk)
    _octet_pipeline(slots // OCTET, gather, consume, stage_a, stage_b)

    a = a_ref[...]
    gelu = 0.5 * a * (1.0 + lax.erf(a * (2.0 ** -0.5)))
    w_ref[0] = (gate_ref[0] * gelu).T


def _peer_scores(e_t, gate_t, x2d, g, table):
    nt, slots, t = e_t.shape
    d = x2d.shape[1]
    stage = pltpu.VMEM((OCTET * t // SUBLANES, d // (2 * LANES) * SUBLANES, LANES), I32)
    return pl.pallas_call(
        _score_kernel,
        grid=(nt,),
        in_specs=[
            pl.BlockSpec((1, slots, t), lambda i: (i, 0, 0), memory_space=pltpu.SMEM),
            pl.BlockSpec((t, d), lambda i: (i, 0)),
            pl.BlockSpec((1, d), lambda i: (0, 0)),
            pl.BlockSpec((1, slots, t), lambda i: (i, 0, 0)),
            pl.BlockSpec(table.shape, lambda i: (0, 0), pipeline_mode=pl.Buffered(1)),
        ],
        out_specs=pl.BlockSpec((1, t, slots), lambda i: (i, 0, 0)),
        out_shape=jax.ShapeDtypeStruct((nt, t, slots), F32),
        scratch_shapes=[stage, stage, pltpu.VMEM((slots, t), F32)],
        compiler_params=_cparams(("arbitrary",)),
        name="peer_scores",
    )(e_t, x2d, g.reshape(1, d), gate_t, table)


def _value_kernel(e_ref, w_ref, x_ref, gf_ref, tab_ref, o_ref, stage_a, stage_b, wb_ref, acc_ref, *, final_norm):
    t, d = x_ref.shape
    nchunk = d // (2 * LANES)
    slots = e_ref.shape[1]
    groups = t // SUBLANES

    acc_ref[...] = x_ref[...]

    def consume(o, stage_ref):
        w_oct = pltpu.roll(w_ref[0], lax.rem(slots - o * OCTET, slots), 1)
        for k in range(OCTET):
            wb_ref[k] = jnp.broadcast_to(w_oct[:, k:k + 1], (t, LANES))
        for c in range(nchunk):
            lo_cols = slice(2 * c * LANES, (2 * c + 1) * LANES)
            hi_cols = slice((2 * c + 1) * LANES, (2 * c + 2) * LANES)
            acc_lo, acc_hi = acc_ref[:, lo_cols], acc_ref[:, hi_cols]
            for k in range(OCTET):
                words = stage_ref[k * groups:(k + 1) * groups, c * SUBLANES:(c + 1) * SUBLANES, :]
                words = words.reshape(t, LANES)
                lo = pltpu.bitcast(words << 16, F32)
                hi = pltpu.bitcast(words & jnp.int32(-65536), F32)
                acc_lo = acc_lo + wb_ref[k] * lo
                acc_hi = acc_hi + wb_ref[k] * hi
            acc_ref[:, lo_cols] = acc_lo
            acc_ref[:, hi_cols] = acc_hi

    gather = lambda o, stage_ref: _gather_octet(e_ref, tab_ref, stage_ref, o, nchunk)
    _octet_pipeline(slots // OCTET, gather, consume, stage_a, stage_b)

    y = acc_ref[...]
    if final_norm:
        y = _rms(y, gf_ref[...])
    o_ref[...] = y


def _peer_values(e_t, w_t, x2d, table, g_final, final_norm):
    nt, slots, t = e_t.shape
    d = x2d.shape[1]
    stage = pltpu.VMEM((OCTET * t // SUBLANES, d // (2 * LANES) * SUBLANES, LANES), I32)
    return pl.pallas_call(
        functools.partial(_value_kernel, final_norm=final_norm),
        grid=(nt,),
        in_specs=[
            pl.BlockSpec((1, slots, t), lambda i: (i, 0, 0), memory_space=pltpu.SMEM),
            pl.BlockSpec((1, t, slots), lambda i: (i, 0, 0)),
            pl.BlockSpec((t, d), lambda i: (i, 0)),
            pl.BlockSpec((1, d), lambda i: (0, 0)),
            pl.BlockSpec(table.shape, lambda i: (0, 0), pipeline_mode=pl.Buffered(1)),
        ],
        out_specs=pl.BlockSpec((t, d), lambda i: (i, 0)),
        out_shape=jax.ShapeDtypeStruct(x2d.shape, F32),
        scratch_shapes=[stage, stage, pltpu.VMEM((OCTET, t, LANES), F32), pltpu.VMEM((t, d), F32)],
        compiler_params=_cparams(("arbitrary",)),
        name="peer_values",
    )(e_t, w_t, x2d, g_final.reshape(1, d), table)


def _pack_pairs(w):
    e, d = w.shape
    bits = lax.bitcast_convert_type(w.astype(BF16), jnp.uint16).astype(jnp.uint32)
    bits = bits.reshape(e * d // (2 * LANES), 2 * LANES)
    packed = bits[:, :LANES] | (bits[:, LANES:] << 16)
    return lax.bitcast_convert_type(packed, I32)


def _sc_value_kernel(tab_hbm, idx_hbm, w_hbm, out_hbm, idx_v, w_v, rows_a, rows_b, acc_a, acc_b, row_sem, out_sem,
                     *, slots):
    n_tok = out_hbm.shape[0] // (SC_CORES * SC_SUBCORES)
    d = out_hbm.shape[1]
    words = d // 2
    win = slots // 2
    wid = lax.axis_index("s") * SC_CORES + lax.axis_index("c")
    base_tok = wid * n_tok

    def gather(tl, half, buf, sem):
        rows = idx_v.at[pl.ds(tl * slots + half * win, win)]
        return pltpu.make_async_copy(tab_hbm.at[rows], buf, sem)

    def accumulate(tl, half, buf, acc):
        @pl.loop(0, win)
        def _(r):
            slot = jnp.full((SC_LANES,), tl * slots + half * win + r, I32)
            wv = plsc.load_gather(w_v, [slot])
            for k in range(words // SC_LANES):
                x = buf[r, pl.ds(k * SC_LANES, SC_LANES)]
                lo = lax.bitcast_convert_type(x << 16, F32)
                hi = lax.bitcast_convert_type(x & jnp.int32(-65536), F32)
                chunk, lane = divmod(k * SC_LANES, LANES)
                plsc.addupdate(acc.at[pl.ds(2 * chunk * LANES + lane, SC_LANES)], wv * lo)
                plsc.addupdate(acc.at[pl.ds((2 * chunk + 1) * LANES + lane, SC_LANES)], wv * hi)

    def write_out(acc, tok, sem):
        return pltpu.make_async_copy(acc, out_hbm.at[tok], sem)

    @pl.loop(0, n_tok // SC_BATCH)
    def _(bi):
        tok0 = base_tok + bi * SC_BATCH
        pltpu.sync_copy(idx_hbm.at[pl.ds(tok0 * slots, SC_BATCH * slots)], idx_v)
        pltpu.sync_copy(w_hbm.at[pl.ds(tok0 * slots, SC_BATCH * slots)], w_v)
        gather(0, 0, rows_a, row_sem.at[0]).start()

        @pl.loop(0, SC_BATCH // 2)
        def _(tp):
            for parity, acc in ((0, acc_a), (1, acc_b)):
                tl = 2 * tp + parity

                @pl.when(bi * SC_BATCH + tl >= 2)
                def _():
                    write_out(acc, tok0, out_sem.at[parity]).wait()
                for k in range(d // SC_LANES):
                    acc[pl.ds(k * SC_LANES, SC_LANES)] = jnp.zeros((SC_LANES,), F32)
                gather(tl, 1, rows_b, row_sem.at[1]).start()
                gather(tl, 0, rows_a, row_sem.at[0]).wait()
                accumulate(tl, 0, rows_a, acc)
                gather(jnp.minimum(tl + 1, SC_BATCH - 1), 0, rows_a, row_sem.at[0]).start()
                gather(tl, 1, rows_b, row_sem.at[1]).wait()
                accumulate(tl, 1, rows_b, acc)
                write_out(acc, tok0 + tl, out_sem.at[parity]).start()

        gather(SC_BATCH - 1, 0, rows_a, row_sem.at[0]).wait()

    write_out(acc_a, base_tok, out_sem.at[0]).wait()
    write_out(acc_b, base_tok, out_sem.at[1]).wait()


def _peer_values_sc(idx, w, table, n_tok, d):
    slots = idx.shape[0] // n_tok
    assert n_tok % (SC_CORES * SC_SUBCORES * SC_BATCH) == 0 and SC_BATCH % 2 == 0
    mesh = plsc.VectorSubcoreMesh(core_axis_name="c", subcore_axis_name="s",
                                  num_cores=SC_CORES, num_subcores=SC_SUBCORES)
    return pl.kernel(
        functools.partial(_sc_value_kernel, slots=slots),
        out_type=jax.ShapeDtypeStruct((n_tok, d), F32),
        mesh=mesh,
        scratch_types=[
            pltpu.VMEM((SC_BATCH * slots,), I32),
            pltpu.VMEM((SC_BATCH * slots,), F32),
            pltpu.VMEM((slots // 2, d // 2), I32),
            pltpu.VMEM((slots // 2, d // 2), I32),
            pltpu.VMEM((d,), F32),
            pltpu.VMEM((d,), F32),
            pltpu.SemaphoreType.DMA((2,)),
            pltpu.SemaphoreType.DMA((2,)),
        ],
        compiler_params=pltpu.CompilerParams(needs_layout_passes=False),
        name="peer_values_sc",
    )(table, idx, w)


def _residual_norm_kernel(x_ref, p_ref, gf_ref, o_ref, *, final_norm):
    y = x_ref[...] + p_ref[...]
    if final_norm:
        y = _rms(y, gf_ref[...])
    o_ref[...] = y


def _residual_norm(x2d, p2d, g_final, final_norm, tm):
    t, d = x2d.shape
    return pl.pallas_call(
        functools.partial(_residual_norm_kernel, final_norm=final_norm),
        grid=(t // tm,),
        in_specs=[
            pl.BlockSpec((tm, d), lambda i: (i, 0)),
            pl.BlockSpec((tm, d), lambda i: (i, 0)),
            pl.BlockSpec((1, d), lambda i: (0, 0)),
        ],
        out_specs=pl.BlockSpec((tm, d), lambda i: (i, 0)),
        out_shape=jax.ShapeDtypeStruct((t, d), F32),
        compiler_params=_cparams(("arbitrary",)),
        name="peer_residual",
    )(x2d, p2d, g_final.reshape(1, d))


def _peer_ffn(x, g, w_q, k1, k2, u_emb, v_emb, g_final, final_norm):
    b, s, d = x.shape
    x2d = x.reshape(b * s, d)
    e_slot, gate_slot = _peer_router(x2d, g, w_q, k1, k2)
    w_tok = _peer_scores(e_slot, gate_slot, x2d, g, _pack_pairs(u_emb))
    v_tab = _pack_pairs(v_emb)
    nt, slots, t = e_slot.shape
    nt_sc = nt * SC_TOKEN_PERCENT // 100
    nt_tc = nt - nt_sc
    out_tc = _peer_values(e_slot[:nt_tc], w_tok[:nt_tc], x2d[:nt_tc * t], v_tab, g_final, final_norm)
    if nt_sc == 0:
        return out_tc.reshape(b, s, d)
    rows_per_expert = d // (2 * LANES)
    idx_sc = (e_slot[nt_tc:] // rows_per_expert).transpose(0, 2, 1).reshape(-1)
    peer_sc = _peer_values_sc(idx_sc, w_tok[nt_tc:].reshape(-1), v_tab.reshape(-1, d // 2), nt_sc * t, d)
    out_sc = _residual_norm(x2d[nt_tc * t:], peer_sc, g_final, final_norm, min(TOKEN_TILE, nt_sc * t))
    return jnp.concatenate([out_tc, out_sc], axis=0).reshape(b, s, d)


def _qkv_kernel(x_ref, g_ref, wqk_hi_ref, wqk_lo_ref, wv_ref, cos_ref, sin_ref,
                qt_ref, k0_ref, k1_ref, vt_ref, km_ref):
    tm, d = x_ref.shape[1], x_ref.shape[2]
    hn = _rms(x_ref[0], g_ref[...])
    hh, hl = _split(hn)
    qk = _dot3(hh, hl, wqk_hi_ref[...], wqk_lo_ref[...])
    v = _dot(hh, wv_ref[...])
    cos = jnp.concatenate([cos_ref[...]] * (d // LANES), axis=1)
    sin = jnp.concatenate([sin_ref[...]] * (d // LANES), axis=1)
    lane = lax.broadcasted_iota(I32, (tm, d), 1)
    first_half = (lane % HEAD_DIM) < (HEAD_DIM // 2)

    def rope(a):
        rot = jnp.where(first_half, pltpu.roll(a, d - HEAD_DIM // 2, 1), pltpu.roll(a, HEAD_DIM // 2, 1))
        return a * cos + rot * sin

    q = rope(qk[:, :d])
    k = rope(qk[:, d:])
    qt_ref[0] = q.T
    vt_ref[0] = v.T.astype(BF16)
    nb = tm // MOBA_BLOCK
    km_ref[0, 0] = jnp.mean(k.reshape(nb, MOBA_BLOCK, d), axis=1)
    row = lax.broadcasted_iota(I32, (tm, d), 0)
    block = (pl.program_id(1) * tm + row) // MOBA_BLOCK
    pair_lane = lane % LANES
    kb = k.astype(BF16)
    k0_ref[0] = jnp.where(pair_lane < HEAD_DIM, kb, jnp.where(pair_lane - HEAD_DIM == block, 1.0, 0.0).astype(BF16))
    k1_ref[0] = jnp.where(pair_lane >= HEAD_DIM, kb, jnp.where(pair_lane == block, 1.0, 0.0).astype(BF16))


def _qkv_rope(x, g, w_qkv, tm):
    b, s, d = x.shape
    half = HEAD_DIM // 2
    inv = ROPE_THETA ** (-jnp.arange(half, dtype=F32) / half)
    ang = jnp.arange(s).astype(F32)[:, None] * inv[None, :]
    cos, sin = jnp.cos(ang), jnp.sin(ang)
    cos128 = jnp.tile(jnp.concatenate([cos, cos], axis=1), (1, LANES // HEAD_DIM))
    sin128 = jnp.tile(jnp.concatenate([-sin, sin], axis=1), (1, LANES // HEAD_DIM))
    wqk_hi, wqk_lo = _split(w_qkv[:, :2 * d])
    wv = w_qkv[:, 2 * d:].astype(BF16)
    nb = tm // MOBA_BLOCK
    const = lambda i, j: (0, 0)
    qt, k0, k1, vt, km = pl.pallas_call(
        _qkv_kernel,
        grid=(b, s // tm),
        in_specs=[
            pl.BlockSpec((1, tm, d), lambda i, j: (i, j, 0)),
            pl.BlockSpec((1, d), const),
            pl.BlockSpec((d, 2 * d), const),
            pl.BlockSpec((d, 2 * d), const),
            pl.BlockSpec((d, d), const),
            pl.BlockSpec((tm, LANES), lambda i, j: (j, 0)),
            pl.BlockSpec((tm, LANES), lambda i, j: (j, 0)),
        ],
        out_specs=[
            pl.BlockSpec((1, d, tm), lambda i, j: (i, 0, j)),
            pl.BlockSpec((1, tm, d), lambda i, j: (i, j, 0)),
            pl.BlockSpec((1, tm, d), lambda i, j: (i, j, 0)),
            pl.BlockSpec((1, d, tm), lambda i, j: (i, 0, j)),
            pl.BlockSpec((1, 1, nb, d), lambda i, j: (i, j, 0, 0)),
        ],
        out_shape=[
            jax.ShapeDtypeStruct((b, d, s), F32),
            jax.ShapeDtypeStruct((b, s, d), BF16),
            jax.ShapeDtypeStruct((b, s, d), BF16),
            jax.ShapeDtypeStruct((b, d, s), BF16),
            jax.ShapeDtypeStruct((b, s // tm, nb, d), F32),
        ],
        compiler_params=_cparams(("arbitrary", "arbitrary")),
        name="qkv_rope",
    )(x, g.reshape(1, d), wqk_hi, wqk_lo, wv, cos128, sin128)
    return qt, (k0, k1), vt, km.reshape(b, s // MOBA_BLOCK, d)


def _moba_kernel(qt_ref, k0_ref, k1_ref, vt_ref, km_ref, o_ref, sa_ref, sb_ref):
    bs = MOBA_BLOCK
    nb = km_ref.shape[1]
    n_heads = LANES // HEAD_DIM
    k_refs = (k0_ref, k1_ref)
    j = pl.program_id(2)
    qt = qt_ref[0]
    km = km_ref[0]
    scale = HEAD_DIM ** -0.5 * 1.4426950408889634
    lane_km = lax.broadcasted_iota(I32, (nb, LANES), 1)
    blk = lax.broadcasted_iota(I32, (nb, bs), 0)
    zeros_pad = jnp.zeros((LANES - HEAD_DIM - nb, bs), F32)
    qh, ql = _split(qt)

    own = pl.ds(pl.multiple_of(j * bs, bs), bs)
    v_own = vt_ref[0, :, own]
    krow = lax.broadcasted_iota(I32, (bs, bs), 0)
    qcol = lax.broadcasted_iota(I32, (bs, bs), 1)

    q_augs, state = [], []
    for hh in range(n_heads):
        head_lo = hh * HEAD_DIM
        in_head_km = (lane_km >= head_lo) & (lane_km < head_lo + HEAD_DIM)
        kmh, kml = _split(jnp.where(in_head_km, km, 0.0))
        gate = _dot3(kmh, kml, qh, ql)
        valid = blk < j
        gate = jnp.where(valid, gate, -jnp.inf)
        sel = jnp.zeros((nb, bs), F32)
        for _ in range(MOBA_TOPK):
            m = jnp.max(gate, axis=0, keepdims=True)
            i = jnp.min(jnp.where(gate == m, blk, nb), axis=0, keepdims=True)
            pick = blk == i
            sel = jnp.where(pick, 1.0, sel)
            gate = jnp.where(pick, -jnp.inf, gate)
        bias_t = jnp.where((sel > 0.0) & valid, 0.0, MASK_NEG)
        q_head = qt[head_lo:head_lo + HEAD_DIM, :] * scale
        no_bias = jnp.zeros((LANES - HEAD_DIM, bs), F32)
        if hh == 0:
            q_aug = jnp.concatenate([q_head, bias_t, zeros_pad], axis=0)
            q_own = jnp.concatenate([q_head, no_bias], axis=0)
        else:
            q_aug = jnp.concatenate([bias_t, zeros_pad, q_head], axis=0)
            q_own = jnp.concatenate([no_bias, q_head], axis=0)
        q_augs.append(q_aug.astype(BF16))

        s_own = jnp.where(krow <= qcol, _dot(k_refs[hh][0, own, :], q_own.astype(BF16)), -1e30)
        m0 = jnp.max(s_own, axis=0, keepdims=True)
        p0 = jnp.exp2(s_own - m0)
        state += [m0, jnp.sum(p0, axis=0, keepdims=True), _dot(v_own, p0.astype(BF16))]

    chunk = KV_CHUNK * bs
    last_chunk = nb // KV_CHUNK - 1

    def score_chunk(c, s_ref):
        rows = pl.ds(pl.multiple_of(jnp.minimum(c, last_chunk) * chunk, chunk), chunk)
        for hh in range(n_heads):
            s_ref[hh] = _dot(k_refs[hh][0, rows, :], q_augs[hh])

    def attend(c, s_ref, state):
        vn = vt_ref[0, :, pl.ds(pl.multiple_of(c * chunk, chunk), chunk)]
        new_state = []
        for hh in range(n_heads):
            m, l, acc = state[3 * hh:3 * hh + 3]
            s = s_ref[hh]
            m_new = jnp.maximum(m, jnp.max(s, axis=0, keepdims=True))
            alpha = jnp.exp2(m - m_new)
            p = jnp.exp2(s - m_new)
            l = alpha * l + jnp.sum(p, axis=0, keepdims=True)
            acc = alpha * acc + _dot(vn, p.astype(BF16))
            new_state += [m_new, l, acc]
        return tuple(new_state)

    score_chunk(0, sa_ref)

    def body(i, state):
        score_chunk(2 * i + 1, sb_ref)
        state = attend(2 * i, sa_ref, state)
        score_chunk(2 * i + 2, sa_ref)
        return attend(2 * i + 1, sb_ref, state)

    state = lax.fori_loop(0, (j + 2 * KV_CHUNK - 1) // (2 * KV_CHUNK), body, tuple(state))
    halves = []
    for hh in range(n_heads):
        _, l, acc = state[3 * hh:3 * hh + 3]
        halves.append((acc / l)[hh * HEAD_DIM:(hh + 1) * HEAD_DIM, :])
    o_ref[0] = jnp.concatenate(halves, axis=0).T


def _moba_attention(qt, k01, vt, km):
    b, d, s = qt.shape
    nb = s // MOBA_BLOCK
    assert nb % (2 * KV_CHUNK) == 0
    return pl.pallas_call(
        _moba_kernel,
        grid=(b, d // LANES, nb),
        in_specs=[
            pl.BlockSpec((1, LANES, MOBA_BLOCK), lambda i, h, j: (i, h, j)),
            pl.BlockSpec((1, s, LANES), lambda i, h, j: (i, 0, h)),
            pl.BlockSpec((1, s, LANES), lambda i, h, j: (i, 0, h)),
            pl.BlockSpec((1, LANES, s), lambda i, h, j: (i, h, 0)),
            pl.BlockSpec((1, nb, LANES), lambda i, h, j: (i, 0, h)),
        ],
        out_specs=pl.BlockSpec((1, MOBA_BLOCK, LANES), lambda i, h, j: (i, j, h)),
        out_shape=jax.ShapeDtypeStruct((b, s, d), F32),
        scratch_shapes=[pltpu.VMEM((LANES // HEAD_DIM, KV_CHUNK * MOBA_BLOCK, MOBA_BLOCK), F32)] * 2,
        compiler_params=_cparams(("arbitrary", "arbitrary", "arbitrary")),
        name="moba_attention",
    )(qt, *k01, vt, km)


def _proj_residual_kernel(x_ref, a_ref, w_ref, o_ref):
    o_ref[...] = x_ref[...] + _dot(a_ref[...].astype(BF16), w_ref[...])


def _proj_residual(x2d, a2d, w, tm):
    t, d = x2d.shape
    return pl.pallas_call(
        _proj_residual_kernel,
        grid=(t // tm,),
        in_specs=[
            pl.BlockSpec((tm, d), lambda i: (i, 0)),
            pl.BlockSpec((tm, a2d.shape[1]), lambda i: (i, 0)),
            pl.BlockSpec(w.shape, lambda i: (0, 0)),
        ],
        out_specs=pl.BlockSpec((tm, d), lambda i: (i, 0)),
        out_shape=jax.ShapeDtypeStruct((t, d), F32),
        compiler_params=_cparams(("arbitrary",)),
        name="attn_out_proj",
    )(x2d, a2d, w.astype(BF16))


def _moba_mixer(x, g, w_qkv, w_o):
    b, s, d = x.shape
    tm = min(TOKEN_TILE, s)
    qt, k, vt, km = _qkv_rope(x, g, w_qkv, tm)
    attn = _moba_attention(qt, k, vt, km)
    return _proj_residual(x.reshape(b * s, d), attn.reshape(b * s, d), w_o, tm).reshape(b, s, d)


def kernel(x, norm_mix, norm_ffn, conv_w_in, conv_w, conv_w_out, attn_w_qkv, attn_w_o,
           peer_w_q, peer_k1, peer_k2, peer_u, peer_v, norm_final):
    depth = norm_mix.shape[0]
    tm = min(TOKEN_TILE, x.shape[1])
    for i in range(depth):
        j = i // 2
        if i % 2 == 0:
            x = _conv_mixer(x, norm_mix[i], conv_w_in[j], conv_w[j], conv_w_out[j], tm)
        else:
            x = _moba_mixer(x, norm_mix[i], attn_w_qkv[j], attn_w_o[j])
        x = _peer_ffn(x, norm_ffn[i], peer_w_q[i], peer_k1[i], peer_k2[i], peer_u[i], peer_v[i],
                      norm_final, final_norm=(i == depth - 1))
    return x
```

```python
import functools

import jax
import jax.numpy as jnp
from jax import lax
from jax.experimental import pallas as pl
from jax.experimental.pallas import tpu as pltpu
from jax.experimental.pallas import tpu_sc as plsc

F32 = jnp.float32
BF16 = jnp.bfloat16
I32 = jnp.int32

RMS_EPS = 1e-6
N_HEADS = 16
HEAD_DIM = 64
MOBA_BLOCK = 256
MOBA_TOPK = 3
ROPE_THETA = 10000.0
PEER_HEADS = 8
PEER_NKEYS = 128
PEER_HALF = 64
PEER_TOPK = 16

LANES = 128
SUBLANES = 8
VMEM_LIMIT = 56 * 1024 * 1024
MASK_NEG = -1e9

TOKEN_TILE = 512
PEER_TILE = 128
OCTET = 8
SC_CORES = 2
SC_SUBCORES = 16
SC_LANES = 16
SC_BATCH = 16
SC_GROUP = 8
SC_TOKEN_PERCENT = 25
KV_CHUNK = 2


def _cparams(sem):
    return pltpu.CompilerParams(dimension_semantics=sem, vmem_limit_bytes=VMEM_LIMIT)


def _rms(x, g):
    ms = jnp.mean(x * x, axis=-1, keepdims=True)
    return x * lax.rsqrt(ms + RMS_EPS) * g


def _split(a):
    hi = a.astype(BF16)
    lo = (a - hi.astype(F32)).astype(BF16)
    return hi, lo


def _dot(a, b):
    return lax.dot_general(a, b, (((1,), (0,)), ((), ())), preferred_element_type=F32)


def _dot_t(a, b):
    return lax.dot_general(a, b, (((1,), (1,)), ((), ())), preferred_element_type=F32)


def _dot3(a_hi, a_lo, b_hi, b_lo):
    return _dot(a_hi, b_hi) + _dot(a_lo, b_hi) + _dot(a_hi, b_lo)


def _dot3_t(a_hi, a_lo, b_hi, b_lo):
    return _dot_t(a_hi, b_hi) + _dot_t(a_lo, b_hi) + _dot_t(a_hi, b_lo)


def _conv_mixer_kernel(x_ref, g_ref, win_ref, cw_ref, wout_ref, o_ref, ubuf_ref):
    tm, d = x_ref.shape[1], x_ref.shape[2]

    @pl.when(pl.program_id(1) == 0)
    def _():
        ubuf_ref[0:SUBLANES, :] = jnp.zeros((SUBLANES, d), F32)

    x = x_ref[0]
    hn = _rms(x, g_ref[...]).astype(BF16)
    bcz = _dot(hn, win_ref[...])
    b_gate, c_gate, z = bcz[:, :d], bcz[:, d:2 * d], bcz[:, 2 * d:]
    u = c_gate * z
    ubuf_ref[SUBLANES:SUBLANES + tm, :] = u
    u1 = ubuf_ref[SUBLANES - 1:SUBLANES - 1 + tm, :]
    u2 = ubuf_ref[SUBLANES - 2:SUBLANES - 2 + tm, :]
    cw = cw_ref[...]
    u_conv = cw[0:1, :] * u2 + cw[1:2, :] * u1 + cw[2:3, :] * u
    ubuf_ref[0:SUBLANES, :] = u[tm - SUBLANES:tm, :]
    y = (b_gate * u_conv).astype(BF16)
    o_ref[0] = x + _dot(y, wout_ref[...])


def _conv_mixer(x, g, w_in, conv_w, w_out, tm):
    b, s, d = x.shape
    return pl.pallas_call(
        _conv_mixer_kernel,
        grid=(b, s // tm),
        in_specs=[
            pl.BlockSpec((1, tm, d), lambda i, j: (i, j, 0)),
            pl.BlockSpec((1, d), lambda i, j: (0, 0)),
            pl.BlockSpec((d, 3 * d), lambda i, j: (0, 0)),
            pl.BlockSpec((3, d), lambda i, j: (0, 0)),
            pl.BlockSpec((d, d), lambda i, j: (0, 0)),
        ],
        out_specs=pl.BlockSpec((1, tm, d), lambda i, j: (i, j, 0)),
        out_shape=jax.ShapeDtypeStruct((b, s, d), F32),
        scratch_shapes=[pltpu.VMEM((tm + SUBLANES, d), F32)],
        compiler_params=_cparams(("arbitrary", "arbitrary")),
        name="conv_mixer",
    )(x, g.reshape(1, d), w_in.astype(BF16), conv_w, w_out.astype(BF16))


def _topk_rows(s, k, order=None, payload=None):
    if order is None:
        order = lax.broadcasted_iota(I32, s.shape, 0)
    big = jnp.iinfo(jnp.int32).max
    vals, outs = [], []
    for _ in range(k):
        m = jnp.max(s, axis=0, keepdims=True)
        i = jnp.min(jnp.where(s == m, order, big), axis=0, keepdims=True)
        pick = order == i
        vals.append(m)
        if payload is None:
            outs.append(i)
        else:
            outs.append(jnp.max(jnp.where(pick, payload, -1), axis=0, keepdims=True))
        s = jnp.where(pick, -jnp.inf, s)
    return jnp.concatenate(vals, axis=0), jnp.concatenate(outs, axis=0)


def _staircase(kk):
    groups = []
    for a in range(2):
        for b0 in range(0, kk // (a + 1), SUBLANES):
            groups.append((a, 0, b0, 1, lambda j, a=a, b0=b0: (a + 1) * (b0 + j + 1) <= kk))
    for b in range(kk // 3):
        for a0 in range(0, kk // (b + 1), SUBLANES):
            groups.append((a0, 1, b, 0, lambda j, a0=a0, b=b: (a0 + j >= 2) & ((a0 + j + 1) * (b + 1) <= kk)))
    return groups


def _staircase_topk(v1, i1, v2, i2, nk):
    kk, t = v1.shape
    j = lax.broadcasted_iota(I32, (SUBLANES, t), 0)

    def rows(x, x0, step):
        if step == 0:
            return jnp.broadcast_to(x[x0:x0 + 1, :], (SUBLANES, t))
        return x[x0:x0 + SUBLANES, :]

    cand, order, cidx = [], [], []
    for a0, a_step, b0, b_step, valid in _staircase(kk):
        ok = valid(j)
        cand.append(jnp.where(ok, rows(v1, a0, a_step) + rows(v2, b0, b_step), -jnp.inf))
        order.append(jnp.where(ok, (a0 + j * a_step) * kk + (b0 + j * b_step), jnp.iinfo(jnp.int32).max - 1))
        cidx.append(rows(i1, a0, a_step) * nk + rows(i2, b0, b_step))
    cat = lambda xs: jnp.concatenate(xs, axis=0)
    return _topk_rows(cat(cand), kk, order=cat(order), payload=cat(cidx))


def _router_kernel(x_ref, g_ref, wq_hi_ref, wq_lo_ref, k1_hi_ref, k1_lo_ref, k2_hi_ref, k2_lo_ref,
                   e_ref, gate_ref):
    table_rows = x_ref.shape[1] // (2 * LANES)
    xn = _rms(x_ref[...], g_ref[...])
    xh, xl = _split(xn)
    q = _dot3(xh, xl, wq_hi_ref[...], wq_lo_ref[...])
    nk = k1_hi_ref.shape[0]
    for h in range(PEER_HEADS):
        qh, ql = _split(q[:, h * LANES:(h + 1) * LANES])
        s1 = _dot3_t(k1_hi_ref[...], k1_lo_ref[...], qh, ql)
        s2 = _dot3_t(k2_hi_ref[...], k2_lo_ref[...], qh, ql)
        v1, i1 = _topk_rows(s1, PEER_TOPK)
        v2, i2 = _topk_rows(s2, PEER_TOPK)
        kk = PEER_TOPK
        top_s, e_idx = _staircase_topk(v1, i1, v2, i2, nk)
        p = jnp.exp(top_s - top_s[0:1, :])
        gate = p / jnp.sum(p, axis=0, keepdims=True)
        e_ref[0, h * kk:(h + 1) * kk, :] = e_idx * table_rows
        gate_ref[0, h * kk:(h + 1) * kk, :] = gate


def _pad_keys(k, lo):
    nk, half = k.shape
    out = jnp.zeros((nk, LANES), F32)
    return out.at[:, lo:lo + half].set(k)


def _peer_router(x2d, g, w_q, k1, k2):
    t, d = x2d.shape
    nt = t // PEER_TILE
    slots = PEER_HEADS * PEER_TOPK
    wq_hi, wq_lo = _split(w_q)
    k1_hi, k1_lo = _split(_pad_keys(k1, 0))
    k2_hi, k2_lo = _split(_pad_keys(k2, PEER_HALF))
    const = lambda i: (0, 0)
    kspec = pl.BlockSpec(k1_hi.shape, const)
    return pl.pallas_call(
        _router_kernel,
        grid=(nt,),
        in_specs=[
            pl.BlockSpec((PEER_TILE, d), lambda i: (i, 0)),
            pl.BlockSpec((1, d), const),
            pl.BlockSpec(wq_hi.shape, const),
            pl.BlockSpec(wq_lo.shape, const),
            kspec, kspec, kspec, kspec,
        ],
        out_specs=[
            pl.BlockSpec((1, slots, PEER_TILE), lambda i: (i, 0, 0)),
            pl.BlockSpec((1, slots, PEER_TILE), lambda i: (i, 0, 0)),
        ],
        out_shape=[
            jax.ShapeDtypeStruct((nt, slots, PEER_TILE), I32),
            jax.ShapeDtypeStruct((nt, slots, PEER_TILE), F32),
        ],
        compiler_params=_cparams(("arbitrary",)),
        name="peer_router",
    )(x2d, g.reshape(1, d), wq_hi, wq_lo, k1_hi, k1_lo, k2_hi, k2_lo)


def _gather_octet(e_ref, tab_ref, stage_ref, octet, nsub):
    t = e_ref.shape[2]
    for k in range(OCTET):
        for tok in range(t):
            off = e_ref[0, octet * OCTET + k, tok]
            stage_ref[k * (t // SUBLANES) + tok // SUBLANES,
                      pl.ds(tok % SUBLANES, nsub, stride=SUBLANES), :] = tab_ref[pl.ds(off, nsub), :]


def _octet_pipeline(n_octets, gather, consume, stage_a, stage_b):
    gather(0, stage_a)

    def pair(p, carry):
        gather(2 * p + 1, stage_b)
        consume(2 * p, stage_a)
        gather(jnp.minimum(2 * p + 2, n_octets - 1), stage_a)
        consume(2 * p + 1, stage_b)
        return carry
    lax.fori_loop(0, n_octets // 2, pair, 0)


def _score_kernel(e_ref, x_ref, g_ref, gate_ref, tab_ref, w_ref, stage_a, stage_b, a_ref):
    t, d = x_ref.shape
    nchunk = d // (2 * LANES)
    slots = e_ref.shape[1]
    rows = OCTET * t

    xn = _rms(x_ref[...], g_ref[...]).astype(BF16)
    x_even = jnp.concatenate([xn[:, (2 * c) * LANES:(2 * c + 1) * LANES] for c in range(nchunk)], axis=1)
    x_odd = jnp.concatenate([xn[:, (2 * c + 1) * LANES:(2 * c + 2) * LANES] for c in range(nchunk)], axis=1)
    rhs = jnp.concatenate([x_even, x_odd], axis=0)

    ri = lax.broadcasted_iota(I32, (2 * t, 2 * t), 0)
    ci = lax.broadcasted_iota(I32, (2 * t, 2 * t), 1)
    diag = (ci == (ri % 2) * t + ri // 2).astype(F32)

    def consume(o, stage_ref):
        planes = []
        for c in range(nchunk):
            plane = stage_ref[:, c * SUBLANES:(c + 1) * SUBLANES, :].reshape(rows, LANES)
            planes.append(pltpu.bitcast(plane, BF16))
        lhs = jnp.concatenate(planes, axis=1)
        prod = _dot_t(lhs, rhs)
        prod = prod.reshape(OCTET, 2 * t, 2 * t) * diag[None]
        a2 = jnp.sum(prod, axis=1)
        a_ref[pl.ds(pl.multiple_of(o * OCTET, OCTET), OCTET), :] = a2[:, :t] + a2[:, t:]

    gather = lambda o, stage_ref: _gather_octet(e_ref, tab_ref, stage_ref, o, nchunk)
    _octet_pipeline(slots // OCTET, gather, consume, stage_a, stage_b)

    a = a_ref[...]
    gelu = 0.5 * a * (1.0 + lax.erf(a * (2.0 ** -0.5)))
    w_ref[0] = (gate_ref[0] * gelu).T


def _peer_scores(e_t, gate_t, x2d, g, table):
    nt, slots, t = e_t.shape
    d = x2d.shape[1]
    stage = pltpu.VMEM((OCTET * t // SUBLANES, d // (2 * LANES) * SUBLANES, LANES), I32)
    return pl.pallas_call(
        _score_kernel,
        grid=(nt,),
        in_specs=[
            pl.BlockSpec((1, slots, t), lambda i: (i, 0, 0), memory_space=pltpu.SMEM),
            pl.BlockSpec((t, d), lambda i: (i, 0)),
            pl.BlockSpec((1, d), lambda i: (0, 0)),
            pl.BlockSpec((1, slots, t), lambda i: (i, 0, 0)),
            pl.BlockSpec(table.shape, lambda i: (0, 0), pipeline_mode=pl.Buffered(1)),
        ],
        out_specs=pl.BlockSpec((1, t, slots), lambda i: (i, 0, 0)),
        out_shape=jax.ShapeDtypeStruct((nt, t, slots), F32),
        scratch_shapes=[stage, stage, pltpu.VMEM((slots, t), F32)],
        compiler_params=_cparams(("arbitrary",)),
        name="peer_scores",
    )(e_t, x2d, g.reshape(1, d), gate_t, table)


def _value_kernel(e_ref, w_ref, x_ref, gf_ref, tab_ref, o_ref, stage_a, stage_b, wb_ref, acc_ref, *, final_norm):
    t, d = x_ref.shape
    nchunk = d // (2 * LANES)
    slots = e_ref.shape[1]
    groups = t // SUBLANES

    acc_ref[...] = x_ref[...]

    def consume(o, stage_ref):
        w_oct = pltpu.roll(w_ref[0], lax.rem(slots - o * OCTET, slots), 1)
        for k in range(OCTET):
            wb_ref[k] = jnp.broadcast_to(w_oct[:, k:k + 1], (t, LANES))
        for c in range(nchunk):
            lo_cols = slice(2 * c * LANES, (2 * c + 1) * LANES)
            hi_cols = slice((2 * c + 1) * LANES, (2 * c + 2) * LANES)
            acc_lo, acc_hi = acc_ref[:, lo_cols], acc_ref[:, hi_cols]
            for k in range(OCTET):
                words = stage_ref[k * groups:(k + 1) * groups, c * SUBLANES:(c + 1) * SUBLANES, :]
                words = words.reshape(t, LANES)
                lo = pltpu.bitcast(words << 16, F32)
                hi = pltpu.bitcast(words & jnp.int32(-65536), F32)
                acc_lo = acc_lo + wb_ref[k] * lo
                acc_hi = acc_hi + wb_ref[k] * hi
            acc_ref[:, lo_cols] = acc_lo
            acc_ref[:, hi_cols] = acc_hi

    gather = lambda o, stage_ref: _gather_octet(e_ref, tab_ref, stage_ref, o, nchunk)
    _octet_pipeline(slots // OCTET, gather, consume, stage_a, stage_b)

    y = acc_ref[...]
    if final_norm:
        y = _rms(y, gf_ref[...])
    o_ref[...] = y


def _peer_values(e_t, w_t, x2d, table, g_final, final_norm):
    nt, slots, t = e_t.shape
    d = x2d.shape[1]
    stage = pltpu.VMEM((OCTET * t // SUBLANES, d // (2 * LANES) * SUBLANES, LANES), I32)
    return pl.pallas_call(
        functools.partial(_value_kernel, final_norm=final_norm),
        grid=(nt,),
        in_specs=[
            pl.BlockSpec((1, slots, t), lambda i: (i, 0, 0), memory_space=pltpu.SMEM),
            pl.BlockSpec((1, t, slots), lambda i: (i, 0, 0)),
            pl.BlockSpec((t, d), lambda i: (i, 0)),
            pl.BlockSpec((1, d), lambda i: (0, 0)),
            pl.BlockSpec(table.shape, lambda i: (0, 0), pipeline_mode=pl.Buffered(1)),
        ],
        out_specs=pl.BlockSpec((t, d), lambda i: (i, 0)),
        out_shape=jax.ShapeDtypeStruct(x2d.shape, F32),
        scratch_shapes=[stage, stage, pltpu.VMEM((OCTET, t, LANES), F32), pltpu.VMEM((t, d), F32)],
        compiler_params=_cparams(("arbitrary",)),
        name="peer_values",
    )(e_t, w_t, x2d, g_final.reshape(1, d), table)


def _pack_pairs(w):
    e, d = w.shape
    bits = lax.bitcast_convert_type(w.astype(BF16), jnp.uint16).astype(jnp.uint32)
    bits = bits.reshape(e * d // (2 * LANES), 2 * LANES)
    packed = bits[:, :LANES] | (bits[:, LANES:] << 16)
    return lax.bitcast_convert_type(packed, I32)


def _sc_value_kernel(tab_hbm, idx_hbm, w_hbm, out_hbm, idx_v, w_v, rows_a, rows_b, acc_a, acc_b, row_sem, out_sem,
                     *, slots):
    n_tok = out_hbm.shape[0] // (SC_CORES * SC_SUBCORES)
    d = out_hbm.shape[1]
    words = d // 2
    win = slots // 2
    wid = lax.axis_index("s") * SC_CORES + lax.axis_index("c")
    base_tok = wid * n_tok

    def gather(tl, half, buf, sem):
        rows = idx_v.at[pl.ds(tl * slots + half * win, win)]
        return pltpu.make_async_copy(tab_hbm.at[rows], buf, sem)

    def accumulate(tl, half, buf, acc):
        for g in range(words // SC_LANES // SC_GROUP):
            def row(r, sums, g=g):
                slot = jnp.full((SC_LANES,), tl * slots + half * win + r, I32)
                wv = plsc.load_gather(w_v, [slot])
                out = []
                for j in range(SC_GROUP):
                    x = buf[r, pl.ds((g * SC_GROUP + j) * SC_LANES, SC_LANES)]
                    lo = lax.bitcast_convert_type(x << 16, F32)
                    hi = lax.bitcast_convert_type(x & jnp.int32(-65536), F32)
                    out += [sums[2 * j] + wv * lo, sums[2 * j + 1] + wv * hi]
                return tuple(out)
            zeros = tuple(jnp.zeros((SC_LANES,), F32) for _ in range(2 * SC_GROUP))
            sums = lax.fori_loop(0, win, row, zeros)
            for j in range(SC_GROUP):
                chunk, lane = divmod((g * SC_GROUP + j) * SC_LANES, LANES)
                plsc.addupdate(acc.at[pl.ds(2 * chunk * LANES + lane, SC_LANES)], sums[2 * j])
                plsc.addupdate(acc.at[pl.ds((2 * chunk + 1) * LANES + lane, SC_LANES)], sums[2 * j + 1])

    def write_out(acc, tok, sem):
        return pltpu.make_async_copy(acc, out_hbm.at[tok], sem)

    @pl.loop(0, n_tok // SC_BATCH)
    def _(bi):
        tok0 = base_tok + bi * SC_BATCH
        pltpu.sync_copy(idx_hbm.at[pl.ds(tok0 * slots, SC_BATCH * slots)], idx_v)
        pltpu.sync_copy(w_hbm.at[pl.ds(tok0 * slots, SC_BATCH * slots)], w_v)
        gather(0, 0, rows_a, row_sem.at[0]).start()

        @pl.loop(0, SC_BATCH // 2)
        def _(tp):
            for parity, acc in ((0, acc_a), (1, acc_b)):
                tl = 2 * tp + parity

                @pl.when(bi * SC_BATCH + tl >= 2)
                def _():
                    write_out(acc, tok0, out_sem.at[parity]).wait()
                for k in range(d // SC_LANES):
                    acc[pl.ds(k * SC_LANES, SC_LANES)] = jnp.zeros((SC_LANES,), F32)
                gather(tl, 1, rows_b, row_sem.at[1]).start()
                gather(tl, 0, rows_a, row_sem.at[0]).wait()
                accumulate(tl, 0, rows_a, acc)
                gather(jnp.minimum(tl + 1, SC_BATCH - 1), 0, rows_a, row_sem.at[0]).start()
                gather(tl, 1, rows_b, row_sem.at[1]).wait()
                accumulate(tl, 1, rows_b, acc)
                write_out(acc, tok0 + tl, out_sem.at[parity]).start()

        gather(SC_BATCH - 1, 0, rows_a, row_sem.at[0]).wait()

    write_out(acc_a, base_tok, out_sem.at[0]).wait()
    write_out(acc_b, base_tok, out_sem.at[1]).wait()


def _peer_values_sc(idx, w, table, n_tok, d):
    slots = idx.shape[0] // n_tok
    assert n_tok % (SC_CORES * SC_SUBCORES * SC_BATCH) == 0 and SC_BATCH % 2 == 0
    mesh = plsc.VectorSubcoreMesh(core_axis_name="c", subcore_axis_name="s",
                                  num_cores=SC_CORES, num_subcores=SC_SUBCORES)
    return pl.kernel(
        functools.partial(_sc_value_kernel, slots=slots),
        out_type=jax.ShapeDtypeStruct((n_tok, d), F32),
        mesh=mesh,
        scratch_types=[
            pltpu.VMEM((SC_BATCH * slots,), I32),
            pltpu.VMEM((SC_BATCH * slots,), F32),
            pltpu.VMEM((slots // 2, d // 2), I32),
            pltpu.VMEM((slots // 2, d // 2), I32),
            pltpu.VMEM((d,), F32),
            pltpu.VMEM((d,), F32),
            pltpu.SemaphoreType.DMA((2,)),
            pltpu.SemaphoreType.DMA((2,)),
        ],
        compiler_params=pltpu.CompilerParams(needs_layout_passes=False),
        name="peer_values_sc",
    )(table, idx, w)


def _residual_norm_kernel(x_ref, p_ref, gf_ref, o_ref, *, final_norm):
    y = x_ref[...] + p_ref[...]
    if final_norm:
        y = _rms(y, gf_ref[...])
    o_ref[...] = y


def _residual_norm(x2d, p2d, g_final, final_norm, tm):
    t, d = x2d.shape
    return pl.pallas_call(
        functools.partial(_residual_norm_kernel, final_norm=final_norm),
        grid=(t // tm,),
        in_specs=[
            pl.BlockSpec((tm, d), lambda i: (i, 0)),
            pl.BlockSpec((tm, d), lambda i: (i, 0)),
            pl.BlockSpec((1, d), lambda i: (0, 0)),
        ],
        out_specs=pl.BlockSpec((tm, d), lambda i: (i, 0)),
        out_shape=jax.ShapeDtypeStruct((t, d), F32),
        compiler_params=_cparams(("arbitrary",)),
        name="peer_residual",
    )(x2d, p2d, g_final.reshape(1, d))


def _peer_ffn(x, g, w_q, k1, k2, u_emb, v_emb, g_final, final_norm):
    b, s, d = x.shape
    x2d = x.reshape(b * s, d)
    e_slot, gate_slot = _peer_router(x2d, g, w_q, k1, k2)
    w_tok = _peer_scores(e_slot, gate_slot, x2d, g, _pack_pairs(u_emb))
    v_tab = _pack_pairs(v_emb)
    nt, slots, t = e_slot.shape
    nt_sc = nt * SC_TOKEN_PERCENT // 100
    nt_tc = nt - nt_sc
    out_tc = _peer_values(e_slot[:nt_tc], w_tok[:nt_tc], x2d[:nt_tc * t], v_tab, g_final, final_norm)
    if nt_sc == 0:
        return out_tc.reshape(b, s, d)
    rows_per_expert = d // (2 * LANES)
    idx_sc = (e_slot[nt_tc:] // rows_per_expert).transpose(0, 2, 1).reshape(-1)
    peer_sc = _peer_values_sc(idx_sc, w_tok[nt_tc:].reshape(-1), v_tab.reshape(-1, d // 2), nt_sc * t, d)
    out_sc = _residual_norm(x2d[nt_tc * t:], peer_sc, g_final, final_norm, min(TOKEN_TILE, nt_sc * t))
    return jnp.concatenate([out_tc, out_sc], axis=0).reshape(b, s, d)


def _qkv_kernel(x_ref, g_ref, wqk_hi_ref, wqk_lo_ref, wv_ref, cos_ref, sin_ref,
                qt_ref, k0_ref, k1_ref, vt_ref, km_ref):
    tm, d = x_ref.shape[1], x_ref.shape[2]
    hn = _rms(x_ref[0], g_ref[...])
    hh, hl = _split(hn)
    qk = _dot3(hh, hl, wqk_hi_ref[...], wqk_lo_ref[...])
    v = _dot(hh, wv_ref[...])
    cos = jnp.concatenate([cos_ref[...]] * (d // LANES), axis=1)
    sin = jnp.concatenate([sin_ref[...]] * (d // LANES), axis=1)
    lane = lax.broadcasted_iota(I32, (tm, d), 1)
    first_half = (lane % HEAD_DIM) < (HEAD_DIM // 2)

    def rope(a):
        rot = jnp.where(first_half, pltpu.roll(a, d - HEAD_DIM // 2, 1), pltpu.roll(a, HEAD_DIM // 2, 1))
        return a * cos + rot * sin

    q = rope(qk[:, :d])
    k = rope(qk[:, d:])
    qt_ref[0] = q.T
    vt_ref[0] = v.T.astype(BF16)
    nb = tm // MOBA_BLOCK
    km_ref[0, 0] = jnp.mean(k.reshape(nb, MOBA_BLOCK, d), axis=1)
    row = lax.broadcasted_iota(I32, (tm, d), 0)
    block = (pl.program_id(1) * tm + row) // MOBA_BLOCK
    pair_lane = lane % LANES
    kb = k.astype(BF16)
    k0_ref[0] = jnp.where(pair_lane < HEAD_DIM, kb, jnp.where(pair_lane - HEAD_DIM == block, 1.0, 0.0).astype(BF16))
    k1_ref[0] = jnp.where(pair_lane >= HEAD_DIM, kb, jnp.where(pair_lane == block, 1.0, 0.0).astype(BF16))


def _qkv_rope(x, g, w_qkv, tm):
    b, s, d = x.shape
    half = HEAD_DIM // 2
    inv = ROPE_THETA ** (-jnp.arange(half, dtype=F32) / half)
    ang = jnp.arange(s).astype(F32)[:, None] * inv[None, :]
    cos, sin = jnp.cos(ang), jnp.sin(ang)
    cos128 = jnp.tile(jnp.concatenate([cos, cos], axis=1), (1, LANES // HEAD_DIM))
    sin128 = jnp.tile(jnp.concatenate([-sin, sin], axis=1), (1, LANES // HEAD_DIM))
    wqk_hi, wqk_lo = _split(w_qkv[:, :2 * d])
    wv = w_qkv[:, 2 * d:].astype(BF16)
    nb = tm // MOBA_BLOCK
    const = lambda i, j: (0, 0)
    qt, k0, k1, vt, km = pl.pallas_call(
        _qkv_kernel,
        grid=(b, s // tm),
        in_specs=[
            pl.BlockSpec((1, tm, d), lambda i, j: (i, j, 0)),
            pl.BlockSpec((1, d), const),
            pl.BlockSpec((d, 2 * d), const),
            pl.BlockSpec((d, 2 * d), const),
            pl.BlockSpec((d, d), const),
            pl.BlockSpec((tm, LANES), lambda i, j: (j, 0)),
            pl.BlockSpec((tm, LANES), lambda i, j: (j, 0)),
        ],
        out_specs=[
            pl.BlockSpec((1, d, tm), lambda i, j: (i, 0, j)),
            pl.BlockSpec((1, tm, d), lambda i, j: (i, j, 0)),
            pl.BlockSpec((1, tm, d), lambda i, j: (i, j, 0)),
            pl.BlockSpec((1, d, tm), lambda i, j: (i, 0, j)),
            pl.BlockSpec((1, 1, nb, d), lambda i, j: (i, j, 0, 0)),
        ],
        out_shape=[
            jax.ShapeDtypeStruct((b, d, s), F32),
            jax.ShapeDtypeStruct((b, s, d), BF16),
            jax.ShapeDtypeStruct((b, s, d), BF16),
            jax.ShapeDtypeStruct((b, d, s), BF16),
            jax.ShapeDtypeStruct((b, s // tm, nb, d), F32),
        ],
        compiler_params=_cparams(("arbitrary", "arbitrary")),
        name="qkv_rope",
    )(x, g.reshape(1, d), wqk_hi, wqk_lo, wv, cos128, sin128)
    return qt, (k0, k1), vt, km.reshape(b, s // MOBA_BLOCK, d)


def _moba_kernel(qt_ref, k0_ref, k1_ref, vt_ref, km_ref, o_ref, sa_ref, sb_ref):
    bs = MOBA_BLOCK
    nb = km_ref.shape[1]
    n_heads = LANES // HEAD_DIM
    k_refs = (k0_ref, k1_ref)
    j = pl.program_id(2)
    qt = qt_ref[0]
    km = km_ref[0]
    scale = HEAD_DIM ** -0.5 * 1.4426950408889634
    lane_km = lax.broadcasted_iota(I32, (nb, LANES), 1)
    blk = lax.broadcasted_iota(I32, (nb, bs), 0)
    zeros_pad = jnp.zeros((LANES - HEAD_DIM - nb, bs), F32)
    qh, ql = _split(qt)

    own = pl.ds(pl.multiple_of(j * bs, bs), bs)
    v_own = vt_ref[0, :, own]
    krow = lax.broadcasted_iota(I32, (bs, bs), 0)
    qcol = lax.broadcasted_iota(I32, (bs, bs), 1)

    q_augs, state = [], []
    for hh in range(n_heads):
        head_lo = hh * HEAD_DIM
        in_head_km = (lane_km >= head_lo) & (lane_km < head_lo + HEAD_DIM)
        kmh, kml = _split(jnp.where(in_head_km, km, 0.0))
        gate = _dot3(kmh, kml, qh, ql)
        valid = blk < j
        gate = jnp.where(valid, gate, -jnp.inf)
        sel = jnp.zeros((nb, bs), F32)
        for _ in range(MOBA_TOPK):
            m = jnp.max(gate, axis=0, keepdims=True)
            i = jnp.min(jnp.where(gate == m, blk, nb), axis=0, keepdims=True)
            pick = blk == i
            sel = jnp.where(pick, 1.0, sel)
            gate = jnp.where(pick, -jnp.inf, gate)
        bias_t = jnp.where((sel > 0.0) & valid, 0.0, MASK_NEG)
        q_head = qt[head_lo:head_lo + HEAD_DIM, :] * scale
        no_bias = jnp.zeros((LANES - HEAD_DIM, bs), F32)
        if hh == 0:
            q_aug = jnp.concatenate([q_head, bias_t, zeros_pad], axis=0)
            q_own = jnp.concatenate([q_head, no_bias], axis=0)
        else:
            q_aug = jnp.concatenate([bias_t, zeros_pad, q_head], axis=0)
            q_own = jnp.concatenate([no_bias, q_head], axis=0)
        q_augs.append(q_aug.astype(BF16))

        s_own = jnp.where(krow <= qcol, _dot(k_refs[hh][0, own, :], q_own.astype(BF16)), -1e30)
        m0 = jnp.max(s_own, axis=0, keepdims=True)
        p0 = jnp.exp2(s_own - m0)
        state += [m0, jnp.sum(p0, axis=0, keepdims=True), _dot(v_own, p0.astype(BF16))]

    chunk = KV_CHUNK * bs
    last_chunk = nb // KV_CHUNK - 1

    def score_chunk(c, s_ref):
        rows = pl.ds(pl.multiple_of(jnp.minimum(c, last_chunk) * chunk, chunk), chunk)
        for hh in range(n_heads):
            s_ref[hh] = _dot(k_refs[hh][0, rows, :], q_augs[hh])

    def attend(c, s_ref, state):
        vn = vt_ref[0, :, pl.ds(pl.multiple_of(c * chunk, chunk), chunk)]
        new_state = []
        for hh in range(n_heads):
            m, l, acc = state[3 * hh:3 * hh + 3]
            s = s_ref[hh]
            m_new = jnp.maximum(m, jnp.max(s, axis=0, keepdims=True))
            alpha = jnp.exp2(m - m_new)
            p = jnp.exp2(s - m_new)
            l = alpha * l + jnp.sum(p, axis=0, keepdims=True)
            acc = alpha * acc + _dot(vn, p.astype(BF16))
            new_state += [m_new, l, acc]
        return tuple(new_state)

    score_chunk(0, sa_ref)

    def body(i, state):
        score_chunk(2 * i + 1, sb_ref)
        state = attend(2 * i, sa_ref, state)
        score_chunk(2 * i + 2, sa_ref)
        return attend(2 * i + 1, sb_ref, state)

    state = lax.fori_loop(0, (j + 2 * KV_CHUNK - 1) // (2 * KV_CHUNK), body, tuple(state))
    halves = []
    for hh in range(n_heads):
        _, l, acc = state[3 * hh:3 * hh + 3]
        halves.append((acc / l)[hh * HEAD_DIM:(hh + 1) * HEAD_DIM, :])
    o_ref[0] = jnp.concatenate(halves, axis=0).T


def _moba_attention(qt, k01, vt, km):
    b, d, s = qt.shape
    nb = s // MOBA_BLOCK
    assert nb % (2 * KV_CHUNK) == 0
    return pl.pallas_call(
        _moba_kernel,
        grid=(b, d // LANES, nb),
        in_specs=[
            pl.BlockSpec((1, LANES, MOBA_BLOCK), lambda i, h, j: (i, h, j)),
            pl.BlockSpec((1, s, LANES), lambda i, h, j: (i, 0, h)),
            pl.BlockSpec((1, s, LANES), lambda i, h, j: (i, 0, h)),
            pl.BlockSpec((1, LANES, s), lambda i, h, j: (i, h, 0)),
            pl.BlockSpec((1, nb, LANES), lambda i, h, j: (i, 0, h)),
        ],
        out_specs=pl.BlockSpec((1, MOBA_BLOCK, LANES), lambda i, h, j: (i, j, h)),
        out_shape=jax.ShapeDtypeStruct((b, s, d), F32),
        scratch_shapes=[pltpu.VMEM((LANES // HEAD_DIM, KV_CHUNK * MOBA_BLOCK, MOBA_BLOCK), F32)] * 2,
        compiler_params=_cparams(("arbitrary", "arbitrary", "arbitrary")),
        name="moba_attention",
    )(qt, *k01, vt, km)


def _proj_residual_kernel(x_ref, a_ref, w_ref, o_ref):
    o_ref[...] = x_ref[...] + _dot(a_ref[...].astype(BF16), w_ref[...])


def _proj_residual(x2d, a2d, w, tm):
    t, d = x2d.shape
    return pl.pallas_call(
        _proj_residual_kernel,
        grid=(t // tm,),
        in_specs=[
            pl.BlockSpec((tm, d), lambda i: (i, 0)),
            pl.BlockSpec((tm, a2d.shape[1]), lambda i: (i, 0)),
            pl.BlockSpec(w.shape, lambda i: (0, 0)),
        ],
        out_specs=pl.BlockSpec((tm, d), lambda i: (i, 0)),
        out_shape=jax.ShapeDtypeStruct((t, d), F32),
        compiler_params=_cparams(("arbitrary",)),
        name="attn_out_proj",
    )(x2d, a2d, w.astype(BF16))


def _moba_mixer(x, g, w_qkv, w_o):
    b, s, d = x.shape
    tm = min(TOKEN_TILE, s)
    qt, k, vt, km = _qkv_rope(x, g, w_qkv, tm)
    attn = _moba_attention(qt, k, vt, km)
    return _proj_residual(x.reshape(b * s, d), attn.reshape(b * s, d), w_o, tm).reshape(b, s, d)


def kernel(x, norm_mix, norm_ffn, conv_w_in, conv_w, conv_w_out, attn_w_qkv, attn_w_o,
           peer_w_q, peer_k1, peer_k2, peer_u, peer_v, norm_final):
    depth = norm_mix.shape[0]
    tm = min(TOKEN_TILE, x.shape[1])
    for i in range(depth):
        j = i // 2
        if i % 2 == 0:
            x = _conv_mixer(x, norm_mix[i], conv_w_in[j], conv_w[j], conv_w_out[j], tm)
        else:
            x = _moba_mixer(x, norm_mix[i], attn_w_qkv[j], attn_w_o[j])
        x = _peer_ffn(x, norm_ffn[i], peer_w_q[i], peer_k1[i], peer_k2[i], peer_u[i], peer_v[i],
                      norm_final, final_norm=(i == depth - 1))
    return x
```

```python
import functools
import math

import jax
import jax.numpy as jnp
from jax import lax
from jax.experimental import pallas as pl
from jax.experimental.pallas import tpu as pltpu
from jax.experimental.pallas import tpu_sc as plsc

F32 = jnp.float32
BF16 = jnp.bfloat16
I32 = jnp.int32

RMS_EPS = 1e-6
N_HEADS = 16
HEAD_DIM = 64
MOBA_BLOCK = 256
MOBA_TOPK = 3
ROPE_THETA = 10000.0
PEER_HEADS = 8
PEER_NKEYS = 128
PEER_HALF = 64
PEER_TOPK = 16

LANES = 128
SUBLANES = 8
VMEM_LIMIT = 56 * 1024 * 1024
MASK_NEG = -1e9

TOKEN_TILE = 512
PEER_TILE = 128
OCTET = 8
SC_CORES = 2
SC_SUBCORES = 16
SC_LANES = 16
SC_BATCH = 16
SC_GROUP = 8
SC_TOKEN_PERCENT = 33
KV_CHUNK = 2


def _cparams(sem):
    return pltpu.CompilerParams(dimension_semantics=sem, vmem_limit_bytes=VMEM_LIMIT)


def _rms(x, g):
    ms = jnp.mean(x * x, axis=-1, keepdims=True)
    return x * lax.rsqrt(ms + RMS_EPS) * g


def _split(a):
    hi = a.astype(BF16)
    lo = (a - hi.astype(F32)).astype(BF16)
    return hi, lo


def _dot(a, b):
    return lax.dot_general(a, b, (((1,), (0,)), ((), ())), preferred_element_type=F32)


def _dot_t(a, b):
    return lax.dot_general(a, b, (((1,), (1,)), ((), ())), preferred_element_type=F32)


def _dot3(a_hi, a_lo, b_hi, b_lo):
    return _dot(a_hi, b_hi) + _dot(a_lo, b_hi) + _dot(a_hi, b_lo)


def _dot3_t(a_hi, a_lo, b_hi, b_lo):
    return _dot_t(a_hi, b_hi) + _dot_t(a_lo, b_hi) + _dot_t(a_hi, b_lo)


def _conv_mixer_kernel(x_ref, g_ref, win_ref, cw_ref, wout_ref, o_ref, ubuf_ref):
    tm, d = x_ref.shape[1], x_ref.shape[2]

    @pl.when(pl.program_id(1) == 0)
    def _():
        ubuf_ref[0:SUBLANES, :] = jnp.zeros((SUBLANES, d), F32)

    x = x_ref[0]
    hn = _rms(x, g_ref[...]).astype(BF16)
    bcz = _dot(hn, win_ref[...])
    b_gate, c_gate, z = bcz[:, :d], bcz[:, d:2 * d], bcz[:, 2 * d:]
    u = c_gate * z
    ubuf_ref[SUBLANES:SUBLANES + tm, :] = u
    u1 = ubuf_ref[SUBLANES - 1:SUBLANES - 1 + tm, :]
    u2 = ubuf_ref[SUBLANES - 2:SUBLANES - 2 + tm, :]
    cw = cw_ref[...]
    u_conv = cw[0:1, :] * u2 + cw[1:2, :] * u1 + cw[2:3, :] * u
    ubuf_ref[0:SUBLANES, :] = u[tm - SUBLANES:tm, :]
    y = (b_gate * u_conv).astype(BF16)
    o_ref[0] = x + _dot(y, wout_ref[...])


def _conv_mixer(x, g, w_in, conv_w, w_out, tm):
    b, s, d = x.shape
    return pl.pallas_call(
        _conv_mixer_kernel,
        grid=(b, s // tm),
        in_specs=[
            pl.BlockSpec((1, tm, d), lambda i, j: (i, j, 0)),
            pl.BlockSpec((1, d), lambda i, j: (0, 0)),
            pl.BlockSpec((d, 3 * d), lambda i, j: (0, 0)),
            pl.BlockSpec((3, d), lambda i, j: (0, 0)),
            pl.BlockSpec((d, d), lambda i, j: (0, 0)),
        ],
        out_specs=pl.BlockSpec((1, tm, d), lambda i, j: (i, j, 0)),
        out_shape=jax.ShapeDtypeStruct((b, s, d), F32),
        scratch_shapes=[pltpu.VMEM((tm + SUBLANES, d), F32)],
        compiler_params=_cparams(("arbitrary", "arbitrary")),
        name="conv_mixer",
    )(x, g.reshape(1, d), w_in.astype(BF16), conv_w, w_out.astype(BF16))


def _topk_rows(s, k, order=None, payload=None):
    if order is None:
        order = lax.broadcasted_iota(I32, s.shape, 0)
    big = jnp.iinfo(jnp.int32).max
    vals, outs = [], []
    for _ in range(k):
        m = jnp.max(s, axis=0, keepdims=True)
        i = jnp.min(jnp.where(s == m, order, big), axis=0, keepdims=True)
        pick = order == i
        vals.append(m)
        if payload is None:
            outs.append(i)
        else:
            outs.append(jnp.max(jnp.where(pick, payload, -1), axis=0, keepdims=True))
        s = jnp.where(pick, -jnp.inf, s)
    return jnp.concatenate(vals, axis=0), jnp.concatenate(outs, axis=0)


def _staircase(kk):
    groups = []
    for a in range(2):
        for b0 in range(0, kk // (a + 1), SUBLANES):
            groups.append((a, 0, b0, 1, lambda j, a=a, b0=b0: (a + 1) * (b0 + j + 1) <= kk))
    for b in range(kk // 3):
        for a0 in range(0, kk // (b + 1), SUBLANES):
            groups.append((a0, 1, b, 0, lambda j, a0=a0, b=b: (a0 + j >= 2) & ((a0 + j + 1) * (b + 1) <= kk)))
    return groups


def _staircase_topk(v1, i1, v2, i2, nk):
    kk, t = v1.shape
    j = lax.broadcasted_iota(I32, (SUBLANES, t), 0)

    def rows(x, x0, step):
        if step == 0:
            return jnp.broadcast_to(x[x0:x0 + 1, :], (SUBLANES, t))
        return x[x0:x0 + SUBLANES, :]

    cand, order, cidx = [], [], []
    for a0, a_step, b0, b_step, valid in _staircase(kk):
        ok = valid(j)
        cand.append(jnp.where(ok, rows(v1, a0, a_step) + rows(v2, b0, b_step), -jnp.inf))
        order.append(jnp.where(ok, (a0 + j * a_step) * kk + (b0 + j * b_step), jnp.iinfo(jnp.int32).max - 1))
        cidx.append(rows(i1, a0, a_step) * nk + rows(i2, b0, b_step))
    cat = lambda xs: jnp.concatenate(xs, axis=0)
    return _topk_rows(cat(cand), kk, order=cat(order), payload=cat(cidx))


def _router_kernel(x_ref, g_ref, wq_hi_ref, wq_lo_ref, k1_hi_ref, k1_lo_ref, k2_hi_ref, k2_lo_ref,
                   e_ref, gate_ref):
    table_rows = x_ref.shape[1] // (2 * LANES)
    xn = _rms(x_ref[...], g_ref[...])
    xh, xl = _split(xn)
    q = _dot3(xh, xl, wq_hi_ref[...], wq_lo_ref[...])
    nk = k1_hi_ref.shape[0]
    for h in range(PEER_HEADS):
        qh, ql = _split(q[:, h * LANES:(h + 1) * LANES])
        s1 = _dot3_t(k1_hi_ref[...], k1_lo_ref[...], qh, ql)
        s2 = _dot3_t(k2_hi_ref[...], k2_lo_ref[...], qh, ql)
        v1, i1 = _topk_rows(s1, PEER_TOPK)
        v2, i2 = _topk_rows(s2, PEER_TOPK)
        kk = PEER_TOPK
        top_s, e_idx = _staircase_topk(v1, i1, v2, i2, nk)
        p = jnp.exp(top_s - top_s[0:1, :])
        gate = p / jnp.sum(p, axis=0, keepdims=True)
        e_ref[0, h * kk:(h + 1) * kk, :] = e_idx * table_rows
        gate_ref[0, h * kk:(h + 1) * kk, :] = gate


def _pad_keys(k, lo):
    nk, half = k.shape
    out = jnp.zeros((nk, LANES), F32)
    return out.at[:, lo:lo + half].set(k)


def _peer_router(x2d, g, w_q, k1, k2):
    t, d = x2d.shape
    nt = t // PEER_TILE
    slots = PEER_HEADS * PEER_TOPK
    wq_hi, wq_lo = _split(w_q)
    k1_hi, k1_lo = _split(_pad_keys(k1, 0))
    k2_hi, k2_lo = _split(_pad_keys(k2, PEER_HALF))
    const = lambda i: (0, 0)
    kspec = pl.BlockSpec(k1_hi.shape, const)
    return pl.pallas_call(
        _router_kernel,
        grid=(nt,),
        in_specs=[
            pl.BlockSpec((PEER_TILE, d), lambda i: (i, 0)),
            pl.BlockSpec((1, d), const),
            pl.BlockSpec(wq_hi.shape, const),
            pl.BlockSpec(wq_lo.shape, const),
            kspec, kspec, kspec, kspec,
        ],
        out_specs=[
            pl.BlockSpec((1, slots, PEER_TILE), lambda i: (i, 0, 0)),
            pl.BlockSpec((1, slots, PEER_TILE), lambda i: (i, 0, 0)),
        ],
        out_shape=[
            jax.ShapeDtypeStruct((nt, slots, PEER_TILE), I32),
            jax.ShapeDtypeStruct((nt, slots, PEER_TILE), F32),
        ],
        compiler_params=_cparams(("arbitrary",)),
        name="peer_router",
    )(x2d, g.reshape(1, d), wq_hi, wq_lo, k1_hi, k1_lo, k2_hi, k2_lo)


def _gather_octet(e_ref, tab_ref, stage_ref, octet, nsub):
    t = e_ref.shape[2]
    for k in range(OCTET):
        for tok in range(t):
            off = e_ref[0, octet * OCTET + k, tok]
            stage_ref[k * (t // SUBLANES) + tok // SUBLANES,
                      pl.ds(tok % SUBLANES, nsub, stride=SUBLANES), :] = tab_ref[pl.ds(off, nsub), :]


def _octet_pipeline(n_octets, gather, consume, stage_a, stage_b):
    gather(0, stage_a)

    def pair(p, carry):
        gather(2 * p + 1, stage_b)
        consume(2 * p, stage_a)
        gather(jnp.minimum(2 * p + 2, n_octets - 1), stage_a)
        consume(2 * p + 1, stage_b)
        return carry
    lax.fori_loop(0, n_octets // 2, pair, 0)


def _score_kernel(e_ref, x_ref, g_ref, gate_ref, tab_ref, w_ref, stage_a, stage_b, a_ref):
    t, d = x_ref.shape
    nchunk = d // (2 * LANES)
    slots = e_ref.shape[1]
    rows = OCTET * t

    xn = _rms(x_ref[...], g_ref[...]).astype(BF16)
    x_even = jnp.concatenate([xn[:, (2 * c) * LANES:(2 * c + 1) * LANES] for c in range(nchunk)], axis=1)
    x_odd = jnp.concatenate([xn[:, (2 * c + 1) * LANES:(2 * c + 2) * LANES] for c in range(nchunk)], axis=1)
    rhs = jnp.concatenate([x_even, x_odd], axis=0)

    ri = lax.broadcasted_iota(I32, (2 * t, 2 * t), 0)
    ci = lax.broadcasted_iota(I32, (2 * t, 2 * t), 1)
    diag = (ci == (ri % 2) * t + ri // 2).astype(F32)

    def consume(o, stage_ref):
        planes = []
        for c in range(nchunk):
            plane = stage_ref[:, c * SUBLANES:(c + 1) * SUBLANES, :].reshape(rows, LANES)
            planes.append(pltpu.bitcast(plane, BF16))
        lhs = jnp.concatenate(planes, axis=1)
        prod = _dot_t(lhs, rhs)
        prod = prod.reshape(OCTET, 2 * t, 2 * t) * diag[None]
        a2 = jnp.sum(prod, axis=1)
        a_ref[pl.ds(pl.multiple_of(o * OCTET, OCTET), OCTET), :] = a2[:, :t] + a2[:, t:]

    gather = lambda o, stage_ref: _gather_octet(e_ref, tab_ref, stage_ref, o, nchunk)
    _octet_pipeline(slots // OCTET, gather, consume, stage_a, stage_b)

    a = a_ref[...]
    gelu = 0.5 * a * (1.0 + lax.erf(a * (2.0 ** -0.5)))
    w_ref[0] = (gate_ref[0] * gelu).T


def _peer_scores(e_t, gate_t, x2d, g, table):
    nt, slots, t = e_t.shape
    d = x2d.shape[1]
    stage = pltpu.VMEM((OCTET * t // SUBLANES, d // (2 * LANES) * SUBLANES, LANES), I32)
    return pl.pallas_call(
        _score_kernel,
        grid=(nt,),
        in_specs=[
            pl.BlockSpec((1, slots, t), lambda i: (i, 0, 0), memory_space=pltpu.SMEM),
            pl.BlockSpec((t, d), lambda i: (i, 0)),
            pl.BlockSpec((1, d), lambda i: (0, 0)),
            pl.BlockSpec((1, slots, t), lambda i: (i, 0, 0)),
            pl.BlockSpec(table.shape, lambda i: (0, 0), pipeline_mode=pl.Buffered(1)),
        ],
        out_specs=pl.BlockSpec((1, t, slots), lambda i: (i, 0, 0)),
        out_shape=jax.ShapeDtypeStruct((nt, t, slots), F32),
        scratch_shapes=[stage, stage, pltpu.VMEM((slots, t), F32)],
        compiler_params=_cparams(("arbitrary",)),
        name="peer_scores",
    )(e_t, x2d, g.reshape(1, d), gate_t, table)


def _value_kernel(e_ref, w_ref, x_ref, gf_ref, tab_ref, o_ref, stage_a, stage_b, wb_ref, acc_ref, *, final_norm):
    t, d = x_ref.shape
    nchunk = d // (2 * LANES)
    slots = e_ref.shape[1]
    groups = t // SUBLANES

    acc_ref[...] = x_ref[...]

    def consume(o, stage_ref):
        w_oct = pltpu.roll(w_ref[0], lax.rem(slots - o * OCTET, slots), 1)
        for k in range(OCTET):
            wb_ref[k] = jnp.broadcast_to(w_oct[:, k:k + 1], (t, LANES))
        for c in range(nchunk):
            lo_cols = slice(2 * c * LANES, (2 * c + 1) * LANES)
            hi_cols = slice((2 * c + 1) * LANES, (2 * c + 2) * LANES)
            acc_lo, acc_hi = acc_ref[:, lo_cols], acc_ref[:, hi_cols]
            for k in range(OCTET):
                words = stage_ref[k * groups:(k + 1) * groups, c * SUBLANES:(c + 1) * SUBLANES, :]
                words = words.reshape(t, LANES)
                lo = pltpu.bitcast(words << 16, F32)
                hi = pltpu.bitcast(words & jnp.int32(-65536), F32)
                acc_lo = acc_lo + wb_ref[k] * lo
                acc_hi = acc_hi + wb_ref[k] * hi
            acc_ref[:, lo_cols] = acc_lo
            acc_ref[:, hi_cols] = acc_hi

    gather = lambda o, stage_ref: _gather_octet(e_ref, tab_ref, stage_ref, o, nchunk)
    _octet_pipeline(slots // OCTET, gather, consume, stage_a, stage_b)

    y = acc_ref[...]
    if final_norm:
        y = _rms(y, gf_ref[...])
    o_ref[...] = y


def _peer_values(e_t, w_t, x2d, table, g_final, final_norm, n_tiles):
    nt, slots, t = e_t.shape
    d = x2d.shape[1]
    stage = pltpu.VMEM((OCTET * t // SUBLANES, d // (2 * LANES) * SUBLANES, LANES), I32)
    return pl.pallas_call(
        functools.partial(_value_kernel, final_norm=final_norm),
        grid=(n_tiles,),
        in_specs=[
            pl.BlockSpec((1, slots, t), lambda i: (i, 0, 0), memory_space=pltpu.SMEM),
            pl.BlockSpec((1, t, slots), lambda i: (i, 0, 0)),
            pl.BlockSpec((t, d), lambda i: (i, 0)),
            pl.BlockSpec((1, d), lambda i: (0, 0)),
            pl.BlockSpec(table.shape, lambda i: (0, 0), pipeline_mode=pl.Buffered(1)),
        ],
        out_specs=pl.BlockSpec((t, d), lambda i: (i, 0)),
        out_shape=jax.ShapeDtypeStruct(x2d.shape, F32),
        scratch_shapes=[stage, stage, pltpu.VMEM((OCTET, t, LANES), F32), pltpu.VMEM((t, d), F32)],
        compiler_params=_cparams(("arbitrary",)),
        name="peer_values",
    )(e_t, w_t, x2d, g_final.reshape(1, d), table)


def _pack_pairs(w):
    e, d = w.shape
    bits = lax.bitcast_convert_type(w.astype(BF16), jnp.uint16).astype(jnp.uint32)
    bits = bits.reshape(e * d // (2 * LANES), 2 * LANES)
    packed = bits[:, :LANES] | (bits[:, LANES:] << 16)
    return lax.bitcast_convert_type(packed, I32)


def _sc_value_kernel(tab_hbm, idx_hbm, w_hbm, out_hbm, idx_v, w_v, rows_a, rows_b, acc_a, acc_b, row_sem, out_sem,
                     *, slots):
    n_tok = out_hbm.shape[0] // (SC_CORES * SC_SUBCORES)
    d = out_hbm.shape[1]
    words = d // 2
    win = slots // 2
    wid = lax.axis_index("s") * SC_CORES + lax.axis_index("c")
    base_tok = wid * n_tok

    def gather(tl, half, buf, sem):
        rows = idx_v.at[pl.ds(tl * slots + half * win, win)]
        return pltpu.make_async_copy(tab_hbm.at[rows], buf, sem)

    def accumulate(tl, half, buf, acc):
        for g in range(words // SC_LANES // SC_GROUP):
            def row(r, sums, g=g):
                slot = jnp.full((SC_LANES,), tl * slots + half * win + r, I32)
                wv = plsc.load_gather(w_v, [slot])
                out = []
                for j in range(SC_GROUP):
                    x = buf[r, pl.ds((g * SC_GROUP + j) * SC_LANES, SC_LANES)]
                    lo = lax.bitcast_convert_type(x << 16, F32)
                    hi = lax.bitcast_convert_type(x & jnp.int32(-65536), F32)
                    out += [sums[2 * j] + wv * lo, sums[2 * j + 1] + wv * hi]
                return tuple(out)
            zeros = tuple(jnp.zeros((SC_LANES,), F32) for _ in range(2 * SC_GROUP))
            sums = lax.fori_loop(0, win, row, zeros)
            for j in range(SC_GROUP):
                chunk, lane = divmod((g * SC_GROUP + j) * SC_LANES, LANES)
                plsc.addupdate(acc.at[pl.ds(2 * chunk * LANES + lane, SC_LANES)], sums[2 * j])
                plsc.addupdate(acc.at[pl.ds((2 * chunk + 1) * LANES + lane, SC_LANES)], sums[2 * j + 1])

    def write_out(acc, tok, sem):
        return pltpu.make_async_copy(acc, out_hbm.at[tok], sem)

    @pl.loop(0, n_tok // SC_BATCH)
    def _(bi):
        tok0 = base_tok + bi * SC_BATCH
        pltpu.sync_copy(idx_hbm.at[pl.ds(tok0 * slots, SC_BATCH * slots)], idx_v)
        pltpu.sync_copy(w_hbm.at[pl.ds(tok0 * slots, SC_BATCH * slots)], w_v)
        gather(0, 0, rows_a, row_sem.at[0]).start()

        @pl.loop(0, SC_BATCH // 2)
        def _(tp):
            for parity, acc in ((0, acc_a), (1, acc_b)):
                tl = 2 * tp + parity

                @pl.when(bi * SC_BATCH + tl >= 2)
                def _():
                    write_out(acc, tok0, out_sem.at[parity]).wait()
                for k in range(d // SC_LANES):
                    acc[pl.ds(k * SC_LANES, SC_LANES)] = jnp.zeros((SC_LANES,), F32)
                gather(tl, 1, rows_b, row_sem.at[1]).start()
                gather(tl, 0, rows_a, row_sem.at[0]).wait()
                accumulate(tl, 0, rows_a, acc)
                gather(jnp.minimum(tl + 1, SC_BATCH - 1), 0, rows_a, row_sem.at[0]).start()
                gather(tl, 1, rows_b, row_sem.at[1]).wait()
                accumulate(tl, 1, rows_b, acc)
                write_out(acc, tok0 + tl, out_sem.at[parity]).start()

        gather(SC_BATCH - 1, 0, rows_a, row_sem.at[0]).wait()

    write_out(acc_a, base_tok, out_sem.at[0]).wait()
    write_out(acc_b, base_tok, out_sem.at[1]).wait()


def _peer_values_sc(idx, w, table, n_tok, d):
    slots = idx.shape[0] // n_tok
    assert n_tok % (SC_CORES * SC_SUBCORES * SC_BATCH) == 0 and SC_BATCH % 2 == 0
    mesh = plsc.VectorSubcoreMesh(core_axis_name="c", subcore_axis_name="s",
                                  num_cores=SC_CORES, num_subcores=SC_SUBCORES)
    return pl.kernel(
        functools.partial(_sc_value_kernel, slots=slots),
        out_type=jax.ShapeDtypeStruct((n_tok, d), F32),
        mesh=mesh,
        scratch_types=[
            pltpu.VMEM((SC_BATCH * slots,), I32),
            pltpu.VMEM((SC_BATCH * slots,), F32),
            pltpu.VMEM((slots // 2, d // 2), I32),
            pltpu.VMEM((slots // 2, d // 2), I32),
            pltpu.VMEM((d,), F32),
            pltpu.VMEM((d,), F32),
            pltpu.SemaphoreType.DMA((2,)),
            pltpu.SemaphoreType.DMA((2,)),
        ],
        compiler_params=pltpu.CompilerParams(needs_layout_passes=False),
        name="peer_values_sc",
    )(table, idx, w)


def _residual_norm_kernel(out_hbm, x_ref, p_ref, gf_ref, o_ref, *, final_norm):
    del out_hbm
    y = x_ref[...] + p_ref[...]
    if final_norm:
        y = _rms(y, gf_ref[...])
    o_ref[...] = y


def _residual_norm(out_full, x2d, p2d, g_final, final_norm, tm):
    t, d = p2d.shape
    first = (x2d.shape[0] - t) // tm
    return pl.pallas_call(
        functools.partial(_residual_norm_kernel, final_norm=final_norm),
        grid=(t // tm,),
        in_specs=[
            pl.BlockSpec(memory_space=pl.ANY),
            pl.BlockSpec((tm, d), lambda i: (first + i, 0)),
            pl.BlockSpec((tm, d), lambda i: (i, 0)),
            pl.BlockSpec((1, d), lambda i: (0, 0)),
        ],
        out_specs=pl.BlockSpec((tm, d), lambda i: (first + i, 0)),
        out_shape=jax.ShapeDtypeStruct(x2d.shape, F32),
        input_output_aliases={0: 0},
        compiler_params=_cparams(("arbitrary",)),
        name="peer_residual",
    )(out_full, x2d, p2d, g_final.reshape(1, d))


def _peer_ffn(x, g, w_q, k1, k2, u_emb, v_emb, g_final, final_norm):
    b, s, d = x.shape
    x2d = x.reshape(b * s, d)
    e_slot, gate_slot = _peer_router(x2d, g, w_q, k1, k2)
    w_tok = _peer_scores(e_slot, gate_slot, x2d, g, _pack_pairs(u_emb))
    v_tab = _pack_pairs(v_emb)
    nt, slots, t = e_slot.shape
    sc_quantum = SC_CORES * SC_SUBCORES * SC_BATCH // t
    nt_sc = nt * SC_TOKEN_PERCENT // 100 // sc_quantum * sc_quantum
    nt_tc = nt - nt_sc
    out = _peer_values(e_slot, w_tok, x2d, v_tab, g_final, final_norm, nt_tc)
    if nt_sc:
        rows_per_expert = d // (2 * LANES)
        idx_sc = (e_slot[nt_tc:] // rows_per_expert).transpose(0, 2, 1).reshape(-1)
        peer_sc = _peer_values_sc(idx_sc, w_tok[nt_tc:].reshape(-1), v_tab.reshape(-1, d // 2), nt_sc * t, d)
        out = _residual_norm(out, x2d, peer_sc, g_final, final_norm, math.gcd(TOKEN_TILE, nt_sc * t))
    return out.reshape(b, s, d)


def _qkv_kernel(x_ref, g_ref, wqk_hi_ref, wqk_lo_ref, wv_ref, cos_ref, sin_ref,
                qt_ref, k0_ref, k1_ref, vt_ref, km_ref):
    tm, d = x_ref.shape[1], x_ref.shape[2]
    hn = _rms(x_ref[0], g_ref[...])
    hh, hl = _split(hn)
    qk = _dot3(hh, hl, wqk_hi_ref[...], wqk_lo_ref[...])
    v = _dot(hh, wv_ref[...])
    cos = jnp.concatenate([cos_ref[...]] * (d // LANES), axis=1)
    sin = jnp.concatenate([sin_ref[...]] * (d // LANES), axis=1)
    lane = lax.broadcasted_iota(I32, (tm, d), 1)
    first_half = (lane % HEAD_DIM) < (HEAD_DIM // 2)

    def rope(a):
        rot = jnp.where(first_half, pltpu.roll(a, d - HEAD_DIM // 2, 1), pltpu.roll(a, HEAD_DIM // 2, 1))
        return a * cos + rot * sin

    q = rope(qk[:, :d])
    k = rope(qk[:, d:])
    qt_ref[0] = q.T
    vt_ref[0] = v.T.astype(BF16)
    nb = tm // MOBA_BLOCK
    km_ref[0, 0] = jnp.mean(k.reshape(nb, MOBA_BLOCK, d), axis=1)
    row = lax.broadcasted_iota(I32, (tm, d), 0)
    block = (pl.program_id(1) * tm + row) // MOBA_BLOCK
    pair_lane = lane % LANES
    kb = k.astype(BF16)
    k0_ref[0] = jnp.where(pair_lane < HEAD_DIM, kb, jnp.where(pair_lane - HEAD_DIM == block, 1.0, 0.0).astype(BF16))
    k1_ref[0] = jnp.where(pair_lane >= HEAD_DIM, kb, jnp.where(pair_lane == block, 1.0, 0.0).astype(BF16))


def _qkv_rope(x, g, w_qkv, tm):
    b, s, d = x.shape
    half = HEAD_DIM // 2
    inv = ROPE_THETA ** (-jnp.arange(half, dtype=F32) / half)
    ang = jnp.arange(s).astype(F32)[:, None] * inv[None, :]
    cos, sin = jnp.cos(ang), jnp.sin(ang)
    cos128 = jnp.tile(jnp.concatenate([cos, cos], axis=1), (1, LANES // HEAD_DIM))
    sin128 = jnp.tile(jnp.concatenate([-sin, sin], axis=1), (1, LANES // HEAD_DIM))
    wqk_hi, wqk_lo = _split(w_qkv[:, :2 * d])
    wv = w_qkv[:, 2 * d:].astype(BF16)
    nb = tm // MOBA_BLOCK
    const = lambda i, j: (0, 0)
    qt, k0, k1, vt, km = pl.pallas_call(
        _qkv_kernel,
        grid=(b, s // tm),
        in_specs=[
            pl.BlockSpec((1, tm, d), lambda i, j: (i, j, 0)),
            pl.BlockSpec((1, d), const),
            pl.BlockSpec((d, 2 * d), const),
            pl.BlockSpec((d, 2 * d), const),
            pl.BlockSpec((d, d), const),
            pl.BlockSpec((tm, LANES), lambda i, j: (j, 0)),
            pl.BlockSpec((tm, LANES), lambda i, j: (j, 0)),
        ],
        out_specs=[
            pl.BlockSpec((1, d, tm), lambda i, j: (i, 0, j)),
            pl.BlockSpec((1, tm, d), lambda i, j: (i, j, 0)),
            pl.BlockSpec((1, tm, d), lambda i, j: (i, j, 0)),
            pl.BlockSpec((1, d, tm), lambda i, j: (i, 0, j)),
            pl.BlockSpec((1, 1, nb, d), lambda i, j: (i, j, 0, 0)),
        ],
        out_shape=[
            jax.ShapeDtypeStruct((b, d, s), F32),
            jax.ShapeDtypeStruct((b, s, d), BF16),
            jax.ShapeDtypeStruct((b, s, d), BF16),
            jax.ShapeDtypeStruct((b, d, s), BF16),
            jax.ShapeDtypeStruct((b, s // tm, nb, d), F32),
        ],
        compiler_params=_cparams(("arbitrary", "arbitrary")),
        name="qkv_rope",
    )(x, g.reshape(1, d), wqk_hi, wqk_lo, wv, cos128, sin128)
    return qt, (k0, k1), vt, km.reshape(b, s // MOBA_BLOCK, d)


def _moba_kernel(qt_ref, k0_ref, k1_ref, vt_ref, km_ref, o_ref, sa_ref, sb_ref):
    bs = MOBA_BLOCK
    nb = km_ref.shape[1]
    n_heads = LANES // HEAD_DIM
    k_refs = (k0_ref, k1_ref)
    j = pl.program_id(2)
    qt = qt_ref[0]
    km = km_ref[0]
    scale = HEAD_DIM ** -0.5 * 1.4426950408889634
    lane_km = lax.broadcasted_iota(I32, (nb, LANES), 1)
    blk = lax.broadcasted_iota(I32, (nb, bs), 0)
    zeros_pad = jnp.zeros((LANES - HEAD_DIM - nb, bs), F32)
    qh, ql = _split(qt)

    own = pl.ds(pl.multiple_of(j * bs, bs), bs)
    v_own = vt_ref[0, :, own]
    krow = lax.broadcasted_iota(I32, (bs, bs), 0)
    qcol = lax.broadcasted_iota(I32, (bs, bs), 1)

    q_augs, state = [], []
    for hh in range(n_heads):
        head_lo = hh * HEAD_DIM
        in_head_km = (lane_km >= head_lo) & (lane_km < head_lo + HEAD_DIM)
        kmh, kml = _split(jnp.where(in_head_km, km, 0.0))
        gate = _dot3(kmh, kml, qh, ql)
        valid = blk < j
        gate = jnp.where(valid, gate, -jnp.inf)
        sel = jnp.zeros((nb, bs), F32)
        for _ in range(MOBA_TOPK):
            m = jnp.max(gate, axis=0, keepdims=True)
            i = jnp.min(jnp.where(gate == m, blk, nb), axis=0, keepdims=True)
            pick = blk == i
            sel = jnp.where(pick, 1.0, sel)
            gate = jnp.where(pick, -jnp.inf, gate)
        bias_t = jnp.where((sel > 0.0) & valid, 0.0, MASK_NEG)
        q_head = qt[head_lo:head_lo + HEAD_DIM, :] * scale
        no_bias = jnp.zeros((LANES - HEAD_DIM, bs), F32)
        if hh == 0:
            q_aug = jnp.concatenate([q_head, bias_t, zeros_pad], axis=0)
            q_own = jnp.concatenate([q_head, no_bias], axis=0)
        else:
            q_aug = jnp.concatenate([bias_t, zeros_pad, q_head], axis=0)
            q_own = jnp.concatenate([no_bias, q_head], axis=0)
        q_augs.append(q_aug.astype(BF16))

        s_own = jnp.where(krow <= qcol, _dot(k_refs[hh][0, own, :], q_own.astype(BF16)), -1e30)
        m0 = jnp.max(s_own, axis=0, keepdims=True)
        p0 = jnp.exp2(s_own - m0)
        state += [m0, jnp.sum(p0, axis=0, keepdims=True), _dot(v_own, p0.astype(BF16))]

    chunk = KV_CHUNK * bs
    last_chunk = nb // KV_CHUNK - 1

    def score_chunk(c, s_ref):
        rows = pl.ds(pl.multiple_of(jnp.minimum(c, last_chunk) * chunk, chunk), chunk)
        for hh in range(n_heads):
            s_ref[hh] = _dot(k_refs[hh][0, rows, :], q_augs[hh])

    def attend(c, s_ref, state):
        vn = vt_ref[0, :, pl.ds(pl.multiple_of(c * chunk, chunk), chunk)]
        new_state = []
        for hh in range(n_heads):
            m, l, acc = state[3 * hh:3 * hh + 3]
            s = s_ref[hh]
            m_new = jnp.maximum(m, jnp.max(s, axis=0, keepdims=True))
            alpha = jnp.exp2(m - m_new)
            p = jnp.exp2(s - m_new)
            l = alpha * l + jnp.sum(p, axis=0, keepdims=True)
            acc = alpha * acc + _dot(vn, p.astype(BF16))
            new_state += [m_new, l, acc]
        return tuple(new_state)

    score_chunk(0, sa_ref)

    def body(i, state):
        score_chunk(2 * i + 1, sb_ref)
        state = attend(2 * i, sa_ref, state)
        score_chunk(2 * i + 2, sa_ref)
        return attend(2 * i + 1, sb_ref, state)

    state = lax.fori_loop(0, (j + 2 * KV_CHUNK - 1) // (2 * KV_CHUNK), body, tuple(state))
    halves = []
    for hh in range(n_heads):
        _, l, acc = state[3 * hh:3 * hh + 3]
        halves.append((acc / l)[hh * HEAD_DIM:(hh + 1) * HEAD_DIM, :])
    o_ref[0] = jnp.concatenate(halves, axis=0).T


def _moba_attention(qt, k01, vt, km):
    b, d, s = qt.shape
    nb = s // MOBA_BLOCK
    assert nb % (2 * KV_CHUNK) == 0
    return pl.pallas_call(
        _moba_kernel,
        grid=(b, d // LANES, nb),
        in_specs=[
            pl.BlockSpec((1, LANES, MOBA_BLOCK), lambda i, h, j: (i, h, j)),
            pl.BlockSpec((1, s, LANES), lambda i, h, j: (i, 0, h)),
            pl.BlockSpec((1, s, LANES), lambda i, h, j: (i, 0, h)),
            pl.BlockSpec((1, LANES, s), lambda i, h, j: (i, h, 0)),
            pl.BlockSpec((1, nb, LANES), lambda i, h, j: (i, 0, h)),
        ],
        out_specs=pl.BlockSpec((1, MOBA_BLOCK, LANES), lambda i, h, j: (i, j, h)),
        out_shape=jax.ShapeDtypeStruct((b, s, d), F32),
        scratch_shapes=[pltpu.VMEM((LANES // HEAD_DIM, KV_CHUNK * MOBA_BLOCK, MOBA_BLOCK), F32)] * 2,
        compiler_params=_cparams(("arbitrary", "arbitrary", "arbitrary")),
        name="moba_attention",
    )(qt, *k01, vt, km)


def _proj_residual_kernel(x_ref, a_ref, w_ref, o_ref):
    o_ref[...] = x_ref[...] + _dot(a_ref[...].astype(BF16), w_ref[...])


def _proj_residual(x2d, a2d, w, tm):
    t, d = x2d.shape
    return pl.pallas_call(
        _proj_residual_kernel,
        grid=(t // tm,),
        in_specs=[
            pl.BlockSpec((tm, d), lambda i: (i, 0)),
            pl.BlockSpec((tm, a2d.shape[1]), lambda i: (i, 0)),
            pl.BlockSpec(w.shape, lambda i: (0, 0)),
        ],
        out_specs=pl.BlockSpec((tm, d), lambda i: (i, 0)),
        out_shape=jax.ShapeDtypeStruct((t, d), F32),
        compiler_params=_cparams(("arbitrary",)),
        name="attn_out_proj",
    )(x2d, a2d, w.astype(BF16))


def _moba_mixer(x, g, w_qkv, w_o):
    b, s, d = x.shape
    tm = min(TOKEN_TILE, s)
    qt, k, vt, km = _qkv_rope(x, g, w_qkv, tm)
    attn = _moba_attention(qt, k, vt, km)
    return _proj_residual(x.reshape(b * s, d), attn.reshape(b * s, d), w_o, tm).reshape(b, s, d)


def kernel(x, norm_mix, norm_ffn, conv_w_in, conv_w, conv_w_out, attn_w_qkv, attn_w_o,
           peer_w_q, peer_k1, peer_k2, peer_u, peer_v, norm_final):
    depth = norm_mix.shape[0]
    tm = min(TOKEN_TILE, x.shape[1])
    for i in range(depth):
        j = i // 2
        if i % 2 == 0:
            x = _conv_mixer(x, norm_mix[i], conv_w_in[j], conv_w[j], conv_w_out[j], tm)
        else:
            x = _moba_mixer(x, norm_mix[i], attn_w_qkv[j], attn_w_o[j])
        x = _peer_ffn(x, norm_ffn[i], peer_w_q[i], peer_k1[i], peer_k2[i], peer_u[i], peer_v[i],
                      norm_final, final_norm=(i == depth - 1))
    return x
```

```python
import functools
import math

import jax
import jax.numpy as jnp
from jax import lax
from jax.experimental import pallas as pl
from jax.experimental.pallas import tpu as pltpu
from jax.experimental.pallas import tpu_sc as plsc

F32 = jnp.float32
BF16 = jnp.bfloat16
I32 = jnp.int32

RMS_EPS = 1e-6
N_HEADS = 16
HEAD_DIM = 64
MOBA_BLOCK = 256
MOBA_TOPK = 3
ROPE_THETA = 10000.0
PEER_HEADS = 8
PEER_NKEYS = 128
PEER_HALF = 64
PEER_TOPK = 16

LANES = 128
SUBLANES = 8
VMEM_LIMIT = 56 * 1024 * 1024
MASK_NEG = -1e9

TOKEN_TILE = 512
PEER_TILE = 128
OCTET = 8
SC_CORES = 2
SC_SUBCORES = 16
SC_LANES = 16
SC_BATCH = 16
SC_GROUP = 8
SC_TOKEN_PERCENT = 33
KV_CHUNK = 2


def _cparams(sem):
    return pltpu.CompilerParams(dimension_semantics=sem, vmem_limit_bytes=VMEM_LIMIT)


def _rms(x, g):
    ms = jnp.mean(x * x, axis=-1, keepdims=True)
    return x * lax.rsqrt(ms + RMS_EPS) * g


def _split(a):
    hi = a.astype(BF16)
    lo = (a - hi.astype(F32)).astype(BF16)
    return hi, lo


def _dot(a, b):
    return lax.dot_general(a, b, (((1,), (0,)), ((), ())), preferred_element_type=F32)


def _dot_t(a, b):
    return lax.dot_general(a, b, (((1,), (1,)), ((), ())), preferred_element_type=F32)


def _dot3(a_hi, a_lo, b_hi, b_lo):
    return _dot(a_hi, b_hi) + _dot(a_lo, b_hi) + _dot(a_hi, b_lo)


def _dot3_t(a_hi, a_lo, b_hi, b_lo):
    return _dot_t(a_hi, b_hi) + _dot_t(a_lo, b_hi) + _dot_t(a_hi, b_lo)


def _conv_mixer_kernel(x_ref, g_ref, win_ref, cw_ref, wout_ref, o_ref, ubuf_ref):
    tm, d = x_ref.shape[1], x_ref.shape[2]

    @pl.when(pl.program_id(1) == 0)
    def _():
        ubuf_ref[0:SUBLANES, :] = jnp.zeros((SUBLANES, d), F32)

    x = x_ref[0]
    hn = _rms(x, g_ref[...]).astype(BF16)
    bcz = _dot(hn, win_ref[...])
    b_gate, c_gate, z = bcz[:, :d], bcz[:, d:2 * d], bcz[:, 2 * d:]
    u = c_gate * z
    ubuf_ref[SUBLANES:SUBLANES + tm, :] = u
    u1 = ubuf_ref[SUBLANES - 1:SUBLANES - 1 + tm, :]
    u2 = ubuf_ref[SUBLANES - 2:SUBLANES - 2 + tm, :]
    cw = cw_ref[...]
    u_conv = cw[0:1, :] * u2 + cw[1:2, :] * u1 + cw[2:3, :] * u
    ubuf_ref[0:SUBLANES, :] = u[tm - SUBLANES:tm, :]
    y = (b_gate * u_conv).astype(BF16)
    o_ref[0] = x + _dot(y, wout_ref[...])


def _conv_mixer(x, g, w_in, conv_w, w_out, tm):
    b, s, d = x.shape
    return pl.pallas_call(
        _conv_mixer_kernel,
        grid=(b, s // tm),
        in_specs=[
            pl.BlockSpec((1, tm, d), lambda i, j: (i, j, 0)),
            pl.BlockSpec((1, d), lambda i, j: (0, 0)),
            pl.BlockSpec((d, 3 * d), lambda i, j: (0, 0)),
            pl.BlockSpec((3, d), lambda i, j: (0, 0)),
            pl.BlockSpec((d, d), lambda i, j: (0, 0)),
        ],
        out_specs=pl.BlockSpec((1, tm, d), lambda i, j: (i, j, 0)),
        out_shape=jax.ShapeDtypeStruct((b, s, d), F32),
        scratch_shapes=[pltpu.VMEM((tm + SUBLANES, d), F32)],
        compiler_params=_cparams(("arbitrary", "arbitrary")),
        name="conv_mixer",
    )(x, g.reshape(1, d), w_in.astype(BF16), conv_w, w_out.astype(BF16))


def _topk_rows(s, k, order=None, payload=None):
    if order is None:
        order = lax.broadcasted_iota(I32, s.shape, 0)
    big = jnp.iinfo(jnp.int32).max
    vals, outs = [], []
    for _ in range(k):
        m = jnp.max(s, axis=0, keepdims=True)
        i = jnp.min(jnp.where(s == m, order, big), axis=0, keepdims=True)
        pick = order == i
        vals.append(m)
        if payload is None:
            outs.append(i)
        else:
            outs.append(jnp.max(jnp.where(pick, payload, -1), axis=0, keepdims=True))
        s = jnp.where(pick, -jnp.inf, s)
    return jnp.concatenate(vals, axis=0), jnp.concatenate(outs, axis=0)


def _staircase(kk):
    groups = []
    for a in range(2):
        for b0 in range(0, kk // (a + 1), SUBLANES):
            groups.append((a, 0, b0, 1, lambda j, a=a, b0=b0: (a + 1) * (b0 + j + 1) <= kk))
    for b in range(kk // 3):
        for a0 in range(0, kk // (b + 1), SUBLANES):
            groups.append((a0, 1, b, 0, lambda j, a0=a0, b=b: (a0 + j >= 2) & ((a0 + j + 1) * (b + 1) <= kk)))
    return groups


def _staircase_topk(v1, i1, v2, i2, nk):
    kk, t = v1.shape
    j = lax.broadcasted_iota(I32, (SUBLANES, t), 0)

    def rows(x, x0, step):
        if step == 0:
            return jnp.broadcast_to(x[x0:x0 + 1, :], (SUBLANES, t))
        return x[x0:x0 + SUBLANES, :]

    cand, order, cidx = [], [], []
    for a0, a_step, b0, b_step, valid in _staircase(kk):
        ok = valid(j)
        cand.append(jnp.where(ok, rows(v1, a0, a_step) + rows(v2, b0, b_step), -jnp.inf))
        order.append(jnp.where(ok, (a0 + j * a_step) * kk + (b0 + j * b_step), jnp.iinfo(jnp.int32).max - 1))
        cidx.append(rows(i1, a0, a_step) * nk + rows(i2, b0, b_step))
    cat = lambda xs: jnp.concatenate(xs, axis=0)
    return _topk_rows(cat(cand), kk, order=cat(order), payload=cat(cidx))


def _router_kernel(x_ref, g_ref, wq_hi_ref, wq_lo_ref, k1_hi_ref, k1_lo_ref, k2_hi_ref, k2_lo_ref,
                   e_ref, gate_ref):
    table_rows = x_ref.shape[1] // (2 * LANES)
    xn = _rms(x_ref[...], g_ref[...])
    xh, xl = _split(xn)
    q = _dot3(xh, xl, wq_hi_ref[...], wq_lo_ref[...])
    nk = k1_hi_ref.shape[0]
    for h in range(PEER_HEADS):
        qh, ql = _split(q[:, h * LANES:(h + 1) * LANES])
        s1 = _dot3_t(k1_hi_ref[...], k1_lo_ref[...], qh, ql)
        s2 = _dot3_t(k2_hi_ref[...], k2_lo_ref[...], qh, ql)
        v1, i1 = _topk_rows(s1, PEER_TOPK)
        v2, i2 = _topk_rows(s2, PEER_TOPK)
        kk = PEER_TOPK
        top_s, e_idx = _staircase_topk(v1, i1, v2, i2, nk)
        p = jnp.exp(top_s - top_s[0:1, :])
        gate = p / jnp.sum(p, axis=0, keepdims=True)
        e_ref[0, h * kk:(h + 1) * kk, :] = e_idx * table_rows
        gate_ref[0, h * kk:(h + 1) * kk, :] = gate


def _pad_keys(k, lo):
    nk, half = k.shape
    out = jnp.zeros((nk, LANES), F32)
    return out.at[:, lo:lo + half].set(k)


def _peer_router(x2d, g, w_q, k1, k2):
    t, d = x2d.shape
    nt = t // PEER_TILE
    slots = PEER_HEADS * PEER_TOPK
    wq_hi, wq_lo = _split(w_q)
    k1_hi, k1_lo = _split(_pad_keys(k1, 0))
    k2_hi, k2_lo = _split(_pad_keys(k2, PEER_HALF))
    const = lambda i: (0, 0)
    kspec = pl.BlockSpec(k1_hi.shape, const)
    return pl.pallas_call(
        _router_kernel,
        grid=(nt,),
        in_specs=[
            pl.BlockSpec((PEER_TILE, d), lambda i: (i, 0)),
            pl.BlockSpec((1, d), const),
            pl.BlockSpec(wq_hi.shape, const),
            pl.BlockSpec(wq_lo.shape, const),
            kspec, kspec, kspec, kspec,
        ],
        out_specs=[
            pl.BlockSpec((1, slots, PEER_TILE), lambda i: (i, 0, 0)),
            pl.BlockSpec((1, slots, PEER_TILE), lambda i: (i, 0, 0)),
        ],
        out_shape=[
            jax.ShapeDtypeStruct((nt, slots, PEER_TILE), I32),
            jax.ShapeDtypeStruct((nt, slots, PEER_TILE), F32),
        ],
        compiler_params=_cparams(("arbitrary",)),
        name="peer_router",
    )(x2d, g.reshape(1, d), wq_hi, wq_lo, k1_hi, k1_lo, k2_hi, k2_lo)


def _gather_octet(e_ref, tab_ref, stage_ref, octet, nsub):
    t = e_ref.shape[2]
    for k in range(OCTET):
        for tok in range(t):
            off = e_ref[0, octet * OCTET + k, tok]
            stage_ref[k * (t // SUBLANES) + tok // SUBLANES,
                      pl.ds(tok % SUBLANES, nsub, stride=SUBLANES), :] = tab_ref[pl.ds(off, nsub), :]


def _octet_pipeline(n_octets, gather, consume, stage_a, stage_b):
    gather(0, stage_a)

    def pair(p, carry):
        gather(2 * p + 1, stage_b)
        consume(2 * p, stage_a)
        gather(jnp.minimum(2 * p + 2, n_octets - 1), stage_a)
        consume(2 * p + 1, stage_b)
        return carry
    lax.fori_loop(0, n_octets // 2, pair, 0)


def _score_kernel(e_ref, x_ref, g_ref, gate_ref, tab_ref, w_ref, stage_a, stage_b, a_ref):
    t, d = x_ref.shape
    nchunk = d // (2 * LANES)
    slots = e_ref.shape[1]
    rows = OCTET * t

    xn = _rms(x_ref[...], g_ref[...]).astype(BF16)
    x_even = jnp.concatenate([xn[:, (2 * c) * LANES:(2 * c + 1) * LANES] for c in range(nchunk)], axis=1)
    x_odd = jnp.concatenate([xn[:, (2 * c + 1) * LANES:(2 * c + 2) * LANES] for c in range(nchunk)], axis=1)
    rhs = jnp.concatenate([x_even, x_odd], axis=0)

    ri = lax.broadcasted_iota(I32, (2 * t, 2 * t), 0)
    ci = lax.broadcasted_iota(I32, (2 * t, 2 * t), 1)
    diag = (ci == (ri % 2) * t + ri // 2).astype(F32)

    def consume(o, stage_ref):
        planes = []
        for c in range(nchunk):
            plane = stage_ref[:, c * SUBLANES:(c + 1) * SUBLANES, :].reshape(rows, LANES)
            planes.append(pltpu.bitcast(plane, BF16))
        lhs = jnp.concatenate(planes, axis=1)
        prod = _dot_t(lhs, rhs)
        prod = prod.reshape(OCTET, 2 * t, 2 * t) * diag[None]
        a2 = jnp.sum(prod, axis=1)
        a_ref[pl.ds(pl.multiple_of(o * OCTET, OCTET), OCTET), :] = a2[:, :t] + a2[:, t:]

    gather = lambda o, stage_ref: _gather_octet(e_ref, tab_ref, stage_ref, o, nchunk)
    _octet_pipeline(slots // OCTET, gather, consume, stage_a, stage_b)

    a = a_ref[...]
    gelu = 0.5 * a * (1.0 + lax.erf(a * (2.0 ** -0.5)))
    w_ref[0] = (gate_ref[0] * gelu).T


def _peer_scores(e_t, gate_t, x2d, g, table, n_tiles):
    nt, slots, t = e_t.shape
    d = x2d.shape[1]
    stage = pltpu.VMEM((OCTET * t // SUBLANES, d // (2 * LANES) * SUBLANES, LANES), I32)
    return pl.pallas_call(
        _score_kernel,
        grid=(n_tiles,),
        in_specs=[
            pl.BlockSpec((1, slots, t), lambda i: (i, 0, 0), memory_space=pltpu.SMEM),
            pl.BlockSpec((t, d), lambda i: (i, 0)),
            pl.BlockSpec((1, d), lambda i: (0, 0)),
            pl.BlockSpec((1, slots, t), lambda i: (i, 0, 0)),
            pl.BlockSpec(table.shape, lambda i: (0, 0), pipeline_mode=pl.Buffered(1)),
        ],
        out_specs=pl.BlockSpec((1, t, slots), lambda i: (i, 0, 0)),
        out_shape=jax.ShapeDtypeStruct((nt, t, slots), F32),
        scratch_shapes=[stage, stage, pltpu.VMEM((slots, t), F32)],
        compiler_params=_cparams(("arbitrary",)),
        name="peer_scores",
    )(e_t, x2d, g.reshape(1, d), gate_t, table)


def _value_kernel(e_ref, w_ref, x_ref, gf_ref, tab_ref, o_ref, stage_a, stage_b, wb_ref, acc_ref, *, final_norm):
    t, d = x_ref.shape
    nchunk = d // (2 * LANES)
    slots = e_ref.shape[1]
    groups = t // SUBLANES

    acc_ref[...] = x_ref[...]

    def consume(o, stage_ref):
        w_oct = pltpu.roll(w_ref[0], lax.rem(slots - o * OCTET, slots), 1)
        for k in range(OCTET):
            wb_ref[k] = jnp.broadcast_to(w_oct[:, k:k + 1], (t, LANES))
        for c in range(nchunk):
            lo_cols = slice(2 * c * LANES, (2 * c + 1) * LANES)
            hi_cols = slice((2 * c + 1) * LANES, (2 * c + 2) * LANES)
            acc_lo, acc_hi = acc_ref[:, lo_cols], acc_ref[:, hi_cols]
            for k in range(OCTET):
                words = stage_ref[k * groups:(k + 1) * groups, c * SUBLANES:(c + 1) * SUBLANES, :]
                words = words.reshape(t, LANES)
                lo = pltpu.bitcast(words << 16, F32)
                hi = pltpu.bitcast(words & jnp.int32(-65536), F32)
                acc_lo = acc_lo + wb_ref[k] * lo
                acc_hi = acc_hi + wb_ref[k] * hi
            acc_ref[:, lo_cols] = acc_lo
            acc_ref[:, hi_cols] = acc_hi

    gather = lambda o, stage_ref: _gather_octet(e_ref, tab_ref, stage_ref, o, nchunk)
    _octet_pipeline(slots // OCTET, gather, consume, stage_a, stage_b)

    y = acc_ref[...]
    if final_norm:
        y = _rms(y, gf_ref[...])
    o_ref[...] = y


def _peer_values(e_t, w_t, x2d, table, g_final, final_norm, n_tiles):
    nt, slots, t = e_t.shape
    d = x2d.shape[1]
    stage = pltpu.VMEM((OCTET * t // SUBLANES, d // (2 * LANES) * SUBLANES, LANES), I32)
    return pl.pallas_call(
        functools.partial(_value_kernel, final_norm=final_norm),
        grid=(n_tiles,),
        in_specs=[
            pl.BlockSpec((1, slots, t), lambda i: (i, 0, 0), memory_space=pltpu.SMEM),
            pl.BlockSpec((1, t, slots), lambda i: (i, 0, 0)),
            pl.BlockSpec((t, d), lambda i: (i, 0)),
            pl.BlockSpec((1, d), lambda i: (0, 0)),
            pl.BlockSpec(table.shape, lambda i: (0, 0), pipeline_mode=pl.Buffered(1)),
        ],
        out_specs=pl.BlockSpec((t, d), lambda i: (i, 0)),
        out_shape=jax.ShapeDtypeStruct(x2d.shape, F32),
        scratch_shapes=[stage, stage, pltpu.VMEM((OCTET, t, LANES), F32), pltpu.VMEM((t, d), F32)],
        compiler_params=_cparams(("arbitrary",)),
        name="peer_values",
    )(e_t, w_t, x2d, g_final.reshape(1, d), table)


def _pack_pairs(w):
    e, d = w.shape
    bits = lax.bitcast_convert_type(w.astype(BF16), jnp.uint16).astype(jnp.uint32)
    bits = bits.reshape(e * d // (2 * LANES), 2 * LANES)
    packed = bits[:, :LANES] | (bits[:, LANES:] << 16)
    return lax.bitcast_convert_type(packed, I32)


def _sc_value_kernel(tab_hbm, idx_hbm, w_hbm, out_hbm, idx_v, w_v, rows_a, rows_b, acc_a, acc_b, row_sem, out_sem,
                     *, slots):
    n_tok = out_hbm.shape[0] // (SC_CORES * SC_SUBCORES)
    d = out_hbm.shape[1]
    words = d // 2
    win = slots // 2
    wid = lax.axis_index("s") * SC_CORES + lax.axis_index("c")
    base_tok = wid * n_tok

    def gather(tl, half, buf, sem):
        rows = idx_v.at[pl.ds(tl * slots + half * win, win)]
        return pltpu.make_async_copy(tab_hbm.at[rows], buf, sem)

    def accumulate(tl, half, buf, acc):
        for g in range(words // SC_LANES // SC_GROUP):
            def row(r, sums, g=g):
                slot = jnp.full((SC_LANES,), tl * slots + half * win + r, I32)
                wv = plsc.load_gather(w_v, [slot])
                out = []
                for j in range(SC_GROUP):
                    x = buf[r, pl.ds((g * SC_GROUP + j) * SC_LANES, SC_LANES)]
                    lo = lax.bitcast_convert_type(x << 16, F32)
                    hi = lax.bitcast_convert_type(x & jnp.int32(-65536), F32)
                    out += [sums[2 * j] + wv * lo, sums[2 * j + 1] + wv * hi]
                return tuple(out)
            zeros = tuple(jnp.zeros((SC_LANES,), F32) for _ in range(2 * SC_GROUP))
            sums = lax.fori_loop(0, win, row, zeros)
            for j in range(SC_GROUP):
                chunk, lane = divmod((g * SC_GROUP + j) * SC_LANES, LANES)
                plsc.addupdate(acc.at[pl.ds(2 * chunk * LANES + lane, SC_LANES)], sums[2 * j])
                plsc.addupdate(acc.at[pl.ds((2 * chunk + 1) * LANES + lane, SC_LANES)], sums[2 * j + 1])

    def write_out(acc, tok, sem):
        return pltpu.make_async_copy(acc, out_hbm.at[tok], sem)

    @pl.loop(0, n_tok // SC_BATCH)
    def _(bi):
        tok0 = base_tok + bi * SC_BATCH
        pltpu.sync_copy(idx_hbm.at[pl.ds(tok0 * slots, SC_BATCH * slots)], idx_v)
        pltpu.sync_copy(w_hbm.at[pl.ds(tok0 * slots, SC_BATCH * slots)], w_v)
        gather(0, 0, rows_a, row_sem.at[0]).start()

        @pl.loop(0, SC_BATCH // 2)
        def _(tp):
            for parity, acc in ((0, acc_a), (1, acc_b)):
                tl = 2 * tp + parity

                @pl.when(bi * SC_BATCH + tl >= 2)
                def _():
                    write_out(acc, tok0, out_sem.at[parity]).wait()
                for k in range(d // SC_LANES):
                    acc[pl.ds(k * SC_LANES, SC_LANES)] = jnp.zeros((SC_LANES,), F32)
                gather(tl, 1, rows_b, row_sem.at[1]).start()
                gather(tl, 0, rows_a, row_sem.at[0]).wait()
                accumulate(tl, 0, rows_a, acc)
                gather(jnp.minimum(tl + 1, SC_BATCH - 1), 0, rows_a, row_sem.at[0]).start()
                gather(tl, 1, rows_b, row_sem.at[1]).wait()
                accumulate(tl, 1, rows_b, acc)
                write_out(acc, tok0 + tl, out_sem.at[parity]).start()

        gather(SC_BATCH - 1, 0, rows_a, row_sem.at[0]).wait()

    write_out(acc_a, base_tok, out_sem.at[0]).wait()
    write_out(acc_b, base_tok, out_sem.at[1]).wait()


def _peer_values_sc(idx, w, table, n_tok, d):
    slots = idx.shape[0] // n_tok
    assert n_tok % (SC_CORES * SC_SUBCORES * SC_BATCH) == 0 and SC_BATCH % 2 == 0
    mesh = plsc.VectorSubcoreMesh(core_axis_name="c", subcore_axis_name="s",
                                  num_cores=SC_CORES, num_subcores=SC_SUBCORES)
    return pl.kernel(
        functools.partial(_sc_value_kernel, slots=slots),
        out_type=jax.ShapeDtypeStruct((n_tok, d), F32),
        mesh=mesh,
        scratch_types=[
            pltpu.VMEM((SC_BATCH * slots,), I32),
            pltpu.VMEM((SC_BATCH * slots,), F32),
            pltpu.VMEM((slots // 2, d // 2), I32),
            pltpu.VMEM((slots // 2, d // 2), I32),
            pltpu.VMEM((d,), F32),
            pltpu.VMEM((d,), F32),
            pltpu.SemaphoreType.DMA((2,)),
            pltpu.SemaphoreType.DMA((2,)),
        ],
        compiler_params=pltpu.CompilerParams(needs_layout_passes=False),
        name="peer_values_sc",
    )(table, idx, w)


def _sc_score_kernel(tab_hbm, idx_hbm, xe_hbm, xo_hbm, out_hbm, idx_v, xe_v, xo_v, rows_a, rows_b, part_v, a_v, row_sem,
                     *, slots):
    n_tok = out_hbm.shape[0] // slots // (SC_CORES * SC_SUBCORES)
    words = rows_a.shape[1]
    win = slots // 2
    n_groups = words // SC_LANES // SC_GROUP
    wid = lax.axis_index("s") * SC_CORES + lax.axis_index("c")
    base_tok = wid * n_tok
    lane_id = lax.iota(I32, SC_LANES)

    def gather(tl, half, buf, sem):
        rows = idx_v.at[pl.ds(tl * slots + half * win, win)]
        return pltpu.make_async_copy(tab_hbm.at[rows], buf, sem)

    def scores(tl, half, buf):
        for r in range(win):
            part_v[pl.ds(r * SC_LANES, SC_LANES)] = jnp.zeros((SC_LANES,), F32)
        for g in range(n_groups):
            xs = []
            for j in range(SC_GROUP):
                at = pl.ds(tl * words + (g * SC_GROUP + j) * SC_LANES, SC_LANES)
                xs += [xe_v[at], xo_v[at]]

            @pl.loop(0, win)
            def _(r, g=g, xs=xs):
                sums = [jnp.zeros((SC_LANES,), F32) for _ in range(4)]
                for j in range(SC_GROUP):
                    x = buf[r, pl.ds((g * SC_GROUP + j) * SC_LANES, SC_LANES)]
                    lo = lax.bitcast_convert_type(x << 16, F32)
                    hi = lax.bitcast_convert_type(x & jnp.int32(-65536), F32)
                    sums[(2 * j) % 4] += lo * xs[2 * j]
                    sums[(2 * j + 1) % 4] += hi * xs[2 * j + 1]
                plsc.addupdate(part_v.at[pl.ds(r * SC_LANES, SC_LANES)], (sums[0] + sums[1]) + (sums[2] + sums[3]))
        for rb in range(win // SC_LANES):
            first = (lane_id + rb * SC_LANES) * SC_LANES
            total = jnp.zeros((SC_LANES,), F32)
            for lane in range(SC_LANES):
                total += plsc.load_gather(part_v, [first + lane])
            a_v[pl.ds(tl * slots + half * win + rb * SC_LANES, SC_LANES)] = total

    @pl.loop(0, n_tok // SC_BATCH)
    def _(bi):
        tok0 = base_tok + bi * SC_BATCH
        pltpu.sync_copy(idx_hbm.at[pl.ds(tok0 * slots, SC_BATCH * slots)], idx_v)
        pltpu.sync_copy(xe_hbm.at[pl.ds(tok0 * words, SC_BATCH * words)], xe_v)
        pltpu.sync_copy(xo_hbm.at[pl.ds(tok0 * words, SC_BATCH * words)], xo_v)
        gather(0, 0, rows_a, row_sem.at[0]).start()

        @pl.loop(0, SC_BATCH)
        def _(tl):
            gather(tl, 1, rows_b, row_sem.at[1]).start()
            gather(tl, 0, rows_a, row_sem.at[0]).wait()
            scores(tl, 0, rows_a)
            gather(jnp.minimum(tl + 1, SC_BATCH - 1), 0, rows_a, row_sem.at[0]).start()
            gather(tl, 1, rows_b, row_sem.at[1]).wait()
            scores(tl, 1, rows_b)

        gather(SC_BATCH - 1, 0, rows_a, row_sem.at[0]).wait()
        pltpu.sync_copy(a_v, out_hbm.at[pl.ds(tok0 * slots, SC_BATCH * slots)])


def _peer_scores_sc(idx, xe, xo, table, n_tok, slots):
    words = table.shape[1]
    assert n_tok % (SC_CORES * SC_SUBCORES * SC_BATCH) == 0
    mesh = plsc.VectorSubcoreMesh(core_axis_name="c", subcore_axis_name="s",
                                  num_cores=SC_CORES, num_subcores=SC_SUBCORES)
    return pl.kernel(
        functools.partial(_sc_score_kernel, slots=slots),
        out_type=jax.ShapeDtypeStruct((n_tok * slots,), F32),
        mesh=mesh,
        scratch_types=[
            pltpu.VMEM((SC_BATCH * slots,), I32),
            pltpu.VMEM((SC_BATCH * words,), F32),
            pltpu.VMEM((SC_BATCH * words,), F32),
            pltpu.VMEM((slots // 2, words), I32),
            pltpu.VMEM((slots // 2, words), I32),
            pltpu.VMEM((slots // 2 * SC_LANES,), F32),
            pltpu.VMEM((SC_BATCH * slots,), F32),
            pltpu.SemaphoreType.DMA((2,)),
        ],
        compiler_params=pltpu.CompilerParams(needs_layout_passes=False),
        name="peer_scores_sc",
    )(table, idx, xe, xo)


def _split_norm_kernel(x_ref, g_ref, xe_ref, xo_ref):
    d = x_ref.shape[1]
    xn = _rms(x_ref[...], g_ref[...])
    nchunk = d // (2 * LANES)
    xe_ref[...] = jnp.concatenate([xn[:, (2 * c) * LANES:(2 * c + 1) * LANES] for c in range(nchunk)], axis=1)
    xo_ref[...] = jnp.concatenate([xn[:, (2 * c + 1) * LANES:(2 * c + 2) * LANES] for c in range(nchunk)], axis=1)


def _split_norm(x2d, g, n_rows, tm):
    d = x2d.shape[1]
    first = (x2d.shape[0] - n_rows) // tm
    half = jax.ShapeDtypeStruct((n_rows, d // 2), F32)
    return pl.pallas_call(
        _split_norm_kernel,
        grid=(n_rows // tm,),
        in_specs=[pl.BlockSpec((tm, d), lambda i: (first + i, 0)), pl.BlockSpec((1, d), lambda i: (0, 0))],
        out_specs=[pl.BlockSpec((tm, d // 2), lambda i: (i, 0))] * 2,
        out_shape=[half, half],
        compiler_params=_cparams(("arbitrary",)),
        name="peer_split_norm",
    )(x2d, g.reshape(1, d))


def _score_finish_kernel(w_hbm, a_ref, gate_ref, w_ref):
    del w_hbm
    a = a_ref[0]
    gelu = 0.5 * a * (1.0 + lax.erf(a * (2.0 ** -0.5)))
    w_ref[0] = gate_ref[0].T * gelu


def _score_finish(w_full, a_sc, gate_t):
    nt, t, slots = w_full.shape
    n_sc = a_sc.shape[0]
    first = nt - n_sc
    return pl.pallas_call(
        _score_finish_kernel,
        grid=(n_sc,),
        in_specs=[
            pl.BlockSpec(memory_space=pl.ANY),
            pl.BlockSpec((1, t, slots), lambda i: (i, 0, 0)),
            pl.BlockSpec((1, slots, t), lambda i: (first + i, 0, 0)),
        ],
        out_specs=pl.BlockSpec((1, t, slots), lambda i: (first + i, 0, 0)),
        out_shape=jax.ShapeDtypeStruct(w_full.shape, F32),
        input_output_aliases={0: 0},
        compiler_params=_cparams(("arbitrary",)),
        name="peer_score_finish",
    )(w_full, a_sc, gate_t)


def _residual_norm_kernel(out_hbm, x_ref, p_ref, gf_ref, o_ref, *, final_norm):
    del out_hbm
    y = x_ref[...] + p_ref[...]
    if final_norm:
        y = _rms(y, gf_ref[...])
    o_ref[...] = y


def _residual_norm(out_full, x2d, p2d, g_final, final_norm, tm):
    t, d = p2d.shape
    first = (x2d.shape[0] - t) // tm
    return pl.pallas_call(
        functools.partial(_residual_norm_kernel, final_norm=final_norm),
        grid=(t // tm,),
        in_specs=[
            pl.BlockSpec(memory_space=pl.ANY),
            pl.BlockSpec((tm, d), lambda i: (first + i, 0)),
            pl.BlockSpec((tm, d), lambda i: (i, 0)),
            pl.BlockSpec((1, d), lambda i: (0, 0)),
        ],
        out_specs=pl.BlockSpec((tm, d), lambda i: (first + i, 0)),
        out_shape=jax.ShapeDtypeStruct(x2d.shape, F32),
        input_output_aliases={0: 0},
        compiler_params=_cparams(("arbitrary",)),
        name="peer_residual",
    )(out_full, x2d, p2d, g_final.reshape(1, d))


def _peer_ffn(x, g, w_q, k1, k2, u_emb, v_emb, g_final, final_norm):
    b, s, d = x.shape
    x2d = x.reshape(b * s, d)
    e_slot, gate_slot = _peer_router(x2d, g, w_q, k1, k2)
    u_tab, v_tab = _pack_pairs(u_emb), _pack_pairs(v_emb)
    nt, slots, t = e_slot.shape
    sc_quantum = SC_CORES * SC_SUBCORES * SC_BATCH // t
    nt_sc = nt * SC_TOKEN_PERCENT // 100 // sc_quantum * sc_quantum
    nt_tc = nt - nt_sc
    w_tok = _peer_scores(e_slot, gate_slot, x2d, g, u_tab, nt_tc)
    if nt_sc:
        rows_per_expert = d // (2 * LANES)
        tm = math.gcd(TOKEN_TILE, nt_sc * t)
        idx_sc = (e_slot[nt_tc:] // rows_per_expert).transpose(0, 2, 1).reshape(-1)
        xe, xo = _split_norm(x2d, g, nt_sc * t, tm)
        a_sc = _peer_scores_sc(idx_sc, xe.reshape(-1), xo.reshape(-1), u_tab.reshape(-1, d // 2), nt_sc * t, slots)
        w_tok = _score_finish(w_tok, a_sc.reshape(nt_sc, t, slots), gate_slot)
    out = _peer_values(e_slot, w_tok, x2d, v_tab, g_final, final_norm, nt_tc)
    if nt_sc:
        peer_sc = _peer_values_sc(idx_sc, w_tok[nt_tc:].reshape(-1), v_tab.reshape(-1, d // 2), nt_sc * t, d)
        out = _residual_norm(out, x2d, peer_sc, g_final, final_norm, tm)
    return out.reshape(b, s, d)


def _qkv_kernel(x_ref, g_ref, wqk_hi_ref, wqk_lo_ref, wv_ref, cos_ref, sin_ref,
                qt_ref, k0_ref, k1_ref, vt_ref, km_ref):
    tm, d = x_ref.shape[1], x_ref.shape[2]
    hn = _rms(x_ref[0], g_ref[...])
    hh, hl = _split(hn)
    qk = _dot3(hh, hl, wqk_hi_ref[...], wqk_lo_ref[...])
    v = _dot(hh, wv_ref[...])
    cos = jnp.concatenate([cos_ref[...]] * (d // LANES), axis=1)
    sin = jnp.concatenate([sin_ref[...]] * (d // LANES), axis=1)
    lane = lax.broadcasted_iota(I32, (tm, d), 1)
    first_half = (lane % HEAD_DIM) < (HEAD_DIM // 2)

    def rope(a):
        rot = jnp.where(first_half, pltpu.roll(a, d - HEAD_DIM // 2, 1), pltpu.roll(a, HEAD_DIM // 2, 1))
        return a * cos + rot * sin

    q = rope(qk[:, :d])
    k = rope(qk[:, d:])
    qt_ref[0] = q.T
    vt_ref[0] = v.T.astype(BF16)
    nb = tm // MOBA_BLOCK
    km_ref[0, 0] = jnp.mean(k.reshape(nb, MOBA_BLOCK, d), axis=1)
    row = lax.broadcasted_iota(I32, (tm, d), 0)
    block = (pl.program_id(1) * tm + row) // MOBA_BLOCK
    pair_lane = lane % LANES
    kb = k.astype(BF16)
    k0_ref[0] = jnp.where(pair_lane < HEAD_DIM, kb, jnp.where(pair_lane - HEAD_DIM == block, 1.0, 0.0).astype(BF16))
    k1_ref[0] = jnp.where(pair_lane >= HEAD_DIM, kb, jnp.where(pair_lane == block, 1.0, 0.0).astype(BF16))


def _qkv_rope(x, g, w_qkv, tm):
    b, s, d = x.shape
    half = HEAD_DIM // 2
    inv = ROPE_THETA ** (-jnp.arange(half, dtype=F32) / half)
    ang = jnp.arange(s).astype(F32)[:, None] * inv[None, :]
    cos, sin = jnp.cos(ang), jnp.sin(ang)
    cos128 = jnp.tile(jnp.concatenate([cos, cos], axis=1), (1, LANES // HEAD_DIM))
    sin128 = jnp.tile(jnp.concatenate([-sin, sin], axis=1), (1, LANES // HEAD_DIM))
    wqk_hi, wqk_lo = _split(w_qkv[:, :2 * d])
    wv = w_qkv[:, 2 * d:].astype(BF16)
    nb = tm // MOBA_BLOCK
    const = lambda i, j: (0, 0)
    qt, k0, k1, vt, km = pl.pallas_call(
        _qkv_kernel,
        grid=(b, s // tm),
        in_specs=[
            pl.BlockSpec((1, tm, d), lambda i, j: (i, j, 0)),
            pl.BlockSpec((1, d), const),
            pl.BlockSpec((d, 2 * d), const),
            pl.BlockSpec((d, 2 * d), const),
            pl.BlockSpec((d, d), const),
            pl.BlockSpec((tm, LANES), lambda i, j: (j, 0)),
            pl.BlockSpec((tm, LANES), lambda i, j: (j, 0)),
        ],
        out_specs=[
            pl.BlockSpec((1, d, tm), lambda i, j: (i, 0, j)),
            pl.BlockSpec((1, tm, d), lambda i, j: (i, j, 0)),
            pl.BlockSpec((1, tm, d), lambda i, j: (i, j, 0)),
            pl.BlockSpec((1, d, tm), lambda i, j: (i, 0, j)),
            pl.BlockSpec((1, 1, nb, d), lambda i, j: (i, j, 0, 0)),
        ],
        out_shape=[
            jax.ShapeDtypeStruct((b, d, s), F32),
            jax.ShapeDtypeStruct((b, s, d), BF16),
            jax.ShapeDtypeStruct((b, s, d), BF16),
            jax.ShapeDtypeStruct((b, d, s), BF16),
            jax.ShapeDtypeStruct((b, s // tm, nb, d), F32),
        ],
        compiler_params=_cparams(("arbitrary", "arbitrary")),
        name="qkv_rope",
    )(x, g.reshape(1, d), wqk_hi, wqk_lo, wv, cos128, sin128)
    return qt, (k0, k1), vt, km.reshape(b, s // MOBA_BLOCK, d)


def _moba_kernel(qt_ref, k0_ref, k1_ref, vt_ref, km_ref, o_ref, sa_ref, sb_ref):
    bs = MOBA_BLOCK
    nb = km_ref.shape[1]
    n_heads = LANES // HEAD_DIM
    k_refs = (k0_ref, k1_ref)
    j = pl.program_id(2)
    qt = qt_ref[0]
    km = km_ref[0]
    scale = HEAD_DIM ** -0.5 * 1.4426950408889634
    lane_km = lax.broadcasted_iota(I32, (nb, LANES), 1)
    blk = lax.broadcasted_iota(I32, (nb, bs), 0)
    zeros_pad = jnp.zeros((LANES - HEAD_DIM - nb, bs), F32)
    qh, ql = _split(qt)

    own = pl.ds(pl.multiple_of(j * bs, bs), bs)
    v_own = vt_ref[0, :, own]
    krow = lax.broadcasted_iota(I32, (bs, bs), 0)
    qcol = lax.broadcasted_iota(I32, (bs, bs), 1)

    q_augs, state = [], []
    for hh in range(n_heads):
        head_lo = hh * HEAD_DIM
        in_head_km = (lane_km >= head_lo) & (lane_km < head_lo + HEAD_DIM)
        kmh, kml = _split(jnp.where(in_head_km, km, 0.0))
        gate = _dot3(kmh, kml, qh, ql)
        valid = blk < j
        gate = jnp.where(valid, gate, -jnp.inf)
        sel = jnp.zeros((nb, bs), F32)
        for _ in range(MOBA_TOPK):
            m = jnp.max(gate, axis=0, keepdims=True)
            i = jnp.min(jnp.where(gate == m, blk, nb), axis=0, keepdims=True)
            pick = blk == i
            sel = jnp.where(pick, 1.0, sel)
            gate = jnp.where(pick, -jnp.inf, gate)
        bias_t = jnp.where((sel > 0.0) & valid, 0.0, MASK_NEG)
        q_head = qt[head_lo:head_lo + HEAD_DIM, :] * scale
        no_bias = jnp.zeros((LANES - HEAD_DIM, bs), F32)
        if hh == 0:
            q_aug = jnp.concatenate([q_head, bias_t, zeros_pad], axis=0)
            q_own = jnp.concatenate([q_head, no_bias], axis=0)
        else:
            q_aug = jnp.concatenate([bias_t, zeros_pad, q_head], axis=0)
            q_own = jnp.concatenate([no_bias, q_head], axis=0)
        q_augs.append(q_aug.astype(BF16))

        s_own = jnp.where(krow <= qcol, _dot(k_refs[hh][0, own, :], q_own.astype(BF16)), -1e30)
        m0 = jnp.max(s_own, axis=0, keepdims=True)
        p0 = jnp.exp2(s_own - m0)
        state += [m0, jnp.sum(p0, axis=0, keepdims=True), _dot(v_own, p0.astype(BF16))]

    chunk = KV_CHUNK * bs
    last_chunk = nb // KV_CHUNK - 1

    def score_chunk(c, s_ref):
        rows = pl.ds(pl.multiple_of(jnp.minimum(c, last_chunk) * chunk, chunk), chunk)
        for hh in range(n_heads):
            s_ref[hh] = _dot(k_refs[hh][0, rows, :], q_augs[hh])

    def attend(c, s_ref, state):
        vn = vt_ref[0, :, pl.ds(pl.multiple_of(c * chunk, chunk), chunk)]
        new_state = []
        for hh in range(n_heads):
            m, l, acc = state[3 * hh:3 * hh + 3]
            s = s_ref[hh]
            m_new = jnp.maximum(m, jnp.max(s, axis=0, keepdims=True))
            alpha = jnp.exp2(m - m_new)
            p = jnp.exp2(s - m_new)
            l = alpha * l + jnp.sum(p, axis=0, keepdims=True)
            acc = alpha * acc + _dot(vn, p.astype(BF16))
            new_state += [m_new, l, acc]
        return tuple(new_state)

    score_chunk(0, sa_ref)

    def body(i, state):
        score_chunk(2 * i + 1, sb_ref)
        state = attend(2 * i, sa_ref, state)
        score_chunk(2 * i + 2, sa_ref)
        return attend(2 * i + 1, sb_ref, state)

    state = lax.fori_loop(0, (j + 2 * KV_CHUNK - 1) // (2 * KV_CHUNK), body, tuple(state))
    halves = []
    for hh in range(n_heads):
        _, l, acc = state[3 * hh:3 * hh + 3]
        halves.append((acc / l)[hh * HEAD_DIM:(hh + 1) * HEAD_DIM, :])
    o_ref[0] = jnp.concatenate(halves, axis=0).T


def _moba_attention(qt, k01, vt, km):
    b, d, s = qt.shape
    nb = s // MOBA_BLOCK
    assert nb % (2 * KV_CHUNK) == 0
    return pl.pallas_call(
        _moba_kernel,
        grid=(b, d // LANES, nb),
        in_specs=[
            pl.BlockSpec((1, LANES, MOBA_BLOCK), lambda i, h, j: (i, h, j)),
            pl.BlockSpec((1, s, LANES), lambda i, h, j: (i, 0, h)),
            pl.BlockSpec((1, s, LANES), lambda i, h, j: (i, 0, h)),
            pl.BlockSpec((1, LANES, s), lambda i, h, j: (i, h, 0)),
            pl.BlockSpec((1, nb, LANES), lambda i, h, j: (i, 0, h)),
        ],
        out_specs=pl.BlockSpec((1, MOBA_BLOCK, LANES), lambda i, h, j: (i, j, h)),
        out_shape=jax.ShapeDtypeStruct((b, s, d), F32),
        scratch_shapes=[pltpu.VMEM((LANES // HEAD_DIM, KV_CHUNK * MOBA_BLOCK, MOBA_BLOCK), F32)] * 2,
        compiler_params=_cparams(("arbitrary", "arbitrary", "arbitrary")),
        name="moba_attention",
    )(qt, *k01, vt, km)


def _proj_residual_kernel(x_ref, a_ref, w_ref, o_ref):
    o_ref[...] = x_ref[...] + _dot(a_ref[...].astype(BF16), w_ref[...])


def _proj_residual(x2d, a2d, w, tm):
    t, d = x2d.shape
    return pl.pallas_call(
        _proj_residual_kernel,
        grid=(t // tm,),
        in_specs=[
            pl.BlockSpec((tm, d), lambda i: (i, 0)),
            pl.BlockSpec((tm, a2d.shape[1]), lambda i: (i, 0)),
            pl.BlockSpec(w.shape, lambda i: (0, 0)),
        ],
        out_specs=pl.BlockSpec((tm, d), lambda i: (i, 0)),
        out_shape=jax.ShapeDtypeStruct((t, d), F32),
        compiler_params=_cparams(("arbitrary",)),
        name="attn_out_proj",
    )(x2d, a2d, w.astype(BF16))


def _moba_mixer(x, g, w_qkv, w_o):
    b, s, d = x.shape
    tm = min(TOKEN_TILE, s)
    qt, k, vt, km = _qkv_rope(x, g, w_qkv, tm)
    attn = _moba_attention(qt, k, vt, km)
    return _proj_residual(x.reshape(b * s, d), attn.reshape(b * s, d), w_o, tm).reshape(b, s, d)


def kernel(x, norm_mix, norm_ffn, conv_w_in, conv_w, conv_w_out, attn_w_qkv, attn_w_o,
           peer_w_q, peer_k1, peer_k2, peer_u, peer_v, norm_final):
    depth = norm_mix.shape[0]
    tm = min(TOKEN_TILE, x.shape[1])
    for i in range(depth):
        j = i // 2
        if i % 2 == 0:
            x = _conv_mixer(x, norm_mix[i], conv_w_in[j], conv_w[j], conv_w_out[j], tm)
        else:
            x = _moba_mixer(x, norm_mix[i], attn_w_qkv[j], attn_w_o[j])
        x = _peer_ffn(x, norm_ffn[i], peer_w_q[i], peer_k1[i], peer_k2[i], peer_u[i], peer_v[i],
                      norm_final, final_norm=(i == depth - 1))
    return x
```

```python
import functools
import math

import jax
import jax.numpy as jnp
from jax import lax
from jax.experimental import pallas as pl
from jax.experimental.pallas import tpu as pltpu
from jax.experimental.pallas import tpu_sc as plsc

F32 = jnp.float32
BF16 = jnp.bfloat16
I32 = jnp.int32

RMS_EPS = 1e-6
N_HEADS = 16
HEAD_DIM = 64
MOBA_BLOCK = 256
MOBA_TOPK = 3
ROPE_THETA = 10000.0
PEER_HEADS = 8
PEER_NKEYS = 128
PEER_HALF = 64
PEER_TOPK = 16

LANES = 128
SUBLANES = 8
VMEM_LIMIT = 56 * 1024 * 1024
MASK_NEG = -1e9

TOKEN_TILE = 512
PEER_TILE = 128
OCTET = 8
SC_CORES = 2
SC_SUBCORES = 16
SC_LANES = 16
SC_BATCH = 16
SC_GROUP = 8
SC_TOKEN_PERCENT = 45
KV_CHUNK = 2


def _cparams(sem):
    return pltpu.CompilerParams(dimension_semantics=sem, vmem_limit_bytes=VMEM_LIMIT)


def _rms(x, g):
    ms = jnp.mean(x * x, axis=-1, keepdims=True)
    return x * lax.rsqrt(ms + RMS_EPS) * g


def _split(a):
    hi = a.astype(BF16)
    lo = (a - hi.astype(F32)).astype(BF16)
    return hi, lo


def _dot(a, b):
    return lax.dot_general(a, b, (((1,), (0,)), ((), ())), preferred_element_type=F32)


def _dot_t(a, b):
    return lax.dot_general(a, b, (((1,), (1,)), ((), ())), preferred_element_type=F32)


def _dot3(a_hi, a_lo, b_hi, b_lo):
    return _dot(a_hi, b_hi) + _dot(a_lo, b_hi) + _dot(a_hi, b_lo)


def _dot3_t(a_hi, a_lo, b_hi, b_lo):
    return _dot_t(a_hi, b_hi) + _dot_t(a_lo, b_hi) + _dot_t(a_hi, b_lo)


def _conv_mixer_kernel(x_ref, g_ref, win_ref, cw_ref, wout_ref, o_ref, ubuf_ref):
    tm, d = x_ref.shape[1], x_ref.shape[2]

    @pl.when(pl.program_id(1) == 0)
    def _():
        ubuf_ref[0:SUBLANES, :] = jnp.zeros((SUBLANES, d), F32)

    x = x_ref[0]
    hn = _rms(x, g_ref[...]).astype(BF16)
    bcz = _dot(hn, win_ref[...])
    b_gate, c_gate, z = bcz[:, :d], bcz[:, d:2 * d], bcz[:, 2 * d:]
    u = c_gate * z
    ubuf_ref[SUBLANES:SUBLANES + tm, :] = u
    u1 = ubuf_ref[SUBLANES - 1:SUBLANES - 1 + tm, :]
    u2 = ubuf_ref[SUBLANES - 2:SUBLANES - 2 + tm, :]
    cw = cw_ref[...]
    u_conv = cw[0:1, :] * u2 + cw[1:2, :] * u1 + cw[2:3, :] * u
    ubuf_ref[0:SUBLANES, :] = u[tm - SUBLANES:tm, :]
    y = (b_gate * u_conv).astype(BF16)
    o_ref[0] = x + _dot(y, wout_ref[...])


def _conv_mixer(x, g, w_in, conv_w, w_out, tm):
    b, s, d = x.shape
    return pl.pallas_call(
        _conv_mixer_kernel,
        grid=(b, s // tm),
        in_specs=[
            pl.BlockSpec((1, tm, d), lambda i, j: (i, j, 0)),
            pl.BlockSpec((1, d), lambda i, j: (0, 0)),
            pl.BlockSpec((d, 3 * d), lambda i, j: (0, 0)),
            pl.BlockSpec((3, d), lambda i, j: (0, 0)),
            pl.BlockSpec((d, d), lambda i, j: (0, 0)),
        ],
        out_specs=pl.BlockSpec((1, tm, d), lambda i, j: (i, j, 0)),
        out_shape=jax.ShapeDtypeStruct((b, s, d), F32),
        scratch_shapes=[pltpu.VMEM((tm + SUBLANES, d), F32)],
        compiler_params=_cparams(("arbitrary", "arbitrary")),
        name="conv_mixer",
    )(x, g.reshape(1, d), w_in.astype(BF16), conv_w, w_out.astype(BF16))


def _topk_rows(s, k, order=None, payload=None):
    if order is None:
        order = lax.broadcasted_iota(I32, s.shape, 0)
    big = jnp.iinfo(jnp.int32).max
    vals, outs = [], []
    for _ in range(k):
        m = jnp.max(s, axis=0, keepdims=True)
        i = jnp.min(jnp.where(s == m, order, big), axis=0, keepdims=True)
        pick = order == i
        vals.append(m)
        if payload is None:
            outs.append(i)
        else:
            outs.append(jnp.max(jnp.where(pick, payload, -1), axis=0, keepdims=True))
        s = jnp.where(pick, -jnp.inf, s)
    return jnp.concatenate(vals, axis=0), jnp.concatenate(outs, axis=0)


def _staircase(kk):
    groups = []
    for a in range(2):
        for b0 in range(0, kk // (a + 1), SUBLANES):
            groups.append((a, 0, b0, 1, lambda j, a=a, b0=b0: (a + 1) * (b0 + j + 1) <= kk))
    for b in range(kk // 3):
        for a0 in range(0, kk // (b + 1), SUBLANES):
            groups.append((a0, 1, b, 0, lambda j, a0=a0, b=b: (a0 + j >= 2) & ((a0 + j + 1) * (b + 1) <= kk)))
    return groups


def _staircase_topk(v1, i1, v2, i2, nk):
    kk, t = v1.shape
    j = lax.broadcasted_iota(I32, (SUBLANES, t), 0)

    def rows(x, x0, step):
        if step == 0:
            return jnp.broadcast_to(x[x0:x0 + 1, :], (SUBLANES, t))
        return x[x0:x0 + SUBLANES, :]

    cand, order, cidx = [], [], []
    for a0, a_step, b0, b_step, valid in _staircase(kk):
        ok = valid(j)
        cand.append(jnp.where(ok, rows(v1, a0, a_step) + rows(v2, b0, b_step), -jnp.inf))
        order.append(jnp.where(ok, (a0 + j * a_step) * kk + (b0 + j * b_step), jnp.iinfo(jnp.int32).max - 1))
        cidx.append(rows(i1, a0, a_step) * nk + rows(i2, b0, b_step))
    cat = lambda xs: jnp.concatenate(xs, axis=0)
    return _topk_rows(cat(cand), kk, order=cat(order), payload=cat(cidx))


def _router_kernel(x_ref, g_ref, wq_hi_ref, wq_lo_ref, k1_hi_ref, k1_lo_ref, k2_hi_ref, k2_lo_ref,
                   e_ref, gate_ref):
    table_rows = x_ref.shape[1] // (2 * LANES)
    xn = _rms(x_ref[...], g_ref[...])
    xh, xl = _split(xn)
    q = _dot3(xh, xl, wq_hi_ref[...], wq_lo_ref[...])
    nk = k1_hi_ref.shape[0]
    for h in range(PEER_HEADS):
        qh, ql = _split(q[:, h * LANES:(h + 1) * LANES])
        s1 = _dot3_t(k1_hi_ref[...], k1_lo_ref[...], qh, ql)
        s2 = _dot3_t(k2_hi_ref[...], k2_lo_ref[...], qh, ql)
        v1, i1 = _topk_rows(s1, PEER_TOPK)
        v2, i2 = _topk_rows(s2, PEER_TOPK)
        kk = PEER_TOPK
        top_s, e_idx = _staircase_topk(v1, i1, v2, i2, nk)
        p = jnp.exp(top_s - top_s[0:1, :])
        gate = p / jnp.sum(p, axis=0, keepdims=True)
        e_ref[0, h * kk:(h + 1) * kk, :] = e_idx * table_rows
        gate_ref[0, h * kk:(h + 1) * kk, :] = gate


def _pad_keys(k, lo):
    nk, half = k.shape
    out = jnp.zeros((nk, LANES), F32)
    return out.at[:, lo:lo + half].set(k)


def _peer_router(x2d, g, w_q, k1, k2):
    t, d = x2d.shape
    nt = t // PEER_TILE
    slots = PEER_HEADS * PEER_TOPK
    wq_hi, wq_lo = _split(w_q)
    k1_hi, k1_lo = _split(_pad_keys(k1, 0))
    k2_hi, k2_lo = _split(_pad_keys(k2, PEER_HALF))
    const = lambda i: (0, 0)
    kspec = pl.BlockSpec(k1_hi.shape, const)
    return pl.pallas_call(
        _router_kernel,
        grid=(nt,),
        in_specs=[
            pl.BlockSpec((PEER_TILE, d), lambda i: (i, 0)),
            pl.BlockSpec((1, d), const),
            pl.BlockSpec(wq_hi.shape, const),
            pl.BlockSpec(wq_lo.shape, const),
            kspec, kspec, kspec, kspec,
        ],
        out_specs=[
            pl.BlockSpec((1, slots, PEER_TILE), lambda i: (i, 0, 0)),
            pl.BlockSpec((1, slots, PEER_TILE), lambda i: (i, 0, 0)),
        ],
        out_shape=[
            jax.ShapeDtypeStruct((nt, slots, PEER_TILE), I32),
            jax.ShapeDtypeStruct((nt, slots, PEER_TILE), F32),
        ],
        compiler_params=_cparams(("arbitrary",)),
        name="peer_router",
    )(x2d, g.reshape(1, d), wq_hi, wq_lo, k1_hi, k1_lo, k2_hi, k2_lo)


def _gather_octet(e_ref, tab_ref, stage_ref, octet, nsub):
    t = e_ref.shape[2]
    for k in range(OCTET):
        for tok in range(t):
            off = e_ref[0, octet * OCTET + k, tok]
            stage_ref[k * (t // SUBLANES) + tok // SUBLANES,
                      pl.ds(tok % SUBLANES, nsub, stride=SUBLANES), :] = tab_ref[pl.ds(off, nsub), :]


def _octet_pipeline(n_octets, gather, consume, stage_a, stage_b):
    gather(0, stage_a)

    def pair(p, carry):
        gather(2 * p + 1, stage_b)
        consume(2 * p, stage_a)
        gather(jnp.minimum(2 * p + 2, n_octets - 1), stage_a)
        consume(2 * p + 1, stage_b)
        return carry
    lax.fori_loop(0, n_octets // 2, pair, 0)


def _score_kernel(e_ref, x_ref, g_ref, gate_ref, tab_ref, w_ref, stage_a, stage_b, a_ref):
    t, d = x_ref.shape
    nchunk = d // (2 * LANES)
    slots = e_ref.shape[1]
    rows = OCTET * t

    xn = _rms(x_ref[...], g_ref[...]).astype(BF16)
    x_even = jnp.concatenate([xn[:, (2 * c) * LANES:(2 * c + 1) * LANES] for c in range(nchunk)], axis=1)
    x_odd = jnp.concatenate([xn[:, (2 * c + 1) * LANES:(2 * c + 2) * LANES] for c in range(nchunk)], axis=1)
    rhs = jnp.concatenate([x_even, x_odd], axis=0)

    ri = lax.broadcasted_iota(I32, (2 * t, 2 * t), 0)
    ci = lax.broadcasted_iota(I32, (2 * t, 2 * t), 1)
    diag = (ci == (ri % 2) * t + ri // 2).astype(F32)

    def consume(o, stage_ref):
        planes = []
        for c in range(nchunk):
            plane = stage_ref[:, c * SUBLANES:(c + 1) * SUBLANES, :].reshape(rows, LANES)
            planes.append(pltpu.bitcast(plane, BF16))
        lhs = jnp.concatenate(planes, axis=1)
        prod = _dot_t(lhs, rhs)
        prod = prod.reshape(OCTET, 2 * t, 2 * t) * diag[None]
        a2 = jnp.sum(prod, axis=1)
        a_ref[pl.ds(pl.multiple_of(o * OCTET, OCTET), OCTET), :] = a2[:, :t] + a2[:, t:]

    gather = lambda o, stage_ref: _gather_octet(e_ref, tab_ref, stage_ref, o, nchunk)
    _octet_pipeline(slots // OCTET, gather, consume, stage_a, stage_b)

    a = a_ref[...]
    gelu = 0.5 * a * (1.0 + lax.erf(a * (2.0 ** -0.5)))
    w_ref[0] = (gate_ref[0] * gelu).T


def _peer_scores(e_t, gate_t, x2d, g, table, n_tiles):
    nt, slots, t = e_t.shape
    d = x2d.shape[1]
    stage = pltpu.VMEM((OCTET * t // SUBLANES, d // (2 * LANES) * SUBLANES, LANES), I32)
    return pl.pallas_call(
        _score_kernel,
        grid=(n_tiles,),
        in_specs=[
            pl.BlockSpec((1, slots, t), lambda i: (i, 0, 0), memory_space=pltpu.SMEM),
            pl.BlockSpec((t, d), lambda i: (i, 0)),
            pl.BlockSpec((1, d), lambda i: (0, 0)),
            pl.BlockSpec((1, slots, t), lambda i: (i, 0, 0)),
            pl.BlockSpec(table.shape, lambda i: (0, 0), pipeline_mode=pl.Buffered(1)),
        ],
        out_specs=pl.BlockSpec((1, t, slots), lambda i: (i, 0, 0)),
        out_shape=jax.ShapeDtypeStruct((nt, t, slots), F32),
        scratch_shapes=[stage, stage, pltpu.VMEM((slots, t), F32)],
        compiler_params=_cparams(("arbitrary",)),
        name="peer_scores",
    )(e_t, x2d, g.reshape(1, d), gate_t, table)


def _value_kernel(e_ref, w_ref, x_ref, gf_ref, tab_ref, o_ref, stage_a, stage_b, wb_ref, acc_ref, *, final_norm):
    t, d = x_ref.shape
    nchunk = d // (2 * LANES)
    slots = e_ref.shape[1]
    groups = t // SUBLANES

    acc_ref[...] = x_ref[...]

    def consume(o, stage_ref):
        w_oct = pltpu.roll(w_ref[0], lax.rem(slots - o * OCTET, slots), 1)
        for k in range(OCTET):
            wb_ref[k] = jnp.broadcast_to(w_oct[:, k:k + 1], (t, LANES))
        for c in range(nchunk):
            lo_cols = slice(2 * c * LANES, (2 * c + 1) * LANES)
            hi_cols = slice((2 * c + 1) * LANES, (2 * c + 2) * LANES)
            acc_lo, acc_hi = acc_ref[:, lo_cols], acc_ref[:, hi_cols]
            for k in range(OCTET):
                words = stage_ref[k * groups:(k + 1) * groups, c * SUBLANES:(c + 1) * SUBLANES, :]
                words = words.reshape(t, LANES)
                lo = pltpu.bitcast(words << 16, F32)
                hi = pltpu.bitcast(words & jnp.int32(-65536), F32)
                acc_lo = acc_lo + wb_ref[k] * lo
                acc_hi = acc_hi + wb_ref[k] * hi
            acc_ref[:, lo_cols] = acc_lo
            acc_ref[:, hi_cols] = acc_hi

    gather = lambda o, stage_ref: _gather_octet(e_ref, tab_ref, stage_ref, o, nchunk)
    _octet_pipeline(slots // OCTET, gather, consume, stage_a, stage_b)

    y = acc_ref[...]
    if final_norm:
        y = _rms(y, gf_ref[...])
    o_ref[...] = y


def _peer_values(e_t, w_t, x2d, table, g_final, final_norm, n_tiles):
    nt, slots, t = e_t.shape
    d = x2d.shape[1]
    stage = pltpu.VMEM((OCTET * t // SUBLANES, d // (2 * LANES) * SUBLANES, LANES), I32)
    return pl.pallas_call(
        functools.partial(_value_kernel, final_norm=final_norm),
        grid=(n_tiles,),
        in_specs=[
            pl.BlockSpec((1, slots, t), lambda i: (i, 0, 0), memory_space=pltpu.SMEM),
            pl.BlockSpec((1, t, slots), lambda i: (i, 0, 0)),
            pl.BlockSpec((t, d), lambda i: (i, 0)),
            pl.BlockSpec((1, d), lambda i: (0, 0)),
            pl.BlockSpec(table.shape, lambda i: (0, 0), pipeline_mode=pl.Buffered(1)),
        ],
        out_specs=pl.BlockSpec((t, d), lambda i: (i, 0)),
        out_shape=jax.ShapeDtypeStruct(x2d.shape, F32),
        scratch_shapes=[stage, stage, pltpu.VMEM((OCTET, t, LANES), F32), pltpu.VMEM((t, d), F32)],
        compiler_params=_cparams(("arbitrary",)),
        name="peer_values",
    )(e_t, w_t, x2d, g_final.reshape(1, d), table)


def _pack_pairs(w):
    e, d = w.shape
    bits = lax.bitcast_convert_type(w.astype(BF16), jnp.uint16).astype(jnp.uint32)
    bits = bits.reshape(e * d // (2 * LANES), 2 * LANES)
    packed = bits[:, :LANES] | (bits[:, LANES:] << 16)
    return lax.bitcast_convert_type(packed, I32)


def _sc_value_kernel(tab_hbm, idx_hbm, w_hbm, out_hbm, idx_v, w_v, rows_a, rows_b, acc_a, acc_b, row_sem, out_sem,
                     *, slots):
    n_tok = out_hbm.shape[0] // (SC_CORES * SC_SUBCORES)
    d = out_hbm.shape[1]
    words = d // 2
    win = slots // 2
    wid = lax.axis_index("s") * SC_CORES + lax.axis_index("c")
    base_tok = wid * n_tok

    def gather(tl, half, buf, sem):
        rows = idx_v.at[pl.ds(tl * slots + half * win, win)]
        return pltpu.make_async_copy(tab_hbm.at[rows], buf, sem)

    def accumulate(tl, half, buf, acc):
        for g in range(words // SC_LANES // SC_GROUP):
            def row(r, sums, g=g):
                slot = jnp.full((SC_LANES,), tl * slots + half * win + r, I32)
                wv = plsc.load_gather(w_v, [slot])
                out = []
                for j in range(SC_GROUP):
                    x = buf[r, pl.ds((g * SC_GROUP + j) * SC_LANES, SC_LANES)]
                    lo = lax.bitcast_convert_type(x << 16, F32)
                    hi = lax.bitcast_convert_type(x & jnp.int32(-65536), F32)
                    out += [sums[2 * j] + wv * lo, sums[2 * j + 1] + wv * hi]
                return tuple(out)
            zeros = tuple(jnp.zeros((SC_LANES,), F32) for _ in range(2 * SC_GROUP))
            sums = lax.fori_loop(0, win, row, zeros)
            for j in range(SC_GROUP):
                chunk, lane = divmod((g * SC_GROUP + j) * SC_LANES, LANES)
                plsc.addupdate(acc.at[pl.ds(2 * chunk * LANES + lane, SC_LANES)], sums[2 * j])
                plsc.addupdate(acc.at[pl.ds((2 * chunk + 1) * LANES + lane, SC_LANES)], sums[2 * j + 1])

    def write_out(acc, tok, sem):
        return pltpu.make_async_copy(acc, out_hbm.at[tok], sem)

    @pl.loop(0, n_tok // SC_BATCH)
    def _(bi):
        tok0 = base_tok + bi * SC_BATCH
        pltpu.sync_copy(idx_hbm.at[pl.ds(tok0 * slots, SC_BATCH * slots)], idx_v)
        pltpu.sync_copy(w_hbm.at[pl.ds(tok0 * slots, SC_BATCH * slots)], w_v)
        gather(0, 0, rows_a, row_sem.at[0]).start()

        @pl.loop(0, SC_BATCH // 2)
        def _(tp):
            for parity, acc in ((0, acc_a), (1, acc_b)):
                tl = 2 * tp + parity

                @pl.when(bi * SC_BATCH + tl >= 2)
                def _():
                    write_out(acc, tok0, out_sem.at[parity]).wait()
                for k in range(d // SC_LANES):
                    acc[pl.ds(k * SC_LANES, SC_LANES)] = jnp.zeros((SC_LANES,), F32)
                gather(tl, 1, rows_b, row_sem.at[1]).start()
                gather(tl, 0, rows_a, row_sem.at[0]).wait()
                accumulate(tl, 0, rows_a, acc)
                gather(jnp.minimum(tl + 1, SC_BATCH - 1), 0, rows_a, row_sem.at[0]).start()
                gather(tl, 1, rows_b, row_sem.at[1]).wait()
                accumulate(tl, 1, rows_b, acc)
                write_out(acc, tok0 + tl, out_sem.at[parity]).start()

        gather(SC_BATCH - 1, 0, rows_a, row_sem.at[0]).wait()

    write_out(acc_a, base_tok, out_sem.at[0]).wait()
    write_out(acc_b, base_tok, out_sem.at[1]).wait()


def _peer_values_sc(idx, w, table, n_tok, d):
    slots = idx.shape[0] // n_tok
    assert n_tok % (SC_CORES * SC_SUBCORES * SC_BATCH) == 0 and SC_BATCH % 2 == 0
    mesh = plsc.VectorSubcoreMesh(core_axis_name="c", subcore_axis_name="s",
                                  num_cores=SC_CORES, num_subcores=SC_SUBCORES)
    return pl.kernel(
        functools.partial(_sc_value_kernel, slots=slots),
        out_type=jax.ShapeDtypeStruct((n_tok, d), F32),
        mesh=mesh,
        scratch_types=[
            pltpu.VMEM((SC_BATCH * slots,), I32),
            pltpu.VMEM((SC_BATCH * slots,), F32),
            pltpu.VMEM((slots // 2, d // 2), I32),
            pltpu.VMEM((slots // 2, d // 2), I32),
            pltpu.VMEM((d,), F32),
            pltpu.VMEM((d,), F32),
            pltpu.SemaphoreType.DMA((2,)),
            pltpu.SemaphoreType.DMA((2,)),
        ],
        compiler_params=pltpu.CompilerParams(needs_layout_passes=False),
        name="peer_values_sc",
    )(table, idx, w)


def _sc_score_kernel(tab_hbm, idx_hbm, xe_hbm, xo_hbm, out_hbm, idx_v, xe_v, xo_v, rows_a, rows_b, part_v, a_v, row_sem,
                     *, slots):
    n_tok = out_hbm.shape[0] // slots // (SC_CORES * SC_SUBCORES)
    words = rows_a.shape[1]
    win = slots // 2
    n_groups = words // SC_LANES // SC_GROUP
    wid = lax.axis_index("s") * SC_CORES + lax.axis_index("c")
    base_tok = wid * n_tok
    lane_id = lax.iota(I32, SC_LANES)

    def gather(tl, half, buf, sem):
        rows = idx_v.at[pl.ds(tl * slots + half * win, win)]
        return pltpu.make_async_copy(tab_hbm.at[rows], buf, sem)

    def scores(tl, half, buf):
        for r in range(win):
            part_v[pl.ds(r * SC_LANES, SC_LANES)] = jnp.zeros((SC_LANES,), F32)
        for g in range(n_groups):
            xs = []
            for j in range(SC_GROUP):
                at = pl.ds(tl * words + (g * SC_GROUP + j) * SC_LANES, SC_LANES)
                xs += [xe_v[at], xo_v[at]]

            @pl.loop(0, win)
            def _(r, g=g, xs=xs):
                sums = [jnp.zeros((SC_LANES,), F32) for _ in range(4)]
                for j in range(SC_GROUP):
                    x = buf[r, pl.ds((g * SC_GROUP + j) * SC_LANES, SC_LANES)]
                    lo = lax.bitcast_convert_type(x << 16, F32)
                    hi = lax.bitcast_convert_type(x & jnp.int32(-65536), F32)
                    sums[(2 * j) % 4] += lo * xs[2 * j]
                    sums[(2 * j + 1) % 4] += hi * xs[2 * j + 1]
                plsc.addupdate(part_v.at[pl.ds(r * SC_LANES, SC_LANES)], (sums[0] + sums[1]) + (sums[2] + sums[3]))
        for rb in range(win // SC_LANES):
            first = (lane_id + rb * SC_LANES) * SC_LANES
            total = jnp.zeros((SC_LANES,), F32)
            for lane in range(SC_LANES):
                total += plsc.load_gather(part_v, [first + lane])
            a_v[pl.ds(tl * slots + half * win + rb * SC_LANES, SC_LANES)] = total

    @pl.loop(0, n_tok // SC_BATCH)
    def _(bi):
        tok0 = base_tok + bi * SC_BATCH
        pltpu.sync_copy(idx_hbm.at[pl.ds(tok0 * slots, SC_BATCH * slots)], idx_v)
        pltpu.sync_copy(xe_hbm.at[pl.ds(tok0 * words, SC_BATCH * words)], xe_v)
        pltpu.sync_copy(xo_hbm.at[pl.ds(tok0 * words, SC_BATCH * words)], xo_v)
        gather(0, 0, rows_a, row_sem.at[0]).start()

        @pl.loop(0, SC_BATCH)
        def _(tl):
            gather(tl, 1, rows_b, row_sem.at[1]).start()
            gather(tl, 0, rows_a, row_sem.at[0]).wait()
            scores(tl, 0, rows_a)
            gather(jnp.minimum(tl + 1, SC_BATCH - 1), 0, rows_a, row_sem.at[0]).start()
            gather(tl, 1, rows_b, row_sem.at[1]).wait()
            scores(tl, 1, rows_b)

        gather(SC_BATCH - 1, 0, rows_a, row_sem.at[0]).wait()
        pltpu.sync_copy(a_v, out_hbm.at[pl.ds(tok0 * slots, SC_BATCH * slots)])


def _peer_scores_sc(idx, xe, xo, table, n_tok, slots):
    words = table.shape[1]
    assert n_tok % (SC_CORES * SC_SUBCORES * SC_BATCH) == 0
    mesh = plsc.VectorSubcoreMesh(core_axis_name="c", subcore_axis_name="s",
                                  num_cores=SC_CORES, num_subcores=SC_SUBCORES)
    return pl.kernel(
        functools.partial(_sc_score_kernel, slots=slots),
        out_type=jax.ShapeDtypeStruct((n_tok * slots,), F32),
        mesh=mesh,
        scratch_types=[
            pltpu.VMEM((SC_BATCH * slots,), I32),
            pltpu.VMEM((SC_BATCH * words,), F32),
            pltpu.VMEM((SC_BATCH * words,), F32),
            pltpu.VMEM((slots // 2, words), I32),
            pltpu.VMEM((slots // 2, words), I32),
            pltpu.VMEM((slots // 2 * SC_LANES,), F32),
            pltpu.VMEM((SC_BATCH * slots,), F32),
            pltpu.SemaphoreType.DMA((2,)),
        ],
        compiler_params=pltpu.CompilerParams(needs_layout_passes=False),
        name="peer_scores_sc",
    )(table, idx, xe, xo)


def _split_norm_kernel(x_ref, g_ref, xe_ref, xo_ref):
    d = x_ref.shape[1]
    xn = _rms(x_ref[...], g_ref[...])
    nchunk = d // (2 * LANES)
    xe_ref[...] = jnp.concatenate([xn[:, (2 * c) * LANES:(2 * c + 1) * LANES] for c in range(nchunk)], axis=1)
    xo_ref[...] = jnp.concatenate([xn[:, (2 * c + 1) * LANES:(2 * c + 2) * LANES] for c in range(nchunk)], axis=1)


def _split_norm(x2d, g, n_rows, tm):
    d = x2d.shape[1]
    first = (x2d.shape[0] - n_rows) // tm
    half = jax.ShapeDtypeStruct((n_rows, d // 2), F32)
    return pl.pallas_call(
        _split_norm_kernel,
        grid=(n_rows // tm,),
        in_specs=[pl.BlockSpec((tm, d), lambda i: (first + i, 0)), pl.BlockSpec((1, d), lambda i: (0, 0))],
        out_specs=[pl.BlockSpec((tm, d // 2), lambda i: (i, 0))] * 2,
        out_shape=[half, half],
        compiler_params=_cparams(("arbitrary",)),
        name="peer_split_norm",
    )(x2d, g.reshape(1, d))


def _score_finish_kernel(w_hbm, a_ref, gate_ref, w_ref):
    del w_hbm
    a = a_ref[0]
    gelu = 0.5 * a * (1.0 + lax.erf(a * (2.0 ** -0.5)))
    w_ref[0] = gate_ref[0].T * gelu


def _score_finish(w_full, a_sc, gate_t):
    nt, t, slots = w_full.shape
    n_sc = a_sc.shape[0]
    first = nt - n_sc
    return pl.pallas_call(
        _score_finish_kernel,
        grid=(n_sc,),
        in_specs=[
            pl.BlockSpec(memory_space=pl.ANY),
            pl.BlockSpec((1, t, slots), lambda i: (i, 0, 0)),
            pl.BlockSpec((1, slots, t), lambda i: (first + i, 0, 0)),
        ],
        out_specs=pl.BlockSpec((1, t, slots), lambda i: (first + i, 0, 0)),
        out_shape=jax.ShapeDtypeStruct(w_full.shape, F32),
        input_output_aliases={0: 0},
        compiler_params=_cparams(("arbitrary",)),
        name="peer_score_finish",
    )(w_full, a_sc, gate_t)


def _residual_norm_kernel(out_hbm, x_ref, p_ref, gf_ref, o_ref, *, final_norm):
    del out_hbm
    y = x_ref[...] + p_ref[...]
    if final_norm:
        y = _rms(y, gf_ref[...])
    o_ref[...] = y


def _residual_norm(out_full, x2d, p2d, g_final, final_norm, tm):
    t, d = p2d.shape
    first = (x2d.shape[0] - t) // tm
    return pl.pallas_call(
        functools.partial(_residual_norm_kernel, final_norm=final_norm),
        grid=(t // tm,),
        in_specs=[
            pl.BlockSpec(memory_space=pl.ANY),
            pl.BlockSpec((tm, d), lambda i: (first + i, 0)),
            pl.BlockSpec((tm, d), lambda i: (i, 0)),
            pl.BlockSpec((1, d), lambda i: (0, 0)),
        ],
        out_specs=pl.BlockSpec((tm, d), lambda i: (first + i, 0)),
        out_shape=jax.ShapeDtypeStruct(x2d.shape, F32),
        input_output_aliases={0: 0},
        compiler_params=_cparams(("arbitrary",)),
        name="peer_residual",
    )(out_full, x2d, p2d, g_final.reshape(1, d))


def _peer_ffn(x, g, w_q, k1, k2, u_emb, v_emb, g_final, final_norm):
    b, s, d = x.shape
    x2d = x.reshape(b * s, d)
    e_slot, gate_slot = _peer_router(x2d, g, w_q, k1, k2)
    u_tab, v_tab = _pack_pairs(u_emb), _pack_pairs(v_emb)
    nt, slots, t = e_slot.shape
    sc_quantum = SC_CORES * SC_SUBCORES * SC_BATCH // t
    nt_sc = nt * SC_TOKEN_PERCENT // 100 // sc_quantum * sc_quantum
    nt_tc = nt - nt_sc
    w_tok = _peer_scores(e_slot, gate_slot, x2d, g, u_tab, nt_tc)
    if nt_sc:
        rows_per_expert = d // (2 * LANES)
        tm = math.gcd(TOKEN_TILE, nt_sc * t)
        idx_sc = (e_slot[nt_tc:] // rows_per_expert).transpose(0, 2, 1).reshape(-1)
        xe, xo = _split_norm(x2d, g, nt_sc * t, tm)
        a_sc = _peer_scores_sc(idx_sc, xe.reshape(-1), xo.reshape(-1), u_tab.reshape(-1, d // 2), nt_sc * t, slots)
        w_tok = _score_finish(w_tok, a_sc.reshape(nt_sc, t, slots), gate_slot)
    out = _peer_values(e_slot, w_tok, x2d, v_tab, g_final, final_norm, nt_tc)
    if nt_sc:
        peer_sc = _peer_values_sc(idx_sc, w_tok[nt_tc:].reshape(-1), v_tab.reshape(-1, d // 2), nt_sc * t, d)
        out = _residual_norm(out, x2d, peer_sc, g_final, final_norm, tm)
    return out.reshape(b, s, d)


def _qkv_kernel(x_ref, g_ref, wqk_hi_ref, wqk_lo_ref, wv_ref, cos_ref, sin_ref,
                qt_ref, k0_ref, k1_ref, vt_ref, km_ref):
    tm, d = x_ref.shape[1], x_ref.shape[2]
    hn = _rms(x_ref[0], g_ref[...])
    hh, hl = _split(hn)
    qk = _dot3(hh, hl, wqk_hi_ref[...], wqk_lo_ref[...])
    v = _dot(hh, wv_ref[...])
    cos = jnp.concatenate([cos_ref[...]] * (d // LANES), axis=1)
    sin = jnp.concatenate([sin_ref[...]] * (d // LANES), axis=1)
    lane = lax.broadcasted_iota(I32, (tm, d), 1)
    first_half = (lane % HEAD_DIM) < (HEAD_DIM // 2)

    def rope(a):
        rot = jnp.where(first_half, pltpu.roll(a, d - HEAD_DIM // 2, 1), pltpu.roll(a, HEAD_DIM // 2, 1))
        return a * cos + rot * sin

    q = rope(qk[:, :d])
    k = rope(qk[:, d:])
    qt_ref[0] = q.T
    vt_ref[0] = v.T.astype(BF16)
    nb = tm // MOBA_BLOCK
    km_ref[0, 0] = jnp.mean(k.reshape(nb, MOBA_BLOCK, d), axis=1)
    row = lax.broadcasted_iota(I32, (tm, d), 0)
    block = (pl.program_id(1) * tm + row) // MOBA_BLOCK
    pair_lane = lane % LANES
    kb = k.astype(BF16)
    k0_ref[0] = jnp.where(pair_lane < HEAD_DIM, kb, jnp.where(pair_lane - HEAD_DIM == block, 1.0, 0.0).astype(BF16))
    k1_ref[0] = jnp.where(pair_lane >= HEAD_DIM, kb, jnp.where(pair_lane == block, 1.0, 0.0).astype(BF16))


def _qkv_rope(x, g, w_qkv, tm):
    b, s, d = x.shape
    half = HEAD_DIM // 2
    inv = ROPE_THETA ** (-jnp.arange(half, dtype=F32) / half)
    ang = jnp.arange(s).astype(F32)[:, None] * inv[None, :]
    cos, sin = jnp.cos(ang), jnp.sin(ang)
    cos128 = jnp.tile(jnp.concatenate([cos, cos], axis=1), (1, LANES // HEAD_DIM))
    sin128 = jnp.tile(jnp.concatenate([-sin, sin], axis=1), (1, LANES // HEAD_DIM))
    wqk_hi, wqk_lo = _split(w_qkv[:, :2 * d])
    wv = w_qkv[:, 2 * d:].astype(BF16)
    nb = tm // MOBA_BLOCK
    const = lambda i, j: (0, 0)
    qt, k0, k1, vt, km = pl.pallas_call(
        _qkv_kernel,
        grid=(b, s // tm),
        in_specs=[
            pl.BlockSpec((1, tm, d), lambda i, j: (i, j, 0)),
            pl.BlockSpec((1, d), const),
            pl.BlockSpec((d, 2 * d), const),
            pl.BlockSpec((d, 2 * d), const),
            pl.BlockSpec((d, d), const),
            pl.BlockSpec((tm, LANES), lambda i, j: (j, 0)),
            pl.BlockSpec((tm, LANES), lambda i, j: (j, 0)),
        ],
        out_specs=[
            pl.BlockSpec((1, d, tm), lambda i, j: (i, 0, j)),
            pl.BlockSpec((1, tm, d), lambda i, j: (i, j, 0)),
            pl.BlockSpec((1, tm, d), lambda i, j: (i, j, 0)),
            pl.BlockSpec((1, d, tm), lambda i, j: (i, 0, j)),
            pl.BlockSpec((1, 1, nb, d), lambda i, j: (i, j, 0, 0)),
        ],
        out_shape=[
            jax.ShapeDtypeStruct((b, d, s), F32),
            jax.ShapeDtypeStruct((b, s, d), BF16),
            jax.ShapeDtypeStruct((b, s, d), BF16),
            jax.ShapeDtypeStruct((b, d, s), BF16),
            jax.ShapeDtypeStruct((b, s // tm, nb, d), F32),
        ],
        compiler_params=_cparams(("arbitrary", "arbitrary")),
        name="qkv_rope",
    )(x, g.reshape(1, d), wqk_hi, wqk_lo, wv, cos128, sin128)
    return qt, (k0, k1), vt, km.reshape(b, s // MOBA_BLOCK, d)


def _moba_kernel(qt_ref, k0_ref, k1_ref, vt_ref, km_ref, o_ref, sa_ref, sb_ref):
    bs = MOBA_BLOCK
    nb = km_ref.shape[1]
    n_heads = LANES // HEAD_DIM
    k_refs = (k0_ref, k1_ref)
    j = pl.program_id(2)
    qt = qt_ref[0]
    km = km_ref[0]
    scale = HEAD_DIM ** -0.5 * 1.4426950408889634
    lane_km = lax.broadcasted_iota(I32, (nb, LANES), 1)
    blk = lax.broadcasted_iota(I32, (nb, bs), 0)
    zeros_pad = jnp.zeros((LANES - HEAD_DIM - nb, bs), F32)
    qh, ql = _split(qt)

    own = pl.ds(pl.multiple_of(j * bs, bs), bs)
    v_own = vt_ref[0, :, own]
    krow = lax.broadcasted_iota(I32, (bs, bs), 0)
    qcol = lax.broadcasted_iota(I32, (bs, bs), 1)

    q_augs, state = [], []
    for hh in range(n_heads):
        head_lo = hh * HEAD_DIM
        in_head_km = (lane_km >= head_lo) & (lane_km < head_lo + HEAD_DIM)
        kmh, kml = _split(jnp.where(in_head_km, km, 0.0))
        gate = _dot3(kmh, kml, qh, ql)
        valid = blk < j
        gate = jnp.where(valid, gate, -jnp.inf)
        sel = jnp.zeros((nb, bs), F32)
        for _ in range(MOBA_TOPK):
            m = jnp.max(gate, axis=0, keepdims=True)
            i = jnp.min(jnp.where(gate == m, blk, nb), axis=0, keepdims=True)
            pick = blk == i
            sel = jnp.where(pick, 1.0, sel)
            gate = jnp.where(pick, -jnp.inf, gate)
        bias_t = jnp.where((sel > 0.0) & valid, 0.0, MASK_NEG)
        q_head = qt[head_lo:head_lo + HEAD_DIM, :] * scale
        no_bias = jnp.zeros((LANES - HEAD_DIM, bs), F32)
        if hh == 0:
            q_aug = jnp.concatenate([q_head, bias_t, zeros_pad], axis=0)
            q_own = jnp.concatenate([q_head, no_bias], axis=0)
        else:
            q_aug = jnp.concatenate([bias_t, zeros_pad, q_head], axis=0)
            q_own = jnp.concatenate([no_bias, q_head], axis=0)
        q_augs.append(q_aug.astype(BF16))

        s_own = jnp.where(krow <= qcol, _dot(k_refs[hh][0, own, :], q_own.astype(BF16)), -1e30)
        m0 = jnp.max(s_own, axis=0, keepdims=True)
        p0 = jnp.exp2(s_own - m0)
        state += [m0, jnp.sum(p0, axis=0, keepdims=True), _dot(v_own, p0.astype(BF16))]

    chunk = KV_CHUNK * bs
    last_chunk = nb // KV_CHUNK - 1

    def score_chunk(c, s_ref):
        rows = pl.ds(pl.multiple_of(jnp.minimum(c, last_chunk) * chunk, chunk), chunk)
        for hh in range(n_heads):
            s_ref[hh] = _dot(k_refs[hh][0, rows, :], q_augs[hh])

    def attend(c, s_ref, state):
        vn = vt_ref[0, :, pl.ds(pl.multiple_of(c * chunk, chunk), chunk)]
        new_state = []
        for hh in range(n_heads):
            m, l, acc = state[3 * hh:3 * hh + 3]
            s = s_ref[hh]
            m_new = jnp.maximum(m, jnp.max(s, axis=0, keepdims=True))
            alpha = jnp.exp2(m - m_new)
            p = jnp.exp2(s - m_new)
            l = alpha * l + jnp.sum(p, axis=0, keepdims=True)
            acc = alpha * acc + _dot(vn, p.astype(BF16))
            new_state += [m_new, l, acc]
        return tuple(new_state)

    score_chunk(0, sa_ref)

    def body(i, state):
        score_chunk(2 * i + 1, sb_ref)
        state = attend(2 * i, sa_ref, state)
        score_chunk(2 * i + 2, sa_ref)
        return attend(2 * i + 1, sb_ref, state)

    state = lax.fori_loop(0, (j + 2 * KV_CHUNK - 1) // (2 * KV_CHUNK), body, tuple(state))
    halves = []
    for hh in range(n_heads):
        _, l, acc = state[3 * hh:3 * hh + 3]
        halves.append((acc / l)[hh * HEAD_DIM:(hh + 1) * HEAD_DIM, :])
    o_ref[0] = jnp.concatenate(halves, axis=0).T


def _moba_attention(qt, k01, vt, km):
    b, d, s = qt.shape
    nb = s // MOBA_BLOCK
    assert nb % (2 * KV_CHUNK) == 0
    return pl.pallas_call(
        _moba_kernel,
        grid=(b, d // LANES, nb),
        in_specs=[
            pl.BlockSpec((1, LANES, MOBA_BLOCK), lambda i, h, j: (i, h, j)),
            pl.BlockSpec((1, s, LANES), lambda i, h, j: (i, 0, h)),
            pl.BlockSpec((1, s, LANES), lambda i, h, j: (i, 0, h)),
            pl.BlockSpec((1, LANES, s), lambda i, h, j: (i, h, 0)),
            pl.BlockSpec((1, nb, LANES), lambda i, h, j: (i, 0, h)),
        ],
        out_specs=pl.BlockSpec((1, MOBA_BLOCK, LANES), lambda i, h, j: (i, j, h)),
        out_shape=jax.ShapeDtypeStruct((b, s, d), F32),
        scratch_shapes=[pltpu.VMEM((LANES // HEAD_DIM, KV_CHUNK * MOBA_BLOCK, MOBA_BLOCK), F32)] * 2,
        compiler_params=_cparams(("arbitrary", "arbitrary", "arbitrary")),
        name="moba_attention",
    )(qt, *k01, vt, km)


def _proj_residual_kernel(x_ref, a_ref, w_ref, o_ref):
    o_ref[...] = x_ref[...] + _dot(a_ref[...].astype(BF16), w_ref[...])


def _proj_residual(x2d, a2d, w, tm):
    t, d = x2d.shape
    return pl.pallas_call(
        _proj_residual_kernel,
        grid=(t // tm,),
        in_specs=[
            pl.BlockSpec((tm, d), lambda i: (i, 0)),
            pl.BlockSpec((tm, a2d.shape[1]), lambda i: (i, 0)),
            pl.BlockSpec(w.shape, lambda i: (0, 0)),
        ],
        out_specs=pl.BlockSpec((tm, d), lambda i: (i, 0)),
        out_shape=jax.ShapeDtypeStruct((t, d), F32),
        compiler_params=_cparams(("arbitrary",)),
        name="attn_out_proj",
    )(x2d, a2d, w.astype(BF16))


def _moba_mixer(x, g, w_qkv, w_o):
    b, s, d = x.shape
    tm = min(TOKEN_TILE, s)
    qt, k, vt, km = _qkv_rope(x, g, w_qkv, tm)
    attn = _moba_attention(qt, k, vt, km)
    return _proj_residual(x.reshape(b * s, d), attn.reshape(b * s, d), w_o, tm).reshape(b, s, d)


def kernel(x, norm_mix, norm_ffn, conv_w_in, conv_w, conv_w_out, attn_w_qkv, attn_w_o,
           peer_w_q, peer_k1, peer_k2, peer_u, peer_v, norm_final):
    depth = norm_mix.shape[0]
    tm = min(TOKEN_TILE, x.shape[1])
    for i in range(depth):
        j = i // 2
        if i % 2 == 0:
            x = _conv_mixer(x, norm_mix[i], conv_w_in[j], conv_w[j], conv_w_out[j], tm)
        else:
            x = _moba_mixer(x, norm_mix[i], attn_w_qkv[j], attn_w_o[j])
        x = _peer_ffn(x, norm_ffn[i], peer_w_q[i], peer_k1[i], peer_k2[i], peer_u[i], peer_v[i],
                      norm_final, final_norm=(i == depth - 1))
    return x
```

```python
import functools
import math

import jax
import jax.numpy as jnp
from jax import lax
from jax.experimental import pallas as pl
from jax.experimental.pallas import tpu as pltpu
from jax.experimental.pallas import tpu_sc as plsc

F32 = jnp.float32
BF16 = jnp.bfloat16
I32 = jnp.int32

RMS_EPS = 1e-6
N_HEADS = 16
HEAD_DIM = 64
MOBA_BLOCK = 256
MOBA_TOPK = 3
ROPE_THETA = 10000.0
PEER_HEADS = 8
PEER_NKEYS = 128
PEER_HALF = 64
PEER_TOPK = 16

LANES = 128
SUBLANES = 8
VMEM_LIMIT = 56 * 1024 * 1024
MASK_NEG = -1e9

TOKEN_TILE = 512
PEER_TILE = 128
OCTET = 8
SC_CORES = 2
SC_SUBCORES = 16
SC_LANES = 16
SC_BATCH = 16
SC_GROUP = 8
SC_SCORE_PERCENT = 34
SC_VALUE_PERCENT = 45
KV_CHUNK = 2


def _cparams(sem):
    return pltpu.CompilerParams(dimension_semantics=sem, vmem_limit_bytes=VMEM_LIMIT)


def _rms(x, g):
    ms = jnp.mean(x * x, axis=-1, keepdims=True)
    return x * lax.rsqrt(ms + RMS_EPS) * g


def _split(a):
    hi = a.astype(BF16)
    lo = (a - hi.astype(F32)).astype(BF16)
    return hi, lo


def _dot(a, b):
    return lax.dot_general(a, b, (((1,), (0,)), ((), ())), preferred_element_type=F32)


def _dot_t(a, b):
    return lax.dot_general(a, b, (((1,), (1,)), ((), ())), preferred_element_type=F32)


def _dot3(a_hi, a_lo, b_hi, b_lo):
    return _dot(a_hi, b_hi) + _dot(a_lo, b_hi) + _dot(a_hi, b_lo)


def _dot3_t(a_hi, a_lo, b_hi, b_lo):
    return _dot_t(a_hi, b_hi) + _dot_t(a_lo, b_hi) + _dot_t(a_hi, b_lo)


def _conv_mixer_kernel(x_ref, g_ref, win_ref, cw_ref, wout_ref, o_ref, ubuf_ref):
    tm, d = x_ref.shape[1], x_ref.shape[2]

    @pl.when(pl.program_id(1) == 0)
    def _():
        ubuf_ref[0:SUBLANES, :] = jnp.zeros((SUBLANES, d), F32)

    x = x_ref[0]
    hn = _rms(x, g_ref[...]).astype(BF16)
    bcz = _dot(hn, win_ref[...])
    b_gate, c_gate, z = bcz[:, :d], bcz[:, d:2 * d], bcz[:, 2 * d:]
    u = c_gate * z
    ubuf_ref[SUBLANES:SUBLANES + tm, :] = u
    u1 = ubuf_ref[SUBLANES - 1:SUBLANES - 1 + tm, :]
    u2 = ubuf_ref[SUBLANES - 2:SUBLANES - 2 + tm, :]
    cw = cw_ref[...]
    u_conv = cw[0:1, :] * u2 + cw[1:2, :] * u1 + cw[2:3, :] * u
    ubuf_ref[0:SUBLANES, :] = u[tm - SUBLANES:tm, :]
    y = (b_gate * u_conv).astype(BF16)
    o_ref[0] = x + _dot(y, wout_ref[...])


def _conv_mixer(x, g, w_in, conv_w, w_out, tm):
    b, s, d = x.shape
    return pl.pallas_call(
        _conv_mixer_kernel,
        grid=(b, s // tm),
        in_specs=[
            pl.BlockSpec((1, tm, d), lambda i, j: (i, j, 0)),
            pl.BlockSpec((1, d), lambda i, j: (0, 0)),
            pl.BlockSpec((d, 3 * d), lambda i, j: (0, 0)),
            pl.BlockSpec((3, d), lambda i, j: (0, 0)),
            pl.BlockSpec((d, d), lambda i, j: (0, 0)),
        ],
        out_specs=pl.BlockSpec((1, tm, d), lambda i, j: (i, j, 0)),
        out_shape=jax.ShapeDtypeStruct((b, s, d), F32),
        scratch_shapes=[pltpu.VMEM((tm + SUBLANES, d), F32)],
        compiler_params=_cparams(("arbitrary", "arbitrary")),
        name="conv_mixer",
    )(x, g.reshape(1, d), w_in.astype(BF16), conv_w, w_out.astype(BF16))


def _topk_rows(s, k, order=None, payload=None):
    if order is None:
        order = lax.broadcasted_iota(I32, s.shape, 0)
    big = jnp.iinfo(jnp.int32).max
    vals, outs = [], []
    for _ in range(k):
        m = jnp.max(s, axis=0, keepdims=True)
        i = jnp.min(jnp.where(s == m, order, big), axis=0, keepdims=True)
        pick = order == i
        vals.append(m)
        if payload is None:
            outs.append(i)
        else:
            outs.append(jnp.max(jnp.where(pick, payload, -1), axis=0, keepdims=True))
        s = jnp.where(pick, -jnp.inf, s)
    return jnp.concatenate(vals, axis=0), jnp.concatenate(outs, axis=0)


def _staircase(kk):
    groups = []
    for a in range(2):
        for b0 in range(0, kk // (a + 1), SUBLANES):
            groups.append((a, 0, b0, 1, lambda j, a=a, b0=b0: (a + 1) * (b0 + j + 1) <= kk))
    for b in range(kk // 3):
        for a0 in range(0, kk // (b + 1), SUBLANES):
            groups.append((a0, 1, b, 0, lambda j, a0=a0, b=b: (a0 + j >= 2) & ((a0 + j + 1) * (b + 1) <= kk)))
    return groups


def _staircase_topk(v1, i1, v2, i2, nk):
    kk, t = v1.shape
    j = lax.broadcasted_iota(I32, (SUBLANES, t), 0)

    def rows(x, x0, step):
        if step == 0:
            return jnp.broadcast_to(x[x0:x0 + 1, :], (SUBLANES, t))
        return x[x0:x0 + SUBLANES, :]

    cand, order, cidx = [], [], []
    for a0, a_step, b0, b_step, valid in _staircase(kk):
        ok = valid(j)
        cand.append(jnp.where(ok, rows(v1, a0, a_step) + rows(v2, b0, b_step), -jnp.inf))
        order.append(jnp.where(ok, (a0 + j * a_step) * kk + (b0 + j * b_step), jnp.iinfo(jnp.int32).max - 1))
        cidx.append(rows(i1, a0, a_step) * nk + rows(i2, b0, b_step))
    cat = lambda xs: jnp.concatenate(xs, axis=0)
    return _topk_rows(cat(cand), kk, order=cat(order), payload=cat(cidx))


def _router_kernel(x_ref, g_ref, wq_hi_ref, wq_lo_ref, k1_hi_ref, k1_lo_ref, k2_hi_ref, k2_lo_ref,
                   e_ref, gate_ref):
    table_rows = x_ref.shape[1] // (2 * LANES)
    xn = _rms(x_ref[...], g_ref[...])
    xh, xl = _split(xn)
    q = _dot3(xh, xl, wq_hi_ref[...], wq_lo_ref[...])
    nk = k1_hi_ref.shape[0]
    for h in range(PEER_HEADS):
        qh, ql = _split(q[:, h * LANES:(h + 1) * LANES])
        s1 = _dot3_t(k1_hi_ref[...], k1_lo_ref[...], qh, ql)
        s2 = _dot3_t(k2_hi_ref[...], k2_lo_ref[...], qh, ql)
        v1, i1 = _topk_rows(s1, PEER_TOPK)
        v2, i2 = _topk_rows(s2, PEER_TOPK)
        kk = PEER_TOPK
        top_s, e_idx = _staircase_topk(v1, i1, v2, i2, nk)
        p = jnp.exp(top_s - top_s[0:1, :])
        gate = p / jnp.sum(p, axis=0, keepdims=True)
        e_ref[0, h * kk:(h + 1) * kk, :] = e_idx * table_rows
        gate_ref[0, h * kk:(h + 1) * kk, :] = gate


def _pad_keys(k, lo):
    nk, half = k.shape
    out = jnp.zeros((nk, LANES), F32)
    return out.at[:, lo:lo + half].set(k)


def _peer_router(x2d, g, w_q, k1, k2):
    t, d = x2d.shape
    nt = t // PEER_TILE
    slots = PEER_HEADS * PEER_TOPK
    wq_hi, wq_lo = _split(w_q)
    k1_hi, k1_lo = _split(_pad_keys(k1, 0))
    k2_hi, k2_lo = _split(_pad_keys(k2, PEER_HALF))
    const = lambda i: (0, 0)
    kspec = pl.BlockSpec(k1_hi.shape, const)
    return pl.pallas_call(
        _router_kernel,
        grid=(nt,),
        in_specs=[
            pl.BlockSpec((PEER_TILE, d), lambda i: (i, 0)),
            pl.BlockSpec((1, d), const),
            pl.BlockSpec(wq_hi.shape, const),
            pl.BlockSpec(wq_lo.shape, const),
            kspec, kspec, kspec, kspec,
        ],
        out_specs=[
            pl.BlockSpec((1, slots, PEER_TILE), lambda i: (i, 0, 0)),
            pl.BlockSpec((1, slots, PEER_TILE), lambda i: (i, 0, 0)),
        ],
        out_shape=[
            jax.ShapeDtypeStruct((nt, slots, PEER_TILE), I32),
            jax.ShapeDtypeStruct((nt, slots, PEER_TILE), F32),
        ],
        compiler_params=_cparams(("arbitrary",)),
        name="peer_router",
    )(x2d, g.reshape(1, d), wq_hi, wq_lo, k1_hi, k1_lo, k2_hi, k2_lo)


def _gather_octet(e_ref, tab_ref, stage_ref, octet, nsub):
    t = e_ref.shape[2]
    for k in range(OCTET):
        for tok in range(t):
            off = e_ref[0, octet * OCTET + k, tok]
            stage_ref[k * (t // SUBLANES) + tok // SUBLANES,
                      pl.ds(tok % SUBLANES, nsub, stride=SUBLANES), :] = tab_ref[pl.ds(off, nsub), :]


def _octet_pipeline(n_octets, gather, consume, stage_a, stage_b):
    gather(0, stage_a)

    def pair(p, carry):
        gather(2 * p + 1, stage_b)
        consume(2 * p, stage_a)
        gather(jnp.minimum(2 * p + 2, n_octets - 1), stage_a)
        consume(2 * p + 1, stage_b)
        return carry
    lax.fori_loop(0, n_octets // 2, pair, 0)


def _score_kernel(e_ref, x_ref, g_ref, gate_ref, tab_ref, w_ref, stage_a, stage_b, a_ref):
    t, d = x_ref.shape
    nchunk = d // (2 * LANES)
    slots = e_ref.shape[1]
    rows = OCTET * t

    xn = _rms(x_ref[...], g_ref[...]).astype(BF16)
    x_even = jnp.concatenate([xn[:, (2 * c) * LANES:(2 * c + 1) * LANES] for c in range(nchunk)], axis=1)
    x_odd = jnp.concatenate([xn[:, (2 * c + 1) * LANES:(2 * c + 2) * LANES] for c in range(nchunk)], axis=1)
    rhs = jnp.concatenate([x_even, x_odd], axis=0)

    ri = lax.broadcasted_iota(I32, (2 * t, 2 * t), 0)
    ci = lax.broadcasted_iota(I32, (2 * t, 2 * t), 1)
    diag = (ci == (ri % 2) * t + ri // 2).astype(F32)

    def consume(o, stage_ref):
        planes = []
        for c in range(nchunk):
            plane = stage_ref[:, c * SUBLANES:(c + 1) * SUBLANES, :].reshape(rows, LANES)
            planes.append(pltpu.bitcast(plane, BF16))
        lhs = jnp.concatenate(planes, axis=1)
        prod = _dot_t(lhs, rhs)
        prod = prod.reshape(OCTET, 2 * t, 2 * t) * diag[None]
        a2 = jnp.sum(prod, axis=1)
        a_ref[pl.ds(pl.multiple_of(o * OCTET, OCTET), OCTET), :] = a2[:, :t] + a2[:, t:]

    gather = lambda o, stage_ref: _gather_octet(e_ref, tab_ref, stage_ref, o, nchunk)
    _octet_pipeline(slots // OCTET, gather, consume, stage_a, stage_b)

    a = a_ref[...]
    gelu = 0.5 * a * (1.0 + lax.erf(a * (2.0 ** -0.5)))
    w_ref[0] = (gate_ref[0] * gelu).T


def _peer_scores(e_t, gate_t, x2d, g, table, n_tiles):
    nt, slots, t = e_t.shape
    d = x2d.shape[1]
    stage = pltpu.VMEM((OCTET * t // SUBLANES, d // (2 * LANES) * SUBLANES, LANES), I32)
    return pl.pallas_call(
        _score_kernel,
        grid=(n_tiles,),
        in_specs=[
            pl.BlockSpec((1, slots, t), lambda i: (i, 0, 0), memory_space=pltpu.SMEM),
            pl.BlockSpec((t, d), lambda i: (i, 0)),
            pl.BlockSpec((1, d), lambda i: (0, 0)),
            pl.BlockSpec((1, slots, t), lambda i: (i, 0, 0)),
            pl.BlockSpec(table.shape, lambda i: (0, 0), pipeline_mode=pl.Buffered(1)),
        ],
        out_specs=pl.BlockSpec((1, t, slots), lambda i: (i, 0, 0)),
        out_shape=jax.ShapeDtypeStruct((nt, t, slots), F32),
        scratch_shapes=[stage, stage, pltpu.VMEM((slots, t), F32)],
        compiler_params=_cparams(("arbitrary",)),
        name="peer_scores",
    )(e_t, x2d, g.reshape(1, d), gate_t, table)


def _value_kernel(e_ref, w_ref, x_ref, gf_ref, tab_ref, o_ref, stage_a, stage_b, wb_ref, acc_ref, *, final_norm):
    t, d = x_ref.shape
    nchunk = d // (2 * LANES)
    slots = e_ref.shape[1]
    groups = t // SUBLANES

    acc_ref[...] = x_ref[...]

    def consume(o, stage_ref):
        w_oct = pltpu.roll(w_ref[0], lax.rem(slots - o * OCTET, slots), 1)
        for k in range(OCTET):
            wb_ref[k] = jnp.broadcast_to(w_oct[:, k:k + 1], (t, LANES))
        for c in range(nchunk):
            lo_cols = slice(2 * c * LANES, (2 * c + 1) * LANES)
            hi_cols = slice((2 * c + 1) * LANES, (2 * c + 2) * LANES)
            acc_lo, acc_hi = acc_ref[:, lo_cols], acc_ref[:, hi_cols]
            for k in range(OCTET):
                words = stage_ref[k * groups:(k + 1) * groups, c * SUBLANES:(c + 1) * SUBLANES, :]
                words = words.reshape(t, LANES)
                lo = pltpu.bitcast(words << 16, F32)
                hi = pltpu.bitcast(words & jnp.int32(-65536), F32)
                acc_lo = acc_lo + wb_ref[k] * lo
                acc_hi = acc_hi + wb_ref[k] * hi
            acc_ref[:, lo_cols] = acc_lo
            acc_ref[:, hi_cols] = acc_hi

    gather = lambda o, stage_ref: _gather_octet(e_ref, tab_ref, stage_ref, o, nchunk)
    _octet_pipeline(slots // OCTET, gather, consume, stage_a, stage_b)

    y = acc_ref[...]
    if final_norm:
        y = _rms(y, gf_ref[...])
    o_ref[...] = y


def _peer_values(e_t, w_t, x2d, table, g_final, final_norm, n_tiles):
    nt, slots, t = e_t.shape
    d = x2d.shape[1]
    stage = pltpu.VMEM((OCTET * t // SUBLANES, d // (2 * LANES) * SUBLANES, LANES), I32)
    return pl.pallas_call(
        functools.partial(_value_kernel, final_norm=final_norm),
        grid=(n_tiles,),
        in_specs=[
            pl.BlockSpec((1, slots, t), lambda i: (i, 0, 0), memory_space=pltpu.SMEM),
            pl.BlockSpec((1, t, slots), lambda i: (i, 0, 0)),
            pl.BlockSpec((t, d), lambda i: (i, 0)),
            pl.BlockSpec((1, d), lambda i: (0, 0)),
            pl.BlockSpec(table.shape, lambda i: (0, 0), pipeline_mode=pl.Buffered(1)),
        ],
        out_specs=pl.BlockSpec((t, d), lambda i: (i, 0)),
        out_shape=jax.ShapeDtypeStruct(x2d.shape, F32),
        scratch_shapes=[stage, stage, pltpu.VMEM((OCTET, t, LANES), F32), pltpu.VMEM((t, d), F32)],
        compiler_params=_cparams(("arbitrary",)),
        name="peer_values",
    )(e_t, w_t, x2d, g_final.reshape(1, d), table)


def _pack_pairs(w):
    e, d = w.shape
    bits = lax.bitcast_convert_type(w.astype(BF16), jnp.uint16).astype(jnp.uint32)
    bits = bits.reshape(e * d // (2 * LANES), 2 * LANES)
    packed = bits[:, :LANES] | (bits[:, LANES:] << 16)
    return lax.bitcast_convert_type(packed, I32)


def _sc_value_kernel(tab_hbm, idx_hbm, w_hbm, out_hbm, idx_v, w_v, rows_a, rows_b, acc_a, acc_b, row_sem, out_sem,
                     *, slots):
    n_tok = out_hbm.shape[0] // (SC_CORES * SC_SUBCORES)
    d = out_hbm.shape[1]
    words = d // 2
    win = slots // 2
    wid = lax.axis_index("s") * SC_CORES + lax.axis_index("c")
    base_tok = wid * n_tok

    def gather(tl, half, buf, sem):
        rows = idx_v.at[pl.ds(tl * slots + half * win, win)]
        return pltpu.make_async_copy(tab_hbm.at[rows], buf, sem)

    def accumulate(tl, half, buf, acc):
        for g in range(words // SC_LANES // SC_GROUP):
            def row(r, sums, g=g):
                slot = jnp.full((SC_LANES,), tl * slots + half * win + r, I32)
                wv = plsc.load_gather(w_v, [slot])
                out = []
                for j in range(SC_GROUP):
                    x = buf[r, pl.ds((g * SC_GROUP + j) * SC_LANES, SC_LANES)]
                    lo = lax.bitcast_convert_type(x << 16, F32)
                    hi = lax.bitcast_convert_type(x & jnp.int32(-65536), F32)
                    out += [sums[2 * j] + wv * lo, sums[2 * j + 1] + wv * hi]
                return tuple(out)
            zeros = tuple(jnp.zeros((SC_LANES,), F32) for _ in range(2 * SC_GROUP))
            sums = lax.fori_loop(0, win, row, zeros)
            for j in range(SC_GROUP):
                chunk, lane = divmod((g * SC_GROUP + j) * SC_LANES, LANES)
                plsc.addupdate(acc.at[pl.ds(2 * chunk * LANES + lane, SC_LANES)], sums[2 * j])
                plsc.addupdate(acc.at[pl.ds((2 * chunk + 1) * LANES + lane, SC_LANES)], sums[2 * j + 1])

    def write_out(acc, tok, sem):
        return pltpu.make_async_copy(acc, out_hbm.at[tok], sem)

    @pl.loop(0, n_tok // SC_BATCH)
    def _(bi):
        tok0 = base_tok + bi * SC_BATCH
        pltpu.sync_copy(idx_hbm.at[pl.ds(tok0 * slots, SC_BATCH * slots)], idx_v)
        pltpu.sync_copy(w_hbm.at[pl.ds(tok0 * slots, SC_BATCH * slots)], w_v)
        gather(0, 0, rows_a, row_sem.at[0]).start()

        @pl.loop(0, SC_BATCH // 2)
        def _(tp):
            for parity, acc in ((0, acc_a), (1, acc_b)):
                tl = 2 * tp + parity

                @pl.when(bi * SC_BATCH + tl >= 2)
                def _():
                    write_out(acc, tok0, out_sem.at[parity]).wait()
                for k in range(d // SC_LANES):
                    acc[pl.ds(k * SC_LANES, SC_LANES)] = jnp.zeros((SC_LANES,), F32)
                gather(tl, 1, rows_b, row_sem.at[1]).start()
                gather(tl, 0, rows_a, row_sem.at[0]).wait()
                accumulate(tl, 0, rows_a, acc)
                gather(jnp.minimum(tl + 1, SC_BATCH - 1), 0, rows_a, row_sem.at[0]).start()
                gather(tl, 1, rows_b, row_sem.at[1]).wait()
                accumulate(tl, 1, rows_b, acc)
                write_out(acc, tok0 + tl, out_sem.at[parity]).start()

        gather(SC_BATCH - 1, 0, rows_a, row_sem.at[0]).wait()

    write_out(acc_a, base_tok, out_sem.at[0]).wait()
    write_out(acc_b, base_tok, out_sem.at[1]).wait()


def _peer_values_sc(idx, w, table, n_tok, d):
    slots = idx.shape[0] // n_tok
    assert n_tok % (SC_CORES * SC_SUBCORES * SC_BATCH) == 0 and SC_BATCH % 2 == 0
    mesh = plsc.VectorSubcoreMesh(core_axis_name="c", subcore_axis_name="s",
                                  num_cores=SC_CORES, num_subcores=SC_SUBCORES)
    return pl.kernel(
        functools.partial(_sc_value_kernel, slots=slots),
        out_type=jax.ShapeDtypeStruct((n_tok, d), F32),
        mesh=mesh,
        scratch_types=[
            pltpu.VMEM((SC_BATCH * slots,), I32),
            pltpu.VMEM((SC_BATCH * slots,), F32),
            pltpu.VMEM((slots // 2, d // 2), I32),
            pltpu.VMEM((slots // 2, d // 2), I32),
            pltpu.VMEM((d,), F32),
            pltpu.VMEM((d,), F32),
            pltpu.SemaphoreType.DMA((2,)),
            pltpu.SemaphoreType.DMA((2,)),
        ],
        compiler_params=pltpu.CompilerParams(needs_layout_passes=False),
        name="peer_values_sc",
    )(table, idx, w)


def _sc_score_kernel(tab_hbm, idx_hbm, xe_hbm, xo_hbm, out_hbm, idx_v, xe_v, xo_v, rows_a, rows_b, part_v, a_v, row_sem,
                     *, slots):
    n_tok = out_hbm.shape[0] // slots // (SC_CORES * SC_SUBCORES)
    words = rows_a.shape[1]
    win = slots // 2
    n_groups = words // SC_LANES // SC_GROUP
    wid = lax.axis_index("s") * SC_CORES + lax.axis_index("c")
    base_tok = wid * n_tok
    lane_id = lax.iota(I32, SC_LANES)

    def gather(tl, half, buf, sem):
        rows = idx_v.at[pl.ds(tl * slots + half * win, win)]
        return pltpu.make_async_copy(tab_hbm.at[rows], buf, sem)

    def scores(tl, half, buf):
        for r in range(win):
            part_v[pl.ds(r * SC_LANES, SC_LANES)] = jnp.zeros((SC_LANES,), F32)
        for g in range(n_groups):
            xs = []
            for j in range(SC_GROUP):
                at = pl.ds(tl * words + (g * SC_GROUP + j) * SC_LANES, SC_LANES)
                xs += [xe_v[at], xo_v[at]]

            @pl.loop(0, win)
            def _(r, g=g, xs=xs):
                sums = [jnp.zeros((SC_LANES,), F32) for _ in range(4)]
                for j in range(SC_GROUP):
                    x = buf[r, pl.ds((g * SC_GROUP + j) * SC_LANES, SC_LANES)]
                    lo = lax.bitcast_convert_type(x << 16, F32)
                    hi = lax.bitcast_convert_type(x & jnp.int32(-65536), F32)
                    sums[(2 * j) % 4] += lo * xs[2 * j]
                    sums[(2 * j + 1) % 4] += hi * xs[2 * j + 1]
                plsc.addupdate(part_v.at[pl.ds(r * SC_LANES, SC_LANES)], (sums[0] + sums[1]) + (sums[2] + sums[3]))
        for rb in range(win // SC_LANES):
            first = (lane_id + rb * SC_LANES) * SC_LANES
            total = jnp.zeros((SC_LANES,), F32)
            for lane in range(SC_LANES):
                total += plsc.load_gather(part_v, [first + lane])
            a_v[pl.ds(tl * slots + half * win + rb * SC_LANES, SC_LANES)] = total

    @pl.loop(0, n_tok // SC_BATCH)
    def _(bi):
        tok0 = base_tok + bi * SC_BATCH
        pltpu.sync_copy(idx_hbm.at[pl.ds(tok0 * slots, SC_BATCH * slots)], idx_v)
        pltpu.sync_copy(xe_hbm.at[pl.ds(tok0 * words, SC_BATCH * words)], xe_v)
        pltpu.sync_copy(xo_hbm.at[pl.ds(tok0 * words, SC_BATCH * words)], xo_v)
        gather(0, 0, rows_a, row_sem.at[0]).start()

        @pl.loop(0, SC_BATCH)
        def _(tl):
            gather(tl, 1, rows_b, row_sem.at[1]).start()
            gather(tl, 0, rows_a, row_sem.at[0]).wait()
            scores(tl, 0, rows_a)
            gather(jnp.minimum(tl + 1, SC_BATCH - 1), 0, rows_a, row_sem.at[0]).start()
            gather(tl, 1, rows_b, row_sem.at[1]).wait()
            scores(tl, 1, rows_b)

        gather(SC_BATCH - 1, 0, rows_a, row_sem.at[0]).wait()
        pltpu.sync_copy(a_v, out_hbm.at[pl.ds(tok0 * slots, SC_BATCH * slots)])


def _peer_scores_sc(idx, xe, xo, table, n_tok, slots):
    words = table.shape[1]
    assert n_tok % (SC_CORES * SC_SUBCORES * SC_BATCH) == 0
    mesh = plsc.VectorSubcoreMesh(core_axis_name="c", subcore_axis_name="s",
                                  num_cores=SC_CORES, num_subcores=SC_SUBCORES)
    return pl.kernel(
        functools.partial(_sc_score_kernel, slots=slots),
        out_type=jax.ShapeDtypeStruct((n_tok * slots,), F32),
        mesh=mesh,
        scratch_types=[
            pltpu.VMEM((SC_BATCH * slots,), I32),
            pltpu.VMEM((SC_BATCH * words,), F32),
            pltpu.VMEM((SC_BATCH * words,), F32),
            pltpu.VMEM((slots // 2, words), I32),
            pltpu.VMEM((slots // 2, words), I32),
            pltpu.VMEM((slots // 2 * SC_LANES,), F32),
            pltpu.VMEM((SC_BATCH * slots,), F32),
            pltpu.SemaphoreType.DMA((2,)),
        ],
        compiler_params=pltpu.CompilerParams(needs_layout_passes=False),
        name="peer_scores_sc",
    )(table, idx, xe, xo)


def _split_norm_kernel(x_ref, g_ref, xe_ref, xo_ref):
    d = x_ref.shape[1]
    xn = _rms(x_ref[...], g_ref[...])
    nchunk = d // (2 * LANES)
    xe_ref[...] = jnp.concatenate([xn[:, (2 * c) * LANES:(2 * c + 1) * LANES] for c in range(nchunk)], axis=1)
    xo_ref[...] = jnp.concatenate([xn[:, (2 * c + 1) * LANES:(2 * c + 2) * LANES] for c in range(nchunk)], axis=1)


def _split_norm(x2d, g, n_rows, tm):
    d = x2d.shape[1]
    first = (x2d.shape[0] - n_rows) // tm
    half = jax.ShapeDtypeStruct((n_rows, d // 2), F32)
    return pl.pallas_call(
        _split_norm_kernel,
        grid=(n_rows // tm,),
        in_specs=[pl.BlockSpec((tm, d), lambda i: (first + i, 0)), pl.BlockSpec((1, d), lambda i: (0, 0))],
        out_specs=[pl.BlockSpec((tm, d // 2), lambda i: (i, 0))] * 2,
        out_shape=[half, half],
        compiler_params=_cparams(("arbitrary",)),
        name="peer_split_norm",
    )(x2d, g.reshape(1, d))


def _score_finish_kernel(w_hbm, a_ref, gate_ref, w_ref):
    del w_hbm
    a = a_ref[0]
    gelu = 0.5 * a * (1.0 + lax.erf(a * (2.0 ** -0.5)))
    w_ref[0] = gate_ref[0].T * gelu


def _score_finish(w_full, a_sc, gate_t):
    nt, t, slots = w_full.shape
    n_sc = a_sc.shape[0]
    first = nt - n_sc
    return pl.pallas_call(
        _score_finish_kernel,
        grid=(n_sc,),
        in_specs=[
            pl.BlockSpec(memory_space=pl.ANY),
            pl.BlockSpec((1, t, slots), lambda i: (i, 0, 0)),
            pl.BlockSpec((1, slots, t), lambda i: (first + i, 0, 0)),
        ],
        out_specs=pl.BlockSpec((1, t, slots), lambda i: (first + i, 0, 0)),
        out_shape=jax.ShapeDtypeStruct(w_full.shape, F32),
        input_output_aliases={0: 0},
        compiler_params=_cparams(("arbitrary",)),
        name="peer_score_finish",
    )(w_full, a_sc, gate_t)


def _residual_norm_kernel(out_hbm, x_ref, p_ref, gf_ref, o_ref, *, final_norm):
    del out_hbm
    y = x_ref[...] + p_ref[...]
    if final_norm:
        y = _rms(y, gf_ref[...])
    o_ref[...] = y


def _residual_norm(out_full, x2d, p2d, g_final, final_norm, tm):
    t, d = p2d.shape
    first = (x2d.shape[0] - t) // tm
    return pl.pallas_call(
        functools.partial(_residual_norm_kernel, final_norm=final_norm),
        grid=(t // tm,),
        in_specs=[
            pl.BlockSpec(memory_space=pl.ANY),
            pl.BlockSpec((tm, d), lambda i: (first + i, 0)),
            pl.BlockSpec((tm, d), lambda i: (i, 0)),
            pl.BlockSpec((1, d), lambda i: (0, 0)),
        ],
        out_specs=pl.BlockSpec((tm, d), lambda i: (first + i, 0)),
        out_shape=jax.ShapeDtypeStruct(x2d.shape, F32),
        input_output_aliases={0: 0},
        compiler_params=_cparams(("arbitrary",)),
        name="peer_residual",
    )(out_full, x2d, p2d, g_final.reshape(1, d))


def _peer_ffn(x, g, w_q, k1, k2, u_emb, v_emb, g_final, final_norm):
    b, s, d = x.shape
    x2d = x.reshape(b * s, d)
    e_slot, gate_slot = _peer_router(x2d, g, w_q, k1, k2)
    u_tab, v_tab = _pack_pairs(u_emb), _pack_pairs(v_emb)
    nt, slots, t = e_slot.shape
    sc_quantum = SC_CORES * SC_SUBCORES * SC_BATCH // t
    ns_score = nt * SC_SCORE_PERCENT // 100 // sc_quantum * sc_quantum
    ns_value = nt * SC_VALUE_PERCENT // 100 // sc_quantum * sc_quantum
    ns_max = max(ns_score, ns_value)
    rows_per_expert = d // (2 * LANES)
    idx_sc = (e_slot[nt - ns_max:] // rows_per_expert).transpose(0, 2, 1).reshape(-1)

    w_tok = _peer_scores(e_slot, gate_slot, x2d, g, u_tab, nt - ns_score)
    if ns_score:
        xe, xo = _split_norm(x2d, g, ns_score * t, math.gcd(TOKEN_TILE, ns_score * t))
        a_sc = _peer_scores_sc(idx_sc[(ns_max - ns_score) * t * slots:], xe.reshape(-1), xo.reshape(-1),
                               u_tab.reshape(-1, d // 2), ns_score * t, slots)
        w_tok = _score_finish(w_tok, a_sc.reshape(ns_score, t, slots), gate_slot)
    out = _peer_values(e_slot, w_tok, x2d, v_tab, g_final, final_norm, nt - ns_value)
    if ns_value:
        peer_sc = _peer_values_sc(idx_sc[(ns_max - ns_value) * t * slots:], w_tok[nt - ns_value:].reshape(-1),
                                  v_tab.reshape(-1, d // 2), ns_value * t, d)
        out = _residual_norm(out, x2d, peer_sc, g_final, final_norm, math.gcd(TOKEN_TILE, ns_value * t))
    return out.reshape(b, s, d)


def _qkv_kernel(x_ref, g_ref, wqk_hi_ref, wqk_lo_ref, wv_ref, cos_ref, sin_ref,
                qt_ref, k0_ref, k1_ref, vt_ref, km_ref):
    tm, d = x_ref.shape[1], x_ref.shape[2]
    hn = _rms(x_ref[0], g_ref[...])
    hh, hl = _split(hn)
    qk = _dot3(hh, hl, wqk_hi_ref[...], wqk_lo_ref[...])
    v = _dot(hh, wv_ref[...])
    cos = jnp.concatenate([cos_ref[...]] * (d // LANES), axis=1)
    sin = jnp.concatenate([sin_ref[...]] * (d // LANES), axis=1)
    lane = lax.broadcasted_iota(I32, (tm, d), 1)
    first_half = (lane % HEAD_DIM) < (HEAD_DIM // 2)

    def rope(a):
        rot = jnp.where(first_half, pltpu.roll(a, d - HEAD_DIM // 2, 1), pltpu.roll(a, HEAD_DIM // 2, 1))
        return a * cos + rot * sin

    q = rope(qk[:, :d])
    k = rope(qk[:, d:])
    qt_ref[0] = q.T
    vt_ref[0] = v.T.astype(BF16)
    nb = tm // MOBA_BLOCK
    km_ref[0, 0] = jnp.mean(k.reshape(nb, MOBA_BLOCK, d), axis=1)
    row = lax.broadcasted_iota(I32, (tm, d), 0)
    block = (pl.program_id(1) * tm + row) // MOBA_BLOCK
    pair_lane = lane % LANES
    kb = k.astype(BF16)
    k0_ref[0] = jnp.where(pair_lane < HEAD_DIM, kb, jnp.where(pair_lane - HEAD_DIM == block, 1.0, 0.0).astype(BF16))
    k1_ref[0] = jnp.where(pair_lane >= HEAD_DIM, kb, jnp.where(pair_lane == block, 1.0, 0.0).astype(BF16))


def _qkv_rope(x, g, w_qkv, tm):
    b, s, d = x.shape
    half = HEAD_DIM // 2
    inv = ROPE_THETA ** (-jnp.arange(half, dtype=F32) / half)
    ang = jnp.arange(s).astype(F32)[:, None] * inv[None, :]
    cos, sin = jnp.cos(ang), jnp.sin(ang)
    cos128 = jnp.tile(jnp.concatenate([cos, cos], axis=1), (1, LANES // HEAD_DIM))
    sin128 = jnp.tile(jnp.concatenate([-sin, sin], axis=1), (1, LANES // HEAD_DIM))
    wqk_hi, wqk_lo = _split(w_qkv[:, :2 * d])
    wv = w_qkv[:, 2 * d:].astype(BF16)
    nb = tm // MOBA_BLOCK
    const = lambda i, j: (0, 0)
    qt, k0, k1, vt, km = pl.pallas_call(
        _qkv_kernel,
        grid=(b, s // tm),
        in_specs=[
            pl.BlockSpec((1, tm, d), lambda i, j: (i, j, 0)),
            pl.BlockSpec((1, d), const),
            pl.BlockSpec((d, 2 * d), const),
            pl.BlockSpec((d, 2 * d), const),
            pl.BlockSpec((d, d), const),
            pl.BlockSpec((tm, LANES), lambda i, j: (j, 0)),
            pl.BlockSpec((tm, LANES), lambda i, j: (j, 0)),
        ],
        out_specs=[
            pl.BlockSpec((1, d, tm), lambda i, j: (i, 0, j)),
            pl.BlockSpec((1, tm, d), lambda i, j: (i, j, 0)),
            pl.BlockSpec((1, tm, d), lambda i, j: (i, j, 0)),
            pl.BlockSpec((1, d, tm), lambda i, j: (i, 0, j)),
            pl.BlockSpec((1, 1, nb, d), lambda i, j: (i, j, 0, 0)),
        ],
        out_shape=[
            jax.ShapeDtypeStruct((b, d, s), F32),
            jax.ShapeDtypeStruct((b, s, d), BF16),
            jax.ShapeDtypeStruct((b, s, d), BF16),
            jax.ShapeDtypeStruct((b, d, s), BF16),
            jax.ShapeDtypeStruct((b, s // tm, nb, d), F32),
        ],
        compiler_params=_cparams(("arbitrary", "arbitrary")),
        name="qkv_rope",
    )(x, g.reshape(1, d), wqk_hi, wqk_lo, wv, cos128, sin128)
    return qt, (k0, k1), vt, km.reshape(b, s // MOBA_BLOCK, d)


def _moba_kernel(qt_ref, k0_ref, k1_ref, vt_ref, km_ref, o_ref, sa_ref, sb_ref):
    bs = MOBA_BLOCK
    nb = km_ref.shape[1]
    n_heads = LANES // HEAD_DIM
    k_refs = (k0_ref, k1_ref)
    j = pl.program_id(2)
    qt = qt_ref[0]
    km = km_ref[0]
    scale = HEAD_DIM ** -0.5 * 1.4426950408889634
    lane_km = lax.broadcasted_iota(I32, (nb, LANES), 1)
    blk = lax.broadcasted_iota(I32, (nb, bs), 0)
    zeros_pad = jnp.zeros((LANES - HEAD_DIM - nb, bs), F32)
    qh, ql = _split(qt)

    own = pl.ds(pl.multiple_of(j * bs, bs), bs)
    v_own = vt_ref[0, :, own]
    krow = lax.broadcasted_iota(I32, (bs, bs), 0)
    qcol = lax.broadcasted_iota(I32, (bs, bs), 1)

    q_augs, state = [], []
    for hh in range(n_heads):
        head_lo = hh * HEAD_DIM
        in_head_km = (lane_km >= head_lo) & (lane_km < head_lo + HEAD_DIM)
        kmh, kml = _split(jnp.where(in_head_km, km, 0.0))
        gate = _dot3(kmh, kml, qh, ql)
        valid = blk < j
        gate = jnp.where(valid, gate, -jnp.inf)
        sel = jnp.zeros((nb, bs), F32)
        for _ in range(MOBA_TOPK):
            m = jnp.max(gate, axis=0, keepdims=True)
            i = jnp.min(jnp.where(gate == m, blk, nb), axis=0, keepdims=True)
            pick = blk == i
            sel = jnp.where(pick, 1.0, sel)
            gate = jnp.where(pick, -jnp.inf, gate)
        bias_t = jnp.where((sel > 0.0) & valid, 0.0, MASK_NEG)
        q_head = qt[head_lo:head_lo + HEAD_DIM, :] * scale
        no_bias = jnp.zeros((LANES - HEAD_DIM, bs), F32)
        if hh == 0:
            q_aug = jnp.concatenate([q_head, bias_t, zeros_pad], axis=0)
            q_own = jnp.concatenate([q_head, no_bias], axis=0)
        else:
            q_aug = jnp.concatenate([bias_t, zeros_pad, q_head], axis=0)
            q_own = jnp.concatenate([no_bias, q_head], axis=0)
        q_augs.append(q_aug.astype(BF16))

        s_own = jnp.where(krow <= qcol, _dot(k_refs[hh][0, own, :], q_own.astype(BF16)), -1e30)
        m0 = jnp.max(s_own, axis=0, keepdims=True)
        p0 = jnp.exp2(s_own - m0)
        state += [m0, jnp.sum(p0, axis=0, keepdims=True), _dot(v_own, p0.astype(BF16))]

    chunk = KV_CHUNK * bs
    last_chunk = nb // KV_CHUNK - 1

    def score_chunk(c, s_ref):
        rows = pl.ds(pl.multiple_of(jnp.minimum(c, last_chunk) * chunk, chunk), chunk)
        for hh in range(n_heads):
            s_ref[hh] = _dot(k_refs[hh][0, rows, :], q_augs[hh])

    def attend(c, s_ref, state):
        vn = vt_ref[0, :, pl.ds(pl.multiple_of(c * chunk, chunk), chunk)]
        new_state = []
        for hh in range(n_heads):
            m, l, acc = state[3 * hh:3 * hh + 3]
            s = s_ref[hh]
            m_new = jnp.maximum(m, jnp.max(s, axis=0, keepdims=True))
            alpha = jnp.exp2(m - m_new)
            p = jnp.exp2(s - m_new)
            l = alpha * l + jnp.sum(p, axis=0, keepdims=True)
            acc = alpha * acc + _dot(vn, p.astype(BF16))
            new_state += [m_new, l, acc]
        return tuple(new_state)

    score_chunk(0, sa_ref)

    def body(i, state):
        score_chunk(2 * i + 1, sb_ref)
        state = attend(2 * i, sa_ref, state)
        score_chunk(2 * i + 2, sa_ref)
        return attend(2 * i + 1, sb_ref, state)

    state = lax.fori_loop(0, (j + 2 * KV_CHUNK - 1) // (2 * KV_CHUNK), body, tuple(state))
    halves = []
    for hh in range(n_heads):
        _, l, acc = state[3 * hh:3 * hh + 3]
        halves.append((acc / l)[hh * HEAD_DIM:(hh + 1) * HEAD_DIM, :])
    o_ref[0] = jnp.concatenate(halves, axis=0).T


def _moba_attention(qt, k01, vt, km):
    b, d, s = qt.shape
    nb = s // MOBA_BLOCK
    assert nb % (2 * KV_CHUNK) == 0
    return pl.pallas_call(
        _moba_kernel,
        grid=(b, d // LANES, nb),
        in_specs=[
            pl.BlockSpec((1, LANES, MOBA_BLOCK), lambda i, h, j: (i, h, j)),
            pl.BlockSpec((1, s, LANES), lambda i, h, j: (i, 0, h)),
            pl.BlockSpec((1, s, LANES), lambda i, h, j: (i, 0, h)),
            pl.BlockSpec((1, LANES, s), lambda i, h, j: (i, h, 0)),
            pl.BlockSpec((1, nb, LANES), lambda i, h, j: (i, 0, h)),
        ],
        out_specs=pl.BlockSpec((1, MOBA_BLOCK, LANES), lambda i, h, j: (i, j, h)),
        out_shape=jax.ShapeDtypeStruct((b, s, d), F32),
        scratch_shapes=[pltpu.VMEM((LANES // HEAD_DIM, KV_CHUNK * MOBA_BLOCK, MOBA_BLOCK), F32)] * 2,
        compiler_params=_cparams(("arbitrary", "arbitrary", "arbitrary")),
        name="moba_attention",
    )(qt, *k01, vt, km)


def _proj_residual_kernel(x_ref, a_ref, w_ref, o_ref):
    o_ref[...] = x_ref[...] + _dot(a_ref[...].astype(BF16), w_ref[...])


def _proj_residual(x2d, a2d, w, tm):
    t, d = x2d.shape
    return pl.pallas_call(
        _proj_residual_kernel,
        grid=(t // tm,),
        in_specs=[
            pl.BlockSpec((tm, d), lambda i: (i, 0)),
            pl.BlockSpec((tm, a2d.shape[1]), lambda i: (i, 0)),
            pl.BlockSpec(w.shape, lambda i: (0, 0)),
        ],
        out_specs=pl.BlockSpec((tm, d), lambda i: (i, 0)),
        out_shape=jax.ShapeDtypeStruct((t, d), F32),
        compiler_params=_cparams(("arbitrary",)),
        name="attn_out_proj",
    )(x2d, a2d, w.astype(BF16))


def _moba_mixer(x, g, w_qkv, w_o):
    b, s, d = x.shape
    tm = min(TOKEN_TILE, s)
    qt, k, vt, km = _qkv_rope(x, g, w_qkv, tm)
    attn = _moba_attention(qt, k, vt, km)
    return _proj_residual(x.reshape(b * s, d), attn.reshape(b * s, d), w_o, tm).reshape(b, s, d)


def kernel(x, norm_mix, norm_ffn, conv_w_in, conv_w, conv_w_out, attn_w_qkv, attn_w_o,
           peer_w_q, peer_k1, peer_k2, peer_u, peer_v, norm_final):
    depth = norm_mix.shape[0]
    tm = min(TOKEN_TILE, x.shape[1])
    for i in range(depth):
        j = i // 2
        if i % 2 == 0:
            x = _conv_mixer(x, norm_mix[i], conv_w_in[j], conv_w[j], conv_w_out[j], tm)
        else:
            x = _moba_mixer(x, norm_mix[i], attn_w_qkv[j], attn_w_o[j])
        x = _peer_ffn(x, norm_ffn[i], peer_w_q[i], peer_k1[i], peer_k2[i], peer_u[i], peer_v[i],
                      norm_final, final_norm=(i == depth - 1))
    return x
```

```python
import functools
import math

import jax
import jax.numpy as jnp
from jax import lax
from jax.experimental import pallas as pl
from jax.experimental.pallas import tpu as pltpu
from jax.experimental.pallas import tpu_sc as plsc

F32 = jnp.float32
BF16 = jnp.bfloat16
I32 = jnp.int32

RMS_EPS = 1e-6
N_HEADS = 16
HEAD_DIM = 64
MOBA_BLOCK = 256
MOBA_TOPK = 3
ROPE_THETA = 10000.0
PEER_HEADS = 8
PEER_NKEYS = 128
PEER_HALF = 64
PEER_TOPK = 16

LANES = 128
SUBLANES = 8
VMEM_LIMIT = 56 * 1024 * 1024
MASK_NEG = -1e9

TOKEN_TILE = 512
PEER_TILE = 128
OCTET = 8
SC_CORES = 2
SC_SUBCORES = 16
SC_LANES = 16
SC_BATCH = 16
SC_GROUP = 8
SC_ROW_UNROLL = 4
SC_SCORE_PERCENT = 40
SC_VALUE_PERCENT = 50
KV_CHUNK = 2


def _cparams(sem):
    return pltpu.CompilerParams(dimension_semantics=sem, vmem_limit_bytes=VMEM_LIMIT)


def _rms(x, g):
    ms = jnp.mean(x * x, axis=-1, keepdims=True)
    return x * lax.rsqrt(ms + RMS_EPS) * g


def _split(a):
    hi = a.astype(BF16)
    lo = (a - hi.astype(F32)).astype(BF16)
    return hi, lo


def _dot(a, b):
    return lax.dot_general(a, b, (((1,), (0,)), ((), ())), preferred_element_type=F32)


def _dot_t(a, b):
    return lax.dot_general(a, b, (((1,), (1,)), ((), ())), preferred_element_type=F32)


def _dot3(a_hi, a_lo, b_hi, b_lo):
    return _dot(a_hi, b_hi) + _dot(a_lo, b_hi) + _dot(a_hi, b_lo)


def _dot3_t(a_hi, a_lo, b_hi, b_lo):
    return _dot_t(a_hi, b_hi) + _dot_t(a_lo, b_hi) + _dot_t(a_hi, b_lo)


def _conv_mixer_kernel(x_ref, g_ref, win_ref, cw_ref, wout_ref, o_ref, ubuf_ref):
    tm, d = x_ref.shape[1], x_ref.shape[2]

    @pl.when(pl.program_id(1) == 0)
    def _():
        ubuf_ref[0:SUBLANES, :] = jnp.zeros((SUBLANES, d), F32)

    x = x_ref[0]
    hn = _rms(x, g_ref[...]).astype(BF16)
    bcz = _dot(hn, win_ref[...])
    b_gate, c_gate, z = bcz[:, :d], bcz[:, d:2 * d], bcz[:, 2 * d:]
    u = c_gate * z
    ubuf_ref[SUBLANES:SUBLANES + tm, :] = u
    u1 = ubuf_ref[SUBLANES - 1:SUBLANES - 1 + tm, :]
    u2 = ubuf_ref[SUBLANES - 2:SUBLANES - 2 + tm, :]
    cw = cw_ref[...]
    u_conv = cw[0:1, :] * u2 + cw[1:2, :] * u1 + cw[2:3, :] * u
    ubuf_ref[0:SUBLANES, :] = u[tm - SUBLANES:tm, :]
    y = (b_gate * u_conv).astype(BF16)
    o_ref[0] = x + _dot(y, wout_ref[...])


def _conv_mixer(x, g, w_in, conv_w, w_out, tm):
    b, s, d = x.shape
    return pl.pallas_call(
        _conv_mixer_kernel,
        grid=(b, s // tm),
        in_specs=[
            pl.BlockSpec((1, tm, d), lambda i, j: (i, j, 0)),
            pl.BlockSpec((1, d), lambda i, j: (0, 0)),
            pl.BlockSpec((d, 3 * d), lambda i, j: (0, 0)),
            pl.BlockSpec((3, d), lambda i, j: (0, 0)),
            pl.BlockSpec((d, d), lambda i, j: (0, 0)),
        ],
        out_specs=pl.BlockSpec((1, tm, d), lambda i, j: (i, j, 0)),
        out_shape=jax.ShapeDtypeStruct((b, s, d), F32),
        scratch_shapes=[pltpu.VMEM((tm + SUBLANES, d), F32)],
        compiler_params=_cparams(("arbitrary", "arbitrary")),
        name="conv_mixer",
    )(x, g.reshape(1, d), w_in.astype(BF16), conv_w, w_out.astype(BF16))


def _topk_rows(s, k, order=None, payload=None):
    if order is None:
        order = lax.broadcasted_iota(I32, s.shape, 0)
    big = jnp.iinfo(jnp.int32).max
    vals, outs = [], []
    for _ in range(k):
        m = jnp.max(s, axis=0, keepdims=True)
        i = jnp.min(jnp.where(s == m, order, big), axis=0, keepdims=True)
        pick = order == i
        vals.append(m)
        if payload is None:
            outs.append(i)
        else:
            outs.append(jnp.max(jnp.where(pick, payload, -1), axis=0, keepdims=True))
        s = jnp.where(pick, -jnp.inf, s)
    return jnp.concatenate(vals, axis=0), jnp.concatenate(outs, axis=0)


def _staircase(kk):
    groups = []
    for a in range(2):
        for b0 in range(0, kk // (a + 1), SUBLANES):
            groups.append((a, 0, b0, 1, lambda j, a=a, b0=b0: (a + 1) * (b0 + j + 1) <= kk))
    for b in range(kk // 3):
        for a0 in range(0, kk // (b + 1), SUBLANES):
            groups.append((a0, 1, b, 0, lambda j, a0=a0, b=b: (a0 + j >= 2) & ((a0 + j + 1) * (b + 1) <= kk)))
    return groups


def _staircase_topk(v1, i1, v2, i2, nk):
    kk, t = v1.shape
    j = lax.broadcasted_iota(I32, (SUBLANES, t), 0)

    def rows(x, x0, step):
        if step == 0:
            return jnp.broadcast_to(x[x0:x0 + 1, :], (SUBLANES, t))
        return x[x0:x0 + SUBLANES, :]

    cand, order, cidx = [], [], []
    for a0, a_step, b0, b_step, valid in _staircase(kk):
        ok = valid(j)
        cand.append(jnp.where(ok, rows(v1, a0, a_step) + rows(v2, b0, b_step), -jnp.inf))
        order.append(jnp.where(ok, (a0 + j * a_step) * kk + (b0 + j * b_step), jnp.iinfo(jnp.int32).max - 1))
        cidx.append(rows(i1, a0, a_step) * nk + rows(i2, b0, b_step))
    cat = lambda xs: jnp.concatenate(xs, axis=0)
    return _topk_rows(cat(cand), kk, order=cat(order), payload=cat(cidx))


def _router_kernel(x_ref, g_ref, wq_hi_ref, wq_lo_ref, k1_hi_ref, k1_lo_ref, k2_hi_ref, k2_lo_ref,
                   e_ref, gate_ref):
    table_rows = x_ref.shape[1] // (2 * LANES)
    xn = _rms(x_ref[...], g_ref[...])
    xh, xl = _split(xn)
    q = _dot3(xh, xl, wq_hi_ref[...], wq_lo_ref[...])
    nk = k1_hi_ref.shape[0]
    for h in range(PEER_HEADS):
        qh, ql = _split(q[:, h * LANES:(h + 1) * LANES])
        s1 = _dot3_t(k1_hi_ref[...], k1_lo_ref[...], qh, ql)
        s2 = _dot3_t(k2_hi_ref[...], k2_lo_ref[...], qh, ql)
        v1, i1 = _topk_rows(s1, PEER_TOPK)
        v2, i2 = _topk_rows(s2, PEER_TOPK)
        kk = PEER_TOPK
        top_s, e_idx = _staircase_topk(v1, i1, v2, i2, nk)
        p = jnp.exp(top_s - top_s[0:1, :])
        gate = p / jnp.sum(p, axis=0, keepdims=True)
        e_ref[0, h * kk:(h + 1) * kk, :] = e_idx * table_rows
        gate_ref[0, h * kk:(h + 1) * kk, :] = gate


def _pad_keys(k, lo):
    nk, half = k.shape
    out = jnp.zeros((nk, LANES), F32)
    return out.at[:, lo:lo + half].set(k)


def _peer_router(x2d, g, w_q, k1, k2):
    t, d = x2d.shape
    nt = t // PEER_TILE
    slots = PEER_HEADS * PEER_TOPK
    wq_hi, wq_lo = _split(w_q)
    k1_hi, k1_lo = _split(_pad_keys(k1, 0))
    k2_hi, k2_lo = _split(_pad_keys(k2, PEER_HALF))
    const = lambda i: (0, 0)
    kspec = pl.BlockSpec(k1_hi.shape, const)
    return pl.pallas_call(
        _router_kernel,
        grid=(nt,),
        in_specs=[
            pl.BlockSpec((PEER_TILE, d), lambda i: (i, 0)),
            pl.BlockSpec((1, d), const),
            pl.BlockSpec(wq_hi.shape, const),
            pl.BlockSpec(wq_lo.shape, const),
            kspec, kspec, kspec, kspec,
        ],
        out_specs=[
            pl.BlockSpec((1, slots, PEER_TILE), lambda i: (i, 0, 0)),
            pl.BlockSpec((1, slots, PEER_TILE), lambda i: (i, 0, 0)),
        ],
        out_shape=[
            jax.ShapeDtypeStruct((nt, slots, PEER_TILE), I32),
            jax.ShapeDtypeStruct((nt, slots, PEER_TILE), F32),
        ],
        compiler_params=_cparams(("arbitrary",)),
        name="peer_router",
    )(x2d, g.reshape(1, d), wq_hi, wq_lo, k1_hi, k1_lo, k2_hi, k2_lo)


def _gather_octet(e_ref, tab_ref, stage_ref, octet, nsub):
    t = e_ref.shape[2]
    for k in range(OCTET):
        for tok in range(t):
            off = e_ref[0, octet * OCTET + k, tok]
            stage_ref[k * (t // SUBLANES) + tok // SUBLANES,
                      pl.ds(tok % SUBLANES, nsub, stride=SUBLANES), :] = tab_ref[pl.ds(off, nsub), :]


def _octet_pipeline(n_octets, gather, consume, stage_a, stage_b):
    gather(0, stage_a)

    def pair(p, carry):
        gather(2 * p + 1, stage_b)
        consume(2 * p, stage_a)
        gather(jnp.minimum(2 * p + 2, n_octets - 1), stage_a)
        consume(2 * p + 1, stage_b)
        return carry
    lax.fori_loop(0, n_octets // 2, pair, 0)


def _score_kernel(e_ref, x_ref, g_ref, gate_ref, tab_ref, w_ref, stage_a, stage_b, a_ref):
    t, d = x_ref.shape
    nchunk = d // (2 * LANES)
    slots = e_ref.shape[1]
    rows = OCTET * t

    xn = _rms(x_ref[...], g_ref[...]).astype(BF16)
    x_even = jnp.concatenate([xn[:, (2 * c) * LANES:(2 * c + 1) * LANES] for c in range(nchunk)], axis=1)
    x_odd = jnp.concatenate([xn[:, (2 * c + 1) * LANES:(2 * c + 2) * LANES] for c in range(nchunk)], axis=1)
    rhs = jnp.concatenate([x_even, x_odd], axis=0)

    ri = lax.broadcasted_iota(I32, (2 * t, 2 * t), 0)
    ci = lax.broadcasted_iota(I32, (2 * t, 2 * t), 1)
    diag = (ci == (ri % 2) * t + ri // 2).astype(F32)

    def consume(o, stage_ref):
        planes = []
        for c in range(nchunk):
            plane = stage_ref[:, c * SUBLANES:(c + 1) * SUBLANES, :].reshape(rows, LANES)
            planes.append(pltpu.bitcast(plane, BF16))
        lhs = jnp.concatenate(planes, axis=1)
        prod = _dot_t(lhs, rhs)
        prod = prod.reshape(OCTET, 2 * t, 2 * t) * diag[None]
        a2 = jnp.sum(prod, axis=1)
        a_ref[pl.ds(pl.multiple_of(o * OCTET, OCTET), OCTET), :] = a2[:, :t] + a2[:, t:]

    gather = lambda o, stage_ref: _gather_octet(e_ref, tab_ref, stage_ref, o, nchunk)
    _octet_pipeline(slots // OCTET, gather, consume, stage_a, stage_b)

    a = a_ref[...]
    gelu = 0.5 * a * (1.0 + lax.erf(a * (2.0 ** -0.5)))
    w_ref[0] = (gate_ref[0] * gelu).T


def _peer_scores(e_t, gate_t, x2d, g, table, n_tiles):
    nt, slots, t = e_t.shape
    d = x2d.shape[1]
    stage = pltpu.VMEM((OCTET * t // SUBLANES, d // (2 * LANES) * SUBLANES, LANES), I32)
    return pl.pallas_call(
        _score_kernel,
        grid=(n_tiles,),
        in_specs=[
            pl.BlockSpec((1, slots, t), lambda i: (i, 0, 0), memory_space=pltpu.SMEM),
            pl.BlockSpec((t, d), lambda i: (i, 0)),
            pl.BlockSpec((1, d), lambda i: (0, 0)),
            pl.BlockSpec((1, slots, t), lambda i: (i, 0, 0)),
            pl.BlockSpec(table.shape, lambda i: (0, 0), pipeline_mode=pl.Buffered(1)),
        ],
        out_specs=pl.BlockSpec((1, t, slots), lambda i: (i, 0, 0)),
        out_shape=jax.ShapeDtypeStruct((nt, t, slots), F32),
        scratch_shapes=[stage, stage, pltpu.VMEM((slots, t), F32)],
        compiler_params=_cparams(("arbitrary",)),
        name="peer_scores",
    )(e_t, x2d, g.reshape(1, d), gate_t, table)


def _value_kernel(e_ref, w_ref, x_ref, gf_ref, tab_ref, o_ref, stage_a, stage_b, wb_ref, acc_ref, *, final_norm):
    t, d = x_ref.shape
    nchunk = d // (2 * LANES)
    slots = e_ref.shape[1]
    groups = t // SUBLANES

    acc_ref[...] = x_ref[...]

    def consume(o, stage_ref):
        w_oct = pltpu.roll(w_ref[0], lax.rem(slots - o * OCTET, slots), 1)
        for k in range(OCTET):
            wb_ref[k] = jnp.broadcast_to(w_oct[:, k:k + 1], (t, LANES))
        for c in range(nchunk):
            lo_cols = slice(2 * c * LANES, (2 * c + 1) * LANES)
            hi_cols = slice((2 * c + 1) * LANES, (2 * c + 2) * LANES)
            acc_lo, acc_hi = acc_ref[:, lo_cols], acc_ref[:, hi_cols]
            for k in range(OCTET):
                words = stage_ref[k * groups:(k + 1) * groups, c * SUBLANES:(c + 1) * SUBLANES, :]
                words = words.reshape(t, LANES)
                lo = pltpu.bitcast(words << 16, F32)
                hi = pltpu.bitcast(words & jnp.int32(-65536), F32)
                acc_lo = acc_lo + wb_ref[k] * lo
                acc_hi = acc_hi + wb_ref[k] * hi
            acc_ref[:, lo_cols] = acc_lo
            acc_ref[:, hi_cols] = acc_hi

    gather = lambda o, stage_ref: _gather_octet(e_ref, tab_ref, stage_ref, o, nchunk)
    _octet_pipeline(slots // OCTET, gather, consume, stage_a, stage_b)

    y = acc_ref[...]
    if final_norm:
        y = _rms(y, gf_ref[...])
    o_ref[...] = y


def _peer_values(e_t, w_t, x2d, table, g_final, final_norm, n_tiles):
    nt, slots, t = e_t.shape
    d = x2d.shape[1]
    stage = pltpu.VMEM((OCTET * t // SUBLANES, d // (2 * LANES) * SUBLANES, LANES), I32)
    return pl.pallas_call(
        functools.partial(_value_kernel, final_norm=final_norm),
        grid=(n_tiles,),
        in_specs=[
            pl.BlockSpec((1, slots, t), lambda i: (i, 0, 0), memory_space=pltpu.SMEM),
            pl.BlockSpec((1, t, slots), lambda i: (i, 0, 0)),
            pl.BlockSpec((t, d), lambda i: (i, 0)),
            pl.BlockSpec((1, d), lambda i: (0, 0)),
            pl.BlockSpec(table.shape, lambda i: (0, 0), pipeline_mode=pl.Buffered(1)),
        ],
        out_specs=pl.BlockSpec((t, d), lambda i: (i, 0)),
        out_shape=jax.ShapeDtypeStruct(x2d.shape, F32),
        scratch_shapes=[stage, stage, pltpu.VMEM((OCTET, t, LANES), F32), pltpu.VMEM((t, d), F32)],
        compiler_params=_cparams(("arbitrary",)),
        name="peer_values",
    )(e_t, w_t, x2d, g_final.reshape(1, d), table)


def _pack_pairs(w):
    e, d = w.shape
    bits = lax.bitcast_convert_type(w.astype(BF16), jnp.uint16).astype(jnp.uint32)
    bits = bits.reshape(e * d // (2 * LANES), 2 * LANES)
    packed = bits[:, :LANES] | (bits[:, LANES:] << 16)
    return lax.bitcast_convert_type(packed, I32)


def _sc_value_kernel(tab_hbm, idx_hbm, w_hbm, out_hbm, idx_v, w_v, rows_a, rows_b, acc_a, acc_b, row_sem, out_sem,
                     *, slots):
    n_tok = out_hbm.shape[0] // (SC_CORES * SC_SUBCORES)
    d = out_hbm.shape[1]
    words = d // 2
    win = slots // 2
    wid = lax.axis_index("s") * SC_CORES + lax.axis_index("c")
    base_tok = wid * n_tok

    def gather(tl, half, buf, sem):
        rows = idx_v.at[pl.ds(tl * slots + half * win, win)]
        return pltpu.make_async_copy(tab_hbm.at[rows], buf, sem)

    def accumulate(tl, half, buf, acc):
        for g in range(words // SC_LANES // SC_GROUP):
            def row(r, sums, g=g):
                slot = jnp.full((SC_LANES,), tl * slots + half * win + r, I32)
                wv = plsc.load_gather(w_v, [slot])
                out = []
                for j in range(SC_GROUP):
                    x = buf[r, pl.ds((g * SC_GROUP + j) * SC_LANES, SC_LANES)]
                    lo = lax.bitcast_convert_type(x << 16, F32)
                    hi = lax.bitcast_convert_type(x & jnp.int32(-65536), F32)
                    out += [sums[2 * j] + wv * lo, sums[2 * j + 1] + wv * hi]
                return tuple(out)
            zeros = tuple(jnp.zeros((SC_LANES,), F32) for _ in range(2 * SC_GROUP))
            sums = lax.fori_loop(0, win, row, zeros)
            for j in range(SC_GROUP):
                chunk, lane = divmod((g * SC_GROUP + j) * SC_LANES, LANES)
                plsc.addupdate(acc.at[pl.ds(2 * chunk * LANES + lane, SC_LANES)], sums[2 * j])
                plsc.addupdate(acc.at[pl.ds((2 * chunk + 1) * LANES + lane, SC_LANES)], sums[2 * j + 1])

    def write_out(acc, tok, sem):
        return pltpu.make_async_copy(acc, out_hbm.at[tok], sem)

    @pl.loop(0, n_tok // SC_BATCH)
    def _(bi):
        tok0 = base_tok + bi * SC_BATCH
        pltpu.sync_copy(idx_hbm.at[pl.ds(tok0 * slots, SC_BATCH * slots)], idx_v)
        pltpu.sync_copy(w_hbm.at[pl.ds(tok0 * slots, SC_BATCH * slots)], w_v)
        gather(0, 0, rows_a, row_sem.at[0]).start()

        @pl.loop(0, SC_BATCH // 2)
        def _(tp):
            for parity, acc in ((0, acc_a), (1, acc_b)):
                tl = 2 * tp + parity

                @pl.when(bi * SC_BATCH + tl >= 2)
                def _():
                    write_out(acc, tok0, out_sem.at[parity]).wait()
                for k in range(d // SC_LANES):
                    acc[pl.ds(k * SC_LANES, SC_LANES)] = jnp.zeros((SC_LANES,), F32)
                gather(tl, 1, rows_b, row_sem.at[1]).start()
                gather(tl, 0, rows_a, row_sem.at[0]).wait()
                accumulate(tl, 0, rows_a, acc)
                gather(jnp.minimum(tl + 1, SC_BATCH - 1), 0, rows_a, row_sem.at[0]).start()
                gather(tl, 1, rows_b, row_sem.at[1]).wait()
                accumulate(tl, 1, rows_b, acc)
                write_out(acc, tok0 + tl, out_sem.at[parity]).start()

        gather(SC_BATCH - 1, 0, rows_a, row_sem.at[0]).wait()

    write_out(acc_a, base_tok, out_sem.at[0]).wait()
    write_out(acc_b, base_tok, out_sem.at[1]).wait()


def _peer_values_sc(idx, w, table, n_tok, d):
    slots = idx.shape[0] // n_tok
    assert n_tok % (SC_CORES * SC_SUBCORES * SC_BATCH) == 0 and SC_BATCH % 2 == 0
    mesh = plsc.VectorSubcoreMesh(core_axis_name="c", subcore_axis_name="s",
                                  num_cores=SC_CORES, num_subcores=SC_SUBCORES)
    return pl.kernel(
        functools.partial(_sc_value_kernel, slots=slots),
        out_type=jax.ShapeDtypeStruct((n_tok, d), F32),
        mesh=mesh,
        scratch_types=[
            pltpu.VMEM((SC_BATCH * slots,), I32),
            pltpu.VMEM((SC_BATCH * slots,), F32),
            pltpu.VMEM((slots // 2, d // 2), I32),
            pltpu.VMEM((slots // 2, d // 2), I32),
            pltpu.VMEM((d,), F32),
            pltpu.VMEM((d,), F32),
            pltpu.SemaphoreType.DMA((2,)),
            pltpu.SemaphoreType.DMA((2,)),
        ],
        compiler_params=pltpu.CompilerParams(needs_layout_passes=False),
        name="peer_values_sc",
    )(table, idx, w)


def _sc_score_kernel(tab_hbm, idx_hbm, xe_hbm, xo_hbm, out_hbm, idx_v, xe_v, xo_v, rows_a, rows_b, part_v, a_v, row_sem,
                     *, slots):
    n_tok = out_hbm.shape[0] // slots // (SC_CORES * SC_SUBCORES)
    words = rows_a.shape[1]
    win = slots // 2
    n_groups = words // SC_LANES // SC_GROUP
    wid = lax.axis_index("s") * SC_CORES + lax.axis_index("c")
    base_tok = wid * n_tok
    lane_id = lax.iota(I32, SC_LANES)

    def gather(tl, half, buf, sem):
        rows = idx_v.at[pl.ds(tl * slots + half * win, win)]
        return pltpu.make_async_copy(tab_hbm.at[rows], buf, sem)

    def scores(tl, half, buf):
        for r in range(win):
            part_v[pl.ds(r * SC_LANES, SC_LANES)] = jnp.zeros((SC_LANES,), F32)
        for g in range(n_groups):
            xs = []
            for j in range(SC_GROUP):
                at = pl.ds(tl * words + (g * SC_GROUP + j) * SC_LANES, SC_LANES)
                xs += [xe_v[at], xo_v[at]]

            @pl.loop(0, win, step=SC_ROW_UNROLL)
            def _(r0, g=g, xs=xs):
                for r in range(SC_ROW_UNROLL):
                    sums = [jnp.zeros((SC_LANES,), F32) for _ in range(4)]
                    for j in range(SC_GROUP):
                        x = buf[r0 + r, pl.ds((g * SC_GROUP + j) * SC_LANES, SC_LANES)]
                        lo = lax.bitcast_convert_type(x << 16, F32)
                        hi = lax.bitcast_convert_type(x & jnp.int32(-65536), F32)
                        sums[(2 * j) % 4] += lo * xs[2 * j]
                        sums[(2 * j + 1) % 4] += hi * xs[2 * j + 1]
                    plsc.addupdate(part_v.at[pl.ds((r0 + r) * SC_LANES, SC_LANES)],
                                   (sums[0] + sums[1]) + (sums[2] + sums[3]))
        for rb in range(win // SC_LANES):
            first = (lane_id + rb * SC_LANES) * SC_LANES
            total = jnp.zeros((SC_LANES,), F32)
            for lane in range(SC_LANES):
                total += plsc.load_gather(part_v, [first + lane])
            a_v[pl.ds(tl * slots + half * win + rb * SC_LANES, SC_LANES)] = total

    @pl.loop(0, n_tok // SC_BATCH)
    def _(bi):
        tok0 = base_tok + bi * SC_BATCH
        pltpu.sync_copy(idx_hbm.at[pl.ds(tok0 * slots, SC_BATCH * slots)], idx_v)
        pltpu.sync_copy(xe_hbm.at[pl.ds(tok0 * words, SC_BATCH * words)], xe_v)
        pltpu.sync_copy(xo_hbm.at[pl.ds(tok0 * words, SC_BATCH * words)], xo_v)
        gather(0, 0, rows_a, row_sem.at[0]).start()

        @pl.loop(0, SC_BATCH)
        def _(tl):
            gather(tl, 1, rows_b, row_sem.at[1]).start()
            gather(tl, 0, rows_a, row_sem.at[0]).wait()
            scores(tl, 0, rows_a)
            gather(jnp.minimum(tl + 1, SC_BATCH - 1), 0, rows_a, row_sem.at[0]).start()
            gather(tl, 1, rows_b, row_sem.at[1]).wait()
            scores(tl, 1, rows_b)

        gather(SC_BATCH - 1, 0, rows_a, row_sem.at[0]).wait()
        pltpu.sync_copy(a_v, out_hbm.at[pl.ds(tok0 * slots, SC_BATCH * slots)])


def _peer_scores_sc(idx, xe, xo, table, n_tok, slots):
    words = table.shape[1]
    assert n_tok % (SC_CORES * SC_SUBCORES * SC_BATCH) == 0
    mesh = plsc.VectorSubcoreMesh(core_axis_name="c", subcore_axis_name="s",
                                  num_cores=SC_CORES, num_subcores=SC_SUBCORES)
    return pl.kernel(
        functools.partial(_sc_score_kernel, slots=slots),
        out_type=jax.ShapeDtypeStruct((n_tok * slots,), F32),
        mesh=mesh,
        scratch_types=[
            pltpu.VMEM((SC_BATCH * slots,), I32),
            pltpu.VMEM((SC_BATCH * words,), F32),
            pltpu.VMEM((SC_BATCH * words,), F32),
            pltpu.VMEM((slots // 2, words), I32),
            pltpu.VMEM((slots // 2, words), I32),
            pltpu.VMEM((slots // 2 * SC_LANES,), F32),
            pltpu.VMEM((SC_BATCH * slots,), F32),
            pltpu.SemaphoreType.DMA((2,)),
        ],
        compiler_params=pltpu.CompilerParams(needs_layout_passes=False),
        name="peer_scores_sc",
    )(table, idx, xe, xo)


def _split_norm_kernel(x_ref, g_ref, xe_ref, xo_ref):
    d = x_ref.shape[1]
    xn = _rms(x_ref[...], g_ref[...])
    nchunk = d // (2 * LANES)
    xe_ref[...] = jnp.concatenate([xn[:, (2 * c) * LANES:(2 * c + 1) * LANES] for c in range(nchunk)], axis=1)
    xo_ref[...] = jnp.concatenate([xn[:, (2 * c + 1) * LANES:(2 * c + 2) * LANES] for c in range(nchunk)], axis=1)


def _split_norm(x2d, g, n_rows, tm):
    d = x2d.shape[1]
    first = (x2d.shape[0] - n_rows) // tm
    half = jax.ShapeDtypeStruct((n_rows, d // 2), F32)
    return pl.pallas_call(
        _split_norm_kernel,
        grid=(n_rows // tm,),
        in_specs=[pl.BlockSpec((tm, d), lambda i: (first + i, 0)), pl.BlockSpec((1, d), lambda i: (0, 0))],
        out_specs=[pl.BlockSpec((tm, d // 2), lambda i: (i, 0))] * 2,
        out_shape=[half, half],
        compiler_params=_cparams(("arbitrary",)),
        name="peer_split_norm",
    )(x2d, g.reshape(1, d))


def _score_finish_kernel(w_hbm, a_ref, gate_ref, w_ref):
    del w_hbm
    a = a_ref[0]
    gelu = 0.5 * a * (1.0 + lax.erf(a * (2.0 ** -0.5)))
    w_ref[0] = gate_ref[0].T * gelu


def _score_finish(w_full, a_sc, gate_t):
    nt, t, slots = w_full.shape
    n_sc = a_sc.shape[0]
    first = nt - n_sc
    return pl.pallas_call(
        _score_finish_kernel,
        grid=(n_sc,),
        in_specs=[
            pl.BlockSpec(memory_space=pl.ANY),
            pl.BlockSpec((1, t, slots), lambda i: (i, 0, 0)),
            pl.BlockSpec((1, slots, t), lambda i: (first + i, 0, 0)),
        ],
        out_specs=pl.BlockSpec((1, t, slots), lambda i: (first + i, 0, 0)),
        out_shape=jax.ShapeDtypeStruct(w_full.shape, F32),
        input_output_aliases={0: 0},
        compiler_params=_cparams(("arbitrary",)),
        name="peer_score_finish",
    )(w_full, a_sc, gate_t)


def _residual_norm_kernel(out_hbm, x_ref, p_ref, gf_ref, o_ref, *, final_norm):
    del out_hbm
    y = x_ref[...] + p_ref[...]
    if final_norm:
        y = _rms(y, gf_ref[...])
    o_ref[...] = y


def _residual_norm(out_full, x2d, p2d, g_final, final_norm, tm):
    t, d = p2d.shape
    first = (x2d.shape[0] - t) // tm
    return pl.pallas_call(
        functools.partial(_residual_norm_kernel, final_norm=final_norm),
        grid=(t // tm,),
        in_specs=[
            pl.BlockSpec(memory_space=pl.ANY),
            pl.BlockSpec((tm, d), lambda i: (first + i, 0)),
            pl.BlockSpec((tm, d), lambda i: (i, 0)),
            pl.BlockSpec((1, d), lambda i: (0, 0)),
        ],
        out_specs=pl.BlockSpec((tm, d), lambda i: (first + i, 0)),
        out_shape=jax.ShapeDtypeStruct(x2d.shape, F32),
        input_output_aliases={0: 0},
        compiler_params=_cparams(("arbitrary",)),
        name="peer_residual",
    )(out_full, x2d, p2d, g_final.reshape(1, d))


def _peer_ffn(x, g, w_q, k1, k2, u_emb, v_emb, g_final, final_norm):
    b, s, d = x.shape
    x2d = x.reshape(b * s, d)
    e_slot, gate_slot = _peer_router(x2d, g, w_q, k1, k2)
    u_tab, v_tab = _pack_pairs(u_emb), _pack_pairs(v_emb)
    nt, slots, t = e_slot.shape
    sc_quantum = SC_CORES * SC_SUBCORES * SC_BATCH // t
    ns_score = nt * SC_SCORE_PERCENT // 100 // sc_quantum * sc_quantum
    ns_value = nt * SC_VALUE_PERCENT // 100 // sc_quantum * sc_quantum
    ns_max = max(ns_score, ns_value)
    rows_per_expert = d // (2 * LANES)
    idx_sc = (e_slot[nt - ns_max:] // rows_per_expert).transpose(0, 2, 1).reshape(-1)

    w_tok = _peer_scores(e_slot, gate_slot, x2d, g, u_tab, nt - ns_score)
    if ns_score:
        xe, xo = _split_norm(x2d, g, ns_score * t, math.gcd(TOKEN_TILE, ns_score * t))
        a_sc = _peer_scores_sc(idx_sc[(ns_max - ns_score) * t * slots:], xe.reshape(-1), xo.reshape(-1),
                               u_tab.reshape(-1, d // 2), ns_score * t, slots)
        w_tok = _score_finish(w_tok, a_sc.reshape(ns_score, t, slots), gate_slot)
    out = _peer_values(e_slot, w_tok, x2d, v_tab, g_final, final_norm, nt - ns_value)
    if ns_value:
        peer_sc = _peer_values_sc(idx_sc[(ns_max - ns_value) * t * slots:], w_tok[nt - ns_value:].reshape(-1),
                                  v_tab.reshape(-1, d // 2), ns_value * t, d)
        out = _residual_norm(out, x2d, peer_sc, g_final, final_norm, math.gcd(TOKEN_TILE, ns_value * t))
    return out.reshape(b, s, d)


def _qkv_kernel(x_ref, g_ref, wqk_hi_ref, wqk_lo_ref, wv_ref, cos_ref, sin_ref,
                qt_ref, k0_ref, k1_ref, vt_ref, km_ref):
    tm, d = x_ref.shape[1], x_ref.shape[2]
    hn = _rms(x_ref[0], g_ref[...])
    hh, hl = _split(hn)
    qk = _dot3(hh, hl, wqk_hi_ref[...], wqk_lo_ref[...])
    v = _dot(hh, wv_ref[...])
    cos = jnp.concatenate([cos_ref[...]] * (d // LANES), axis=1)
    sin = jnp.concatenate([sin_ref[...]] * (d // LANES), axis=1)
    lane = lax.broadcasted_iota(I32, (tm, d), 1)
    first_half = (lane % HEAD_DIM) < (HEAD_DIM // 2)

    def rope(a):
        rot = jnp.where(first_half, pltpu.roll(a, d - HEAD_DIM // 2, 1), pltpu.roll(a, HEAD_DIM // 2, 1))
        return a * cos + rot * sin

    q = rope(qk[:, :d])
    k = rope(qk[:, d:])
    qt_ref[0] = q.T
    vt_ref[0] = v.T.astype(BF16)
    nb = tm // MOBA_BLOCK
    km_ref[0, 0] = jnp.mean(k.reshape(nb, MOBA_BLOCK, d), axis=1)
    row = lax.broadcasted_iota(I32, (tm, d), 0)
    block = (pl.program_id(1) * tm + row) // MOBA_BLOCK
    pair_lane = lane % LANES
    kb = k.astype(BF16)
    k0_ref[0] = jnp.where(pair_lane < HEAD_DIM, kb, jnp.where(pair_lane - HEAD_DIM == block, 1.0, 0.0).astype(BF16))
    k1_ref[0] = jnp.where(pair_lane >= HEAD_DIM, kb, jnp.where(pair_lane == block, 1.0, 0.0).astype(BF16))


def _qkv_rope(x, g, w_qkv, tm):
    b, s, d = x.shape
    half = HEAD_DIM // 2
    inv = ROPE_THETA ** (-jnp.arange(half, dtype=F32) / half)
    ang = jnp.arange(s).astype(F32)[:, None] * inv[None, :]
    cos, sin = jnp.cos(ang), jnp.sin(ang)
    cos128 = jnp.tile(jnp.concatenate([cos, cos], axis=1), (1, LANES // HEAD_DIM))
    sin128 = jnp.tile(jnp.concatenate([-sin, sin], axis=1), (1, LANES // HEAD_DIM))
    wqk_hi, wqk_lo = _split(w_qkv[:, :2 * d])
    wv = w_qkv[:, 2 * d:].astype(BF16)
    nb = tm // MOBA_BLOCK
    const = lambda i, j: (0, 0)
    qt, k0, k1, vt, km = pl.pallas_call(
        _qkv_kernel,
        grid=(b, s // tm),
        in_specs=[
            pl.BlockSpec((1, tm, d), lambda i, j: (i, j, 0)),
            pl.BlockSpec((1, d), const),
            pl.BlockSpec((d, 2 * d), const),
            pl.BlockSpec((d, 2 * d), const),
            pl.BlockSpec((d, d), const),
            pl.BlockSpec((tm, LANES), lambda i, j: (j, 0)),
            pl.BlockSpec((tm, LANES), lambda i, j: (j, 0)),
        ],
        out_specs=[
            pl.BlockSpec((1, d, tm), lambda i, j: (i, 0, j)),
            pl.BlockSpec((1, tm, d), lambda i, j: (i, j, 0)),
            pl.BlockSpec((1, tm, d), lambda i, j: (i, j, 0)),
            pl.BlockSpec((1, d, tm), lambda i, j: (i, 0, j)),
            pl.BlockSpec((1, 1, nb, d), lambda i, j: (i, j, 0, 0)),
        ],
        out_shape=[
            jax.ShapeDtypeStruct((b, d, s), F32),
            jax.ShapeDtypeStruct((b, s, d), BF16),
            jax.ShapeDtypeStruct((b, s, d), BF16),
            jax.ShapeDtypeStruct((b, d, s), BF16),
            jax.ShapeDtypeStruct((b, s // tm, nb, d), F32),
        ],
        compiler_params=_cparams(("arbitrary", "arbitrary")),
        name="qkv_rope",
    )(x, g.reshape(1, d), wqk_hi, wqk_lo, wv, cos128, sin128)
    return qt, (k0, k1), vt, km.reshape(b, s // MOBA_BLOCK, d)


def _moba_kernel(qt_ref, k0_ref, k1_ref, vt_ref, km_ref, o_ref, sa_ref, sb_ref):
    bs = MOBA_BLOCK
    nb = km_ref.shape[1]
    n_heads = LANES // HEAD_DIM
    k_refs = (k0_ref, k1_ref)
    j = pl.program_id(2)
    qt = qt_ref[0]
    km = km_ref[0]
    scale = HEAD_DIM ** -0.5 * 1.4426950408889634
    lane_km = lax.broadcasted_iota(I32, (nb, LANES), 1)
    blk = lax.broadcasted_iota(I32, (nb, bs), 0)
    zeros_pad = jnp.zeros((LANES - HEAD_DIM - nb, bs), F32)
    qh, ql = _split(qt)

    own = pl.ds(pl.multiple_of(j * bs, bs), bs)
    v_own = vt_ref[0, :, own]
    krow = lax.broadcasted_iota(I32, (bs, bs), 0)
    qcol = lax.broadcasted_iota(I32, (bs, bs), 1)

    q_augs, state = [], []
    for hh in range(n_heads):
        head_lo = hh * HEAD_DIM
        in_head_km = (lane_km >= head_lo) & (lane_km < head_lo + HEAD_DIM)
        kmh, kml = _split(jnp.where(in_head_km, km, 0.0))
        gate = _dot3(kmh, kml, qh, ql)
        valid = blk < j
        gate = jnp.where(valid, gate, -jnp.inf)
        sel = jnp.zeros((nb, bs), F32)
        for _ in range(MOBA_TOPK):
            m = jnp.max(gate, axis=0, keepdims=True)
            i = jnp.min(jnp.where(gate == m, blk, nb), axis=0, keepdims=True)
            pick = blk == i
            sel = jnp.where(pick, 1.0, sel)
            gate = jnp.where(pick, -jnp.inf, gate)
        bias_t = jnp.where((sel > 0.0) & valid, 0.0, MASK_NEG)
        q_head = qt[head_lo:head_lo + HEAD_DIM, :] * scale
        no_bias = jnp.zeros((LANES - HEAD_DIM, bs), F32)
        if hh == 0:
            q_aug = jnp.concatenate([q_head, bias_t, zeros_pad], axis=0)
            q_own = jnp.concatenate([q_head, no_bias], axis=0)
        else:
            q_aug = jnp.concatenate([bias_t, zeros_pad, q_head], axis=0)
            q_own = jnp.concatenate([no_bias, q_head], axis=0)
        q_augs.append(q_aug.astype(BF16))

        s_own = jnp.where(krow <= qcol, _dot(k_refs[hh][0, own, :], q_own.astype(BF16)), -1e30)
        m0 = jnp.max(s_own, axis=0, keepdims=True)
        p0 = jnp.exp2(s_own - m0)
        state += [m0, jnp.sum(p0, axis=0, keepdims=True), _dot(v_own, p0.astype(BF16))]

    chunk = KV_CHUNK * bs
    last_chunk = nb // KV_CHUNK - 1

    def score_chunk(c, s_ref):
        rows = pl.ds(pl.multiple_of(jnp.minimum(c, last_chunk) * chunk, chunk), chunk)
        for hh in range(n_heads):
            s_ref[hh] = _dot(k_refs[hh][0, rows, :], q_augs[hh])

    def attend(c, s_ref, state):
        vn = vt_ref[0, :, pl.ds(pl.multiple_of(c * chunk, chunk), chunk)]
        new_state = []
        for hh in range(n_heads):
            m, l, acc = state[3 * hh:3 * hh + 3]
            s = s_ref[hh]
            m_new = jnp.maximum(m, jnp.max(s, axis=0, keepdims=True))
            alpha = jnp.exp2(m - m_new)
            p = jnp.exp2(s - m_new)
            l = alpha * l + jnp.sum(p, axis=0, keepdims=True)
            acc = alpha * acc + _dot(vn, p.astype(BF16))
            new_state += [m_new, l, acc]
        return tuple(new_state)

    score_chunk(0, sa_ref)

    def body(i, state):
        score_chunk(2 * i + 1, sb_ref)
        state = attend(2 * i, sa_ref, state)
        score_chunk(2 * i + 2, sa_ref)
        return attend(2 * i + 1, sb_ref, state)

    state = lax.fori_loop(0, (j + 2 * KV_CHUNK - 1) // (2 * KV_CHUNK), body, tuple(state))
    halves = []
    for hh in range(n_heads):
        _, l, acc = state[3 * hh:3 * hh + 3]
        halves.append((acc / l)[hh * HEAD_DIM:(hh + 1) * HEAD_DIM, :])
    o_ref[0] = jnp.concatenate(halves, axis=0).T


def _moba_attention(qt, k01, vt, km):
    b, d, s = qt.shape
    nb = s // MOBA_BLOCK
    assert nb % (2 * KV_CHUNK) == 0
    return pl.pallas_call(
        _moba_kernel,
        grid=(b, d // LANES, nb),
        in_specs=[
            pl.BlockSpec((1, LANES, MOBA_BLOCK), lambda i, h, j: (i, h, j)),
            pl.BlockSpec((1, s, LANES), lambda i, h, j: (i, 0, h)),
            pl.BlockSpec((1, s, LANES), lambda i, h, j: (i, 0, h)),
            pl.BlockSpec((1, LANES, s), lambda i, h, j: (i, h, 0)),
            pl.BlockSpec((1, nb, LANES), lambda i, h, j: (i, 0, h)),
        ],
        out_specs=pl.BlockSpec((1, MOBA_BLOCK, LANES), lambda i, h, j: (i, j, h)),
        out_shape=jax.ShapeDtypeStruct((b, s, d), F32),
        scratch_shapes=[pltpu.VMEM((LANES // HEAD_DIM, KV_CHUNK * MOBA_BLOCK, MOBA_BLOCK), F32)] * 2,
        compiler_params=_cparams(("arbitrary", "arbitrary", "arbitrary")),
        name="moba_attention",
    )(qt, *k01, vt, km)


def _proj_residual_kernel(x_ref, a_ref, w_ref, o_ref):
    o_ref[...] = x_ref[...] + _dot(a_ref[...].astype(BF16), w_ref[...])


def _proj_residual(x2d, a2d, w, tm):
    t, d = x2d.shape
    return pl.pallas_call(
        _proj_residual_kernel,
        grid=(t // tm,),
        in_specs=[
            pl.BlockSpec((tm, d), lambda i: (i, 0)),
            pl.BlockSpec((tm, a2d.shape[1]), lambda i: (i, 0)),
            pl.BlockSpec(w.shape, lambda i: (0, 0)),
        ],
        out_specs=pl.BlockSpec((tm, d), lambda i: (i, 0)),
        out_shape=jax.ShapeDtypeStruct((t, d), F32),
        compiler_params=_cparams(("arbitrary",)),
        name="attn_out_proj",
    )(x2d, a2d, w.astype(BF16))


def _moba_mixer(x, g, w_qkv, w_o):
    b, s, d = x.shape
    tm = min(TOKEN_TILE, s)
    qt, k, vt, km = _qkv_rope(x, g, w_qkv, tm)
    attn = _moba_attention(qt, k, vt, km)
    return _proj_residual(x.reshape(b * s, d), attn.reshape(b * s, d), w_o, tm).reshape(b, s, d)


def kernel(x, norm_mix, norm_ffn, conv_w_in, conv_w, conv_w_out, attn_w_qkv, attn_w_o,
           peer_w_q, peer_k1, peer_k2, peer_u, peer_v, norm_final):
    depth = norm_mix.shape[0]
    tm = min(TOKEN_TILE, x.shape[1])
    for i in range(depth):
        j = i // 2
        if i % 2 == 0:
            x = _conv_mixer(x, norm_mix[i], conv_w_in[j], conv_w[j], conv_w_out[j], tm)
        else:
            x = _moba_mixer(x, norm_mix[i], attn_w_qkv[j], attn_w_o[j])
        x = _peer_ffn(x, norm_ffn[i], peer_w_q[i], peer_k1[i], peer_k2[i], peer_u[i], peer_v[i],
                      norm_final, final_norm=(i == depth - 1))
    return x
```

```python
import functools
import math

import jax
import jax.numpy as jnp
from jax import lax
from jax.experimental import pallas as pl
from jax.experimental.pallas import tpu as pltpu
from jax.experimental.pallas import tpu_sc as plsc

F32 = jnp.float32
BF16 = jnp.bfloat16
I32 = jnp.int32

RMS_EPS = 1e-6
N_HEADS = 16
HEAD_DIM = 64
MOBA_BLOCK = 256
MOBA_TOPK = 3
ROPE_THETA = 10000.0
PEER_HEADS = 8
PEER_NKEYS = 128
PEER_HALF = 64
PEER_TOPK = 16

LANES = 128
SUBLANES = 8
VMEM_LIMIT = 56 * 1024 * 1024
MASK_NEG = -1e9

TOKEN_TILE = 512
PEER_TILE = 128
OCTET = 8
SC_CORES = 2
SC_SUBCORES = 16
SC_LANES = 16
SC_BATCH = 16
SC_GROUP = 8
SC_ROW_UNROLL = 4
SC_SCORE_PERCENT = 40
SC_VALUE_PERCENT = 47
KV_CHUNK = 2


def _cparams(sem):
    return pltpu.CompilerParams(dimension_semantics=sem, vmem_limit_bytes=VMEM_LIMIT)


def _rms(x, g):
    ms = jnp.mean(x * x, axis=-1, keepdims=True)
    return x * lax.rsqrt(ms + RMS_EPS) * g


def _split(a):
    hi = a.astype(BF16)
    lo = (a - hi.astype(F32)).astype(BF16)
    return hi, lo


def _dot(a, b):
    return lax.dot_general(a, b, (((1,), (0,)), ((), ())), preferred_element_type=F32)


def _dot_t(a, b):
    return lax.dot_general(a, b, (((1,), (1,)), ((), ())), preferred_element_type=F32)


def _dot3(a_hi, a_lo, b_hi, b_lo):
    return _dot(a_hi, b_hi) + _dot(a_lo, b_hi) + _dot(a_hi, b_lo)


def _dot3_t(a_hi, a_lo, b_hi, b_lo):
    return _dot_t(a_hi, b_hi) + _dot_t(a_lo, b_hi) + _dot_t(a_hi, b_lo)


def _conv_mixer_kernel(x_ref, g_ref, win_ref, cw_ref, wout_ref, o_ref, ubuf_ref):
    tm, d = x_ref.shape[1], x_ref.shape[2]

    @pl.when(pl.program_id(1) == 0)
    def _():
        ubuf_ref[0:SUBLANES, :] = jnp.zeros((SUBLANES, d), F32)

    x = x_ref[0]
    hn = _rms(x, g_ref[...]).astype(BF16)
    bcz = _dot(hn, win_ref[...])
    b_gate, c_gate, z = bcz[:, :d], bcz[:, d:2 * d], bcz[:, 2 * d:]
    u = c_gate * z
    ubuf_ref[SUBLANES:SUBLANES + tm, :] = u
    u1 = ubuf_ref[SUBLANES - 1:SUBLANES - 1 + tm, :]
    u2 = ubuf_ref[SUBLANES - 2:SUBLANES - 2 + tm, :]
    cw = cw_ref[...]
    u_conv = cw[0:1, :] * u2 + cw[1:2, :] * u1 + cw[2:3, :] * u
    ubuf_ref[0:SUBLANES, :] = u[tm - SUBLANES:tm, :]
    y = (b_gate * u_conv).astype(BF16)
    o_ref[0] = x + _dot(y, wout_ref[...])


def _conv_mixer(x, g, w_in, conv_w, w_out, tm):
    b, s, d = x.shape
    return pl.pallas_call(
        _conv_mixer_kernel,
        grid=(b, s // tm),
        in_specs=[
            pl.BlockSpec((1, tm, d), lambda i, j: (i, j, 0)),
            pl.BlockSpec((1, d), lambda i, j: (0, 0)),
            pl.BlockSpec((d, 3 * d), lambda i, j: (0, 0)),
            pl.BlockSpec((3, d), lambda i, j: (0, 0)),
            pl.BlockSpec((d, d), lambda i, j: (0, 0)),
        ],
        out_specs=pl.BlockSpec((1, tm, d), lambda i, j: (i, j, 0)),
        out_shape=jax.ShapeDtypeStruct((b, s, d), F32),
        scratch_shapes=[pltpu.VMEM((tm + SUBLANES, d), F32)],
        compiler_params=_cparams(("arbitrary", "arbitrary")),
        name="conv_mixer",
    )(x, g.reshape(1, d), w_in.astype(BF16), conv_w, w_out.astype(BF16))


def _topk_rows(s, k, order=None, payload=None):
    if order is None:
        order = lax.broadcasted_iota(I32, s.shape, 0)
    big = jnp.iinfo(jnp.int32).max
    vals, outs = [], []
    for _ in range(k):
        m = jnp.max(s, axis=0, keepdims=True)
        i = jnp.min(jnp.where(s == m, order, big), axis=0, keepdims=True)
        pick = order == i
        vals.append(m)
        if payload is None:
            outs.append(i)
        else:
            outs.append(jnp.max(jnp.where(pick, payload, -1), axis=0, keepdims=True))
        s = jnp.where(pick, -jnp.inf, s)
    return jnp.concatenate(vals, axis=0), jnp.concatenate(outs, axis=0)


def _staircase(kk):
    groups = []
    for a in range(2):
        for b0 in range(0, kk // (a + 1), SUBLANES):
            groups.append((a, 0, b0, 1, lambda j, a=a, b0=b0: (a + 1) * (b0 + j + 1) <= kk))
    for b in range(kk // 3):
        for a0 in range(0, kk // (b + 1), SUBLANES):
            groups.append((a0, 1, b, 0, lambda j, a0=a0, b=b: (a0 + j >= 2) & ((a0 + j + 1) * (b + 1) <= kk)))
    return groups


def _staircase_topk(v1, i1, v2, i2, nk):
    kk, t = v1.shape
    j = lax.broadcasted_iota(I32, (SUBLANES, t), 0)

    def rows(x, x0, step):
        if step == 0:
            return jnp.broadcast_to(x[x0:x0 + 1, :], (SUBLANES, t))
        return x[x0:x0 + SUBLANES, :]

    cand, order, cidx = [], [], []
    for a0, a_step, b0, b_step, valid in _staircase(kk):
        ok = valid(j)
        cand.append(jnp.where(ok, rows(v1, a0, a_step) + rows(v2, b0, b_step), -jnp.inf))
        order.append(jnp.where(ok, (a0 + j * a_step) * kk + (b0 + j * b_step), jnp.iinfo(jnp.int32).max - 1))
        cidx.append(rows(i1, a0, a_step) * nk + rows(i2, b0, b_step))
    cat = lambda xs: jnp.concatenate(xs, axis=0)
    return _topk_rows(cat(cand), kk, order=cat(order), payload=cat(cidx))


def _router_kernel(x_ref, g_ref, wq_hi_ref, wq_lo_ref, k1_hi_ref, k1_lo_ref, k2_hi_ref, k2_lo_ref,
                   e_ref, gate_ref):
    table_rows = x_ref.shape[1] // (2 * LANES)
    xn = _rms(x_ref[...], g_ref[...])
    xh, xl = _split(xn)
    q = _dot3(xh, xl, wq_hi_ref[...], wq_lo_ref[...])
    nk = k1_hi_ref.shape[0]
    for h in range(PEER_HEADS):
        qh, ql = _split(q[:, h * LANES:(h + 1) * LANES])
        s1 = _dot3_t(k1_hi_ref[...], k1_lo_ref[...], qh, ql)
        s2 = _dot3_t(k2_hi_ref[...], k2_lo_ref[...], qh, ql)
        v1, i1 = _topk_rows(s1, PEER_TOPK)
        v2, i2 = _topk_rows(s2, PEER_TOPK)
        kk = PEER_TOPK
        top_s, e_idx = _staircase_topk(v1, i1, v2, i2, nk)
        p = jnp.exp(top_s - top_s[0:1, :])
        gate = p / jnp.sum(p, axis=0, keepdims=True)
        e_ref[0, h * kk:(h + 1) * kk, :] = e_idx * table_rows
        gate_ref[0, h * kk:(h + 1) * kk, :] = gate


def _pad_keys(k, lo):
    nk, half = k.shape
    out = jnp.zeros((nk, LANES), F32)
    return out.at[:, lo:lo + half].set(k)


def _peer_router(x2d, g, w_q, k1, k2):
    t, d = x2d.shape
    nt = t // PEER_TILE
    slots = PEER_HEADS * PEER_TOPK
    wq_hi, wq_lo = _split(w_q)
    k1_hi, k1_lo = _split(_pad_keys(k1, 0))
    k2_hi, k2_lo = _split(_pad_keys(k2, PEER_HALF))
    const = lambda i: (0, 0)
    kspec = pl.BlockSpec(k1_hi.shape, const)
    return pl.pallas_call(
        _router_kernel,
        grid=(nt,),
        in_specs=[
            pl.BlockSpec((PEER_TILE, d), lambda i: (i, 0)),
            pl.BlockSpec((1, d), const),
            pl.BlockSpec(wq_hi.shape, const),
            pl.BlockSpec(wq_lo.shape, const),
            kspec, kspec, kspec, kspec,
        ],
        out_specs=[
            pl.BlockSpec((1, slots, PEER_TILE), lambda i: (i, 0, 0)),
            pl.BlockSpec((1, slots, PEER_TILE), lambda i: (i, 0, 0)),
        ],
        out_shape=[
            jax.ShapeDtypeStruct((nt, slots, PEER_TILE), I32),
            jax.ShapeDtypeStruct((nt, slots, PEER_TILE), F32),
        ],
        compiler_params=_cparams(("arbitrary",)),
        name="peer_router",
    )(x2d, g.reshape(1, d), wq_hi, wq_lo, k1_hi, k1_lo, k2_hi, k2_lo)


def _gather_octet(e_ref, tab_ref, stage_ref, octet, nsub):
    t = e_ref.shape[2]
    for k in range(OCTET):
        for tok in range(t):
            off = e_ref[0, octet * OCTET + k, tok]
            stage_ref[k * (t // SUBLANES) + tok // SUBLANES,
                      pl.ds(tok % SUBLANES, nsub, stride=SUBLANES), :] = tab_ref[pl.ds(off, nsub), :]


def _octet_pipeline(n_octets, gather, consume, stage_a, stage_b):
    gather(0, stage_a)

    def pair(p, carry):
        gather(2 * p + 1, stage_b)
        consume(2 * p, stage_a)
        gather(jnp.minimum(2 * p + 2, n_octets - 1), stage_a)
        consume(2 * p + 1, stage_b)
        return carry
    lax.fori_loop(0, n_octets // 2, pair, 0)


def _score_kernel(e_ref, x_ref, g_ref, gate_ref, tab_ref, w_ref, stage_a, stage_b, a_ref):
    t, d = x_ref.shape
    nchunk = d // (2 * LANES)
    slots = e_ref.shape[1]
    rows = OCTET * t

    xn = _rms(x_ref[...], g_ref[...]).astype(BF16)
    x_even = jnp.concatenate([xn[:, (2 * c) * LANES:(2 * c + 1) * LANES] for c in range(nchunk)], axis=1)
    x_odd = jnp.concatenate([xn[:, (2 * c + 1) * LANES:(2 * c + 2) * LANES] for c in range(nchunk)], axis=1)
    rhs = jnp.concatenate([x_even, x_odd], axis=0)

    ri = lax.broadcasted_iota(I32, (2 * t, 2 * t), 0)
    ci = lax.broadcasted_iota(I32, (2 * t, 2 * t), 1)
    diag = (ci == (ri % 2) * t + ri // 2).astype(F32)

    def consume(o, stage_ref):
        planes = []
        for c in range(nchunk):
            plane = stage_ref[:, c * SUBLANES:(c + 1) * SUBLANES, :].reshape(rows, LANES)
            planes.append(pltpu.bitcast(plane, BF16))
        lhs = jnp.concatenate(planes, axis=1)
        prod = _dot_t(lhs, rhs)
        prod = prod.reshape(OCTET, 2 * t, 2 * t) * diag[None]
        a2 = jnp.sum(prod, axis=1)
        a_ref[pl.ds(pl.multiple_of(o * OCTET, OCTET), OCTET), :] = a2[:, :t] + a2[:, t:]

    gather = lambda o, stage_ref: _gather_octet(e_ref, tab_ref, stage_ref, o, nchunk)
    _octet_pipeline(slots // OCTET, gather, consume, stage_a, stage_b)

    a = a_ref[...]
    gelu = 0.5 * a * (1.0 + lax.erf(a * (2.0 ** -0.5)))
    w_ref[0] = (gate_ref[0] * gelu).T


def _peer_scores(e_t, gate_t, x2d, g, table, n_tiles):
    nt, slots, t = e_t.shape
    d = x2d.shape[1]
    stage = pltpu.VMEM((OCTET * t // SUBLANES, d // (2 * LANES) * SUBLANES, LANES), I32)
    return pl.pallas_call(
        _score_kernel,
        grid=(n_tiles,),
        in_specs=[
            pl.BlockSpec((1, slots, t), lambda i: (i, 0, 0), memory_space=pltpu.SMEM),
            pl.BlockSpec((t, d), lambda i: (i, 0)),
            pl.BlockSpec((1, d), lambda i: (0, 0)),
            pl.BlockSpec((1, slots, t), lambda i: (i, 0, 0)),
            pl.BlockSpec(table.shape, lambda i: (0, 0), pipeline_mode=pl.Buffered(1)),
        ],
        out_specs=pl.BlockSpec((1, t, slots), lambda i: (i, 0, 0)),
        out_shape=jax.ShapeDtypeStruct((nt, t, slots), F32),
        scratch_shapes=[stage, stage, pltpu.VMEM((slots, t), F32)],
        compiler_params=_cparams(("arbitrary",)),
        name="peer_scores",
    )(e_t, x2d, g.reshape(1, d), gate_t, table)


def _value_kernel(e_ref, w_ref, x_ref, gf_ref, tab_ref, o_ref, stage_a, stage_b, wb_ref, acc_ref, *, final_norm):
    t, d = x_ref.shape
    nchunk = d // (2 * LANES)
    slots = e_ref.shape[1]
    groups = t // SUBLANES

    acc_ref[...] = x_ref[...]

    def consume(o, stage_ref):
        w_oct = pltpu.roll(w_ref[0], lax.rem(slots - o * OCTET, slots), 1)
        for k in range(OCTET):
            wb_ref[k] = jnp.broadcast_to(w_oct[:, k:k + 1], (t, LANES))
        for c in range(nchunk):
            lo_cols = slice(2 * c * LANES, (2 * c + 1) * LANES)
            hi_cols = slice((2 * c + 1) * LANES, (2 * c + 2) * LANES)
            acc_lo, acc_hi = acc_ref[:, lo_cols], acc_ref[:, hi_cols]
            for k in range(OCTET):
                words = stage_ref[k * groups:(k + 1) * groups, c * SUBLANES:(c + 1) * SUBLANES, :]
                words = words.reshape(t, LANES)
                lo = pltpu.bitcast(words << 16, F32)
                hi = pltpu.bitcast(words & jnp.int32(-65536), F32)
                acc_lo = acc_lo + wb_ref[k] * lo
                acc_hi = acc_hi + wb_ref[k] * hi
            acc_ref[:, lo_cols] = acc_lo
            acc_ref[:, hi_cols] = acc_hi

    gather = lambda o, stage_ref: _gather_octet(e_ref, tab_ref, stage_ref, o, nchunk)
    _octet_pipeline(slots // OCTET, gather, consume, stage_a, stage_b)

    y = acc_ref[...]
    if final_norm:
        y = _rms(y, gf_ref[...])
    o_ref[...] = y


def _peer_values(e_t, w_t, x2d, table, g_final, final_norm, n_tiles):
    nt, slots, t = e_t.shape
    d = x2d.shape[1]
    stage = pltpu.VMEM((OCTET * t // SUBLANES, d // (2 * LANES) * SUBLANES, LANES), I32)
    return pl.pallas_call(
        functools.partial(_value_kernel, final_norm=final_norm),
        grid=(n_tiles,),
        in_specs=[
            pl.BlockSpec((1, slots, t), lambda i: (i, 0, 0), memory_space=pltpu.SMEM),
            pl.BlockSpec((1, t, slots), lambda i: (i, 0, 0)),
            pl.BlockSpec((t, d), lambda i: (i, 0)),
            pl.BlockSpec((1, d), lambda i: (0, 0)),
            pl.BlockSpec(table.shape, lambda i: (0, 0), pipeline_mode=pl.Buffered(1)),
        ],
        out_specs=pl.BlockSpec((t, d), lambda i: (i, 0)),
        out_shape=jax.ShapeDtypeStruct(x2d.shape, F32),
        scratch_shapes=[stage, stage, pltpu.VMEM((OCTET, t, LANES), F32), pltpu.VMEM((t, d), F32)],
        compiler_params=_cparams(("arbitrary",)),
        name="peer_values",
    )(e_t, w_t, x2d, g_final.reshape(1, d), table)


def _pack_pairs(w):
    e, d = w.shape
    bits = lax.bitcast_convert_type(w.astype(BF16), jnp.uint16).astype(jnp.uint32)
    bits = bits.reshape(e * d // (2 * LANES), 2 * LANES)
    packed = bits[:, :LANES] | (bits[:, LANES:] << 16)
    return lax.bitcast_convert_type(packed, I32)


def _sc_value_kernel(tab_hbm, idx_hbm, w_hbm, out_hbm, idx_v, w_v, rows_a, rows_b, acc_a, acc_b, row_sem, out_sem,
                     *, slots):
    n_tok = out_hbm.shape[0] // (SC_CORES * SC_SUBCORES)
    d = out_hbm.shape[1]
    words = d // 2
    win = slots // 2
    wid = lax.axis_index("s") * SC_CORES + lax.axis_index("c")
    base_tok = wid * n_tok

    def gather(tl, half, buf, sem):
        rows = idx_v.at[pl.ds(tl * slots + half * win, win)]
        return pltpu.make_async_copy(tab_hbm.at[rows], buf, sem)

    def accumulate(tl, half, buf, acc):
        for g in range(words // SC_LANES // SC_GROUP):
            def row(r, sums, g=g):
                slot = jnp.full((SC_LANES,), tl * slots + half * win + r, I32)
                wv = plsc.load_gather(w_v, [slot])
                out = []
                for j in range(SC_GROUP):
                    x = buf[r, pl.ds((g * SC_GROUP + j) * SC_LANES, SC_LANES)]
                    lo = lax.bitcast_convert_type(x << 16, F32)
                    hi = lax.bitcast_convert_type(x & jnp.int32(-65536), F32)
                    out += [sums[2 * j] + wv * lo, sums[2 * j + 1] + wv * hi]
                return tuple(out)
            zeros = tuple(jnp.zeros((SC_LANES,), F32) for _ in range(2 * SC_GROUP))
            sums = lax.fori_loop(0, win, row, zeros)
            for j in range(SC_GROUP):
                chunk, lane = divmod((g * SC_GROUP + j) * SC_LANES, LANES)
                plsc.addupdate(acc.at[pl.ds(2 * chunk * LANES + lane, SC_LANES)], sums[2 * j])
                plsc.addupdate(acc.at[pl.ds((2 * chunk + 1) * LANES + lane, SC_LANES)], sums[2 * j + 1])

    def write_out(acc, tok, sem):
        return pltpu.make_async_copy(acc, out_hbm.at[tok], sem)

    @pl.loop(0, n_tok // SC_BATCH)
    def _(bi):
        tok0 = base_tok + bi * SC_BATCH
        pltpu.sync_copy(idx_hbm.at[pl.ds(tok0 * slots, SC_BATCH * slots)], idx_v)
        pltpu.sync_copy(w_hbm.at[pl.ds(tok0 * slots, SC_BATCH * slots)], w_v)
        gather(0, 0, rows_a, row_sem.at[0]).start()

        @pl.loop(0, SC_BATCH // 2)
        def _(tp):
            for parity, acc in ((0, acc_a), (1, acc_b)):
                tl = 2 * tp + parity

                @pl.when(bi * SC_BATCH + tl >= 2)
                def _():
                    write_out(acc, tok0, out_sem.at[parity]).wait()
                for k in range(d // SC_LANES):
                    acc[pl.ds(k * SC_LANES, SC_LANES)] = jnp.zeros((SC_LANES,), F32)
                gather(tl, 1, rows_b, row_sem.at[1]).start()
                gather(tl, 0, rows_a, row_sem.at[0]).wait()
                accumulate(tl, 0, rows_a, acc)
                gather(jnp.minimum(tl + 1, SC_BATCH - 1), 0, rows_a, row_sem.at[0]).start()
                gather(tl, 1, rows_b, row_sem.at[1]).wait()
                accumulate(tl, 1, rows_b, acc)
                write_out(acc, tok0 + tl, out_sem.at[parity]).start()

        gather(SC_BATCH - 1, 0, rows_a, row_sem.at[0]).wait()

    write_out(acc_a, base_tok, out_sem.at[0]).wait()
    write_out(acc_b, base_tok, out_sem.at[1]).wait()


def _peer_values_sc(idx, w, table, n_tok, d):
    slots = idx.shape[0] // n_tok
    assert n_tok % (SC_CORES * SC_SUBCORES * SC_BATCH) == 0 and SC_BATCH % 2 == 0
    mesh = plsc.VectorSubcoreMesh(core_axis_name="c", subcore_axis_name="s",
                                  num_cores=SC_CORES, num_subcores=SC_SUBCORES)
    return pl.kernel(
        functools.partial(_sc_value_kernel, slots=slots),
        out_type=jax.ShapeDtypeStruct((n_tok, d), F32),
        mesh=mesh,
        scratch_types=[
            pltpu.VMEM((SC_BATCH * slots,), I32),
            pltpu.VMEM((SC_BATCH * slots,), F32),
            pltpu.VMEM((slots // 2, d // 2), I32),
            pltpu.VMEM((slots // 2, d // 2), I32),
            pltpu.VMEM((d,), F32),
            pltpu.VMEM((d,), F32),
            pltpu.SemaphoreType.DMA((2,)),
            pltpu.SemaphoreType.DMA((2,)),
        ],
        compiler_params=pltpu.CompilerParams(needs_layout_passes=False),
        name="peer_values_sc",
    )(table, idx, w)


def _sc_score_kernel(tab_hbm, idx_hbm, xe_hbm, xo_hbm, out_hbm, idx_v, xe_v, xo_v, rows_a, rows_b, part_v, a_v, row_sem,
                     *, slots):
    n_tok = out_hbm.shape[0] // slots // (SC_CORES * SC_SUBCORES)
    words = rows_a.shape[1]
    win = slots // 2
    n_groups = words // SC_LANES // SC_GROUP
    wid = lax.axis_index("s") * SC_CORES + lax.axis_index("c")
    base_tok = wid * n_tok
    lane_id = lax.iota(I32, SC_LANES)

    def gather(tl, half, buf, sem):
        rows = idx_v.at[pl.ds(tl * slots + half * win, win)]
        return pltpu.make_async_copy(tab_hbm.at[rows], buf, sem)

    def scores(tl, half, buf):
        for r in range(win):
            part_v[pl.ds(r * SC_LANES, SC_LANES)] = jnp.zeros((SC_LANES,), F32)
        for g in range(n_groups):
            xs = []
            for j in range(SC_GROUP):
                at = pl.ds(tl * words + (g * SC_GROUP + j) * SC_LANES, SC_LANES)
                xs += [xe_v[at], xo_v[at]]

            def trip(i, xs, g=g):
                for r in range(SC_ROW_UNROLL):
                    row = i * SC_ROW_UNROLL + r
                    sums = [jnp.zeros((SC_LANES,), F32) for _ in range(4)]
                    for j in range(SC_GROUP):
                        x = buf[row, pl.ds((g * SC_GROUP + j) * SC_LANES, SC_LANES)]
                        lo = lax.bitcast_convert_type(x << 16, F32)
                        hi = lax.bitcast_convert_type(x & jnp.int32(-65536), F32)
                        sums[(2 * j) % 4] += lo * xs[2 * j]
                        sums[(2 * j + 1) % 4] += hi * xs[2 * j + 1]
                    plsc.addupdate(part_v.at[pl.ds(row * SC_LANES, SC_LANES)],
                                   (sums[0] + sums[1]) + (sums[2] + sums[3]))
                return xs
            lax.fori_loop(0, win // SC_ROW_UNROLL, trip, tuple(xs))
        for rb in range(win // SC_LANES):
            first = (lane_id + rb * SC_LANES) * SC_LANES
            total = jnp.zeros((SC_LANES,), F32)
            for lane in range(SC_LANES):
                total += plsc.load_gather(part_v, [first + lane])
            a_v[pl.ds(tl * slots + half * win + rb * SC_LANES, SC_LANES)] = total

    @pl.loop(0, n_tok // SC_BATCH)
    def _(bi):
        tok0 = base_tok + bi * SC_BATCH
        pltpu.sync_copy(idx_hbm.at[pl.ds(tok0 * slots, SC_BATCH * slots)], idx_v)
        pltpu.sync_copy(xe_hbm.at[pl.ds(tok0 * words, SC_BATCH * words)], xe_v)
        pltpu.sync_copy(xo_hbm.at[pl.ds(tok0 * words, SC_BATCH * words)], xo_v)
        gather(0, 0, rows_a, row_sem.at[0]).start()

        @pl.loop(0, SC_BATCH)
        def _(tl):
            gather(tl, 1, rows_b, row_sem.at[1]).start()
            gather(tl, 0, rows_a, row_sem.at[0]).wait()
            scores(tl, 0, rows_a)
            gather(jnp.minimum(tl + 1, SC_BATCH - 1), 0, rows_a, row_sem.at[0]).start()
            gather(tl, 1, rows_b, row_sem.at[1]).wait()
            scores(tl, 1, rows_b)

        gather(SC_BATCH - 1, 0, rows_a, row_sem.at[0]).wait()
        pltpu.sync_copy(a_v, out_hbm.at[pl.ds(tok0 * slots, SC_BATCH * slots)])


def _peer_scores_sc(idx, xe, xo, table, n_tok, slots):
    words = table.shape[1]
    assert n_tok % (SC_CORES * SC_SUBCORES * SC_BATCH) == 0
    mesh = plsc.VectorSubcoreMesh(core_axis_name="c", subcore_axis_name="s",
                                  num_cores=SC_CORES, num_subcores=SC_SUBCORES)
    return pl.kernel(
        functools.partial(_sc_score_kernel, slots=slots),
        out_type=jax.ShapeDtypeStruct((n_tok * slots,), F32),
        mesh=mesh,
        scratch_types=[
            pltpu.VMEM((SC_BATCH * slots,), I32),
            pltpu.VMEM((SC_BATCH * words,), F32),
            pltpu.VMEM((SC_BATCH * words,), F32),
            pltpu.VMEM((slots // 2, words), I32),
            pltpu.VMEM((slots // 2, words), I32),
            pltpu.VMEM((slots // 2 * SC_LANES,), F32),
            pltpu.VMEM((SC_BATCH * slots,), F32),
            pltpu.SemaphoreType.DMA((2,)),
        ],
        compiler_params=pltpu.CompilerParams(needs_layout_passes=False),
        name="peer_scores_sc",
    )(table, idx, xe, xo)


def _split_norm_kernel(x_ref, g_ref, xe_ref, xo_ref):
    d = x_ref.shape[1]
    xn = _rms(x_ref[...], g_ref[...])
    nchunk = d // (2 * LANES)
    xe_ref[...] = jnp.concatenate([xn[:, (2 * c) * LANES:(2 * c + 1) * LANES] for c in range(nchunk)], axis=1)
    xo_ref[...] = jnp.concatenate([xn[:, (2 * c + 1) * LANES:(2 * c + 2) * LANES] for c in range(nchunk)], axis=1)


def _split_norm(x2d, g, n_rows, tm):
    d = x2d.shape[1]
    first = (x2d.shape[0] - n_rows) // tm
    half = jax.ShapeDtypeStruct((n_rows, d // 2), F32)
    return pl.pallas_call(
        _split_norm_kernel,
        grid=(n_rows // tm,),
        in_specs=[pl.BlockSpec((tm, d), lambda i: (first + i, 0)), pl.BlockSpec((1, d), lambda i: (0, 0))],
        out_specs=[pl.BlockSpec((tm, d // 2), lambda i: (i, 0))] * 2,
        out_shape=[half, half],
        compiler_params=_cparams(("arbitrary",)),
        name="peer_split_norm",
    )(x2d, g.reshape(1, d))


def _score_finish_kernel(w_hbm, a_ref, gate_ref, w_ref):
    del w_hbm
    a = a_ref[0]
    gelu = 0.5 * a * (1.0 + lax.erf(a * (2.0 ** -0.5)))
    w_ref[0] = gate_ref[0].T * gelu


def _score_finish(w_full, a_sc, gate_t):
    nt, t, slots = w_full.shape
    n_sc = a_sc.shape[0]
    first = nt - n_sc
    return pl.pallas_call(
        _score_finish_kernel,
        grid=(n_sc,),
        in_specs=[
            pl.BlockSpec(memory_space=pl.ANY),
            pl.BlockSpec((1, t, slots), lambda i: (i, 0, 0)),
            pl.BlockSpec((1, slots, t), lambda i: (first + i, 0, 0)),
        ],
        out_specs=pl.BlockSpec((1, t, slots), lambda i: (first + i, 0, 0)),
        out_shape=jax.ShapeDtypeStruct(w_full.shape, F32),
        input_output_aliases={0: 0},
        compiler_params=_cparams(("arbitrary",)),
        name="peer_score_finish",
    )(w_full, a_sc, gate_t)


def _residual_norm_kernel(out_hbm, x_ref, p_ref, gf_ref, o_ref, *, final_norm):
    del out_hbm
    y = x_ref[...] + p_ref[...]
    if final_norm:
        y = _rms(y, gf_ref[...])
    o_ref[...] = y


def _residual_norm(out_full, x2d, p2d, g_final, final_norm, tm):
    t, d = p2d.shape
    first = (x2d.shape[0] - t) // tm
    return pl.pallas_call(
        functools.partial(_residual_norm_kernel, final_norm=final_norm),
        grid=(t // tm,),
        in_specs=[
            pl.BlockSpec(memory_space=pl.ANY),
            pl.BlockSpec((tm, d), lambda i: (first + i, 0)),
            pl.BlockSpec((tm, d), lambda i: (i, 0)),
            pl.BlockSpec((1, d), lambda i: (0, 0)),
        ],
        out_specs=pl.BlockSpec((tm, d), lambda i: (first + i, 0)),
        out_shape=jax.ShapeDtypeStruct(x2d.shape, F32),
        input_output_aliases={0: 0},
        compiler_params=_cparams(("arbitrary",)),
        name="peer_residual",
    )(out_full, x2d, p2d, g_final.reshape(1, d))


def _peer_ffn(x, g, w_q, k1, k2, u_emb, v_emb, g_final, final_norm):
    b, s, d = x.shape
    x2d = x.reshape(b * s, d)
    e_slot, gate_slot = _peer_router(x2d, g, w_q, k1, k2)
    u_tab, v_tab = _pack_pairs(u_emb), _pack_pairs(v_emb)
    nt, slots, t = e_slot.shape
    sc_quantum = SC_CORES * SC_SUBCORES * SC_BATCH // t
    ns_score = nt * SC_SCORE_PERCENT // 100 // sc_quantum * sc_quantum
    ns_value = nt * SC_VALUE_PERCENT // 100 // sc_quantum * sc_quantum
    ns_max = max(ns_score, ns_value)
    rows_per_expert = d // (2 * LANES)
    idx_sc = (e_slot[nt - ns_max:] // rows_per_expert).transpose(0, 2, 1).reshape(-1)

    w_tok = _peer_scores(e_slot, gate_slot, x2d, g, u_tab, nt - ns_score)
    if ns_score:
        xe, xo = _split_norm(x2d, g, ns_score * t, math.gcd(TOKEN_TILE, ns_score * t))
        a_sc = _peer_scores_sc(idx_sc[(ns_max - ns_score) * t * slots:], xe.reshape(-1), xo.reshape(-1),
                               u_tab.reshape(-1, d // 2), ns_score * t, slots)
        w_tok = _score_finish(w_tok, a_sc.reshape(ns_score, t, slots), gate_slot)
    out = _peer_values(e_slot, w_tok, x2d, v_tab, g_final, final_norm, nt - ns_value)
    if ns_value:
        peer_sc = _peer_values_sc(idx_sc[(ns_max - ns_value) * t * slots:], w_tok[nt - ns_value:].reshape(-1),
                                  v_tab.reshape(-1, d // 2), ns_value * t, d)
        out = _residual_norm(out, x2d, peer_sc, g_final, final_norm, math.gcd(TOKEN_TILE, ns_value * t))
    return out.reshape(b, s, d)


def _qkv_kernel(x_ref, g_ref, wqk_hi_ref, wqk_lo_ref, wv_ref, cos_ref, sin_ref,
                qt_ref, k0_ref, k1_ref, vt_ref, km_ref):
    tm, d = x_ref.shape[1], x_ref.shape[2]
    hn = _rms(x_ref[0], g_ref[...])
    hh, hl = _split(hn)
    qk = _dot3(hh, hl, wqk_hi_ref[...], wqk_lo_ref[...])
    v = _dot(hh, wv_ref[...])
    cos = jnp.concatenate([cos_ref[...]] * (d // LANES), axis=1)
    sin = jnp.concatenate([sin_ref[...]] * (d // LANES), axis=1)
    lane = lax.broadcasted_iota(I32, (tm, d), 1)
    first_half = (lane % HEAD_DIM) < (HEAD_DIM // 2)

    def rope(a):
        rot = jnp.where(first_half, pltpu.roll(a, d - HEAD_DIM // 2, 1), pltpu.roll(a, HEAD_DIM // 2, 1))
        return a * cos + rot * sin

    q = rope(qk[:, :d])
    k = rope(qk[:, d:])
    qt_ref[0] = q.T
    vt_ref[0] = v.T.astype(BF16)
    nb = tm // MOBA_BLOCK
    km_ref[0, 0] = jnp.mean(k.reshape(nb, MOBA_BLOCK, d), axis=1)
    row = lax.broadcasted_iota(I32, (tm, d), 0)
    block = (pl.program_id(1) * tm + row) // MOBA_BLOCK
    pair_lane = lane % LANES
    kb = k.astype(BF16)
    k0_ref[0] = jnp.where(pair_lane < HEAD_DIM, kb, jnp.where(pair_lane - HEAD_DIM == block, 1.0, 0.0).astype(BF16))
    k1_ref[0] = jnp.where(pair_lane >= HEAD_DIM, kb, jnp.where(pair_lane == block, 1.0, 0.0).astype(BF16))


def _qkv_rope(x, g, w_qkv, tm):
    b, s, d = x.shape
    half = HEAD_DIM // 2
    inv = ROPE_THETA ** (-jnp.arange(half, dtype=F32) / half)
    ang = jnp.arange(s).astype(F32)[:, None] * inv[None, :]
    cos, sin = jnp.cos(ang), jnp.sin(ang)
    cos128 = jnp.tile(jnp.concatenate([cos, cos], axis=1), (1, LANES // HEAD_DIM))
    sin128 = jnp.tile(jnp.concatenate([-sin, sin], axis=1), (1, LANES // HEAD_DIM))
    wqk_hi, wqk_lo = _split(w_qkv[:, :2 * d])
    wv = w_qkv[:, 2 * d:].astype(BF16)
    nb = tm // MOBA_BLOCK
    const = lambda i, j: (0, 0)
    qt, k0, k1, vt, km = pl.pallas_call(
        _qkv_kernel,
        grid=(b, s // tm),
        in_specs=[
            pl.BlockSpec((1, tm, d), lambda i, j: (i, j, 0)),
            pl.BlockSpec((1, d), const),
            pl.BlockSpec((d, 2 * d), const),
            pl.BlockSpec((d, 2 * d), const),
            pl.BlockSpec((d, d), const),
            pl.BlockSpec((tm, LANES), lambda i, j: (j, 0)),
            pl.BlockSpec((tm, LANES), lambda i, j: (j, 0)),
        ],
        out_specs=[
            pl.BlockSpec((1, d, tm), lambda i, j: (i, 0, j)),
            pl.BlockSpec((1, tm, d), lambda i, j: (i, j, 0)),
            pl.BlockSpec((1, tm, d), lambda i, j: (i, j, 0)),
            pl.BlockSpec((1, d, tm), lambda i, j: (i, 0, j)),
            pl.BlockSpec((1, 1, nb, d), lambda i, j: (i, j, 0, 0)),
        ],
        out_shape=[
            jax.ShapeDtypeStruct((b, d, s), F32),
            jax.ShapeDtypeStruct((b, s, d), BF16),
            jax.ShapeDtypeStruct((b, s, d), BF16),
            jax.ShapeDtypeStruct((b, d, s), BF16),
            jax.ShapeDtypeStruct((b, s // tm, nb, d), F32),
        ],
        compiler_params=_cparams(("arbitrary", "arbitrary")),
        name="qkv_rope",
    )(x, g.reshape(1, d), wqk_hi, wqk_lo, wv, cos128, sin128)
    return qt, (k0, k1), vt, km.reshape(b, s // MOBA_BLOCK, d)


def _moba_kernel(qt_ref, k0_ref, k1_ref, vt_ref, km_ref, o_ref, sa_ref, sb_ref):
    bs = MOBA_BLOCK
    nb = km_ref.shape[1]
    n_heads = LANES // HEAD_DIM
    k_refs = (k0_ref, k1_ref)
    j = pl.program_id(2)
    qt = qt_ref[0]
    km = km_ref[0]
    scale = HEAD_DIM ** -0.5 * 1.4426950408889634
    lane_km = lax.broadcasted_iota(I32, (nb, LANES), 1)
    blk = lax.broadcasted_iota(I32, (nb, bs), 0)
    zeros_pad = jnp.zeros((LANES - HEAD_DIM - nb, bs), F32)
    qh, ql = _split(qt)

    own = pl.ds(pl.multiple_of(j * bs, bs), bs)
    v_own = vt_ref[0, :, own]
    krow = lax.broadcasted_iota(I32, (bs, bs), 0)
    qcol = lax.broadcasted_iota(I32, (bs, bs), 1)

    q_augs, state = [], []
    for hh in range(n_heads):
        head_lo = hh * HEAD_DIM
        in_head_km = (lane_km >= head_lo) & (lane_km < head_lo + HEAD_DIM)
        kmh, kml = _split(jnp.where(in_head_km, km, 0.0))
        gate = _dot3(kmh, kml, qh, ql)
        valid = blk < j
        gate = jnp.where(valid, gate, -jnp.inf)
        sel = jnp.zeros((nb, bs), F32)
        for _ in range(MOBA_TOPK):
            m = jnp.max(gate, axis=0, keepdims=True)
            i = jnp.min(jnp.where(gate == m, blk, nb), axis=0, keepdims=True)
            pick = blk == i
            sel = jnp.where(pick, 1.0, sel)
            gate = jnp.where(pick, -jnp.inf, gate)
        bias_t = jnp.where((sel > 0.0) & valid, 0.0, MASK_NEG)
        q_head = qt[head_lo:head_lo + HEAD_DIM, :] * scale
        no_bias = jnp.zeros((LANES - HEAD_DIM, bs), F32)
        if hh == 0:
            q_aug = jnp.concatenate([q_head, bias_t, zeros_pad], axis=0)
            q_own = jnp.concatenate([q_head, no_bias], axis=0)
        else:
            q_aug = jnp.concatenate([bias_t, zeros_pad, q_head], axis=0)
            q_own = jnp.concatenate([no_bias, q_head], axis=0)
        q_augs.append(q_aug.astype(BF16))

        s_own = jnp.where(krow <= qcol, _dot(k_refs[hh][0, own, :], q_own.astype(BF16)), -1e30)
        m0 = jnp.max(s_own, axis=0, keepdims=True)
        p0 = jnp.exp2(s_own - m0)
        state += [m0, jnp.sum(p0, axis=0, keepdims=True), _dot(v_own, p0.astype(BF16))]

    chunk = KV_CHUNK * bs
    last_chunk = nb // KV_CHUNK - 1

    def score_chunk(c, s_ref):
        rows = pl.ds(pl.multiple_of(jnp.minimum(c, last_chunk) * chunk, chunk), chunk)
        for hh in range(n_heads):
            s_ref[hh] = _dot(k_refs[hh][0, rows, :], q_augs[hh])

    def attend(c, s_ref, state):
        vn = vt_ref[0, :, pl.ds(pl.multiple_of(c * chunk, chunk), chunk)]
        new_state = []
        for hh in range(n_heads):
            m, l, acc = state[3 * hh:3 * hh + 3]
            s = s_ref[hh]
            m_new = jnp.maximum(m, jnp.max(s, axis=0, keepdims=True))
            alpha = jnp.exp2(m - m_new)
            p = jnp.exp2(s - m_new)
            l = alpha * l + jnp.sum(p, axis=0, keepdims=True)
            acc = alpha * acc + _dot(vn, p.astype(BF16))
            new_state += [m_new, l, acc]
        return tuple(new_state)

    score_chunk(0, sa_ref)

    def body(i, state):
        score_chunk(2 * i + 1, sb_ref)
        state = attend(2 * i, sa_ref, state)
        score_chunk(2 * i + 2, sa_ref)
        return attend(2 * i + 1, sb_ref, state)

    state = lax.fori_loop(0, (j + 2 * KV_CHUNK - 1) // (2 * KV_CHUNK), body, tuple(state))
    halves = []
    for hh in range(n_heads):
        _, l, acc = state[3 * hh:3 * hh + 3]
        halves.append((acc / l)[hh * HEAD_DIM:(hh + 1) * HEAD_DIM, :])
    o_ref[0] = jnp.concatenate(halves, axis=0).T


def _moba_attention(qt, k01, vt, km):
    b, d, s = qt.shape
    nb = s // MOBA_BLOCK
    assert nb % (2 * KV_CHUNK) == 0
    return pl.pallas_call(
        _moba_kernel,
        grid=(b, d // LANES, nb),
        in_specs=[
            pl.BlockSpec((1, LANES, MOBA_BLOCK), lambda i, h, j: (i, h, j)),
            pl.BlockSpec((1, s, LANES), lambda i, h, j: (i, 0, h)),
            pl.BlockSpec((1, s, LANES), lambda i, h, j: (i, 0, h)),
            pl.BlockSpec((1, LANES, s), lambda i, h, j: (i, h, 0)),
            pl.BlockSpec((1, nb, LANES), lambda i, h, j: (i, 0, h)),
        ],
        out_specs=pl.BlockSpec((1, MOBA_BLOCK, LANES), lambda i, h, j: (i, j, h)),
        out_shape=jax.ShapeDtypeStruct((b, s, d), F32),
        scratch_shapes=[pltpu.VMEM((LANES // HEAD_DIM, KV_CHUNK * MOBA_BLOCK, MOBA_BLOCK), F32)] * 2,
        compiler_params=_cparams(("arbitrary", "arbitrary", "arbitrary")),
        name="moba_attention",
    )(qt, *k01, vt, km)


def _proj_residual_kernel(x_ref, a_ref, w_ref, o_ref):
    o_ref[...] = x_ref[...] + _dot(a_ref[...].astype(BF16), w_ref[...])


def _proj_residual(x2d, a2d, w, tm):
    t, d = x2d.shape
    return pl.pallas_call(
        _proj_residual_kernel,
        grid=(t // tm,),
        in_specs=[
            pl.BlockSpec((tm, d), lambda i: (i, 0)),
            pl.BlockSpec((tm, a2d.shape[1]), lambda i: (i, 0)),
            pl.BlockSpec(w.shape, lambda i: (0, 0)),
        ],
        out_specs=pl.BlockSpec((tm, d), lambda i: (i, 0)),
        out_shape=jax.ShapeDtypeStruct((t, d), F32),
        compiler_params=_cparams(("arbitrary",)),
        name="attn_out_proj",
    )(x2d, a2d, w.astype(BF16))


def _moba_mixer(x, g, w_qkv, w_o):
    b, s, d = x.shape
    tm = min(TOKEN_TILE, s)
    qt, k, vt, km = _qkv_rope(x, g, w_qkv, tm)
    attn = _moba_attention(qt, k, vt, km)
    return _proj_residual(x.reshape(b * s, d), attn.reshape(b * s, d), w_o, tm).reshape(b, s, d)


def kernel(x, norm_mix, norm_ffn, conv_w_in, conv_w, conv_w_out, attn_w_qkv, attn_w_o,
           peer_w_q, peer_k1, peer_k2, peer_u, peer_v, norm_final):
    depth = norm_mix.shape[0]
    tm = min(TOKEN_TILE, x.shape[1])
    for i in range(depth):
        j = i // 2
        if i % 2 == 0:
            x = _conv_mixer(x, norm_mix[i], conv_w_in[j], conv_w[j], conv_w_out[j], tm)
        else:
            x = _moba_mixer(x, norm_mix[i], attn_w_qkv[j], attn_w_o[j])
        x = _peer_ffn(x, norm_ffn[i], peer_w_q[i], peer_k1[i], peer_k2[i], peer_u[i], peer_v[i],
                      norm_final, final_norm=(i == depth - 1))
    return x
```

```python
import functools
import math

import jax
import jax.numpy as jnp
from jax import lax
from jax.experimental import pallas as pl
from jax.experimental.pallas import tpu as pltpu
from jax.experimental.pallas import tpu_sc as plsc

F32 = jnp.float32
BF16 = jnp.bfloat16
I32 = jnp.int32

RMS_EPS = 1e-6
N_HEADS = 16
HEAD_DIM = 64
MOBA_BLOCK = 256
MOBA_TOPK = 3
ROPE_THETA = 10000.0
PEER_HEADS = 8
PEER_NKEYS = 128
PEER_HALF = 64
PEER_TOPK = 16

LANES = 128
SUBLANES = 8
VMEM_LIMIT = 56 * 1024 * 1024
MASK_NEG = -1e9

TOKEN_TILE = 512
PEER_TILE = 128
OCTET = 8
SC_CORES = 2
SC_SUBCORES = 16
SC_LANES = 16
SC_BATCH = 16
SC_GROUP = 8
SC_SCORE_PERCENT = 35
SC_VALUE_PERCENT = 47
KV_CHUNK = 2


def _cparams(sem):
    return pltpu.CompilerParams(dimension_semantics=sem, vmem_limit_bytes=VMEM_LIMIT)


def _rms(x, g):
    ms = jnp.mean(x * x, axis=-1, keepdims=True)
    return x * lax.rsqrt(ms + RMS_EPS) * g


def _split(a):
    hi = a.astype(BF16)
    lo = (a - hi.astype(F32)).astype(BF16)
    return hi, lo


def _dot(a, b):
    return lax.dot_general(a, b, (((1,), (0,)), ((), ())), preferred_element_type=F32)


def _dot_t(a, b):
    return lax.dot_general(a, b, (((1,), (1,)), ((), ())), preferred_element_type=F32)


def _dot3(a_hi, a_lo, b_hi, b_lo):
    return _dot(a_hi, b_hi) + _dot(a_lo, b_hi) + _dot(a_hi, b_lo)


def _dot3_t(a_hi, a_lo, b_hi, b_lo):
    return _dot_t(a_hi, b_hi) + _dot_t(a_lo, b_hi) + _dot_t(a_hi, b_lo)


def _conv_mixer_kernel(x_ref, g_ref, win_ref, cw_ref, wout_ref, o_ref, ubuf_ref):
    tm, d = x_ref.shape[1], x_ref.shape[2]

    @pl.when(pl.program_id(1) == 0)
    def _():
        ubuf_ref[0:SUBLANES, :] = jnp.zeros((SUBLANES, d), F32)

    x = x_ref[0]
    hn = _rms(x, g_ref[...]).astype(BF16)
    bcz = _dot(hn, win_ref[...])
    b_gate, c_gate, z = bcz[:, :d], bcz[:, d:2 * d], bcz[:, 2 * d:]
    u = c_gate * z
    ubuf_ref[SUBLANES:SUBLANES + tm, :] = u
    u1 = ubuf_ref[SUBLANES - 1:SUBLANES - 1 + tm, :]
    u2 = ubuf_ref[SUBLANES - 2:SUBLANES - 2 + tm, :]
    cw = cw_ref[...]
    u_conv = cw[0:1, :] * u2 + cw[1:2, :] * u1 + cw[2:3, :] * u
    ubuf_ref[0:SUBLANES, :] = u[tm - SUBLANES:tm, :]
    y = (b_gate * u_conv).astype(BF16)
    o_ref[0] = x + _dot(y, wout_ref[...])


def _conv_mixer(x, g, w_in, conv_w, w_out, tm):
    b, s, d = x.shape
    return pl.pallas_call(
        _conv_mixer_kernel,
        grid=(b, s // tm),
        in_specs=[
            pl.BlockSpec((1, tm, d), lambda i, j: (i, j, 0)),
            pl.BlockSpec((1, d), lambda i, j: (0, 0)),
            pl.BlockSpec((d, 3 * d), lambda i, j: (0, 0)),
            pl.BlockSpec((3, d), lambda i, j: (0, 0)),
            pl.BlockSpec((d, d), lambda i, j: (0, 0)),
        ],
        out_specs=pl.BlockSpec((1, tm, d), lambda i, j: (i, j, 0)),
        out_shape=jax.ShapeDtypeStruct((b, s, d), F32),
        scratch_shapes=[pltpu.VMEM((tm + SUBLANES, d), F32)],
        compiler_params=_cparams(("arbitrary", "arbitrary")),
        name="conv_mixer",
    )(x, g.reshape(1, d), w_in.astype(BF16), conv_w, w_out.astype(BF16))


def _topk_rows(s, k, order=None, payload=None):
    if order is None:
        order = lax.broadcasted_iota(I32, s.shape, 0)
    big = jnp.iinfo(jnp.int32).max
    vals, outs = [], []
    for _ in range(k):
        m = jnp.max(s, axis=0, keepdims=True)
        i = jnp.min(jnp.where(s == m, order, big), axis=0, keepdims=True)
        pick = order == i
        vals.append(m)
        if payload is None:
            outs.append(i)
        else:
            outs.append(jnp.max(jnp.where(pick, payload, -1), axis=0, keepdims=True))
        s = jnp.where(pick, -jnp.inf, s)
    return jnp.concatenate(vals, axis=0), jnp.concatenate(outs, axis=0)


def _staircase(kk):
    groups = []
    for a in range(2):
        for b0 in range(0, kk // (a + 1), SUBLANES):
            groups.append((a, 0, b0, 1, lambda j, a=a, b0=b0: (a + 1) * (b0 + j + 1) <= kk))
    for b in range(kk // 3):
        for a0 in range(0, kk // (b + 1), SUBLANES):
            groups.append((a0, 1, b, 0, lambda j, a0=a0, b=b: (a0 + j >= 2) & ((a0 + j + 1) * (b + 1) <= kk)))
    return groups


def _staircase_topk(v1, i1, v2, i2, nk):
    kk, t = v1.shape
    j = lax.broadcasted_iota(I32, (SUBLANES, t), 0)

    def rows(x, x0, step):
        if step == 0:
            return jnp.broadcast_to(x[x0:x0 + 1, :], (SUBLANES, t))
        return x[x0:x0 + SUBLANES, :]

    cand, order, cidx = [], [], []
    for a0, a_step, b0, b_step, valid in _staircase(kk):
        ok = valid(j)
        cand.append(jnp.where(ok, rows(v1, a0, a_step) + rows(v2, b0, b_step), -jnp.inf))
        order.append(jnp.where(ok, (a0 + j * a_step) * kk + (b0 + j * b_step), jnp.iinfo(jnp.int32).max - 1))
        cidx.append(rows(i1, a0, a_step) * nk + rows(i2, b0, b_step))
    cat = lambda xs: jnp.concatenate(xs, axis=0)
    return _topk_rows(cat(cand), kk, order=cat(order), payload=cat(cidx))


def _router_kernel(x_ref, g_ref, wq_hi_ref, wq_lo_ref, k1_hi_ref, k1_lo_ref, k2_hi_ref, k2_lo_ref,
                   e_ref, gate_ref):
    table_rows = x_ref.shape[1] // (2 * LANES)
    xn = _rms(x_ref[...], g_ref[...])
    xh, xl = _split(xn)
    q = _dot3(xh, xl, wq_hi_ref[...], wq_lo_ref[...])
    nk = k1_hi_ref.shape[0]
    for h in range(PEER_HEADS):
        qh, ql = _split(q[:, h * LANES:(h + 1) * LANES])
        s1 = _dot3_t(k1_hi_ref[...], k1_lo_ref[...], qh, ql)
        s2 = _dot3_t(k2_hi_ref[...], k2_lo_ref[...], qh, ql)
        v1, i1 = _topk_rows(s1, PEER_TOPK)
        v2, i2 = _topk_rows(s2, PEER_TOPK)
        kk = PEER_TOPK
        top_s, e_idx = _staircase_topk(v1, i1, v2, i2, nk)
        p = jnp.exp(top_s - top_s[0:1, :])
        gate = p / jnp.sum(p, axis=0, keepdims=True)
        e_ref[0, h * kk:(h + 1) * kk, :] = e_idx * table_rows
        gate_ref[0, h * kk:(h + 1) * kk, :] = gate


def _pad_keys(k, lo):
    nk, half = k.shape
    out = jnp.zeros((nk, LANES), F32)
    return out.at[:, lo:lo + half].set(k)


def _peer_router(x2d, g, w_q, k1, k2):
    t, d = x2d.shape
    nt = t // PEER_TILE
    slots = PEER_HEADS * PEER_TOPK
    wq_hi, wq_lo = _split(w_q)
    k1_hi, k1_lo = _split(_pad_keys(k1, 0))
    k2_hi, k2_lo = _split(_pad_keys(k2, PEER_HALF))
    const = lambda i: (0, 0)
    kspec = pl.BlockSpec(k1_hi.shape, const)
    return pl.pallas_call(
        _router_kernel,
        grid=(nt,),
        in_specs=[
            pl.BlockSpec((PEER_TILE, d), lambda i: (i, 0)),
            pl.BlockSpec((1, d), const),
            pl.BlockSpec(wq_hi.shape, const),
            pl.BlockSpec(wq_lo.shape, const),
            kspec, kspec, kspec, kspec,
        ],
        out_specs=[
            pl.BlockSpec((1, slots, PEER_TILE), lambda i: (i, 0, 0)),
            pl.BlockSpec((1, slots, PEER_TILE), lambda i: (i, 0, 0)),
        ],
        out_shape=[
            jax.ShapeDtypeStruct((nt, slots, PEER_TILE), I32),
            jax.ShapeDtypeStruct((nt, slots, PEER_TILE), F32),
        ],
        compiler_params=_cparams(("arbitrary",)),
        name="peer_router",
    )(x2d, g.reshape(1, d), wq_hi, wq_lo, k1_hi, k1_lo, k2_hi, k2_lo)


def _gather_octet(e_ref, tab_ref, stage_ref, octet, nsub):
    t = e_ref.shape[2]
    for k in range(OCTET):
        for tok in range(t):
            off = e_ref[0, octet * OCTET + k, tok]
            stage_ref[k * (t // SUBLANES) + tok // SUBLANES,
                      pl.ds(tok % SUBLANES, nsub, stride=SUBLANES), :] = tab_ref[pl.ds(off, nsub), :]


def _octet_pipeline(n_octets, gather, consume, stage_a, stage_b):
    gather(0, stage_a)

    def pair(p, carry):
        gather(2 * p + 1, stage_b)
        consume(2 * p, stage_a)
        gather(jnp.minimum(2 * p + 2, n_octets - 1), stage_a)
        consume(2 * p + 1, stage_b)
        return carry
    lax.fori_loop(0, n_octets // 2, pair, 0)


def _score_kernel(e_ref, x_ref, g_ref, gate_ref, tab_ref, w_ref, stage_a, stage_b, a_ref):
    t, d = x_ref.shape
    nchunk = d // (2 * LANES)
    slots = e_ref.shape[1]
    rows = OCTET * t

    xn = _rms(x_ref[...], g_ref[...]).astype(BF16)
    x_even = jnp.concatenate([xn[:, (2 * c) * LANES:(2 * c + 1) * LANES] for c in range(nchunk)], axis=1)
    x_odd = jnp.concatenate([xn[:, (2 * c + 1) * LANES:(2 * c + 2) * LANES] for c in range(nchunk)], axis=1)
    rhs = jnp.concatenate([x_even, x_odd], axis=0)

    ri = lax.broadcasted_iota(I32, (2 * t, 2 * t), 0)
    ci = lax.broadcasted_iota(I32, (2 * t, 2 * t), 1)
    diag = (ci == (ri % 2) * t + ri // 2).astype(F32)

    def consume(o, stage_ref):
        planes = []
        for c in range(nchunk):
            plane = stage_ref[:, c * SUBLANES:(c + 1) * SUBLANES, :].reshape(rows, LANES)
            planes.append(pltpu.bitcast(plane, BF16))
        lhs = jnp.concatenate(planes, axis=1)
        prod = _dot_t(lhs, rhs)
        prod = prod.reshape(OCTET, 2 * t, 2 * t) * diag[None]
        a2 = jnp.sum(prod, axis=1)
        a_ref[pl.ds(pl.multiple_of(o * OCTET, OCTET), OCTET), :] = a2[:, :t] + a2[:, t:]

    gather = lambda o, stage_ref: _gather_octet(e_ref, tab_ref, stage_ref, o, nchunk)
    _octet_pipeline(slots // OCTET, gather, consume, stage_a, stage_b)

    a = a_ref[...]
    gelu = 0.5 * a * (1.0 + lax.erf(a * (2.0 ** -0.5)))
    w_ref[0] = (gate_ref[0] * gelu).T


def _peer_scores(e_t, gate_t, x2d, g, table, n_tiles):
    nt, slots, t = e_t.shape
    d = x2d.shape[1]
    stage = pltpu.VMEM((OCTET * t // SUBLANES, d // (2 * LANES) * SUBLANES, LANES), I32)
    return pl.pallas_call(
        _score_kernel,
        grid=(n_tiles,),
        in_specs=[
            pl.BlockSpec((1, slots, t), lambda i: (i, 0, 0), memory_space=pltpu.SMEM),
            pl.BlockSpec((t, d), lambda i: (i, 0)),
            pl.BlockSpec((1, d), lambda i: (0, 0)),
            pl.BlockSpec((1, slots, t), lambda i: (i, 0, 0)),
            pl.BlockSpec(table.shape, lambda i: (0, 0), pipeline_mode=pl.Buffered(1)),
        ],
        out_specs=pl.BlockSpec((1, t, slots), lambda i: (i, 0, 0)),
        out_shape=jax.ShapeDtypeStruct((nt, t, slots), F32),
        scratch_shapes=[stage, stage, pltpu.VMEM((slots, t), F32)],
        compiler_params=_cparams(("arbitrary",)),
        name="peer_scores",
    )(e_t, x2d, g.reshape(1, d), gate_t, table)


def _value_kernel(e_ref, w_ref, x_ref, gf_ref, tab_ref, o_ref, stage_a, stage_b, wb_ref, acc_ref, *, final_norm):
    t, d = x_ref.shape
    nchunk = d // (2 * LANES)
    slots = e_ref.shape[1]
    groups = t // SUBLANES

    acc_ref[...] = x_ref[...]

    def consume(o, stage_ref):
        w_oct = pltpu.roll(w_ref[0], lax.rem(slots - o * OCTET, slots), 1)
        for k in range(OCTET):
            wb_ref[k] = jnp.broadcast_to(w_oct[:, k:k + 1], (t, LANES))
        for c in range(nchunk):
            lo_cols = slice(2 * c * LANES, (2 * c + 1) * LANES)
            hi_cols = slice((2 * c + 1) * LANES, (2 * c + 2) * LANES)
            acc_lo, acc_hi = acc_ref[:, lo_cols], acc_ref[:, hi_cols]
            for k in range(OCTET):
                words = stage_ref[k * groups:(k + 1) * groups, c * SUBLANES:(c + 1) * SUBLANES, :]
                words = words.reshape(t, LANES)
                lo = pltpu.bitcast(words << 16, F32)
                hi = pltpu.bitcast(words & jnp.int32(-65536), F32)
                acc_lo = acc_lo + wb_ref[k] * lo
                acc_hi = acc_hi + wb_ref[k] * hi
            acc_ref[:, lo_cols] = acc_lo
            acc_ref[:, hi_cols] = acc_hi

    gather = lambda o, stage_ref: _gather_octet(e_ref, tab_ref, stage_ref, o, nchunk)
    _octet_pipeline(slots // OCTET, gather, consume, stage_a, stage_b)

    y = acc_ref[...]
    if final_norm:
        y = _rms(y, gf_ref[...])
    o_ref[...] = y


def _peer_values(e_t, w_t, x2d, table, g_final, final_norm, n_tiles):
    nt, slots, t = e_t.shape
    d = x2d.shape[1]
    stage = pltpu.VMEM((OCTET * t // SUBLANES, d // (2 * LANES) * SUBLANES, LANES), I32)
    return pl.pallas_call(
        functools.partial(_value_kernel, final_norm=final_norm),
        grid=(n_tiles,),
        in_specs=[
            pl.BlockSpec((1, slots, t), lambda i: (i, 0, 0), memory_space=pltpu.SMEM),
            pl.BlockSpec((1, t, slots), lambda i: (i, 0, 0)),
            pl.BlockSpec((t, d), lambda i: (i, 0)),
            pl.BlockSpec((1, d), lambda i: (0, 0)),
            pl.BlockSpec(table.shape, lambda i: (0, 0), pipeline_mode=pl.Buffered(1)),
        ],
        out_specs=pl.BlockSpec((t, d), lambda i: (i, 0)),
        out_shape=jax.ShapeDtypeStruct(x2d.shape, F32),
        scratch_shapes=[stage, stage, pltpu.VMEM((OCTET, t, LANES), F32), pltpu.VMEM((t, d), F32)],
        compiler_params=_cparams(("arbitrary",)),
        name="peer_values",
    )(e_t, w_t, x2d, g_final.reshape(1, d), table)


def _pack_pairs(w):
    e, d = w.shape
    bits = lax.bitcast_convert_type(w.astype(BF16), jnp.uint16).astype(jnp.uint32)
    bits = bits.reshape(e * d // (2 * LANES), 2 * LANES)
    packed = bits[:, :LANES] | (bits[:, LANES:] << 16)
    return lax.bitcast_convert_type(packed, I32)


def _sc_value_kernel(tab_hbm, idx_hbm, w_hbm, out_hbm, idx_v, w_v, rows_a, rows_b, acc_a, acc_b, row_sem, out_sem,
                     *, slots):
    n_tok = out_hbm.shape[0] // (SC_CORES * SC_SUBCORES)
    d = out_hbm.shape[1]
    words = d // 2
    win = slots // 2
    wid = lax.axis_index("s") * SC_CORES + lax.axis_index("c")
    base_tok = wid * n_tok

    def gather(tl, half, buf, sem):
        rows = idx_v.at[pl.ds(tl * slots + half * win, win)]
        return pltpu.make_async_copy(tab_hbm.at[rows], buf, sem)

    def accumulate(tl, half, buf, acc):
        for g in range(words // SC_LANES // SC_GROUP):
            def row(r, sums, g=g):
                slot = jnp.full((SC_LANES,), tl * slots + half * win + r, I32)
                wv = plsc.load_gather(w_v, [slot])
                out = []
                for j in range(SC_GROUP):
                    x = buf[r, pl.ds((g * SC_GROUP + j) * SC_LANES, SC_LANES)]
                    lo = lax.bitcast_convert_type(x << 16, F32)
                    hi = lax.bitcast_convert_type(x & jnp.int32(-65536), F32)
                    out += [sums[2 * j] + wv * lo, sums[2 * j + 1] + wv * hi]
                return tuple(out)
            zeros = tuple(jnp.zeros((SC_LANES,), F32) for _ in range(2 * SC_GROUP))
            sums = lax.fori_loop(0, win, row, zeros)
            for j in range(SC_GROUP):
                chunk, lane = divmod((g * SC_GROUP + j) * SC_LANES, LANES)
                plsc.addupdate(acc.at[pl.ds(2 * chunk * LANES + lane, SC_LANES)], sums[2 * j])
                plsc.addupdate(acc.at[pl.ds((2 * chunk + 1) * LANES + lane, SC_LANES)], sums[2 * j + 1])

    def write_out(acc, tok, sem):
        return pltpu.make_async_copy(acc, out_hbm.at[tok], sem)

    @pl.loop(0, n_tok // SC_BATCH)
    def _(bi):
        tok0 = base_tok + bi * SC_BATCH
        pltpu.sync_copy(idx_hbm.at[pl.ds(tok0 * slots, SC_BATCH * slots)], idx_v)
        pltpu.sync_copy(w_hbm.at[pl.ds(tok0 * slots, SC_BATCH * slots)], w_v)
        gather(0, 0, rows_a, row_sem.at[0]).start()

        @pl.loop(0, SC_BATCH // 2)
        def _(tp):
            for parity, acc in ((0, acc_a), (1, acc_b)):
                tl = 2 * tp + parity

                @pl.when(bi * SC_BATCH + tl >= 2)
                def _():
                    write_out(acc, tok0, out_sem.at[parity]).wait()
                for k in range(d // SC_LANES):
                    acc[pl.ds(k * SC_LANES, SC_LANES)] = jnp.zeros((SC_LANES,), F32)
                gather(tl, 1, rows_b, row_sem.at[1]).start()
                gather(tl, 0, rows_a, row_sem.at[0]).wait()
                accumulate(tl, 0, rows_a, acc)
                gather(jnp.minimum(tl + 1, SC_BATCH - 1), 0, rows_a, row_sem.at[0]).start()
                gather(tl, 1, rows_b, row_sem.at[1]).wait()
                accumulate(tl, 1, rows_b, acc)
                write_out(acc, tok0 + tl, out_sem.at[parity]).start()

        gather(SC_BATCH - 1, 0, rows_a, row_sem.at[0]).wait()

    write_out(acc_a, base_tok, out_sem.at[0]).wait()
    write_out(acc_b, base_tok, out_sem.at[1]).wait()


def _peer_values_sc(idx, w, table, n_tok, d):
    slots = idx.shape[0] // n_tok
    assert n_tok % (SC_CORES * SC_SUBCORES * SC_BATCH) == 0 and SC_BATCH % 2 == 0
    mesh = plsc.VectorSubcoreMesh(core_axis_name="c", subcore_axis_name="s",
                                  num_cores=SC_CORES, num_subcores=SC_SUBCORES)
    return pl.kernel(
        functools.partial(_sc_value_kernel, slots=slots),
        out_type=jax.ShapeDtypeStruct((n_tok, d), F32),
        mesh=mesh,
        scratch_types=[
            pltpu.VMEM((SC_BATCH * slots,), I32),
            pltpu.VMEM((SC_BATCH * slots,), F32),
            pltpu.VMEM((slots // 2, d // 2), I32),
            pltpu.VMEM((slots // 2, d // 2), I32),
            pltpu.VMEM((d,), F32),
            pltpu.VMEM((d,), F32),
            pltpu.SemaphoreType.DMA((2,)),
            pltpu.SemaphoreType.DMA((2,)),
        ],
        compiler_params=pltpu.CompilerParams(needs_layout_passes=False),
        name="peer_values_sc",
    )(table, idx, w)


def _sc_score_kernel(tab_hbm, idx_hbm, xe_hbm, xo_hbm, out_hbm, idx_v, xe_v, xo_v, rows_a, rows_b, part_v, a_v, row_sem,
                     *, slots):
    n_tok = out_hbm.shape[0] // slots // (SC_CORES * SC_SUBCORES)
    words = rows_a.shape[1]
    win = slots // 2
    n_groups = words // SC_LANES // SC_GROUP
    wid = lax.axis_index("s") * SC_CORES + lax.axis_index("c")
    base_tok = wid * n_tok
    lane_id = lax.iota(I32, SC_LANES)

    def gather(tl, half, buf, sem):
        rows = idx_v.at[pl.ds(tl * slots + half * win, win)]
        return pltpu.make_async_copy(tab_hbm.at[rows], buf, sem)

    def scores(tl, half, buf):
        for r in range(win):
            part_v[pl.ds(r * SC_LANES, SC_LANES)] = jnp.zeros((SC_LANES,), F32)
        for g in range(n_groups):
            xs = []
            for j in range(SC_GROUP):
                at = pl.ds(tl * words + (g * SC_GROUP + j) * SC_LANES, SC_LANES)
                xs += [xe_v[at], xo_v[at]]

            @pl.loop(0, win)
            def _(r, g=g, xs=xs):
                sums = [jnp.zeros((SC_LANES,), F32) for _ in range(4)]
                for j in range(SC_GROUP):
                    x = buf[r, pl.ds((g * SC_GROUP + j) * SC_LANES, SC_LANES)]
                    lo = lax.bitcast_convert_type(x << 16, F32)
                    hi = lax.bitcast_convert_type(x & jnp.int32(-65536), F32)
                    sums[(2 * j) % 4] += lo * xs[2 * j]
                    sums[(2 * j + 1) % 4] += hi * xs[2 * j + 1]
                plsc.addupdate(part_v.at[pl.ds(r * SC_LANES, SC_LANES)], (sums[0] + sums[1]) + (sums[2] + sums[3]))
        for rb in range(win // SC_LANES):
            first = (lane_id + rb * SC_LANES) * SC_LANES
            total = jnp.zeros((SC_LANES,), F32)
            for lane in range(SC_LANES):
                total += plsc.load_gather(part_v, [first + lane])
            a_v[pl.ds(tl * slots + half * win + rb * SC_LANES, SC_LANES)] = total

    @pl.loop(0, n_tok // SC_BATCH)
    def _(bi):
        tok0 = base_tok + bi * SC_BATCH
        pltpu.sync_copy(idx_hbm.at[pl.ds(tok0 * slots, SC_BATCH * slots)], idx_v)
        pltpu.sync_copy(xe_hbm.at[pl.ds(tok0 * words, SC_BATCH * words)], xe_v)
        pltpu.sync_copy(xo_hbm.at[pl.ds(tok0 * words, SC_BATCH * words)], xo_v)
        gather(0, 0, rows_a, row_sem.at[0]).start()

        @pl.loop(0, SC_BATCH)
        def _(tl):
            gather(tl, 1, rows_b, row_sem.at[1]).start()
            gather(tl, 0, rows_a, row_sem.at[0]).wait()
            scores(tl, 0, rows_a)
            gather(jnp.minimum(tl + 1, SC_BATCH - 1), 0, rows_a, row_sem.at[0]).start()
            gather(tl, 1, rows_b, row_sem.at[1]).wait()
            scores(tl, 1, rows_b)

        gather(SC_BATCH - 1, 0, rows_a, row_sem.at[0]).wait()
        pltpu.sync_copy(a_v, out_hbm.at[pl.ds(tok0 * slots, SC_BATCH * slots)])


def _peer_scores_sc(idx, xe, xo, table, n_tok, slots):
    words = table.shape[1]
    assert n_tok % (SC_CORES * SC_SUBCORES * SC_BATCH) == 0
    mesh = plsc.VectorSubcoreMesh(core_axis_name="c", subcore_axis_name="s",
                                  num_cores=SC_CORES, num_subcores=SC_SUBCORES)
    return pl.kernel(
        functools.partial(_sc_score_kernel, slots=slots),
        out_type=jax.ShapeDtypeStruct((n_tok * slots,), F32),
        mesh=mesh,
        scratch_types=[
            pltpu.VMEM((SC_BATCH * slots,), I32),
            pltpu.VMEM((SC_BATCH * words,), F32),
            pltpu.VMEM((SC_BATCH * words,), F32),
            pltpu.VMEM((slots // 2, words), I32),
            pltpu.VMEM((slots // 2, words), I32),
            pltpu.VMEM((slots // 2 * SC_LANES,), F32),
            pltpu.VMEM((SC_BATCH * slots,), F32),
            pltpu.SemaphoreType.DMA((2,)),
        ],
        compiler_params=pltpu.CompilerParams(needs_layout_passes=False),
        name="peer_scores_sc",
    )(table, idx, xe, xo)


def _split_norm_kernel(x_ref, g_ref, xe_ref, xo_ref):
    d = x_ref.shape[1]
    xn = _rms(x_ref[...], g_ref[...])
    nchunk = d // (2 * LANES)
    xe_ref[...] = jnp.concatenate([xn[:, (2 * c) * LANES:(2 * c + 1) * LANES] for c in range(nchunk)], axis=1)
    xo_ref[...] = jnp.concatenate([xn[:, (2 * c + 1) * LANES:(2 * c + 2) * LANES] for c in range(nchunk)], axis=1)


def _split_norm(x2d, g, n_rows, tm):
    d = x2d.shape[1]
    first = (x2d.shape[0] - n_rows) // tm
    half = jax.ShapeDtypeStruct((n_rows, d // 2), F32)
    return pl.pallas_call(
        _split_norm_kernel,
        grid=(n_rows // tm,),
        in_specs=[pl.BlockSpec((tm, d), lambda i: (first + i, 0)), pl.BlockSpec((1, d), lambda i: (0, 0))],
        out_specs=[pl.BlockSpec((tm, d // 2), lambda i: (i, 0))] * 2,
        out_shape=[half, half],
        compiler_params=_cparams(("arbitrary",)),
        name="peer_split_norm",
    )(x2d, g.reshape(1, d))


def _score_finish_kernel(w_hbm, a_ref, gate_ref, w_ref):
    del w_hbm
    a = a_ref[0]
    gelu = 0.5 * a * (1.0 + lax.erf(a * (2.0 ** -0.5)))
    w_ref[0] = gate_ref[0].T * gelu


def _score_finish(w_full, a_sc, gate_t):
    nt, t, slots = w_full.shape
    n_sc = a_sc.shape[0]
    first = nt - n_sc
    return pl.pallas_call(
        _score_finish_kernel,
        grid=(n_sc,),
        in_specs=[
            pl.BlockSpec(memory_space=pl.ANY),
            pl.BlockSpec((1, t, slots), lambda i: (i, 0, 0)),
            pl.BlockSpec((1, slots, t), lambda i: (first + i, 0, 0)),
        ],
        out_specs=pl.BlockSpec((1, t, slots), lambda i: (first + i, 0, 0)),
        out_shape=jax.ShapeDtypeStruct(w_full.shape, F32),
        input_output_aliases={0: 0},
        compiler_params=_cparams(("arbitrary",)),
        name="peer_score_finish",
    )(w_full, a_sc, gate_t)


def _residual_norm_kernel(out_hbm, x_ref, p_ref, gf_ref, o_ref, *, final_norm):
    del out_hbm
    y = x_ref[...] + p_ref[...]
    if final_norm:
        y = _rms(y, gf_ref[...])
    o_ref[...] = y


def _residual_norm(out_full, x2d, p2d, g_final, final_norm, tm):
    t, d = p2d.shape
    first = (x2d.shape[0] - t) // tm
    return pl.pallas_call(
        functools.partial(_residual_norm_kernel, final_norm=final_norm),
        grid=(t // tm,),
        in_specs=[
            pl.BlockSpec(memory_space=pl.ANY),
            pl.BlockSpec((tm, d), lambda i: (first + i, 0)),
            pl.BlockSpec((tm, d), lambda i: (i, 0)),
            pl.BlockSpec((1, d), lambda i: (0, 0)),
        ],
        out_specs=pl.BlockSpec((tm, d), lambda i: (first + i, 0)),
        out_shape=jax.ShapeDtypeStruct(x2d.shape, F32),
        input_output_aliases={0: 0},
        compiler_params=_cparams(("arbitrary",)),
        name="peer_residual",
    )(out_full, x2d, p2d, g_final.reshape(1, d))


def _peer_ffn(x, g, w_q, k1, k2, u_emb, v_emb, g_final, final_norm):
    b, s, d = x.shape
    x2d = x.reshape(b * s, d)
    e_slot, gate_slot = _peer_router(x2d, g, w_q, k1, k2)
    u_tab, v_tab = _pack_pairs(u_emb), _pack_pairs(v_emb)
    nt, slots, t = e_slot.shape
    sc_quantum = SC_CORES * SC_SUBCORES * SC_BATCH // t
    ns_score = nt * SC_SCORE_PERCENT // 100 // sc_quantum * sc_quantum
    ns_value = nt * SC_VALUE_PERCENT // 100 // sc_quantum * sc_quantum
    ns_max = max(ns_score, ns_value)
    rows_per_expert = d // (2 * LANES)
    idx_sc = (e_slot[nt - ns_max:] // rows_per_expert).transpose(0, 2, 1).reshape(-1)

    w_tok = _peer_scores(e_slot, gate_slot, x2d, g, u_tab, nt - ns_score)
    if ns_score:
        xe, xo = _split_norm(x2d, g, ns_score * t, math.gcd(TOKEN_TILE, ns_score * t))
        a_sc = _peer_scores_sc(idx_sc[(ns_max - ns_score) * t * slots:], xe.reshape(-1), xo.reshape(-1),
                               u_tab.reshape(-1, d // 2), ns_score * t, slots)
        w_tok = _score_finish(w_tok, a_sc.reshape(ns_score, t, slots), gate_slot)
    out = _peer_values(e_slot, w_tok, x2d, v_tab, g_final, final_norm, nt - ns_value)
    if ns_value:
        peer_sc = _peer_values_sc(idx_sc[(ns_max - ns_value) * t * slots:], w_tok[nt - ns_value:].reshape(-1),
                                  v_tab.reshape(-1, d // 2), ns_value * t, d)
        out = _residual_norm(out, x2d, peer_sc, g_final, final_norm, math.gcd(TOKEN_TILE, ns_value * t))
    return out.reshape(b, s, d)


def _qkv_kernel(x_ref, g_ref, wqk_hi_ref, wqk_lo_ref, wv_ref, cos_ref, sin_ref,
                qt_ref, k0_ref, k1_ref, vt_ref, km_ref):
    tm, d = x_ref.shape[1], x_ref.shape[2]
    hn = _rms(x_ref[0], g_ref[...])
    hh, hl = _split(hn)
    qk = _dot3(hh, hl, wqk_hi_ref[...], wqk_lo_ref[...])
    v = _dot(hh, wv_ref[...])
    cos = jnp.concatenate([cos_ref[...]] * (d // LANES), axis=1)
    sin = jnp.concatenate([sin_ref[...]] * (d // LANES), axis=1)
    lane = lax.broadcasted_iota(I32, (tm, d), 1)
    first_half = (lane % HEAD_DIM) < (HEAD_DIM // 2)

    def rope(a):
        rot = jnp.where(first_half, pltpu.roll(a, d - HEAD_DIM // 2, 1), pltpu.roll(a, HEAD_DIM // 2, 1))
        return a * cos + rot * sin

    q = rope(qk[:, :d])
    k = rope(qk[:, d:])
    qt_ref[0] = q.T
    vt_ref[0] = v.T.astype(BF16)
    nb = tm // MOBA_BLOCK
    km_ref[0, 0] = jnp.mean(k.reshape(nb, MOBA_BLOCK, d), axis=1)
    row = lax.broadcasted_iota(I32, (tm, d), 0)
    block = (pl.program_id(1) * tm + row) // MOBA_BLOCK
    pair_lane = lane % LANES
    kb = k.astype(BF16)
    k0_ref[0] = jnp.where(pair_lane < HEAD_DIM, kb, jnp.where(pair_lane - HEAD_DIM == block, 1.0, 0.0).astype(BF16))
    k1_ref[0] = jnp.where(pair_lane >= HEAD_DIM, kb, jnp.where(pair_lane == block, 1.0, 0.0).astype(BF16))


def _qkv_rope(x, g, w_qkv, tm):
    b, s, d = x.shape
    half = HEAD_DIM // 2
    inv = ROPE_THETA ** (-jnp.arange(half, dtype=F32) / half)
    ang = jnp.arange(s).astype(F32)[:, None] * inv[None, :]
    cos, sin = jnp.cos(ang), jnp.sin(ang)
    cos128 = jnp.tile(jnp.concatenate([cos, cos], axis=1), (1, LANES // HEAD_DIM))
    sin128 = jnp.tile(jnp.concatenate([-sin, sin], axis=1), (1, LANES // HEAD_DIM))
    wqk_hi, wqk_lo = _split(w_qkv[:, :2 * d])
    wv = w_qkv[:, 2 * d:].astype(BF16)
    nb = tm // MOBA_BLOCK
    const = lambda i, j: (0, 0)
    qt, k0, k1, vt, km = pl.pallas_call(
        _qkv_kernel,
        grid=(b, s // tm),
        in_specs=[
            pl.BlockSpec((1, tm, d), lambda i, j: (i, j, 0)),
            pl.BlockSpec((1, d), const),
            pl.BlockSpec((d, 2 * d), const),
            pl.BlockSpec((d, 2 * d), const),
            pl.BlockSpec((d, d), const),
            pl.BlockSpec((tm, LANES), lambda i, j: (j, 0)),
            pl.BlockSpec((tm, LANES), lambda i, j: (j, 0)),
        ],
        out_specs=[
            pl.BlockSpec((1, d, tm), lambda i, j: (i, 0, j)),
            pl.BlockSpec((1, tm, d), lambda i, j: (i, j, 0)),
            pl.BlockSpec((1, tm, d), lambda i, j: (i, j, 0)),
            pl.BlockSpec((1, d, tm), lambda i, j: (i, 0, j)),
            pl.BlockSpec((1, 1, nb, d), lambda i, j: (i, j, 0, 0)),
        ],
        out_shape=[
            jax.ShapeDtypeStruct((b, d, s), F32),
            jax.ShapeDtypeStruct((b, s, d), BF16),
            jax.ShapeDtypeStruct((b, s, d), BF16),
            jax.ShapeDtypeStruct((b, d, s), BF16),
            jax.ShapeDtypeStruct((b, s // tm, nb, d), F32),
        ],
        compiler_params=_cparams(("arbitrary", "arbitrary")),
        name="qkv_rope",
    )(x, g.reshape(1, d), wqk_hi, wqk_lo, wv, cos128, sin128)
    return qt, (k0, k1), vt, km.reshape(b, s // MOBA_BLOCK, d)


def _moba_kernel(qt_ref, k0_ref, k1_ref, vt_ref, km_ref, o_ref, sa_ref, sb_ref):
    bs = MOBA_BLOCK
    nb = km_ref.shape[1]
    n_heads = LANES // HEAD_DIM
    k_refs = (k0_ref, k1_ref)
    j = pl.program_id(2)
    qt = qt_ref[0]
    km = km_ref[0]
    scale = HEAD_DIM ** -0.5 * 1.4426950408889634
    lane_km = lax.broadcasted_iota(I32, (nb, LANES), 1)
    blk = lax.broadcasted_iota(I32, (nb, bs), 0)
    zeros_pad = jnp.zeros((LANES - HEAD_DIM - nb, bs), F32)
    qh, ql = _split(qt)

    own = pl.ds(pl.multiple_of(j * bs, bs), bs)
    v_own = vt_ref[0, :, own]
    krow = lax.broadcasted_iota(I32, (bs, bs), 0)
    qcol = lax.broadcasted_iota(I32, (bs, bs), 1)

    q_augs, state = [], []
    for hh in range(n_heads):
        head_lo = hh * HEAD_DIM
        in_head_km = (lane_km >= head_lo) & (lane_km < head_lo + HEAD_DIM)
        kmh, kml = _split(jnp.where(in_head_km, km, 0.0))
        gate = _dot3(kmh, kml, qh, ql)
        valid = blk < j
        gate = jnp.where(valid, gate, -jnp.inf)
        sel = jnp.zeros((nb, bs), F32)
        for _ in range(MOBA_TOPK):
            m = jnp.max(gate, axis=0, keepdims=True)
            i = jnp.min(jnp.where(gate == m, blk, nb), axis=0, keepdims=True)
            pick = blk == i
            sel = jnp.where(pick, 1.0, sel)
            gate = jnp.where(pick, -jnp.inf, gate)
        bias_t = jnp.where((sel > 0.0) & valid, 0.0, MASK_NEG)
        q_head = qt[head_lo:head_lo + HEAD_DIM, :] * scale
        no_bias = jnp.zeros((LANES - HEAD_DIM, bs), F32)
        if hh == 0:
            q_aug = jnp.concatenate([q_head, bias_t, zeros_pad], axis=0)
            q_own = jnp.concatenate([q_head, no_bias], axis=0)
        else:
            q_aug = jnp.concatenate([bias_t, zeros_pad, q_head], axis=0)
            q_own = jnp.concatenate([no_bias, q_head], axis=0)
        q_augs.append(q_aug.astype(BF16))

        s_own = jnp.where(krow <= qcol, _dot(k_refs[hh][0, own, :], q_own.astype(BF16)), -1e30)
        m0 = jnp.max(s_own, axis=0, keepdims=True)
        p0 = jnp.exp2(s_own - m0)
        state += [m0, jnp.sum(p0, axis=0, keepdims=True), _dot(v_own, p0.astype(BF16))]

    chunk = KV_CHUNK * bs
    last_chunk = nb // KV_CHUNK - 1

    def score_chunk(c, s_ref):
        rows = pl.ds(pl.multiple_of(jnp.minimum(c, last_chunk) * chunk, chunk), chunk)
        for hh in range(n_heads):
            s_ref[hh] = _dot(k_refs[hh][0, rows, :], q_augs[hh])

    def attend(c, s_ref, state):
        vn = vt_ref[0, :, pl.ds(pl.multiple_of(c * chunk, chunk), chunk)]
        new_state = []
        for hh in range(n_heads):
            m, l, acc = state[3 * hh:3 * hh + 3]
            s = s_ref[hh]
            m_new = jnp.maximum(m, jnp.max(s, axis=0, keepdims=True))
            alpha = jnp.exp2(m - m_new)
            p = jnp.exp2(s - m_new)
            l = alpha * l + jnp.sum(p, axis=0, keepdims=True)
            acc = alpha * acc + _dot(vn, p.astype(BF16))
            new_state += [m_new, l, acc]
        return tuple(new_state)

    score_chunk(0, sa_ref)

    def body(i, state):
        score_chunk(2 * i + 1, sb_ref)
        state = attend(2 * i, sa_ref, state)
        score_chunk(2 * i + 2, sa_ref)
        return attend(2 * i + 1, sb_ref, state)

    state = lax.fori_loop(0, (j + 2 * KV_CHUNK - 1) // (2 * KV_CHUNK), body, tuple(state))
    halves = []
    for hh in range(n_heads):
        _, l, acc = state[3 * hh:3 * hh + 3]
        halves.append((acc / l)[hh * HEAD_DIM:(hh + 1) * HEAD_DIM, :])
    o_ref[0] = jnp.concatenate(halves, axis=0).T


def _moba_attention(qt, k01, vt, km):
    b, d, s = qt.shape
    nb = s // MOBA_BLOCK
    assert nb % (2 * KV_CHUNK) == 0
    return pl.pallas_call(
        _moba_kernel,
        grid=(b, d // LANES, nb),
        in_specs=[
            pl.BlockSpec((1, LANES, MOBA_BLOCK), lambda i, h, j: (i, h, j)),
            pl.BlockSpec((1, s, LANES), lambda i, h, j: (i, 0, h)),
            pl.BlockSpec((1, s, LANES), lambda i, h, j: (i, 0, h)),
            pl.BlockSpec((1, LANES, s), lambda i, h, j: (i, h, 0)),
            pl.BlockSpec((1, nb, LANES), lambda i, h, j: (i, 0, h)),
        ],
        out_specs=pl.BlockSpec((1, MOBA_BLOCK, LANES), lambda i, h, j: (i, j, h)),
        out_shape=jax.ShapeDtypeStruct((b, s, d), F32),
        scratch_shapes=[pltpu.VMEM((LANES // HEAD_DIM, KV_CHUNK * MOBA_BLOCK, MOBA_BLOCK), F32)] * 2,
        compiler_params=_cparams(("arbitrary", "arbitrary", "arbitrary")),
        name="moba_attention",
    )(qt, *k01, vt, km)


def _proj_residual_kernel(x_ref, a_ref, w_ref, o_ref):
    o_ref[...] = x_ref[...] + _dot(a_ref[...].astype(BF16), w_ref[...])


def _proj_residual(x2d, a2d, w, tm):
    t, d = x2d.shape
    return pl.pallas_call(
        _proj_residual_kernel,
        grid=(t // tm,),
        in_specs=[
            pl.BlockSpec((tm, d), lambda i: (i, 0)),
            pl.BlockSpec((tm, a2d.shape[1]), lambda i: (i, 0)),
            pl.BlockSpec(w.shape, lambda i: (0, 0)),
        ],
        out_specs=pl.BlockSpec((tm, d), lambda i: (i, 0)),
        out_shape=jax.ShapeDtypeStruct((t, d), F32),
        compiler_params=_cparams(("arbitrary",)),
        name="attn_out_proj",
    )(x2d, a2d, w.astype(BF16))


def _moba_mixer(x, g, w_qkv, w_o):
    b, s, d = x.shape
    tm = min(TOKEN_TILE, s)
    qt, k, vt, km = _qkv_rope(x, g, w_qkv, tm)
    attn = _moba_attention(qt, k, vt, km)
    return _proj_residual(x.reshape(b * s, d), attn.reshape(b * s, d), w_o, tm).reshape(b, s, d)


def kernel(x, norm_mix, norm_ffn, conv_w_in, conv_w, conv_w_out, attn_w_qkv, attn_w_o,
           peer_w_q, peer_k1, peer_k2, peer_u, peer_v, norm_final):
    depth = norm_mix.shape[0]
    tm = min(TOKEN_TILE, x.shape[1])
    for i in range(depth):
        j = i // 2
        if i % 2 == 0:
            x = _conv_mixer(x, norm_mix[i], conv_w_in[j], conv_w[j], conv_w_out[j], tm)
        else:
            x = _moba_mixer(x, norm_mix[i], attn_w_qkv[j], attn_w_o[j])
        x = _peer_ffn(x, norm_ffn[i], peer_w_q[i], peer_k1[i], peer_k2[i], peer_u[i], peer_v[i],
                      norm_final, final_norm=(i == depth - 1))
    return x
```

```python
import functools
import math

import jax
import jax.numpy as jnp
from jax import lax
from jax.experimental import pallas as pl
from jax.experimental.pallas import tpu as pltpu
from jax.experimental.pallas import tpu_sc as plsc

F32 = jnp.float32
BF16 = jnp.bfloat16
I32 = jnp.int32

RMS_EPS = 1e-6
N_HEADS = 16
HEAD_DIM = 64
MOBA_BLOCK = 256
MOBA_TOPK = 3
ROPE_THETA = 10000.0
PEER_HEADS = 8
PEER_NKEYS = 128
PEER_HALF = 64
PEER_TOPK = 16

LANES = 128
SUBLANES = 8
VMEM_LIMIT = 56 * 1024 * 1024
MASK_NEG = -1e9

TOKEN_TILE = 512
PEER_TILE = 128
PACK_TILE = 512
OCTET = 8
SC_CORES = 2
SC_SUBCORES = 16
SC_LANES = 16
SC_BATCH = 16
SC_GROUP = 8
SC_SCORE_PERCENT = 35
SC_VALUE_PERCENT = 47
KV_CHUNK = 2


def _cparams(sem):
    return pltpu.CompilerParams(dimension_semantics=sem, vmem_limit_bytes=VMEM_LIMIT)


def _rms(x, g):
    ms = jnp.mean(x * x, axis=-1, keepdims=True)
    return x * lax.rsqrt(ms + RMS_EPS) * g


def _split(a):
    hi = a.astype(BF16)
    lo = (a - hi.astype(F32)).astype(BF16)
    return hi, lo


def _dot(a, b):
    return lax.dot_general(a, b, (((1,), (0,)), ((), ())), preferred_element_type=F32)


def _dot_t(a, b):
    return lax.dot_general(a, b, (((1,), (1,)), ((), ())), preferred_element_type=F32)


def _dot3(a_hi, a_lo, b_hi, b_lo):
    return _dot(a_hi, b_hi) + _dot(a_lo, b_hi) + _dot(a_hi, b_lo)


def _dot3_t(a_hi, a_lo, b_hi, b_lo):
    return _dot_t(a_hi, b_hi) + _dot_t(a_lo, b_hi) + _dot_t(a_hi, b_lo)


def _conv_mixer_kernel(x_ref, g_ref, win_ref, cw_ref, wout_ref, o_ref, ubuf_ref):
    tm, d = x_ref.shape[1], x_ref.shape[2]

    @pl.when(pl.program_id(1) == 0)
    def _():
        ubuf_ref[0:SUBLANES, :] = jnp.zeros((SUBLANES, d), F32)

    x = x_ref[0]
    hn = _rms(x, g_ref[...]).astype(BF16)
    bcz = _dot(hn, win_ref[...])
    b_gate, c_gate, z = bcz[:, :d], bcz[:, d:2 * d], bcz[:, 2 * d:]
    u = c_gate * z
    ubuf_ref[SUBLANES:SUBLANES + tm, :] = u
    u1 = ubuf_ref[SUBLANES - 1:SUBLANES - 1 + tm, :]
    u2 = ubuf_ref[SUBLANES - 2:SUBLANES - 2 + tm, :]
    cw = cw_ref[...]
    u_conv = cw[0:1, :] * u2 + cw[1:2, :] * u1 + cw[2:3, :] * u
    ubuf_ref[0:SUBLANES, :] = u[tm - SUBLANES:tm, :]
    y = (b_gate * u_conv).astype(BF16)
    o_ref[0] = x + _dot(y, wout_ref[...])


def _conv_mixer(x, g, w_in, conv_w, w_out, tm):
    b, s, d = x.shape
    return pl.pallas_call(
        _conv_mixer_kernel,
        grid=(b, s // tm),
        in_specs=[
            pl.BlockSpec((1, tm, d), lambda i, j: (i, j, 0)),
            pl.BlockSpec((1, d), lambda i, j: (0, 0)),
            pl.BlockSpec((d, 3 * d), lambda i, j: (0, 0)),
            pl.BlockSpec((3, d), lambda i, j: (0, 0)),
            pl.BlockSpec((d, d), lambda i, j: (0, 0)),
        ],
        out_specs=pl.BlockSpec((1, tm, d), lambda i, j: (i, j, 0)),
        out_shape=jax.ShapeDtypeStruct((b, s, d), F32),
        scratch_shapes=[pltpu.VMEM((tm + SUBLANES, d), F32)],
        compiler_params=_cparams(("arbitrary", "arbitrary")),
        name="conv_mixer",
    )(x, g.reshape(1, d), w_in.astype(BF16), conv_w, w_out.astype(BF16))


def _topk_rows(s, k, order=None, payload=None):
    if order is None:
        order = lax.broadcasted_iota(I32, s.shape, 0)
    big = jnp.iinfo(jnp.int32).max
    vals, outs = [], []
    for _ in range(k):
        m = jnp.max(s, axis=0, keepdims=True)
        i = jnp.min(jnp.where(s == m, order, big), axis=0, keepdims=True)
        pick = order == i
        vals.append(m)
        if payload is None:
            outs.append(i)
        else:
            outs.append(jnp.max(jnp.where(pick, payload, -1), axis=0, keepdims=True))
        s = jnp.where(pick, -jnp.inf, s)
    return jnp.concatenate(vals, axis=0), jnp.concatenate(outs, axis=0)


def _staircase(kk):
    groups = []
    for a in range(2):
        for b0 in range(0, kk // (a + 1), SUBLANES):
            groups.append((a, 0, b0, 1, lambda j, a=a, b0=b0: (a + 1) * (b0 + j + 1) <= kk))
    for b in range(kk // 3):
        for a0 in range(0, kk // (b + 1), SUBLANES):
            groups.append((a0, 1, b, 0, lambda j, a0=a0, b=b: (a0 + j >= 2) & ((a0 + j + 1) * (b + 1) <= kk)))
    return groups


def _staircase_topk(v1, i1, v2, i2, nk):
    kk, t = v1.shape
    j = lax.broadcasted_iota(I32, (SUBLANES, t), 0)

    def rows(x, x0, step):
        if step == 0:
            return jnp.broadcast_to(x[x0:x0 + 1, :], (SUBLANES, t))
        return x[x0:x0 + SUBLANES, :]

    cand, order, cidx = [], [], []
    for a0, a_step, b0, b_step, valid in _staircase(kk):
        ok = valid(j)
        cand.append(jnp.where(ok, rows(v1, a0, a_step) + rows(v2, b0, b_step), -jnp.inf))
        order.append(jnp.where(ok, (a0 + j * a_step) * kk + (b0 + j * b_step), jnp.iinfo(jnp.int32).max - 1))
        cidx.append(rows(i1, a0, a_step) * nk + rows(i2, b0, b_step))
    cat = lambda xs: jnp.concatenate(xs, axis=0)
    return _topk_rows(cat(cand), kk, order=cat(order), payload=cat(cidx))


def _router_kernel(x_ref, g_ref, wq_hi_ref, wq_lo_ref, k1_hi_ref, k1_lo_ref, k2_hi_ref, k2_lo_ref,
                   e_ref, gate_ref):
    table_rows = x_ref.shape[1] // (2 * LANES)
    xn = _rms(x_ref[...], g_ref[...])
    xh, xl = _split(xn)
    q = _dot3(xh, xl, wq_hi_ref[...], wq_lo_ref[...])
    nk = k1_hi_ref.shape[0]
    for h in range(PEER_HEADS):
        qh, ql = _split(q[:, h * LANES:(h + 1) * LANES])
        s1 = _dot3_t(k1_hi_ref[...], k1_lo_ref[...], qh, ql)
        s2 = _dot3_t(k2_hi_ref[...], k2_lo_ref[...], qh, ql)
        v1, i1 = _topk_rows(s1, PEER_TOPK)
        v2, i2 = _topk_rows(s2, PEER_TOPK)
        kk = PEER_TOPK
        top_s, e_idx = _staircase_topk(v1, i1, v2, i2, nk)
        p = jnp.exp(top_s - top_s[0:1, :])
        gate = p / jnp.sum(p, axis=0, keepdims=True)
        e_ref[0, h * kk:(h + 1) * kk, :] = e_idx * table_rows
        gate_ref[0, h * kk:(h + 1) * kk, :] = gate


def _pad_keys(k, lo):
    nk, half = k.shape
    out = jnp.zeros((nk, LANES), F32)
    return out.at[:, lo:lo + half].set(k)


def _peer_router(x2d, g, w_q, k1, k2):
    t, d = x2d.shape
    nt = t // PEER_TILE
    slots = PEER_HEADS * PEER_TOPK
    wq_hi, wq_lo = _split(w_q)
    k1_hi, k1_lo = _split(_pad_keys(k1, 0))
    k2_hi, k2_lo = _split(_pad_keys(k2, PEER_HALF))
    const = lambda i: (0, 0)
    kspec = pl.BlockSpec(k1_hi.shape, const)
    return pl.pallas_call(
        _router_kernel,
        grid=(nt,),
        in_specs=[
            pl.BlockSpec((PEER_TILE, d), lambda i: (i, 0)),
            pl.BlockSpec((1, d), const),
            pl.BlockSpec(wq_hi.shape, const),
            pl.BlockSpec(wq_lo.shape, const),
            kspec, kspec, kspec, kspec,
        ],
        out_specs=[
            pl.BlockSpec((1, slots, PEER_TILE), lambda i: (i, 0, 0)),
            pl.BlockSpec((1, slots, PEER_TILE), lambda i: (i, 0, 0)),
        ],
        out_shape=[
            jax.ShapeDtypeStruct((nt, slots, PEER_TILE), I32),
            jax.ShapeDtypeStruct((nt, slots, PEER_TILE), F32),
        ],
        compiler_params=_cparams(("arbitrary",)),
        name="peer_router",
    )(x2d, g.reshape(1, d), wq_hi, wq_lo, k1_hi, k1_lo, k2_hi, k2_lo)


def _gather_octet(e_ref, tab_ref, stage_ref, octet, nsub):
    t = e_ref.shape[2]
    for k in range(OCTET):
        for tok in range(t):
            off = e_ref[0, octet * OCTET + k, tok]
            stage_ref[k * (t // SUBLANES) + tok // SUBLANES,
                      pl.ds(tok % SUBLANES, nsub, stride=SUBLANES), :] = tab_ref[pl.ds(off, nsub), :]


def _octet_pipeline(n_octets, gather, consume, stage_a, stage_b):
    gather(0, stage_a)

    def pair(p, carry):
        gather(2 * p + 1, stage_b)
        consume(2 * p, stage_a)
        gather(jnp.minimum(2 * p + 2, n_octets - 1), stage_a)
        consume(2 * p + 1, stage_b)
        return carry
    lax.fori_loop(0, n_octets // 2, pair, 0)


def _score_kernel(e_ref, x_ref, g_ref, gate_ref, tab_ref, w_ref, stage_a, stage_b, a_ref):
    t, d = x_ref.shape
    nchunk = d // (2 * LANES)
    slots = e_ref.shape[1]
    rows = OCTET * t

    xn = _rms(x_ref[...], g_ref[...]).astype(BF16)
    x_even = jnp.concatenate([xn[:, (2 * c) * LANES:(2 * c + 1) * LANES] for c in range(nchunk)], axis=1)
    x_odd = jnp.concatenate([xn[:, (2 * c + 1) * LANES:(2 * c + 2) * LANES] for c in range(nchunk)], axis=1)
    rhs = jnp.concatenate([x_even, x_odd], axis=0)

    ri = lax.broadcasted_iota(I32, (2 * t, 2 * t), 0)
    ci = lax.broadcasted_iota(I32, (2 * t, 2 * t), 1)
    diag = (ci == (ri % 2) * t + ri // 2).astype(F32)

    def consume(o, stage_ref):
        planes = []
        for c in range(nchunk):
            plane = stage_ref[:, c * SUBLANES:(c + 1) * SUBLANES, :].reshape(rows, LANES)
            planes.append(pltpu.bitcast(plane, BF16))
        lhs = jnp.concatenate(planes, axis=1)
        prod = _dot_t(lhs, rhs)
        prod = prod.reshape(OCTET, 2 * t, 2 * t) * diag[None]
        a2 = jnp.sum(prod, axis=1)
        a_ref[pl.ds(pl.multiple_of(o * OCTET, OCTET), OCTET), :] = a2[:, :t] + a2[:, t:]

    gather = lambda o, stage_ref: _gather_octet(e_ref, tab_ref, stage_ref, o, nchunk)
    _octet_pipeline(slots // OCTET, gather, consume, stage_a, stage_b)

    a = a_ref[...]
    gelu = 0.5 * a * (1.0 + lax.erf(a * (2.0 ** -0.5)))
    w_ref[0] = (gate_ref[0] * gelu).T


def _peer_scores(e_t, gate_t, x2d, g, table, n_tiles):
    nt, slots, t = e_t.shape
    d = x2d.shape[1]
    stage = pltpu.VMEM((OCTET * t // SUBLANES, d // (2 * LANES) * SUBLANES, LANES), I32)
    return pl.pallas_call(
        _score_kernel,
        grid=(n_tiles,),
        in_specs=[
            pl.BlockSpec((1, slots, t), lambda i: (i, 0, 0), memory_space=pltpu.SMEM),
            pl.BlockSpec((t, d), lambda i: (i, 0)),
            pl.BlockSpec((1, d), lambda i: (0, 0)),
            pl.BlockSpec((1, slots, t), lambda i: (i, 0, 0)),
            pl.BlockSpec(table.shape, lambda i: (0, 0), pipeline_mode=pl.Buffered(1)),
        ],
        out_specs=pl.BlockSpec((1, t, slots), lambda i: (i, 0, 0)),
        out_shape=jax.ShapeDtypeStruct((nt, t, slots), F32),
        scratch_shapes=[stage, stage, pltpu.VMEM((slots, t), F32)],
        compiler_params=_cparams(("arbitrary",)),
        name="peer_scores",
    )(e_t, x2d, g.reshape(1, d), gate_t, table)


def _value_kernel(e_ref, w_ref, x_ref, gf_ref, tab_ref, o_ref, stage_a, stage_b, wb_ref, acc_ref, *, final_norm):
    t, d = x_ref.shape
    nchunk = d // (2 * LANES)
    slots = e_ref.shape[1]
    groups = t // SUBLANES

    acc_ref[...] = x_ref[...]

    def consume(o, stage_ref):
        w_oct = pltpu.roll(w_ref[0], lax.rem(slots - o * OCTET, slots), 1)
        for k in range(OCTET):
            wb_ref[k] = jnp.broadcast_to(w_oct[:, k:k + 1], (t, LANES))
        for c in range(nchunk):
            lo_cols = slice(2 * c * LANES, (2 * c + 1) * LANES)
            hi_cols = slice((2 * c + 1) * LANES, (2 * c + 2) * LANES)
            acc_lo, acc_hi = acc_ref[:, lo_cols], acc_ref[:, hi_cols]
            for k in range(OCTET):
                words = stage_ref[k * groups:(k + 1) * groups, c * SUBLANES:(c + 1) * SUBLANES, :]
                words = words.reshape(t, LANES)
                lo = pltpu.bitcast(words << 16, F32)
                hi = pltpu.bitcast(words & jnp.int32(-65536), F32)
                acc_lo = acc_lo + wb_ref[k] * lo
                acc_hi = acc_hi + wb_ref[k] * hi
            acc_ref[:, lo_cols] = acc_lo
            acc_ref[:, hi_cols] = acc_hi

    gather = lambda o, stage_ref: _gather_octet(e_ref, tab_ref, stage_ref, o, nchunk)
    _octet_pipeline(slots // OCTET, gather, consume, stage_a, stage_b)

    y = acc_ref[...]
    if final_norm:
        y = _rms(y, gf_ref[...])
    o_ref[...] = y


def _peer_values(e_t, w_t, x2d, table, g_final, final_norm, n_tiles):
    nt, slots, t = e_t.shape
    d = x2d.shape[1]
    stage = pltpu.VMEM((OCTET * t // SUBLANES, d // (2 * LANES) * SUBLANES, LANES), I32)
    return pl.pallas_call(
        functools.partial(_value_kernel, final_norm=final_norm),
        grid=(n_tiles,),
        in_specs=[
            pl.BlockSpec((1, slots, t), lambda i: (i, 0, 0), memory_space=pltpu.SMEM),
            pl.BlockSpec((1, t, slots), lambda i: (i, 0, 0)),
            pl.BlockSpec((t, d), lambda i: (i, 0)),
            pl.BlockSpec((1, d), lambda i: (0, 0)),
            pl.BlockSpec(table.shape, lambda i: (0, 0), pipeline_mode=pl.Buffered(1)),
        ],
        out_specs=pl.BlockSpec((t, d), lambda i: (i, 0)),
        out_shape=jax.ShapeDtypeStruct(x2d.shape, F32),
        scratch_shapes=[stage, stage, pltpu.VMEM((OCTET, t, LANES), F32), pltpu.VMEM((t, d), F32)],
        compiler_params=_cparams(("arbitrary",)),
        name="peer_values",
    )(e_t, w_t, x2d, g_final.reshape(1, d), table)


def _pack_kernel(w_ref, tc_ref, sc_ref):
    te, d = w_ref.shape
    nchunk = d // (2 * LANES)
    words = []
    for c in range(nchunk):
        lo = w_ref[:, (2 * c) * LANES:(2 * c + 1) * LANES].astype(BF16).astype(F32)
        hi = w_ref[:, (2 * c + 1) * LANES:(2 * c + 2) * LANES].astype(BF16).astype(F32)
        word = (pltpu.bitcast(hi, I32) & jnp.int32(-65536)) | lax.shift_right_logical(pltpu.bitcast(lo, I32), 16)
        tc_ref[pl.ds(c, te, stride=nchunk), :] = word
        words.append(word)
    sc_ref[...] = jnp.concatenate(words, axis=1)


def _pack_pairs(w):
    e, d = w.shape
    te = math.gcd(e, PACK_TILE)
    nchunk = d // (2 * LANES)
    return pl.pallas_call(
        _pack_kernel,
        grid=(e // te,),
        in_specs=[pl.BlockSpec((te, d), lambda i: (i, 0))],
        out_specs=[pl.BlockSpec((te * nchunk, LANES), lambda i: (i, 0)), pl.BlockSpec((te, d // 2), lambda i: (i, 0))],
        out_shape=[jax.ShapeDtypeStruct((e * nchunk, LANES), I32), jax.ShapeDtypeStruct((e, d // 2), I32)],
        compiler_params=_cparams(("arbitrary",)),
        name="peer_pack_table",
    )(w)


def _sc_value_kernel(tab_hbm, idx_hbm, w_hbm, out_hbm, idx_v, w_v, rows_a, rows_b, acc_a, acc_b, row_sem, out_sem,
                     *, slots):
    n_tok = out_hbm.shape[0] // (SC_CORES * SC_SUBCORES)
    d = out_hbm.shape[1]
    words = d // 2
    win = slots // 2
    wid = lax.axis_index("s") * SC_CORES + lax.axis_index("c")
    base_tok = wid * n_tok

    def gather(tl, half, buf, sem):
        rows = idx_v.at[pl.ds(tl * slots + half * win, win)]
        return pltpu.make_async_copy(tab_hbm.at[rows], buf, sem)

    def accumulate(tl, half, buf, acc):
        for g in range(words // SC_LANES // SC_GROUP):
            def row(r, sums, g=g):
                slot = jnp.full((SC_LANES,), tl * slots + half * win + r, I32)
                wv = plsc.load_gather(w_v, [slot])
                out = []
                for j in range(SC_GROUP):
                    x = buf[r, pl.ds((g * SC_GROUP + j) * SC_LANES, SC_LANES)]
                    lo = lax.bitcast_convert_type(x << 16, F32)
                    hi = lax.bitcast_convert_type(x & jnp.int32(-65536), F32)
                    out += [sums[2 * j] + wv * lo, sums[2 * j + 1] + wv * hi]
                return tuple(out)
            zeros = tuple(jnp.zeros((SC_LANES,), F32) for _ in range(2 * SC_GROUP))
            sums = lax.fori_loop(0, win, row, zeros)
            for j in range(SC_GROUP):
                chunk, lane = divmod((g * SC_GROUP + j) * SC_LANES, LANES)
                plsc.addupdate(acc.at[pl.ds(2 * chunk * LANES + lane, SC_LANES)], sums[2 * j])
                plsc.addupdate(acc.at[pl.ds((2 * chunk + 1) * LANES + lane, SC_LANES)], sums[2 * j + 1])

    def write_out(acc, tok, sem):
        return pltpu.make_async_copy(acc, out_hbm.at[tok], sem)

    @pl.loop(0, n_tok // SC_BATCH)
    def _(bi):
        tok0 = base_tok + bi * SC_BATCH
        pltpu.sync_copy(idx_hbm.at[pl.ds(tok0 * slots, SC_BATCH * slots)], idx_v)
        pltpu.sync_copy(w_hbm.at[pl.ds(tok0 * slots, SC_BATCH * slots)], w_v)
        gather(0, 0, rows_a, row_sem.at[0]).start()

        @pl.loop(0, SC_BATCH // 2)
        def _(tp):
            for parity, acc in ((0, acc_a), (1, acc_b)):
                tl = 2 * tp + parity

                @pl.when(bi * SC_BATCH + tl >= 2)
                def _():
                    write_out(acc, tok0, out_sem.at[parity]).wait()
                for k in range(d // SC_LANES):
                    acc[pl.ds(k * SC_LANES, SC_LANES)] = jnp.zeros((SC_LANES,), F32)
                gather(tl, 1, rows_b, row_sem.at[1]).start()
                gather(tl, 0, rows_a, row_sem.at[0]).wait()
                accumulate(tl, 0, rows_a, acc)
                gather(jnp.minimum(tl + 1, SC_BATCH - 1), 0, rows_a, row_sem.at[0]).start()
                gather(tl, 1, rows_b, row_sem.at[1]).wait()
                accumulate(tl, 1, rows_b, acc)
                write_out(acc, tok0 + tl, out_sem.at[parity]).start()

        gather(SC_BATCH - 1, 0, rows_a, row_sem.at[0]).wait()

    write_out(acc_a, base_tok, out_sem.at[0]).wait()
    write_out(acc_b, base_tok, out_sem.at[1]).wait()


def _peer_values_sc(idx, w, table, n_tok, d):
    slots = idx.shape[0] // n_tok
    assert n_tok % (SC_CORES * SC_SUBCORES * SC_BATCH) == 0 and SC_BATCH % 2 == 0
    mesh = plsc.VectorSubcoreMesh(core_axis_name="c", subcore_axis_name="s",
                                  num_cores=SC_CORES, num_subcores=SC_SUBCORES)
    return pl.kernel(
        functools.partial(_sc_value_kernel, slots=slots),
        out_type=jax.ShapeDtypeStruct((n_tok, d), F32),
        mesh=mesh,
        scratch_types=[
            pltpu.VMEM((SC_BATCH * slots,), I32),
            pltpu.VMEM((SC_BATCH * slots,), F32),
            pltpu.VMEM((slots // 2, d // 2), I32),
            pltpu.VMEM((slots // 2, d // 2), I32),
            pltpu.VMEM((d,), F32),
            pltpu.VMEM((d,), F32),
            pltpu.SemaphoreType.DMA((2,)),
            pltpu.SemaphoreType.DMA((2,)),
        ],
        compiler_params=pltpu.CompilerParams(needs_layout_passes=False),
        name="peer_values_sc",
    )(table, idx, w)


def _sc_score_kernel(tab_hbm, idx_hbm, xe_hbm, xo_hbm, out_hbm, idx_v, xe_v, xo_v, rows_a, rows_b, part_v, a_v, row_sem,
                     *, slots):
    n_tok = out_hbm.shape[0] // slots // (SC_CORES * SC_SUBCORES)
    words = rows_a.shape[1]
    win = slots // 2
    n_groups = words // SC_LANES // SC_GROUP
    wid = lax.axis_index("s") * SC_CORES + lax.axis_index("c")
    base_tok = wid * n_tok
    lane_id = lax.iota(I32, SC_LANES)

    def gather(tl, half, buf, sem):
        rows = idx_v.at[pl.ds(tl * slots + half * win, win)]
        return pltpu.make_async_copy(tab_hbm.at[rows], buf, sem)

    def scores(tl, half, buf):
        for r in range(win):
            part_v[pl.ds(r * SC_LANES, SC_LANES)] = jnp.zeros((SC_LANES,), F32)
        for g in range(n_groups):
            xs = []
            for j in range(SC_GROUP):
                at = pl.ds(tl * words + (g * SC_GROUP + j) * SC_LANES, SC_LANES)
                xs += [xe_v[at], xo_v[at]]

            @pl.loop(0, win)
            def _(r, g=g, xs=xs):
                sums = [jnp.zeros((SC_LANES,), F32) for _ in range(4)]
                for j in range(SC_GROUP):
                    x = buf[r, pl.ds((g * SC_GROUP + j) * SC_LANES, SC_LANES)]
                    lo = lax.bitcast_convert_type(x << 16, F32)
                    hi = lax.bitcast_convert_type(x & jnp.int32(-65536), F32)
                    sums[(2 * j) % 4] += lo * xs[2 * j]
                    sums[(2 * j + 1) % 4] += hi * xs[2 * j + 1]
                plsc.addupdate(part_v.at[pl.ds(r * SC_LANES, SC_LANES)], (sums[0] + sums[1]) + (sums[2] + sums[3]))
        for rb in range(win // SC_LANES):
            first = (lane_id + rb * SC_LANES) * SC_LANES
            total = jnp.zeros((SC_LANES,), F32)
            for lane in range(SC_LANES):
                total += plsc.load_gather(part_v, [first + lane])
            a_v[pl.ds(tl * slots + half * win + rb * SC_LANES, SC_LANES)] = total

    @pl.loop(0, n_tok // SC_BATCH)
    def _(bi):
        tok0 = base_tok + bi * SC_BATCH
        pltpu.sync_copy(idx_hbm.at[pl.ds(tok0 * slots, SC_BATCH * slots)], idx_v)
        pltpu.sync_copy(xe_hbm.at[pl.ds(tok0 * words, SC_BATCH * words)], xe_v)
        pltpu.sync_copy(xo_hbm.at[pl.ds(tok0 * words, SC_BATCH * words)], xo_v)
        gather(0, 0, rows_a, row_sem.at[0]).start()

        @pl.loop(0, SC_BATCH)
        def _(tl):
            gather(tl, 1, rows_b, row_sem.at[1]).start()
            gather(tl, 0, rows_a, row_sem.at[0]).wait()
            scores(tl, 0, rows_a)
            gather(jnp.minimum(tl + 1, SC_BATCH - 1), 0, rows_a, row_sem.at[0]).start()
            gather(tl, 1, rows_b, row_sem.at[1]).wait()
            scores(tl, 1, rows_b)

        gather(SC_BATCH - 1, 0, rows_a, row_sem.at[0]).wait()
        pltpu.sync_copy(a_v, out_hbm.at[pl.ds(tok0 * slots, SC_BATCH * slots)])


def _peer_scores_sc(idx, xe, xo, table, n_tok, slots):
    words = table.shape[1]
    assert n_tok % (SC_CORES * SC_SUBCORES * SC_BATCH) == 0
    mesh = plsc.VectorSubcoreMesh(core_axis_name="c", subcore_axis_name="s",
                                  num_cores=SC_CORES, num_subcores=SC_SUBCORES)
    return pl.kernel(
        functools.partial(_sc_score_kernel, slots=slots),
        out_type=jax.ShapeDtypeStruct((n_tok * slots,), F32),
        mesh=mesh,
        scratch_types=[
            pltpu.VMEM((SC_BATCH * slots,), I32),
            pltpu.VMEM((SC_BATCH * words,), F32),
            pltpu.VMEM((SC_BATCH * words,), F32),
            pltpu.VMEM((slots // 2, words), I32),
            pltpu.VMEM((slots // 2, words), I32),
            pltpu.VMEM((slots // 2 * SC_LANES,), F32),
            pltpu.VMEM((SC_BATCH * slots,), F32),
            pltpu.SemaphoreType.DMA((2,)),
        ],
        compiler_params=pltpu.CompilerParams(needs_layout_passes=False),
        name="peer_scores_sc",
    )(table, idx, xe, xo)


def _split_norm_kernel(x_ref, g_ref, xe_ref, xo_ref):
    d = x_ref.shape[1]
    xn = _rms(x_ref[...], g_ref[...])
    nchunk = d // (2 * LANES)
    xe_ref[...] = jnp.concatenate([xn[:, (2 * c) * LANES:(2 * c + 1) * LANES] for c in range(nchunk)], axis=1)
    xo_ref[...] = jnp.concatenate([xn[:, (2 * c + 1) * LANES:(2 * c + 2) * LANES] for c in range(nchunk)], axis=1)


def _split_norm(x2d, g, n_rows, tm):
    d = x2d.shape[1]
    first = (x2d.shape[0] - n_rows) // tm
    half = jax.ShapeDtypeStruct((n_rows, d // 2), F32)
    return pl.pallas_call(
        _split_norm_kernel,
        grid=(n_rows // tm,),
        in_specs=[pl.BlockSpec((tm, d), lambda i: (first + i, 0)), pl.BlockSpec((1, d), lambda i: (0, 0))],
        out_specs=[pl.BlockSpec((tm, d // 2), lambda i: (i, 0))] * 2,
        out_shape=[half, half],
        compiler_params=_cparams(("arbitrary",)),
        name="peer_split_norm",
    )(x2d, g.reshape(1, d))


def _score_finish_kernel(w_hbm, a_ref, gate_ref, w_ref):
    del w_hbm
    a = a_ref[0]
    gelu = 0.5 * a * (1.0 + lax.erf(a * (2.0 ** -0.5)))
    w_ref[0] = gate_ref[0].T * gelu


def _score_finish(w_full, a_sc, gate_t):
    nt, t, slots = w_full.shape
    n_sc = a_sc.shape[0]
    first = nt - n_sc
    return pl.pallas_call(
        _score_finish_kernel,
        grid=(n_sc,),
        in_specs=[
            pl.BlockSpec(memory_space=pl.ANY),
            pl.BlockSpec((1, t, slots), lambda i: (i, 0, 0)),
            pl.BlockSpec((1, slots, t), lambda i: (first + i, 0, 0)),
        ],
        out_specs=pl.BlockSpec((1, t, slots), lambda i: (first + i, 0, 0)),
        out_shape=jax.ShapeDtypeStruct(w_full.shape, F32),
        input_output_aliases={0: 0},
        compiler_params=_cparams(("arbitrary",)),
        name="peer_score_finish",
    )(w_full, a_sc, gate_t)


def _residual_norm_kernel(out_hbm, x_ref, p_ref, gf_ref, o_ref, *, final_norm):
    del out_hbm
    y = x_ref[...] + p_ref[...]
    if final_norm:
        y = _rms(y, gf_ref[...])
    o_ref[...] = y


def _residual_norm(out_full, x2d, p2d, g_final, final_norm, tm):
    t, d = p2d.shape
    first = (x2d.shape[0] - t) // tm
    return pl.pallas_call(
        functools.partial(_residual_norm_kernel, final_norm=final_norm),
        grid=(t // tm,),
        in_specs=[
            pl.BlockSpec(memory_space=pl.ANY),
            pl.BlockSpec((tm, d), lambda i: (first + i, 0)),
            pl.BlockSpec((tm, d), lambda i: (i, 0)),
            pl.BlockSpec((1, d), lambda i: (0, 0)),
        ],
        out_specs=pl.BlockSpec((tm, d), lambda i: (first + i, 0)),
        out_shape=jax.ShapeDtypeStruct(x2d.shape, F32),
        input_output_aliases={0: 0},
        compiler_params=_cparams(("arbitrary",)),
        name="peer_residual",
    )(out_full, x2d, p2d, g_final.reshape(1, d))


def _peer_ffn(x, g, w_q, k1, k2, u_emb, v_emb, g_final, final_norm):
    b, s, d = x.shape
    x2d = x.reshape(b * s, d)
    e_slot, gate_slot = _peer_router(x2d, g, w_q, k1, k2)
    (u_tab, u_tab_sc), (v_tab, v_tab_sc) = _pack_pairs(u_emb), _pack_pairs(v_emb)
    nt, slots, t = e_slot.shape
    sc_quantum = SC_CORES * SC_SUBCORES * SC_BATCH // t
    ns_score = nt * SC_SCORE_PERCENT // 100 // sc_quantum * sc_quantum
    ns_value = nt * SC_VALUE_PERCENT // 100 // sc_quantum * sc_quantum
    ns_max = max(ns_score, ns_value)
    rows_per_expert = d // (2 * LANES)
    idx_sc = (e_slot[nt - ns_max:] // rows_per_expert).transpose(0, 2, 1).reshape(-1)

    w_tok = _peer_scores(e_slot, gate_slot, x2d, g, u_tab, nt - ns_score)
    if ns_score:
        xe, xo = _split_norm(x2d, g, ns_score * t, math.gcd(TOKEN_TILE, ns_score * t))
        a_sc = _peer_scores_sc(idx_sc[(ns_max - ns_score) * t * slots:], xe.reshape(-1), xo.reshape(-1),
                               u_tab_sc, ns_score * t, slots)
        w_tok = _score_finish(w_tok, a_sc.reshape(ns_score, t, slots), gate_slot)
    out = _peer_values(e_slot, w_tok, x2d, v_tab, g_final, final_norm, nt - ns_value)
    if ns_value:
        peer_sc = _peer_values_sc(idx_sc[(ns_max - ns_value) * t * slots:], w_tok[nt - ns_value:].reshape(-1),
                                  v_tab_sc, ns_value * t, d)
        out = _residual_norm(out, x2d, peer_sc, g_final, final_norm, math.gcd(TOKEN_TILE, ns_value * t))
    return out.reshape(b, s, d)


def _qkv_kernel(x_ref, g_ref, wqk_hi_ref, wqk_lo_ref, wv_ref, cos_ref, sin_ref,
                qt_ref, k0_ref, k1_ref, vt_ref, km_ref):
    tm, d = x_ref.shape[1], x_ref.shape[2]
    hn = _rms(x_ref[0], g_ref[...])
    hh, hl = _split(hn)
    qk = _dot3(hh, hl, wqk_hi_ref[...], wqk_lo_ref[...])
    v = _dot(hh, wv_ref[...])
    cos = jnp.concatenate([cos_ref[...]] * (d // LANES), axis=1)
    sin = jnp.concatenate([sin_ref[...]] * (d // LANES), axis=1)
    lane = lax.broadcasted_iota(I32, (tm, d), 1)
    first_half = (lane % HEAD_DIM) < (HEAD_DIM // 2)

    def rope(a):
        rot = jnp.where(first_half, pltpu.roll(a, d - HEAD_DIM // 2, 1), pltpu.roll(a, HEAD_DIM // 2, 1))
        return a * cos + rot * sin

    q = rope(qk[:, :d])
    k = rope(qk[:, d:])
    qt_ref[0] = q.T
    vt_ref[0] = v.T.astype(BF16)
    nb = tm // MOBA_BLOCK
    km_ref[0, 0] = jnp.mean(k.reshape(nb, MOBA_BLOCK, d), axis=1)
    row = lax.broadcasted_iota(I32, (tm, d), 0)
    block = (pl.program_id(1) * tm + row) // MOBA_BLOCK
    pair_lane = lane % LANES
    kb = k.astype(BF16)
    k0_ref[0] = jnp.where(pair_lane < HEAD_DIM, kb, jnp.where(pair_lane - HEAD_DIM == block, 1.0, 0.0).astype(BF16))
    k1_ref[0] = jnp.where(pair_lane >= HEAD_DIM, kb, jnp.where(pair_lane == block, 1.0, 0.0).astype(BF16))


def _qkv_rope(x, g, w_qkv, tm):
    b, s, d = x.shape
    half = HEAD_DIM // 2
    inv = ROPE_THETA ** (-jnp.arange(half, dtype=F32) / half)
    ang = jnp.arange(s).astype(F32)[:, None] * inv[None, :]
    cos, sin = jnp.cos(ang), jnp.sin(ang)
    cos128 = jnp.tile(jnp.concatenate([cos, cos], axis=1), (1, LANES // HEAD_DIM))
    sin128 = jnp.tile(jnp.concatenate([-sin, sin], axis=1), (1, LANES // HEAD_DIM))
    wqk_hi, wqk_lo = _split(w_qkv[:, :2 * d])
    wv = w_qkv[:, 2 * d:].astype(BF16)
    nb = tm // MOBA_BLOCK
    const = lambda i, j: (0, 0)
    qt, k0, k1, vt, km = pl.pallas_call(
        _qkv_kernel,
        grid=(b, s // tm),
        in_specs=[
            pl.BlockSpec((1, tm, d), lambda i, j: (i, j, 0)),
            pl.BlockSpec((1, d), const),
            pl.BlockSpec((d, 2 * d), const),
            pl.BlockSpec((d, 2 * d), const),
            pl.BlockSpec((d, d), const),
            pl.BlockSpec((tm, LANES), lambda i, j: (j, 0)),
            pl.BlockSpec((tm, LANES), lambda i, j: (j, 0)),
        ],
        out_specs=[
            pl.BlockSpec((1, d, tm), lambda i, j: (i, 0, j)),
            pl.BlockSpec((1, tm, d), lambda i, j: (i, j, 0)),
            pl.BlockSpec((1, tm, d), lambda i, j: (i, j, 0)),
            pl.BlockSpec((1, d, tm), lambda i, j: (i, 0, j)),
            pl.BlockSpec((1, 1, nb, d), lambda i, j: (i, j, 0, 0)),
        ],
        out_shape=[
            jax.ShapeDtypeStruct((b, d, s), F32),
            jax.ShapeDtypeStruct((b, s, d), BF16),
            jax.ShapeDtypeStruct((b, s, d), BF16),
            jax.ShapeDtypeStruct((b, d, s), BF16),
            jax.ShapeDtypeStruct((b, s // tm, nb, d), F32),
        ],
        compiler_params=_cparams(("arbitrary", "arbitrary")),
        name="qkv_rope",
    )(x, g.reshape(1, d), wqk_hi, wqk_lo, wv, cos128, sin128)
    return qt, (k0, k1), vt, km.reshape(b, s // MOBA_BLOCK, d)


def _moba_kernel(qt_ref, k0_ref, k1_ref, vt_ref, km_ref, o_ref, sa_ref, sb_ref):
    bs = MOBA_BLOCK
    nb = km_ref.shape[1]
    n_heads = LANES // HEAD_DIM
    k_refs = (k0_ref, k1_ref)
    j = pl.program_id(2)
    qt = qt_ref[0]
    km = km_ref[0]
    scale = HEAD_DIM ** -0.5 * 1.4426950408889634
    lane_km = lax.broadcasted_iota(I32, (nb, LANES), 1)
    blk = lax.broadcasted_iota(I32, (nb, bs), 0)
    zeros_pad = jnp.zeros((LANES - HEAD_DIM - nb, bs), F32)
    qh, ql = _split(qt)

    own = pl.ds(pl.multiple_of(j * bs, bs), bs)
    v_own = vt_ref[0, :, own]
    krow = lax.broadcasted_iota(I32, (bs, bs), 0)
    qcol = lax.broadcasted_iota(I32, (bs, bs), 1)

    q_augs, state = [], []
    for hh in range(n_heads):
        head_lo = hh * HEAD_DIM
        in_head_km = (lane_km >= head_lo) & (lane_km < head_lo + HEAD_DIM)
        kmh, kml = _split(jnp.where(in_head_km, km, 0.0))
        gate = _dot3(kmh, kml, qh, ql)
        valid = blk < j
        gate = jnp.where(valid, gate, -jnp.inf)
        sel = jnp.zeros((nb, bs), F32)
        for _ in range(MOBA_TOPK):
            m = jnp.max(gate, axis=0, keepdims=True)
            i = jnp.min(jnp.where(gate == m, blk, nb), axis=0, keepdims=True)
            pick = blk == i
            sel = jnp.where(pick, 1.0, sel)
            gate = jnp.where(pick, -jnp.inf, gate)
        bias_t = jnp.where((sel > 0.0) & valid, 0.0, MASK_NEG)
        q_head = qt[head_lo:head_lo + HEAD_DIM, :] * scale
        no_bias = jnp.zeros((LANES - HEAD_DIM, bs), F32)
        if hh == 0:
            q_aug = jnp.concatenate([q_head, bias_t, zeros_pad], axis=0)
            q_own = jnp.concatenate([q_head, no_bias], axis=0)
        else:
            q_aug = jnp.concatenate([bias_t, zeros_pad, q_head], axis=0)
            q_own = jnp.concatenate([no_bias, q_head], axis=0)
        q_augs.append(q_aug.astype(BF16))

        s_own = jnp.where(krow <= qcol, _dot(k_refs[hh][0, own, :], q_own.astype(BF16)), -1e30)
        m0 = jnp.max(s_own, axis=0, keepdims=True)
        p0 = jnp.exp2(s_own - m0)
        state += [m0, jnp.sum(p0, axis=0, keepdims=True), _dot(v_own, p0.astype(BF16))]

    chunk = KV_CHUNK * bs
    last_chunk = nb // KV_CHUNK - 1

    def score_chunk(c, s_ref):
        rows = pl.ds(pl.multiple_of(jnp.minimum(c, last_chunk) * chunk, chunk), chunk)
        for hh in range(n_heads):
            s_ref[hh] = _dot(k_refs[hh][0, rows, :], q_augs[hh])

    def attend(c, s_ref, state):
        vn = vt_ref[0, :, pl.ds(pl.multiple_of(c * chunk, chunk), chunk)]
        new_state = []
        for hh in range(n_heads):
            m, l, acc = state[3 * hh:3 * hh + 3]
            s = s_ref[hh]
            m_new = jnp.maximum(m, jnp.max(s, axis=0, keepdims=True))
            alpha = jnp.exp2(m - m_new)
            p = jnp.exp2(s - m_new)
            l = alpha * l + jnp.sum(p, axis=0, keepdims=True)
            acc = alpha * acc + _dot(vn, p.astype(BF16))
            new_state += [m_new, l, acc]
        return tuple(new_state)

    score_chunk(0, sa_ref)

    def body(i, state):
        score_chunk(2 * i + 1, sb_ref)
        state = attend(2 * i, sa_ref, state)
        score_chunk(2 * i + 2, sa_ref)
        return attend(2 * i + 1, sb_ref, state)

    state = lax.fori_loop(0, (j + 2 * KV_CHUNK - 1) // (2 * KV_CHUNK), body, tuple(state))
    halves = []
    for hh in range(n_heads):
        _, l, acc = state[3 * hh:3 * hh + 3]
        halves.append((acc / l)[hh * HEAD_DIM:(hh + 1) * HEAD_DIM, :])
    o_ref[0] = jnp.concatenate(halves, axis=0).T


def _moba_attention(qt, k01, vt, km):
    b, d, s = qt.shape
    nb = s // MOBA_BLOCK
    assert nb % (2 * KV_CHUNK) == 0
    return pl.pallas_call(
        _moba_kernel,
        grid=(b, d // LANES, nb),
        in_specs=[
            pl.BlockSpec((1, LANES, MOBA_BLOCK), lambda i, h, j: (i, h, j)),
            pl.BlockSpec((1, s, LANES), lambda i, h, j: (i, 0, h)),
            pl.BlockSpec((1, s, LANES), lambda i, h, j: (i, 0, h)),
            pl.BlockSpec((1, LANES, s), lambda i, h, j: (i, h, 0)),
            pl.BlockSpec((1, nb, LANES), lambda i, h, j: (i, 0, h)),
        ],
        out_specs=pl.BlockSpec((1, MOBA_BLOCK, LANES), lambda i, h, j: (i, j, h)),
        out_shape=jax.ShapeDtypeStruct((b, s, d), F32),
        scratch_shapes=[pltpu.VMEM((LANES // HEAD_DIM, KV_CHUNK * MOBA_BLOCK, MOBA_BLOCK), F32)] * 2,
        compiler_params=_cparams(("arbitrary", "arbitrary", "arbitrary")),
        name="moba_attention",
    )(qt, *k01, vt, km)


def _proj_residual_kernel(x_ref, a_ref, w_ref, o_ref):
    o_ref[...] = x_ref[...] + _dot(a_ref[...].astype(BF16), w_ref[...])


def _proj_residual(x2d, a2d, w, tm):
    t, d = x2d.shape
    return pl.pallas_call(
        _proj_residual_kernel,
        grid=(t // tm,),
        in_specs=[
            pl.BlockSpec((tm, d), lambda i: (i, 0)),
            pl.BlockSpec((tm, a2d.shape[1]), lambda i: (i, 0)),
            pl.BlockSpec(w.shape, lambda i: (0, 0)),
        ],
        out_specs=pl.BlockSpec((tm, d), lambda i: (i, 0)),
        out_shape=jax.ShapeDtypeStruct((t, d), F32),
        compiler_params=_cparams(("arbitrary",)),
        name="attn_out_proj",
    )(x2d, a2d, w.astype(BF16))


def _moba_mixer(x, g, w_qkv, w_o):
    b, s, d = x.shape
    tm = min(TOKEN_TILE, s)
    qt, k, vt, km = _qkv_rope(x, g, w_qkv, tm)
    attn = _moba_attention(qt, k, vt, km)
    return _proj_residual(x.reshape(b * s, d), attn.reshape(b * s, d), w_o, tm).reshape(b, s, d)


def kernel(x, norm_mix, norm_ffn, conv_w_in, conv_w, conv_w_out, attn_w_qkv, attn_w_o,
           peer_w_q, peer_k1, peer_k2, peer_u, peer_v, norm_final):
    depth = norm_mix.shape[0]
    tm = min(TOKEN_TILE, x.shape[1])
    for i in range(depth):
        j = i // 2
        if i % 2 == 0:
            x = _conv_mixer(x, norm_mix[i], conv_w_in[j], conv_w[j], conv_w_out[j], tm)
        else:
            x = _moba_mixer(x, norm_mix[i], attn_w_qkv[j], attn_w_o[j])
        x = _peer_ffn(x, norm_ffn[i], peer_w_q[i], peer_k1[i], peer_k2[i], peer_u[i], peer_v[i],
                      norm_final, final_norm=(i == depth - 1))
    return x
```

```python
import functools
import math

import jax
import jax.numpy as jnp
from jax import lax
from jax.experimental import pallas as pl
from jax.experimental.pallas import tpu as pltpu
from jax.experimental.pallas import tpu_sc as plsc

F32 = jnp.float32
BF16 = jnp.bfloat16
I32 = jnp.int32

RMS_EPS = 1e-6
N_HEADS = 16
HEAD_DIM = 64
MOBA_BLOCK = 256
MOBA_TOPK = 3
ROPE_THETA = 10000.0
PEER_HEADS = 8
PEER_NKEYS = 128
PEER_HALF = 64
PEER_TOPK = 16

LANES = 128
SUBLANES = 8
VMEM_LIMIT = 56 * 1024 * 1024
MASK_NEG = -1e9

TOKEN_TILE = 512
PEER_TILE = 128
PACK_TILE = 512
OCTET = 8
SC_CORES = 2
SC_SUBCORES = 16
SC_LANES = 16
SC_BATCH = 16
SC_GROUP = 8
SC_SCORE_PERCENT = 35
SC_VALUE_PERCENT = 47
KV_CHUNK = 2


def _cparams(sem):
    return pltpu.CompilerParams(dimension_semantics=sem, vmem_limit_bytes=VMEM_LIMIT)


def _rms(x, g):
    ms = jnp.mean(x * x, axis=-1, keepdims=True)
    return x * lax.rsqrt(ms + RMS_EPS) * g


def _split(a):
    hi = a.astype(BF16)
    lo = (a - hi.astype(F32)).astype(BF16)
    return hi, lo


def _dot(a, b):
    return lax.dot_general(a, b, (((1,), (0,)), ((), ())), preferred_element_type=F32)


def _dot_t(a, b):
    return lax.dot_general(a, b, (((1,), (1,)), ((), ())), preferred_element_type=F32)


def _dot3(a_hi, a_lo, b_hi, b_lo):
    return _dot(a_hi, b_hi) + _dot(a_lo, b_hi) + _dot(a_hi, b_lo)


def _dot3_t(a_hi, a_lo, b_hi, b_lo):
    return _dot_t(a_hi, b_hi) + _dot_t(a_lo, b_hi) + _dot_t(a_hi, b_lo)


def _conv_mixer_kernel(x_ref, g_ref, win_ref, cw_ref, wout_ref, o_ref, ubuf_ref):
    tm, d = x_ref.shape[1], x_ref.shape[2]

    @pl.when(pl.program_id(1) == 0)
    def _():
        ubuf_ref[0:SUBLANES, :] = jnp.zeros((SUBLANES, d), F32)

    x = x_ref[0]
    hn = _rms(x, g_ref[...]).astype(BF16)
    bcz = _dot(hn, win_ref[...])
    b_gate, c_gate, z = bcz[:, :d], bcz[:, d:2 * d], bcz[:, 2 * d:]
    u = c_gate * z
    ubuf_ref[SUBLANES:SUBLANES + tm, :] = u
    u1 = ubuf_ref[SUBLANES - 1:SUBLANES - 1 + tm, :]
    u2 = ubuf_ref[SUBLANES - 2:SUBLANES - 2 + tm, :]
    cw = cw_ref[...]
    u_conv = cw[0:1, :] * u2 + cw[1:2, :] * u1 + cw[2:3, :] * u
    ubuf_ref[0:SUBLANES, :] = u[tm - SUBLANES:tm, :]
    y = (b_gate * u_conv).astype(BF16)
    o_ref[0] = x + _dot(y, wout_ref[...])


def _conv_mixer(x, g, w_in, conv_w, w_out, tm):
    b, s, d = x.shape
    return pl.pallas_call(
        _conv_mixer_kernel,
        grid=(b, s // tm),
        in_specs=[
            pl.BlockSpec((1, tm, d), lambda i, j: (i, j, 0)),
            pl.BlockSpec((1, d), lambda i, j: (0, 0)),
            pl.BlockSpec((d, 3 * d), lambda i, j: (0, 0)),
            pl.BlockSpec((3, d), lambda i, j: (0, 0)),
            pl.BlockSpec((d, d), lambda i, j: (0, 0)),
        ],
        out_specs=pl.BlockSpec((1, tm, d), lambda i, j: (i, j, 0)),
        out_shape=jax.ShapeDtypeStruct((b, s, d), F32),
        scratch_shapes=[pltpu.VMEM((tm + SUBLANES, d), F32)],
        compiler_params=_cparams(("arbitrary", "arbitrary")),
        name="conv_mixer",
    )(x, g.reshape(1, d), w_in.astype(BF16), conv_w, w_out.astype(BF16))


def _topk_rows(s, k, order=None, payload=None):
    if order is None:
        order = lax.broadcasted_iota(I32, s.shape, 0)
    big = jnp.iinfo(jnp.int32).max
    vals, outs = [], []
    for _ in range(k):
        m = jnp.max(s, axis=0, keepdims=True)
        i = jnp.min(jnp.where(s == m, order, big), axis=0, keepdims=True)
        pick = order == i
        vals.append(m)
        if payload is None:
            outs.append(i)
        else:
            outs.append(jnp.max(jnp.where(pick, payload, -1), axis=0, keepdims=True))
        s = jnp.where(pick, -jnp.inf, s)
    return jnp.concatenate(vals, axis=0), jnp.concatenate(outs, axis=0)


def _staircase(kk):
    groups = []
    for a in range(2):
        for b0 in range(0, kk // (a + 1), SUBLANES):
            groups.append((a, 0, b0, 1, lambda j, a=a, b0=b0: (a + 1) * (b0 + j + 1) <= kk))
    for b in range(kk // 3):
        for a0 in range(0, kk // (b + 1), SUBLANES):
            groups.append((a0, 1, b, 0, lambda j, a0=a0, b=b: (a0 + j >= 2) & ((a0 + j + 1) * (b + 1) <= kk)))
    return groups


def _staircase_topk(v1, i1, v2, i2, nk):
    kk, t = v1.shape
    j = lax.broadcasted_iota(I32, (SUBLANES, t), 0)

    def rows(x, x0, step):
        if step == 0:
            return jnp.broadcast_to(x[x0:x0 + 1, :], (SUBLANES, t))
        return x[x0:x0 + SUBLANES, :]

    cand, order, cidx = [], [], []
    for a0, a_step, b0, b_step, valid in _staircase(kk):
        ok = valid(j)
        cand.append(jnp.where(ok, rows(v1, a0, a_step) + rows(v2, b0, b_step), -jnp.inf))
        order.append(jnp.where(ok, (a0 + j * a_step) * kk + (b0 + j * b_step), jnp.iinfo(jnp.int32).max - 1))
        cidx.append(rows(i1, a0, a_step) * nk + rows(i2, b0, b_step))
    cat = lambda xs: jnp.concatenate(xs, axis=0)
    return _topk_rows(cat(cand), kk, order=cat(order), payload=cat(cidx))


def _router_kernel(x_ref, g_ref, wq_hi_ref, wq_lo_ref, k1_hi_ref, k1_lo_ref, k2_hi_ref, k2_lo_ref,
                   e_ref, gate_ref, e_tok_ref):
    table_rows = x_ref.shape[1] // (2 * LANES)
    ids = []
    xn = _rms(x_ref[...], g_ref[...])
    xh, xl = _split(xn)
    q = _dot3(xh, xl, wq_hi_ref[...], wq_lo_ref[...])
    nk = k1_hi_ref.shape[0]
    for h in range(PEER_HEADS):
        qh, ql = _split(q[:, h * LANES:(h + 1) * LANES])
        s1 = _dot3_t(k1_hi_ref[...], k1_lo_ref[...], qh, ql)
        s2 = _dot3_t(k2_hi_ref[...], k2_lo_ref[...], qh, ql)
        v1, i1 = _topk_rows(s1, PEER_TOPK)
        v2, i2 = _topk_rows(s2, PEER_TOPK)
        kk = PEER_TOPK
        top_s, e_idx = _staircase_topk(v1, i1, v2, i2, nk)
        p = jnp.exp(top_s - top_s[0:1, :])
        gate = p / jnp.sum(p, axis=0, keepdims=True)
        e_ref[0, h * kk:(h + 1) * kk, :] = e_idx * table_rows
        gate_ref[0, h * kk:(h + 1) * kk, :] = gate
        ids.append(e_idx)
    e_tok_ref[0] = jnp.concatenate(ids, axis=0).T


def _pad_keys(k, lo):
    nk, half = k.shape
    out = jnp.zeros((nk, LANES), F32)
    return out.at[:, lo:lo + half].set(k)


def _peer_router(x2d, g, w_q, k1, k2):
    t, d = x2d.shape
    nt = t // PEER_TILE
    slots = PEER_HEADS * PEER_TOPK
    wq_hi, wq_lo = _split(w_q)
    k1_hi, k1_lo = _split(_pad_keys(k1, 0))
    k2_hi, k2_lo = _split(_pad_keys(k2, PEER_HALF))
    const = lambda i: (0, 0)
    kspec = pl.BlockSpec(k1_hi.shape, const)
    return pl.pallas_call(
        _router_kernel,
        grid=(nt,),
        in_specs=[
            pl.BlockSpec((PEER_TILE, d), lambda i: (i, 0)),
            pl.BlockSpec((1, d), const),
            pl.BlockSpec(wq_hi.shape, const),
            pl.BlockSpec(wq_lo.shape, const),
            kspec, kspec, kspec, kspec,
        ],
        out_specs=[
            pl.BlockSpec((1, slots, PEER_TILE), lambda i: (i, 0, 0)),
            pl.BlockSpec((1, slots, PEER_TILE), lambda i: (i, 0, 0)),
            pl.BlockSpec((1, PEER_TILE, slots), lambda i: (i, 0, 0)),
        ],
        out_shape=[
            jax.ShapeDtypeStruct((nt, slots, PEER_TILE), I32),
            jax.ShapeDtypeStruct((nt, slots, PEER_TILE), F32),
            jax.ShapeDtypeStruct((nt, PEER_TILE, slots), I32),
        ],
        compiler_params=_cparams(("arbitrary",)),
        name="peer_router",
    )(x2d, g.reshape(1, d), wq_hi, wq_lo, k1_hi, k1_lo, k2_hi, k2_lo)


def _gather_octet(e_ref, tab_ref, stage_ref, octet, nsub):
    t = e_ref.shape[2]
    for k in range(OCTET):
        for tok in range(t):
            off = e_ref[0, octet * OCTET + k, tok]
            stage_ref[k * (t // SUBLANES) + tok // SUBLANES,
                      pl.ds(tok % SUBLANES, nsub, stride=SUBLANES), :] = tab_ref[pl.ds(off, nsub), :]


def _octet_pipeline(n_octets, gather, consume, stage_a, stage_b):
    gather(0, stage_a)

    def pair(p, carry):
        gather(2 * p + 1, stage_b)
        consume(2 * p, stage_a)
        gather(jnp.minimum(2 * p + 2, n_octets - 1), stage_a)
        consume(2 * p + 1, stage_b)
        return carry
    lax.fori_loop(0, n_octets // 2, pair, 0)


def _score_kernel(e_ref, x_ref, g_ref, gate_ref, tab_ref, w_ref, stage_a, stage_b, a_ref):
    t, d = x_ref.shape
    nchunk = d // (2 * LANES)
    slots = e_ref.shape[1]
    rows = OCTET * t

    xn = _rms(x_ref[...], g_ref[...]).astype(BF16)
    x_even = jnp.concatenate([xn[:, (2 * c) * LANES:(2 * c + 1) * LANES] for c in range(nchunk)], axis=1)
    x_odd = jnp.concatenate([xn[:, (2 * c + 1) * LANES:(2 * c + 2) * LANES] for c in range(nchunk)], axis=1)
    rhs = jnp.concatenate([x_even, x_odd], axis=0)

    ri = lax.broadcasted_iota(I32, (2 * t, 2 * t), 0)
    ci = lax.broadcasted_iota(I32, (2 * t, 2 * t), 1)
    diag = (ci == (ri % 2) * t + ri // 2).astype(F32)

    def consume(o, stage_ref):
        planes = []
        for c in range(nchunk):
            plane = stage_ref[:, c * SUBLANES:(c + 1) * SUBLANES, :].reshape(rows, LANES)
            planes.append(pltpu.bitcast(plane, BF16))
        lhs = jnp.concatenate(planes, axis=1)
        prod = _dot_t(lhs, rhs)
        prod = prod.reshape(OCTET, 2 * t, 2 * t) * diag[None]
        a2 = jnp.sum(prod, axis=1)
        a_ref[pl.ds(pl.multiple_of(o * OCTET, OCTET), OCTET), :] = a2[:, :t] + a2[:, t:]

    gather = lambda o, stage_ref: _gather_octet(e_ref, tab_ref, stage_ref, o, nchunk)
    _octet_pipeline(slots // OCTET, gather, consume, stage_a, stage_b)

    a = a_ref[...]
    gelu = 0.5 * a * (1.0 + lax.erf(a * (2.0 ** -0.5)))
    w_ref[0] = (gate_ref[0] * gelu).T


def _peer_scores(e_t, gate_t, x2d, g, table, n_tiles):
    nt, slots, t = e_t.shape
    d = x2d.shape[1]
    stage = pltpu.VMEM((OCTET * t // SUBLANES, d // (2 * LANES) * SUBLANES, LANES), I32)
    return pl.pallas_call(
        _score_kernel,
        grid=(n_tiles,),
        in_specs=[
            pl.BlockSpec((1, slots, t), lambda i: (i, 0, 0), memory_space=pltpu.SMEM),
            pl.BlockSpec((t, d), lambda i: (i, 0)),
            pl.BlockSpec((1, d), lambda i: (0, 0)),
            pl.BlockSpec((1, slots, t), lambda i: (i, 0, 0)),
            pl.BlockSpec(table.shape, lambda i: (0, 0), pipeline_mode=pl.Buffered(1)),
        ],
        out_specs=pl.BlockSpec((1, t, slots), lambda i: (i, 0, 0)),
        out_shape=jax.ShapeDtypeStruct((nt, t, slots), F32),
        scratch_shapes=[stage, stage, pltpu.VMEM((slots, t), F32)],
        compiler_params=_cparams(("arbitrary",)),
        name="peer_scores",
    )(e_t, x2d, g.reshape(1, d), gate_t, table)


def _value_kernel(e_ref, w_ref, x_ref, gf_ref, tab_ref, o_ref, stage_a, stage_b, wb_ref, acc_ref, *, final_norm):
    t, d = x_ref.shape
    nchunk = d // (2 * LANES)
    slots = e_ref.shape[1]
    groups = t // SUBLANES

    acc_ref[...] = x_ref[...]

    def consume(o, stage_ref):
        w_oct = pltpu.roll(w_ref[0], lax.rem(slots - o * OCTET, slots), 1)
        for k in range(OCTET):
            wb_ref[k] = jnp.broadcast_to(w_oct[:, k:k + 1], (t, LANES))
        for c in range(nchunk):
            lo_cols = slice(2 * c * LANES, (2 * c + 1) * LANES)
            hi_cols = slice((2 * c + 1) * LANES, (2 * c + 2) * LANES)
            acc_lo, acc_hi = acc_ref[:, lo_cols], acc_ref[:, hi_cols]
            for k in range(OCTET):
                words = stage_ref[k * groups:(k + 1) * groups, c * SUBLANES:(c + 1) * SUBLANES, :]
                words = words.reshape(t, LANES)
                lo = pltpu.bitcast(words << 16, F32)
                hi = pltpu.bitcast(words & jnp.int32(-65536), F32)
                acc_lo = acc_lo + wb_ref[k] * lo
                acc_hi = acc_hi + wb_ref[k] * hi
            acc_ref[:, lo_cols] = acc_lo
            acc_ref[:, hi_cols] = acc_hi

    gather = lambda o, stage_ref: _gather_octet(e_ref, tab_ref, stage_ref, o, nchunk)
    _octet_pipeline(slots // OCTET, gather, consume, stage_a, stage_b)

    y = acc_ref[...]
    if final_norm:
        y = _rms(y, gf_ref[...])
    o_ref[...] = y


def _peer_values(e_t, w_t, x2d, table, g_final, final_norm, n_tiles):
    nt, slots, t = e_t.shape
    d = x2d.shape[1]
    stage = pltpu.VMEM((OCTET * t // SUBLANES, d // (2 * LANES) * SUBLANES, LANES), I32)
    return pl.pallas_call(
        functools.partial(_value_kernel, final_norm=final_norm),
        grid=(n_tiles,),
        in_specs=[
            pl.BlockSpec((1, slots, t), lambda i: (i, 0, 0), memory_space=pltpu.SMEM),
            pl.BlockSpec((1, t, slots), lambda i: (i, 0, 0)),
            pl.BlockSpec((t, d), lambda i: (i, 0)),
            pl.BlockSpec((1, d), lambda i: (0, 0)),
            pl.BlockSpec(table.shape, lambda i: (0, 0), pipeline_mode=pl.Buffered(1)),
        ],
        out_specs=pl.BlockSpec((t, d), lambda i: (i, 0)),
        out_shape=jax.ShapeDtypeStruct(x2d.shape, F32),
        scratch_shapes=[stage, stage, pltpu.VMEM((OCTET, t, LANES), F32), pltpu.VMEM((t, d), F32)],
        compiler_params=_cparams(("arbitrary",)),
        name="peer_values",
    )(e_t, w_t, x2d, g_final.reshape(1, d), table)


def _pack_kernel(w_ref, tc_ref, sc_ref):
    te, d = w_ref.shape
    nchunk = d // (2 * LANES)
    words = []
    for c in range(nchunk):
        lo = w_ref[:, (2 * c) * LANES:(2 * c + 1) * LANES].astype(BF16).astype(F32)
        hi = w_ref[:, (2 * c + 1) * LANES:(2 * c + 2) * LANES].astype(BF16).astype(F32)
        word = (pltpu.bitcast(hi, I32) & jnp.int32(-65536)) | lax.shift_right_logical(pltpu.bitcast(lo, I32), 16)
        tc_ref[pl.ds(c, te, stride=nchunk), :] = word
        words.append(word)
    sc_ref[...] = jnp.concatenate(words, axis=1)


def _pack_pairs(w):
    e, d = w.shape
    te = math.gcd(e, PACK_TILE)
    nchunk = d // (2 * LANES)
    return pl.pallas_call(
        _pack_kernel,
        grid=(e // te,),
        in_specs=[pl.BlockSpec((te, d), lambda i: (i, 0))],
        out_specs=[pl.BlockSpec((te * nchunk, LANES), lambda i: (i, 0)), pl.BlockSpec((te, d // 2), lambda i: (i, 0))],
        out_shape=[jax.ShapeDtypeStruct((e * nchunk, LANES), I32), jax.ShapeDtypeStruct((e, d // 2), I32)],
        compiler_params=_cparams(("arbitrary",)),
        name="peer_pack_table",
    )(w)


def _sc_value_kernel(tab_hbm, idx_hbm, w_hbm, out_hbm, idx_v, w_v, rows_a, rows_b, acc_a, acc_b, row_sem, out_sem,
                     *, slots):
    n_tok = out_hbm.shape[0] // (SC_CORES * SC_SUBCORES)
    d = out_hbm.shape[1]
    words = d // 2
    win = slots // 2
    wid = lax.axis_index("s") * SC_CORES + lax.axis_index("c")
    base_tok = wid * n_tok

    def gather(tl, half, buf, sem):
        rows = idx_v.at[pl.ds(tl * slots + half * win, win)]
        return pltpu.make_async_copy(tab_hbm.at[rows], buf, sem)

    def accumulate(tl, half, buf, acc):
        for g in range(words // SC_LANES // SC_GROUP):
            def row(r, sums, g=g):
                slot = jnp.full((SC_LANES,), tl * slots + half * win + r, I32)
                wv = plsc.load_gather(w_v, [slot])
                out = []
                for j in range(SC_GROUP):
                    x = buf[r, pl.ds((g * SC_GROUP + j) * SC_LANES, SC_LANES)]
                    lo = lax.bitcast_convert_type(x << 16, F32)
                    hi = lax.bitcast_convert_type(x & jnp.int32(-65536), F32)
                    out += [sums[2 * j] + wv * lo, sums[2 * j + 1] + wv * hi]
                return tuple(out)
            zeros = tuple(jnp.zeros((SC_LANES,), F32) for _ in range(2 * SC_GROUP))
            sums = lax.fori_loop(0, win, row, zeros)
            for j in range(SC_GROUP):
                chunk, lane = divmod((g * SC_GROUP + j) * SC_LANES, LANES)
                plsc.addupdate(acc.at[pl.ds(2 * chunk * LANES + lane, SC_LANES)], sums[2 * j])
                plsc.addupdate(acc.at[pl.ds((2 * chunk + 1) * LANES + lane, SC_LANES)], sums[2 * j + 1])

    def write_out(acc, tok, sem):
        return pltpu.make_async_copy(acc, out_hbm.at[tok], sem)

    @pl.loop(0, n_tok // SC_BATCH)
    def _(bi):
        tok0 = base_tok + bi * SC_BATCH
        pltpu.sync_copy(idx_hbm.at[pl.ds(tok0 * slots, SC_BATCH * slots)], idx_v)
        pltpu.sync_copy(w_hbm.at[pl.ds(tok0 * slots, SC_BATCH * slots)], w_v)
        gather(0, 0, rows_a, row_sem.at[0]).start()

        @pl.loop(0, SC_BATCH // 2)
        def _(tp):
            for parity, acc in ((0, acc_a), (1, acc_b)):
                tl = 2 * tp + parity

                @pl.when(bi * SC_BATCH + tl >= 2)
                def _():
                    write_out(acc, tok0, out_sem.at[parity]).wait()
                for k in range(d // SC_LANES):
                    acc[pl.ds(k * SC_LANES, SC_LANES)] = jnp.zeros((SC_LANES,), F32)
                gather(tl, 1, rows_b, row_sem.at[1]).start()
                gather(tl, 0, rows_a, row_sem.at[0]).wait()
                accumulate(tl, 0, rows_a, acc)
                gather(jnp.minimum(tl + 1, SC_BATCH - 1), 0, rows_a, row_sem.at[0]).start()
                gather(tl, 1, rows_b, row_sem.at[1]).wait()
                accumulate(tl, 1, rows_b, acc)
                write_out(acc, tok0 + tl, out_sem.at[parity]).start()

        gather(SC_BATCH - 1, 0, rows_a, row_sem.at[0]).wait()

    write_out(acc_a, base_tok, out_sem.at[0]).wait()
    write_out(acc_b, base_tok, out_sem.at[1]).wait()


def _peer_values_sc(idx, w, table, n_tok, d):
    slots = idx.shape[0] // n_tok
    assert n_tok % (SC_CORES * SC_SUBCORES * SC_BATCH) == 0 and SC_BATCH % 2 == 0
    mesh = plsc.VectorSubcoreMesh(core_axis_name="c", subcore_axis_name="s",
                                  num_cores=SC_CORES, num_subcores=SC_SUBCORES)
    return pl.kernel(
        functools.partial(_sc_value_kernel, slots=slots),
        out_type=jax.ShapeDtypeStruct((n_tok, d), F32),
        mesh=mesh,
        scratch_types=[
            pltpu.VMEM((SC_BATCH * slots,), I32),
            pltpu.VMEM((SC_BATCH * slots,), F32),
            pltpu.VMEM((slots // 2, d // 2), I32),
            pltpu.VMEM((slots // 2, d // 2), I32),
            pltpu.VMEM((d,), F32),
            pltpu.VMEM((d,), F32),
            pltpu.SemaphoreType.DMA((2,)),
            pltpu.SemaphoreType.DMA((2,)),
        ],
        compiler_params=pltpu.CompilerParams(needs_layout_passes=False),
        name="peer_values_sc",
    )(table, idx, w)


def _sc_score_kernel(tab_hbm, idx_hbm, xe_hbm, xo_hbm, out_hbm, idx_v, xe_v, xo_v, rows_a, rows_b, part_v, a_v, row_sem,
                     *, slots):
    n_tok = out_hbm.shape[0] // slots // (SC_CORES * SC_SUBCORES)
    words = rows_a.shape[1]
    win = slots // 2
    n_groups = words // SC_LANES // SC_GROUP
    wid = lax.axis_index("s") * SC_CORES + lax.axis_index("c")
    base_tok = wid * n_tok
    lane_id = lax.iota(I32, SC_LANES)

    def gather(tl, half, buf, sem):
        rows = idx_v.at[pl.ds(tl * slots + half * win, win)]
        return pltpu.make_async_copy(tab_hbm.at[rows], buf, sem)

    def scores(tl, half, buf):
        for r in range(win):
            part_v[pl.ds(r * SC_LANES, SC_LANES)] = jnp.zeros((SC_LANES,), F32)
        for g in range(n_groups):
            xs = []
            for j in range(SC_GROUP):
                at = pl.ds(tl * words + (g * SC_GROUP + j) * SC_LANES, SC_LANES)
                xs += [xe_v[at], xo_v[at]]

            @pl.loop(0, win)
            def _(r, g=g, xs=xs):
                sums = [jnp.zeros((SC_LANES,), F32) for _ in range(4)]
                for j in range(SC_GROUP):
                    x = buf[r, pl.ds((g * SC_GROUP + j) * SC_LANES, SC_LANES)]
                    lo = lax.bitcast_convert_type(x << 16, F32)
                    hi = lax.bitcast_convert_type(x & jnp.int32(-65536), F32)
                    sums[(2 * j) % 4] += lo * xs[2 * j]
                    sums[(2 * j + 1) % 4] += hi * xs[2 * j + 1]
                plsc.addupdate(part_v.at[pl.ds(r * SC_LANES, SC_LANES)], (sums[0] + sums[1]) + (sums[2] + sums[3]))
        for rb in range(win // SC_LANES):
            first = (lane_id + rb * SC_LANES) * SC_LANES
            total = jnp.zeros((SC_LANES,), F32)
            for lane in range(SC_LANES):
                total += plsc.load_gather(part_v, [first + lane])
            a_v[pl.ds(tl * slots + half * win + rb * SC_LANES, SC_LANES)] = total

    @pl.loop(0, n_tok // SC_BATCH)
    def _(bi):
        tok0 = base_tok + bi * SC_BATCH
        pltpu.sync_copy(idx_hbm.at[pl.ds(tok0 * slots, SC_BATCH * slots)], idx_v)
        pltpu.sync_copy(xe_hbm.at[pl.ds(tok0 * words, SC_BATCH * words)], xe_v)
        pltpu.sync_copy(xo_hbm.at[pl.ds(tok0 * words, SC_BATCH * words)], xo_v)
        gather(0, 0, rows_a, row_sem.at[0]).start()

        @pl.loop(0, SC_BATCH)
        def _(tl):
            gather(tl, 1, rows_b, row_sem.at[1]).start()
            gather(tl, 0, rows_a, row_sem.at[0]).wait()
            scores(tl, 0, rows_a)
            gather(jnp.minimum(tl + 1, SC_BATCH - 1), 0, rows_a, row_sem.at[0]).start()
            gather(tl, 1, rows_b, row_sem.at[1]).wait()
            scores(tl, 1, rows_b)

        gather(SC_BATCH - 1, 0, rows_a, row_sem.at[0]).wait()
        pltpu.sync_copy(a_v, out_hbm.at[pl.ds(tok0 * slots, SC_BATCH * slots)])


def _peer_scores_sc(idx, xe, xo, table, n_tok, slots):
    words = table.shape[1]
    assert n_tok % (SC_CORES * SC_SUBCORES * SC_BATCH) == 0
    mesh = plsc.VectorSubcoreMesh(core_axis_name="c", subcore_axis_name="s",
                                  num_cores=SC_CORES, num_subcores=SC_SUBCORES)
    return pl.kernel(
        functools.partial(_sc_score_kernel, slots=slots),
        out_type=jax.ShapeDtypeStruct((n_tok * slots,), F32),
        mesh=mesh,
        scratch_types=[
            pltpu.VMEM((SC_BATCH * slots,), I32),
            pltpu.VMEM((SC_BATCH * words,), F32),
            pltpu.VMEM((SC_BATCH * words,), F32),
            pltpu.VMEM((slots // 2, words), I32),
            pltpu.VMEM((slots // 2, words), I32),
            pltpu.VMEM((slots // 2 * SC_LANES,), F32),
            pltpu.VMEM((SC_BATCH * slots,), F32),
            pltpu.SemaphoreType.DMA((2,)),
        ],
        compiler_params=pltpu.CompilerParams(needs_layout_passes=False),
        name="peer_scores_sc",
    )(table, idx, xe, xo)


def _split_norm_kernel(x_ref, g_ref, xe_ref, xo_ref):
    d = x_ref.shape[1]
    xn = _rms(x_ref[...], g_ref[...])
    nchunk = d // (2 * LANES)
    xe_ref[...] = jnp.concatenate([xn[:, (2 * c) * LANES:(2 * c + 1) * LANES] for c in range(nchunk)], axis=1)
    xo_ref[...] = jnp.concatenate([xn[:, (2 * c + 1) * LANES:(2 * c + 2) * LANES] for c in range(nchunk)], axis=1)


def _split_norm(x2d, g, n_rows, tm):
    d = x2d.shape[1]
    first = (x2d.shape[0] - n_rows) // tm
    half = jax.ShapeDtypeStruct((n_rows, d // 2), F32)
    return pl.pallas_call(
        _split_norm_kernel,
        grid=(n_rows // tm,),
        in_specs=[pl.BlockSpec((tm, d), lambda i: (first + i, 0)), pl.BlockSpec((1, d), lambda i: (0, 0))],
        out_specs=[pl.BlockSpec((tm, d // 2), lambda i: (i, 0))] * 2,
        out_shape=[half, half],
        compiler_params=_cparams(("arbitrary",)),
        name="peer_split_norm",
    )(x2d, g.reshape(1, d))


def _score_finish_kernel(w_hbm, a_ref, gate_ref, w_ref):
    del w_hbm
    a = a_ref[0]
    gelu = 0.5 * a * (1.0 + lax.erf(a * (2.0 ** -0.5)))
    w_ref[0] = gate_ref[0].T * gelu


def _score_finish(w_full, a_sc, gate_t):
    nt, t, slots = w_full.shape
    n_sc = a_sc.shape[0]
    first = nt - n_sc
    return pl.pallas_call(
        _score_finish_kernel,
        grid=(n_sc,),
        in_specs=[
            pl.BlockSpec(memory_space=pl.ANY),
            pl.BlockSpec((1, t, slots), lambda i: (i, 0, 0)),
            pl.BlockSpec((1, slots, t), lambda i: (first + i, 0, 0)),
        ],
        out_specs=pl.BlockSpec((1, t, slots), lambda i: (first + i, 0, 0)),
        out_shape=jax.ShapeDtypeStruct(w_full.shape, F32),
        input_output_aliases={0: 0},
        compiler_params=_cparams(("arbitrary",)),
        name="peer_score_finish",
    )(w_full, a_sc, gate_t)


def _residual_norm_kernel(out_hbm, x_ref, p_ref, gf_ref, o_ref, *, final_norm):
    del out_hbm
    y = x_ref[...] + p_ref[...]
    if final_norm:
        y = _rms(y, gf_ref[...])
    o_ref[...] = y


def _residual_norm(out_full, x2d, p2d, g_final, final_norm, tm):
    t, d = p2d.shape
    first = (x2d.shape[0] - t) // tm
    return pl.pallas_call(
        functools.partial(_residual_norm_kernel, final_norm=final_norm),
        grid=(t // tm,),
        in_specs=[
            pl.BlockSpec(memory_space=pl.ANY),
            pl.BlockSpec((tm, d), lambda i: (first + i, 0)),
            pl.BlockSpec((tm, d), lambda i: (i, 0)),
            pl.BlockSpec((1, d), lambda i: (0, 0)),
        ],
        out_specs=pl.BlockSpec((tm, d), lambda i: (first + i, 0)),
        out_shape=jax.ShapeDtypeStruct(x2d.shape, F32),
        input_output_aliases={0: 0},
        compiler_params=_cparams(("arbitrary",)),
        name="peer_residual",
    )(out_full, x2d, p2d, g_final.reshape(1, d))


def _peer_ffn(x, g, w_q, k1, k2, u_emb, v_emb, g_final, final_norm):
    b, s, d = x.shape
    x2d = x.reshape(b * s, d)
    e_slot, gate_slot, e_tok = _peer_router(x2d, g, w_q, k1, k2)
    (u_tab, u_tab_sc), (v_tab, v_tab_sc) = _pack_pairs(u_emb), _pack_pairs(v_emb)
    nt, slots, t = e_slot.shape
    sc_quantum = SC_CORES * SC_SUBCORES * SC_BATCH // t
    ns_score = nt * SC_SCORE_PERCENT // 100 // sc_quantum * sc_quantum
    ns_value = nt * SC_VALUE_PERCENT // 100 // sc_quantum * sc_quantum
    ns_max = max(ns_score, ns_value)
    idx_sc = e_tok[nt - ns_max:].reshape(-1)

    w_tok = _peer_scores(e_slot, gate_slot, x2d, g, u_tab, nt - ns_score)
    if ns_score:
        xe, xo = _split_norm(x2d, g, ns_score * t, math.gcd(TOKEN_TILE, ns_score * t))
        a_sc = _peer_scores_sc(idx_sc[(ns_max - ns_score) * t * slots:], xe.reshape(-1), xo.reshape(-1),
                               u_tab_sc, ns_score * t, slots)
        w_tok = _score_finish(w_tok, a_sc.reshape(ns_score, t, slots), gate_slot)
    out = _peer_values(e_slot, w_tok, x2d, v_tab, g_final, final_norm, nt - ns_value)
    if ns_value:
        peer_sc = _peer_values_sc(idx_sc[(ns_max - ns_value) * t * slots:], w_tok[nt - ns_value:].reshape(-1),
                                  v_tab_sc, ns_value * t, d)
        out = _residual_norm(out, x2d, peer_sc, g_final, final_norm, math.gcd(TOKEN_TILE, ns_value * t))
    return out.reshape(b, s, d)


def _qkv_kernel(x_ref, g_ref, wqk_hi_ref, wqk_lo_ref, wv_ref, cos_ref, sin_ref,
                qt_ref, k0_ref, k1_ref, vt_ref, km_ref):
    tm, d = x_ref.shape[1], x_ref.shape[2]
    hn = _rms(x_ref[0], g_ref[...])
    hh, hl = _split(hn)
    qk = _dot3(hh, hl, wqk_hi_ref[...], wqk_lo_ref[...])
    v = _dot(hh, wv_ref[...])
    cos = jnp.concatenate([cos_ref[...]] * (d // LANES), axis=1)
    sin = jnp.concatenate([sin_ref[...]] * (d // LANES), axis=1)
    lane = lax.broadcasted_iota(I32, (tm, d), 1)
    first_half = (lane % HEAD_DIM) < (HEAD_DIM // 2)

    def rope(a):
        rot = jnp.where(first_half, pltpu.roll(a, d - HEAD_DIM // 2, 1), pltpu.roll(a, HEAD_DIM // 2, 1))
        return a * cos + rot * sin

    q = rope(qk[:, :d])
    k = rope(qk[:, d:])
    qt_ref[0] = q.T
    vt_ref[0] = v.T.astype(BF16)
    nb = tm // MOBA_BLOCK
    km_ref[0, 0] = jnp.mean(k.reshape(nb, MOBA_BLOCK, d), axis=1)
    row = lax.broadcasted_iota(I32, (tm, d), 0)
    block = (pl.program_id(1) * tm + row) // MOBA_BLOCK
    pair_lane = lane % LANES
    kb = k.astype(BF16)
    k0_ref[0] = jnp.where(pair_lane < HEAD_DIM, kb, jnp.where(pair_lane - HEAD_DIM == block, 1.0, 0.0).astype(BF16))
    k1_ref[0] = jnp.where(pair_lane >= HEAD_DIM, kb, jnp.where(pair_lane == block, 1.0, 0.0).astype(BF16))


def _qkv_rope(x, g, w_qkv, tm):
    b, s, d = x.shape
    half = HEAD_DIM // 2
    inv = ROPE_THETA ** (-jnp.arange(half, dtype=F32) / half)
    ang = jnp.arange(s).astype(F32)[:, None] * inv[None, :]
    cos, sin = jnp.cos(ang), jnp.sin(ang)
    cos128 = jnp.tile(jnp.concatenate([cos, cos], axis=1), (1, LANES // HEAD_DIM))
    sin128 = jnp.tile(jnp.concatenate([-sin, sin], axis=1), (1, LANES // HEAD_DIM))
    wqk_hi, wqk_lo = _split(w_qkv[:, :2 * d])
    wv = w_qkv[:, 2 * d:].astype(BF16)
    nb = tm // MOBA_BLOCK
    const = lambda i, j: (0, 0)
    qt, k0, k1, vt, km = pl.pallas_call(
        _qkv_kernel,
        grid=(b, s // tm),
        in_specs=[
            pl.BlockSpec((1, tm, d), lambda i, j: (i, j, 0)),
            pl.BlockSpec((1, d), const),
            pl.BlockSpec((d, 2 * d), const),
            pl.BlockSpec((d, 2 * d), const),
            pl.BlockSpec((d, d), const),
            pl.BlockSpec((tm, LANES), lambda i, j: (j, 0)),
            pl.BlockSpec((tm, LANES), lambda i, j: (j, 0)),
        ],
        out_specs=[
            pl.BlockSpec((1, d, tm), lambda i, j: (i, 0, j)),
            pl.BlockSpec((1, tm, d), lambda i, j: (i, j, 0)),
            pl.BlockSpec((1, tm, d), lambda i, j: (i, j, 0)),
            pl.BlockSpec((1, d, tm), lambda i, j: (i, 0, j)),
            pl.BlockSpec((1, 1, nb, d), lambda i, j: (i, j, 0, 0)),
        ],
        out_shape=[
            jax.ShapeDtypeStruct((b, d, s), F32),
            jax.ShapeDtypeStruct((b, s, d), BF16),
            jax.ShapeDtypeStruct((b, s, d), BF16),
            jax.ShapeDtypeStruct((b, d, s), BF16),
            jax.ShapeDtypeStruct((b, s // tm, nb, d), F32),
        ],
        compiler_params=_cparams(("arbitrary", "arbitrary")),
        name="qkv_rope",
    )(x, g.reshape(1, d), wqk_hi, wqk_lo, wv, cos128, sin128)
    return qt, (k0, k1), vt, km.reshape(b, s // MOBA_BLOCK, d)


def _moba_kernel(qt_ref, k0_ref, k1_ref, vt_ref, km_ref, o_ref, sa_ref, sb_ref):
    bs = MOBA_BLOCK
    nb = km_ref.shape[1]
    n_heads = LANES // HEAD_DIM
    k_refs = (k0_ref, k1_ref)
    j = pl.program_id(2)
    qt = qt_ref[0]
    km = km_ref[0]
    scale = HEAD_DIM ** -0.5 * 1.4426950408889634
    lane_km = lax.broadcasted_iota(I32, (nb, LANES), 1)
    blk = lax.broadcasted_iota(I32, (nb, bs), 0)
    zeros_pad = jnp.zeros((LANES - HEAD_DIM - nb, bs), F32)
    qh, ql = _split(qt)

    own = pl.ds(pl.multiple_of(j * bs, bs), bs)
    v_own = vt_ref[0, :, own]
    krow = lax.broadcasted_iota(I32, (bs, bs), 0)
    qcol = lax.broadcasted_iota(I32, (bs, bs), 1)

    q_augs, state = [], []
    for hh in range(n_heads):
        head_lo = hh * HEAD_DIM
        in_head_km = (lane_km >= head_lo) & (lane_km < head_lo + HEAD_DIM)
        kmh, kml = _split(jnp.where(in_head_km, km, 0.0))
        gate = _dot3(kmh, kml, qh, ql)
        valid = blk < j
        gate = jnp.where(valid, gate, -jnp.inf)
        sel = jnp.zeros((nb, bs), F32)
        for _ in range(MOBA_TOPK):
            m = jnp.max(gate, axis=0, keepdims=True)
            i = jnp.min(jnp.where(gate == m, blk, nb), axis=0, keepdims=True)
            pick = blk == i
            sel = jnp.where(pick, 1.0, sel)
            gate = jnp.where(pick, -jnp.inf, gate)
        bias_t = jnp.where((sel > 0.0) & valid, 0.0, MASK_NEG)
        q_head = qt[head_lo:head_lo + HEAD_DIM, :] * scale
        no_bias = jnp.zeros((LANES - HEAD_DIM, bs), F32)
        if hh == 0:
            q_aug = jnp.concatenate([q_head, bias_t, zeros_pad], axis=0)
            q_own = jnp.concatenate([q_head, no_bias], axis=0)
        else:
            q_aug = jnp.concatenate([bias_t, zeros_pad, q_head], axis=0)
            q_own = jnp.concatenate([no_bias, q_head], axis=0)
        q_augs.append(q_aug.astype(BF16))

        s_own = jnp.where(krow <= qcol, _dot(k_refs[hh][0, own, :], q_own.astype(BF16)), -1e30)
        m0 = jnp.max(s_own, axis=0, keepdims=True)
        p0 = jnp.exp2(s_own - m0)
        state += [m0, jnp.sum(p0, axis=0, keepdims=True), _dot(v_own, p0.astype(BF16))]

    chunk = KV_CHUNK * bs
    last_chunk = nb // KV_CHUNK - 1

    def score_chunk(c, s_ref):
        rows = pl.ds(pl.multiple_of(jnp.minimum(c, last_chunk) * chunk, chunk), chunk)
        for hh in range(n_heads):
            s_ref[hh] = _dot(k_refs[hh][0, rows, :], q_augs[hh])

    def attend(c, s_ref, state):
        vn = vt_ref[0, :, pl.ds(pl.multiple_of(c * chunk, chunk), chunk)]
        new_state = []
        for hh in range(n_heads):
            m, l, acc = state[3 * hh:3 * hh + 3]
            s = s_ref[hh]
            m_new = jnp.maximum(m, jnp.max(s, axis=0, keepdims=True))
            alpha = jnp.exp2(m - m_new)
            p = jnp.exp2(s - m_new)
            l = alpha * l + jnp.sum(p, axis=0, keepdims=True)
            acc = alpha * acc + _dot(vn, p.astype(BF16))
            new_state += [m_new, l, acc]
        return tuple(new_state)

    score_chunk(0, sa_ref)

    def body(i, state):
        score_chunk(2 * i + 1, sb_ref)
        state = attend(2 * i, sa_ref, state)
        score_chunk(2 * i + 2, sa_ref)
        return attend(2 * i + 1, sb_ref, state)

    state = lax.fori_loop(0, (j + 2 * KV_CHUNK - 1) // (2 * KV_CHUNK), body, tuple(state))
    halves = []
    for hh in range(n_heads):
        _, l, acc = state[3 * hh:3 * hh + 3]
        halves.append((acc / l)[hh * HEAD_DIM:(hh + 1) * HEAD_DIM, :])
    o_ref[0] = jnp.concatenate(halves, axis=0).T


def _moba_attention(qt, k01, vt, km):
    b, d, s = qt.shape
    nb = s // MOBA_BLOCK
    assert nb % (2 * KV_CHUNK) == 0
    return pl.pallas_call(
        _moba_kernel,
        grid=(b, d // LANES, nb),
        in_specs=[
            pl.BlockSpec((1, LANES, MOBA_BLOCK), lambda i, h, j: (i, h, j)),
            pl.BlockSpec((1, s, LANES), lambda i, h, j: (i, 0, h)),
            pl.BlockSpec((1, s, LANES), lambda i, h, j: (i, 0, h)),
            pl.BlockSpec((1, LANES, s), lambda i, h, j: (i, h, 0)),
            pl.BlockSpec((1, nb, LANES), lambda i, h, j: (i, 0, h)),
        ],
        out_specs=pl.BlockSpec((1, MOBA_BLOCK, LANES), lambda i, h, j: (i, j, h)),
        out_shape=jax.ShapeDtypeStruct((b, s, d), F32),
        scratch_shapes=[pltpu.VMEM((LANES // HEAD_DIM, KV_CHUNK * MOBA_BLOCK, MOBA_BLOCK), F32)] * 2,
        compiler_params=_cparams(("arbitrary", "arbitrary", "arbitrary")),
        name="moba_attention",
    )(qt, *k01, vt, km)


def _proj_residual_kernel(x_ref, a_ref, w_ref, o_ref):
    o_ref[...] = x_ref[...] + _dot(a_ref[...].astype(BF16), w_ref[...])


def _proj_residual(x2d, a2d, w, tm):
    t, d = x2d.shape
    return pl.pallas_call(
        _proj_residual_kernel,
        grid=(t // tm,),
        in_specs=[
            pl.BlockSpec((tm, d), lambda i: (i, 0)),
            pl.BlockSpec((tm, a2d.shape[1]), lambda i: (i, 0)),
            pl.BlockSpec(w.shape, lambda i: (0, 0)),
        ],
        out_specs=pl.BlockSpec((tm, d), lambda i: (i, 0)),
        out_shape=jax.ShapeDtypeStruct((t, d), F32),
        compiler_params=_cparams(("arbitrary",)),
        name="attn_out_proj",
    )(x2d, a2d, w.astype(BF16))


def _moba_mixer(x, g, w_qkv, w_o):
    b, s, d = x.shape
    tm = min(TOKEN_TILE, s)
    qt, k, vt, km = _qkv_rope(x, g, w_qkv, tm)
    attn = _moba_attention(qt, k, vt, km)
    return _proj_residual(x.reshape(b * s, d), attn.reshape(b * s, d), w_o, tm).reshape(b, s, d)


def kernel(x, norm_mix, norm_ffn, conv_w_in, conv_w, conv_w_out, attn_w_qkv, attn_w_o,
           peer_w_q, peer_k1, peer_k2, peer_u, peer_v, norm_final):
    depth = norm_mix.shape[0]
    tm = min(TOKEN_TILE, x.shape[1])
    for i in range(depth):
        j = i // 2
        if i % 2 == 0:
            x = _conv_mixer(x, norm_mix[i], conv_w_in[j], conv_w[j], conv_w_out[j], tm)
        else:
            x = _moba_mixer(x, norm_mix[i], attn_w_qkv[j], attn_w_o[j])
        x = _peer_ffn(x, norm_ffn[i], peer_w_q[i], peer_k1[i], peer_k2[i], peer_u[i], peer_v[i],
                      norm_final, final_norm=(i == depth - 1))
    return x
```

```python
import functools
import math

import jax
import jax.numpy as jnp
from jax import lax
from jax.experimental import pallas as pl
from jax.experimental.pallas import tpu as pltpu
from jax.experimental.pallas import tpu_sc as plsc

F32 = jnp.float32
BF16 = jnp.bfloat16
I32 = jnp.int32

RMS_EPS = 1e-6
N_HEADS = 16
HEAD_DIM = 64
MOBA_BLOCK = 256
MOBA_TOPK = 3
ROPE_THETA = 10000.0
PEER_HEADS = 8
PEER_NKEYS = 128
PEER_HALF = 64
PEER_TOPK = 16

LANES = 128
SUBLANES = 8
VMEM_LIMIT = 56 * 1024 * 1024
MASK_NEG = -1e9

TOKEN_TILE = 512
PEER_TILE = 128
PACK_TILE = 512
OCTET = 8
SC_CORES = 2
SC_SUBCORES = 16
SC_LANES = 16
SC_BATCH = 16
SC_GROUP = 8
SC_SCORE_PERCENT = 35
SC_VALUE_PERCENT = 47
KV_CHUNK = 2


def _cparams(sem):
    return pltpu.CompilerParams(dimension_semantics=sem, vmem_limit_bytes=VMEM_LIMIT)


def _rms(x, g):
    ms = jnp.mean(x * x, axis=-1, keepdims=True)
    return x * lax.rsqrt(ms + RMS_EPS) * g


def _split(a):
    hi = a.astype(BF16)
    lo = (a - hi.astype(F32)).astype(BF16)
    return hi, lo


def _dot(a, b):
    return lax.dot_general(a, b, (((1,), (0,)), ((), ())), preferred_element_type=F32)


def _dot_t(a, b):
    return lax.dot_general(a, b, (((1,), (1,)), ((), ())), preferred_element_type=F32)


def _dot3(a_hi, a_lo, b_hi, b_lo):
    return _dot(a_hi, b_hi) + _dot(a_lo, b_hi) + _dot(a_hi, b_lo)


def _dot3_t(a_hi, a_lo, b_hi, b_lo):
    return _dot_t(a_hi, b_hi) + _dot_t(a_lo, b_hi) + _dot_t(a_hi, b_lo)


def _conv_mixer_kernel(x_ref, g_ref, win_ref, cw_ref, wout_ref, o_ref, ubuf_ref):
    tm, d = x_ref.shape[1], x_ref.shape[2]

    @pl.when(pl.program_id(1) == 0)
    def _():
        ubuf_ref[0:SUBLANES, :] = jnp.zeros((SUBLANES, d), F32)

    x = x_ref[0]
    hn = _rms(x, g_ref[...]).astype(BF16)
    bcz = _dot(hn, win_ref[...])
    b_gate, c_gate, z = bcz[:, :d], bcz[:, d:2 * d], bcz[:, 2 * d:]
    u = c_gate * z
    ubuf_ref[SUBLANES:SUBLANES + tm, :] = u
    u1 = ubuf_ref[SUBLANES - 1:SUBLANES - 1 + tm, :]
    u2 = ubuf_ref[SUBLANES - 2:SUBLANES - 2 + tm, :]
    cw = cw_ref[...]
    u_conv = cw[0:1, :] * u2 + cw[1:2, :] * u1 + cw[2:3, :] * u
    ubuf_ref[0:SUBLANES, :] = u[tm - SUBLANES:tm, :]
    y = (b_gate * u_conv).astype(BF16)
    o_ref[0] = x + _dot(y, wout_ref[...])


def _conv_mixer(x, g, w_in, conv_w, w_out, tm):
    b, s, d = x.shape
    return pl.pallas_call(
        _conv_mixer_kernel,
        grid=(b, s // tm),
        in_specs=[
            pl.BlockSpec((1, tm, d), lambda i, j: (i, j, 0)),
            pl.BlockSpec((1, d), lambda i, j: (0, 0)),
            pl.BlockSpec((d, 3 * d), lambda i, j: (0, 0)),
            pl.BlockSpec((3, d), lambda i, j: (0, 0)),
            pl.BlockSpec((d, d), lambda i, j: (0, 0)),
        ],
        out_specs=pl.BlockSpec((1, tm, d), lambda i, j: (i, j, 0)),
        out_shape=jax.ShapeDtypeStruct((b, s, d), F32),
        scratch_shapes=[pltpu.VMEM((tm + SUBLANES, d), F32)],
        compiler_params=_cparams(("arbitrary", "arbitrary")),
        name="conv_mixer",
    )(x, g.reshape(1, d), w_in.astype(BF16), conv_w, w_out.astype(BF16))


def _topk_rows(s, k, order=None, payload=None):
    if order is None:
        order = lax.broadcasted_iota(I32, s.shape, 0)
    big = jnp.iinfo(jnp.int32).max
    vals, outs = [], []
    for _ in range(k):
        m = jnp.max(s, axis=0, keepdims=True)
        i = jnp.min(jnp.where(s == m, order, big), axis=0, keepdims=True)
        pick = order == i
        vals.append(m)
        if payload is None:
            outs.append(i)
        else:
            outs.append(jnp.max(jnp.where(pick, payload, -1), axis=0, keepdims=True))
        s = jnp.where(pick, -jnp.inf, s)
    return jnp.concatenate(vals, axis=0), jnp.concatenate(outs, axis=0)


def _staircase(kk):
    groups = []
    for a in range(2):
        for b0 in range(0, kk // (a + 1), SUBLANES):
            groups.append((a, 0, b0, 1, lambda j, a=a, b0=b0: (a + 1) * (b0 + j + 1) <= kk))
    for b in range(kk // 3):
        for a0 in range(0, kk // (b + 1), SUBLANES):
            groups.append((a0, 1, b, 0, lambda j, a0=a0, b=b: (a0 + j >= 2) & ((a0 + j + 1) * (b + 1) <= kk)))
    return groups


def _staircase_topk(v1, i1, v2, i2, nk):
    kk, t = v1.shape
    j = lax.broadcasted_iota(I32, (SUBLANES, t), 0)

    def rows(x, x0, step):
        if step == 0:
            return jnp.broadcast_to(x[x0:x0 + 1, :], (SUBLANES, t))
        return x[x0:x0 + SUBLANES, :]

    cand, order, cidx = [], [], []
    for a0, a_step, b0, b_step, valid in _staircase(kk):
        ok = valid(j)
        cand.append(jnp.where(ok, rows(v1, a0, a_step) + rows(v2, b0, b_step), -jnp.inf))
        order.append(jnp.where(ok, (a0 + j * a_step) * kk + (b0 + j * b_step), jnp.iinfo(jnp.int32).max - 1))
        cidx.append(rows(i1, a0, a_step) * nk + rows(i2, b0, b_step))
    cat = lambda xs: jnp.concatenate(xs, axis=0)
    return _topk_rows(cat(cand), kk, order=cat(order), payload=cat(cidx))


def _router_kernel(x_ref, g_ref, wq_hi_ref, wq_lo_ref, k1_hi_ref, k1_lo_ref, k2_hi_ref, k2_lo_ref,
                   e_ref, gate_ref, e_tok_ref):
    table_rows = x_ref.shape[1] // (2 * LANES)
    ids = []
    xn = _rms(x_ref[...], g_ref[...])
    xh, xl = _split(xn)
    q = _dot3(xh, xl, wq_hi_ref[...], wq_lo_ref[...])
    nk = k1_hi_ref.shape[0]
    for h in range(PEER_HEADS):
        qh, ql = _split(q[:, h * LANES:(h + 1) * LANES])
        s1 = _dot3_t(k1_hi_ref[...], k1_lo_ref[...], qh, ql)
        s2 = _dot3_t(k2_hi_ref[...], k2_lo_ref[...], qh, ql)
        v1, i1 = _topk_rows(s1, PEER_TOPK)
        v2, i2 = _topk_rows(s2, PEER_TOPK)
        kk = PEER_TOPK
        top_s, e_idx = _staircase_topk(v1, i1, v2, i2, nk)
        p = jnp.exp(top_s - top_s[0:1, :])
        gate = p / jnp.sum(p, axis=0, keepdims=True)
        e_ref[0, h * kk:(h + 1) * kk, :] = e_idx * table_rows
        gate_ref[0, h * kk:(h + 1) * kk, :] = gate
        ids.append(e_idx)
    e_tok_ref[0] = jnp.concatenate(ids, axis=0).T


def _pad_keys(k, lo):
    nk, half = k.shape
    out = jnp.zeros((nk, LANES), F32)
    return out.at[:, lo:lo + half].set(k)


def _peer_router(x2d, g, w_q, k1, k2):
    t, d = x2d.shape
    nt = t // PEER_TILE
    slots = PEER_HEADS * PEER_TOPK
    wq_hi, wq_lo = _split(w_q)
    k1_hi, k1_lo = _split(_pad_keys(k1, 0))
    k2_hi, k2_lo = _split(_pad_keys(k2, PEER_HALF))
    const = lambda i: (0, 0)
    kspec = pl.BlockSpec(k1_hi.shape, const)
    return pl.pallas_call(
        _router_kernel,
        grid=(nt,),
        in_specs=[
            pl.BlockSpec((PEER_TILE, d), lambda i: (i, 0)),
            pl.BlockSpec((1, d), const),
            pl.BlockSpec(wq_hi.shape, const),
            pl.BlockSpec(wq_lo.shape, const),
            kspec, kspec, kspec, kspec,
        ],
        out_specs=[
            pl.BlockSpec((1, slots, PEER_TILE), lambda i: (i, 0, 0)),
            pl.BlockSpec((1, slots, PEER_TILE), lambda i: (i, 0, 0)),
            pl.BlockSpec((1, PEER_TILE, slots), lambda i: (i, 0, 0)),
        ],
        out_shape=[
            jax.ShapeDtypeStruct((nt, slots, PEER_TILE), I32),
            jax.ShapeDtypeStruct((nt, slots, PEER_TILE), F32),
            jax.ShapeDtypeStruct((nt, PEER_TILE, slots), I32),
        ],
        compiler_params=_cparams(("arbitrary",)),
        name="peer_router",
    )(x2d, g.reshape(1, d), wq_hi, wq_lo, k1_hi, k1_lo, k2_hi, k2_lo)


def _gather_octet(e_ref, tab_ref, stage_ref, octet, nsub):
    t = e_ref.shape[2]
    for k in range(OCTET):
        for tok in range(t):
            off = e_ref[0, octet * OCTET + k, tok]
            stage_ref[k * (t // SUBLANES) + tok // SUBLANES,
                      pl.ds(tok % SUBLANES, nsub, stride=SUBLANES), :] = tab_ref[pl.ds(off, nsub), :]


def _octet_pipeline(n_octets, gather, consume, stage_a, stage_b):
    gather(0, stage_a)

    def pair(p, carry):
        gather(2 * p + 1, stage_b)
        consume(2 * p, stage_a)
        gather(jnp.minimum(2 * p + 2, n_octets - 1), stage_a)
        consume(2 * p + 1, stage_b)
        return carry
    lax.fori_loop(0, n_octets // 2, pair, 0)


def _score_kernel(e_ref, x_ref, g_ref, gate_ref, tab_ref, w_ref, stage_a, stage_b, a_ref):
    t, d = x_ref.shape
    nchunk = d // (2 * LANES)
    slots = e_ref.shape[1]
    rows = OCTET * t

    xn = _rms(x_ref[...], g_ref[...]).astype(BF16)
    x_even = jnp.concatenate([xn[:, (2 * c) * LANES:(2 * c + 1) * LANES] for c in range(nchunk)], axis=1)
    x_odd = jnp.concatenate([xn[:, (2 * c + 1) * LANES:(2 * c + 2) * LANES] for c in range(nchunk)], axis=1)
    rhs = jnp.concatenate([x_even, x_odd], axis=0)

    ri = lax.broadcasted_iota(I32, (2 * t, 2 * t), 0)
    ci = lax.broadcasted_iota(I32, (2 * t, 2 * t), 1)
    diag = (ci == (ri % 2) * t + ri // 2).astype(F32)

    def consume(o, stage_ref):
        planes = []
        for c in range(nchunk):
            plane = stage_ref[:, c * SUBLANES:(c + 1) * SUBLANES, :].reshape(rows, LANES)
            planes.append(pltpu.bitcast(plane, BF16))
        lhs = jnp.concatenate(planes, axis=1)
        prod = _dot_t(lhs, rhs)
        prod = prod.reshape(OCTET, 2 * t, 2 * t) * diag[None]
        a2 = jnp.sum(prod, axis=1)
        a_ref[pl.ds(pl.multiple_of(o * OCTET, OCTET), OCTET), :] = a2[:, :t] + a2[:, t:]

    gather = lambda o, stage_ref: _gather_octet(e_ref, tab_ref, stage_ref, o, nchunk)
    _octet_pipeline(slots // OCTET, gather, consume, stage_a, stage_b)

    a = a_ref[...]
    gelu = 0.5 * a * (1.0 + lax.erf(a * (2.0 ** -0.5)))
    w_ref[0] = (gate_ref[0] * gelu).T


def _peer_scores(e_t, gate_t, x2d, g, table, n_tiles):
    nt, slots, t = e_t.shape
    d = x2d.shape[1]
    stage = pltpu.VMEM((OCTET * t // SUBLANES, d // (2 * LANES) * SUBLANES, LANES), I32)
    return pl.pallas_call(
        _score_kernel,
        grid=(n_tiles,),
        in_specs=[
            pl.BlockSpec((1, slots, t), lambda i: (i, 0, 0), memory_space=pltpu.SMEM),
            pl.BlockSpec((t, d), lambda i: (i, 0)),
            pl.BlockSpec((1, d), lambda i: (0, 0)),
            pl.BlockSpec((1, slots, t), lambda i: (i, 0, 0)),
            pl.BlockSpec(table.shape, lambda i: (0, 0), pipeline_mode=pl.Buffered(1)),
        ],
        out_specs=pl.BlockSpec((1, t, slots), lambda i: (i, 0, 0)),
        out_shape=jax.ShapeDtypeStruct((nt, t, slots), F32),
        scratch_shapes=[stage, stage, pltpu.VMEM((slots, t), F32)],
        compiler_params=_cparams(("arbitrary",)),
        name="peer_scores",
    )(e_t, x2d, g.reshape(1, d), gate_t, table)


def _value_kernel(e_ref, w_ref, x_ref, gf_ref, tab_ref, o_ref, stage_a, stage_b, wb_ref, acc_ref, *, final_norm):
    t, d = x_ref.shape
    nchunk = d // (2 * LANES)
    slots = e_ref.shape[1]
    groups = t // SUBLANES

    acc_ref[...] = x_ref[...]

    def consume(o, stage_ref):
        w_oct = pltpu.roll(w_ref[0], lax.rem(slots - o * OCTET, slots), 1)
        for k in range(OCTET):
            wb_ref[k] = jnp.broadcast_to(w_oct[:, k:k + 1], (t, LANES))
        for c in range(nchunk):
            lo_cols = slice(2 * c * LANES, (2 * c + 1) * LANES)
            hi_cols = slice((2 * c + 1) * LANES, (2 * c + 2) * LANES)
            acc_lo, acc_hi = acc_ref[:, lo_cols], acc_ref[:, hi_cols]
            for k in range(OCTET):
                words = stage_ref[k * groups:(k + 1) * groups, c * SUBLANES:(c + 1) * SUBLANES, :]
                words = words.reshape(t, LANES)
                lo = pltpu.bitcast(words << 16, F32)
                hi = pltpu.bitcast(words & jnp.int32(-65536), F32)
                acc_lo = acc_lo + wb_ref[k] * lo
                acc_hi = acc_hi + wb_ref[k] * hi
            acc_ref[:, lo_cols] = acc_lo
            acc_ref[:, hi_cols] = acc_hi

    gather = lambda o, stage_ref: _gather_octet(e_ref, tab_ref, stage_ref, o, nchunk)
    _octet_pipeline(slots // OCTET, gather, consume, stage_a, stage_b)

    y = acc_ref[...]
    if final_norm:
        y = _rms(y, gf_ref[...])
    o_ref[...] = y


def _peer_values(e_t, w_t, x2d, table, g_final, final_norm, n_tiles):
    nt, slots, t = e_t.shape
    d = x2d.shape[1]
    stage = pltpu.VMEM((OCTET * t // SUBLANES, d // (2 * LANES) * SUBLANES, LANES), I32)
    return pl.pallas_call(
        functools.partial(_value_kernel, final_norm=final_norm),
        grid=(n_tiles,),
        in_specs=[
            pl.BlockSpec((1, slots, t), lambda i: (i, 0, 0), memory_space=pltpu.SMEM),
            pl.BlockSpec((1, t, slots), lambda i: (i, 0, 0)),
            pl.BlockSpec((t, d), lambda i: (i, 0)),
            pl.BlockSpec((1, d), lambda i: (0, 0)),
            pl.BlockSpec(table.shape, lambda i: (0, 0), pipeline_mode=pl.Buffered(1)),
        ],
        out_specs=pl.BlockSpec((t, d), lambda i: (i, 0)),
        out_shape=jax.ShapeDtypeStruct(x2d.shape, F32),
        scratch_shapes=[stage, stage, pltpu.VMEM((OCTET, t, LANES), F32), pltpu.VMEM((t, d), F32)],
        compiler_params=_cparams(("arbitrary",)),
        name="peer_values",
    )(e_t, w_t, x2d, g_final.reshape(1, d), table)


def _pack_kernel(w_ref, tc_ref, sc_ref):
    te, d = w_ref.shape
    nchunk = d // (2 * LANES)
    words = []
    for c in range(nchunk):
        lo = w_ref[:, (2 * c) * LANES:(2 * c + 1) * LANES].astype(BF16).astype(F32)
        hi = w_ref[:, (2 * c + 1) * LANES:(2 * c + 2) * LANES].astype(BF16).astype(F32)
        word = (pltpu.bitcast(hi, I32) & jnp.int32(-65536)) | lax.shift_right_logical(pltpu.bitcast(lo, I32), 16)
        tc_ref[pl.ds(c, te, stride=nchunk), :] = word
        words.append(word)
    sc_ref[...] = jnp.concatenate(words, axis=1)


def _pack_pairs(w):
    e, d = w.shape
    te = math.gcd(e, PACK_TILE)
    nchunk = d // (2 * LANES)
    return pl.pallas_call(
        _pack_kernel,
        grid=(e // te,),
        in_specs=[pl.BlockSpec((te, d), lambda i: (i, 0))],
        out_specs=[pl.BlockSpec((te * nchunk, LANES), lambda i: (i, 0)), pl.BlockSpec((te, d // 2), lambda i: (i, 0))],
        out_shape=[jax.ShapeDtypeStruct((e * nchunk, LANES), I32), jax.ShapeDtypeStruct((e, d // 2), I32)],
        compiler_params=_cparams(("arbitrary",)),
        name="peer_pack_table",
    )(w)


def _sc_value_kernel(tab_hbm, idx_hbm, w_hbm, out_hbm, idx_v, w_v, rows_a, rows_b, acc_a, acc_b, row_sem, out_sem,
                     *, slots, first_tok):
    n_tok = out_hbm.shape[0] // (SC_CORES * SC_SUBCORES)
    d = out_hbm.shape[1]
    words = d // 2
    win = slots // 2
    wid = lax.axis_index("s") * SC_CORES + lax.axis_index("c")
    base_tok = wid * n_tok

    def gather(tl, half, buf, sem):
        rows = idx_v.at[pl.ds(tl * slots + half * win, win)]
        return pltpu.make_async_copy(tab_hbm.at[rows], buf, sem)

    def accumulate(tl, half, buf, acc):
        for g in range(words // SC_LANES // SC_GROUP):
            def row(r, sums, g=g):
                slot = jnp.full((SC_LANES,), tl * slots + half * win + r, I32)
                wv = plsc.load_gather(w_v, [slot])
                out = []
                for j in range(SC_GROUP):
                    x = buf[r, pl.ds((g * SC_GROUP + j) * SC_LANES, SC_LANES)]
                    lo = lax.bitcast_convert_type(x << 16, F32)
                    hi = lax.bitcast_convert_type(x & jnp.int32(-65536), F32)
                    out += [sums[2 * j] + wv * lo, sums[2 * j + 1] + wv * hi]
                return tuple(out)
            zeros = tuple(jnp.zeros((SC_LANES,), F32) for _ in range(2 * SC_GROUP))
            sums = lax.fori_loop(0, win, row, zeros)
            for j in range(SC_GROUP):
                chunk, lane = divmod((g * SC_GROUP + j) * SC_LANES, LANES)
                plsc.addupdate(acc.at[pl.ds(2 * chunk * LANES + lane, SC_LANES)], sums[2 * j])
                plsc.addupdate(acc.at[pl.ds((2 * chunk + 1) * LANES + lane, SC_LANES)], sums[2 * j + 1])

    def write_out(acc, tok, sem):
        return pltpu.make_async_copy(acc, out_hbm.at[tok], sem)

    @pl.loop(0, n_tok // SC_BATCH)
    def _(bi):
        tok0 = base_tok + bi * SC_BATCH
        pltpu.sync_copy(idx_hbm.at[pl.ds((first_tok + tok0) * slots, SC_BATCH * slots)], idx_v)
        pltpu.sync_copy(w_hbm.at[pl.ds((first_tok + tok0) * slots, SC_BATCH * slots)], w_v)
        gather(0, 0, rows_a, row_sem.at[0]).start()

        @pl.loop(0, SC_BATCH // 2)
        def _(tp):
            for parity, acc in ((0, acc_a), (1, acc_b)):
                tl = 2 * tp + parity

                @pl.when(bi * SC_BATCH + tl >= 2)
                def _():
                    write_out(acc, tok0, out_sem.at[parity]).wait()
                for k in range(d // SC_LANES):
                    acc[pl.ds(k * SC_LANES, SC_LANES)] = jnp.zeros((SC_LANES,), F32)
                gather(tl, 1, rows_b, row_sem.at[1]).start()
                gather(tl, 0, rows_a, row_sem.at[0]).wait()
                accumulate(tl, 0, rows_a, acc)
                gather(jnp.minimum(tl + 1, SC_BATCH - 1), 0, rows_a, row_sem.at[0]).start()
                gather(tl, 1, rows_b, row_sem.at[1]).wait()
                accumulate(tl, 1, rows_b, acc)
                write_out(acc, tok0 + tl, out_sem.at[parity]).start()

        gather(SC_BATCH - 1, 0, rows_a, row_sem.at[0]).wait()

    write_out(acc_a, base_tok, out_sem.at[0]).wait()
    write_out(acc_b, base_tok, out_sem.at[1]).wait()


def _peer_values_sc(idx, w, table, n_tok, d, slots):
    assert n_tok % (SC_CORES * SC_SUBCORES * SC_BATCH) == 0 and SC_BATCH % 2 == 0
    mesh = plsc.VectorSubcoreMesh(core_axis_name="c", subcore_axis_name="s",
                                  num_cores=SC_CORES, num_subcores=SC_SUBCORES)
    return pl.kernel(
        functools.partial(_sc_value_kernel, slots=slots, first_tok=idx.shape[0] // slots - n_tok),
        out_type=jax.ShapeDtypeStruct((n_tok, d), F32),
        mesh=mesh,
        scratch_types=[
            pltpu.VMEM((SC_BATCH * slots,), I32),
            pltpu.VMEM((SC_BATCH * slots,), F32),
            pltpu.VMEM((slots // 2, d // 2), I32),
            pltpu.VMEM((slots // 2, d // 2), I32),
            pltpu.VMEM((d,), F32),
            pltpu.VMEM((d,), F32),
            pltpu.SemaphoreType.DMA((2,)),
            pltpu.SemaphoreType.DMA((2,)),
        ],
        compiler_params=pltpu.CompilerParams(needs_layout_passes=False),
        name="peer_values_sc",
    )(table, idx, w)


def _sc_score_kernel(tab_hbm, idx_hbm, xe_hbm, xo_hbm, out_hbm, idx_v, xe_v, xo_v, rows_a, rows_b, part_v, a_v, row_sem,
                     *, slots, first_tok):
    n_tok = out_hbm.shape[0] // slots // (SC_CORES * SC_SUBCORES)
    words = rows_a.shape[1]
    win = slots // 2
    n_groups = words // SC_LANES // SC_GROUP
    wid = lax.axis_index("s") * SC_CORES + lax.axis_index("c")
    base_tok = wid * n_tok
    lane_id = lax.iota(I32, SC_LANES)

    def gather(tl, half, buf, sem):
        rows = idx_v.at[pl.ds(tl * slots + half * win, win)]
        return pltpu.make_async_copy(tab_hbm.at[rows], buf, sem)

    def scores(tl, half, buf):
        for r in range(win):
            part_v[pl.ds(r * SC_LANES, SC_LANES)] = jnp.zeros((SC_LANES,), F32)
        for g in range(n_groups):
            xs = []
            for j in range(SC_GROUP):
                at = pl.ds(tl * words + (g * SC_GROUP + j) * SC_LANES, SC_LANES)
                xs += [xe_v[at], xo_v[at]]

            @pl.loop(0, win)
            def _(r, g=g, xs=xs):
                sums = [jnp.zeros((SC_LANES,), F32) for _ in range(4)]
                for j in range(SC_GROUP):
                    x = buf[r, pl.ds((g * SC_GROUP + j) * SC_LANES, SC_LANES)]
                    lo = lax.bitcast_convert_type(x << 16, F32)
                    hi = lax.bitcast_convert_type(x & jnp.int32(-65536), F32)
                    sums[(2 * j) % 4] += lo * xs[2 * j]
                    sums[(2 * j + 1) % 4] += hi * xs[2 * j + 1]
                plsc.addupdate(part_v.at[pl.ds(r * SC_LANES, SC_LANES)], (sums[0] + sums[1]) + (sums[2] + sums[3]))
        for rb in range(win // SC_LANES):
            first = (lane_id + rb * SC_LANES) * SC_LANES
            total = jnp.zeros((SC_LANES,), F32)
            for lane in range(SC_LANES):
                total += plsc.load_gather(part_v, [first + lane])
            a_v[pl.ds(tl * slots + half * win + rb * SC_LANES, SC_LANES)] = total

    @pl.loop(0, n_tok // SC_BATCH)
    def _(bi):
        tok0 = base_tok + bi * SC_BATCH
        pltpu.sync_copy(idx_hbm.at[pl.ds((first_tok + tok0) * slots, SC_BATCH * slots)], idx_v)
        pltpu.sync_copy(xe_hbm.at[pl.ds(tok0 * words, SC_BATCH * words)], xe_v)
        pltpu.sync_copy(xo_hbm.at[pl.ds(tok0 * words, SC_BATCH * words)], xo_v)
        gather(0, 0, rows_a, row_sem.at[0]).start()

        @pl.loop(0, SC_BATCH)
        def _(tl):
            gather(tl, 1, rows_b, row_sem.at[1]).start()
            gather(tl, 0, rows_a, row_sem.at[0]).wait()
            scores(tl, 0, rows_a)
            gather(jnp.minimum(tl + 1, SC_BATCH - 1), 0, rows_a, row_sem.at[0]).start()
            gather(tl, 1, rows_b, row_sem.at[1]).wait()
            scores(tl, 1, rows_b)

        gather(SC_BATCH - 1, 0, rows_a, row_sem.at[0]).wait()
        pltpu.sync_copy(a_v, out_hbm.at[pl.ds(tok0 * slots, SC_BATCH * slots)])


def _peer_scores_sc(idx, xe, xo, table, n_tok, slots):
    words = table.shape[1]
    assert n_tok % (SC_CORES * SC_SUBCORES * SC_BATCH) == 0
    mesh = plsc.VectorSubcoreMesh(core_axis_name="c", subcore_axis_name="s",
                                  num_cores=SC_CORES, num_subcores=SC_SUBCORES)
    return pl.kernel(
        functools.partial(_sc_score_kernel, slots=slots, first_tok=idx.shape[0] // slots - n_tok),
        out_type=jax.ShapeDtypeStruct((n_tok * slots,), F32),
        mesh=mesh,
        scratch_types=[
            pltpu.VMEM((SC_BATCH * slots,), I32),
            pltpu.VMEM((SC_BATCH * words,), F32),
            pltpu.VMEM((SC_BATCH * words,), F32),
            pltpu.VMEM((slots // 2, words), I32),
            pltpu.VMEM((slots // 2, words), I32),
            pltpu.VMEM((slots // 2 * SC_LANES,), F32),
            pltpu.VMEM((SC_BATCH * slots,), F32),
            pltpu.SemaphoreType.DMA((2,)),
        ],
        compiler_params=pltpu.CompilerParams(needs_layout_passes=False),
        name="peer_scores_sc",
    )(table, idx, xe, xo)


def _split_norm_kernel(x_ref, g_ref, xe_ref, xo_ref):
    d = x_ref.shape[1]
    xn = _rms(x_ref[...], g_ref[...])
    nchunk = d // (2 * LANES)
    xe_ref[...] = jnp.concatenate([xn[:, (2 * c) * LANES:(2 * c + 1) * LANES] for c in range(nchunk)], axis=1)
    xo_ref[...] = jnp.concatenate([xn[:, (2 * c + 1) * LANES:(2 * c + 2) * LANES] for c in range(nchunk)], axis=1)


def _split_norm(x2d, g, n_rows, tm):
    d = x2d.shape[1]
    first = (x2d.shape[0] - n_rows) // tm
    half = jax.ShapeDtypeStruct((n_rows, d // 2), F32)
    return pl.pallas_call(
        _split_norm_kernel,
        grid=(n_rows // tm,),
        in_specs=[pl.BlockSpec((tm, d), lambda i: (first + i, 0)), pl.BlockSpec((1, d), lambda i: (0, 0))],
        out_specs=[pl.BlockSpec((tm, d // 2), lambda i: (i, 0))] * 2,
        out_shape=[half, half],
        compiler_params=_cparams(("arbitrary",)),
        name="peer_split_norm",
    )(x2d, g.reshape(1, d))


def _score_finish_kernel(w_hbm, a_ref, gate_ref, w_ref):
    del w_hbm
    a = a_ref[0]
    gelu = 0.5 * a * (1.0 + lax.erf(a * (2.0 ** -0.5)))
    w_ref[0] = gate_ref[0].T * gelu


def _score_finish(w_full, a_sc, gate_t):
    nt, t, slots = w_full.shape
    n_sc = a_sc.shape[0]
    first = nt - n_sc
    return pl.pallas_call(
        _score_finish_kernel,
        grid=(n_sc,),
        in_specs=[
            pl.BlockSpec(memory_space=pl.ANY),
            pl.BlockSpec((1, t, slots), lambda i: (i, 0, 0)),
            pl.BlockSpec((1, slots, t), lambda i: (first + i, 0, 0)),
        ],
        out_specs=pl.BlockSpec((1, t, slots), lambda i: (first + i, 0, 0)),
        out_shape=jax.ShapeDtypeStruct(w_full.shape, F32),
        input_output_aliases={0: 0},
        compiler_params=_cparams(("arbitrary",)),
        name="peer_score_finish",
    )(w_full, a_sc, gate_t)


def _residual_norm_kernel(out_hbm, x_ref, p_ref, gf_ref, o_ref, *, final_norm):
    del out_hbm
    y = x_ref[...] + p_ref[...]
    if final_norm:
        y = _rms(y, gf_ref[...])
    o_ref[...] = y


def _residual_norm(out_full, x2d, p2d, g_final, final_norm, tm):
    t, d = p2d.shape
    first = (x2d.shape[0] - t) // tm
    return pl.pallas_call(
        functools.partial(_residual_norm_kernel, final_norm=final_norm),
        grid=(t // tm,),
        in_specs=[
            pl.BlockSpec(memory_space=pl.ANY),
            pl.BlockSpec((tm, d), lambda i: (first + i, 0)),
            pl.BlockSpec((tm, d), lambda i: (i, 0)),
            pl.BlockSpec((1, d), lambda i: (0, 0)),
        ],
        out_specs=pl.BlockSpec((tm, d), lambda i: (first + i, 0)),
        out_shape=jax.ShapeDtypeStruct(x2d.shape, F32),
        input_output_aliases={0: 0},
        compiler_params=_cparams(("arbitrary",)),
        name="peer_residual",
    )(out_full, x2d, p2d, g_final.reshape(1, d))


def _peer_ffn(x, g, w_q, k1, k2, u_emb, v_emb, g_final, final_norm):
    b, s, d = x.shape
    x2d = x.reshape(b * s, d)
    e_slot, gate_slot, e_tok = _peer_router(x2d, g, w_q, k1, k2)
    (u_tab, u_tab_sc), (v_tab, v_tab_sc) = _pack_pairs(u_emb), _pack_pairs(v_emb)
    nt, slots, t = e_slot.shape
    sc_quantum = SC_CORES * SC_SUBCORES * SC_BATCH // t
    ns_score = nt * SC_SCORE_PERCENT // 100 // sc_quantum * sc_quantum
    ns_value = nt * SC_VALUE_PERCENT // 100 // sc_quantum * sc_quantum
    idx_tok = e_tok.reshape(-1)

    w_tok = _peer_scores(e_slot, gate_slot, x2d, g, u_tab, nt - ns_score)
    if ns_score:
        xe, xo = _split_norm(x2d, g, ns_score * t, math.gcd(TOKEN_TILE, ns_score * t))
        a_sc = _peer_scores_sc(idx_tok, xe.reshape(-1), xo.reshape(-1), u_tab_sc, ns_score * t, slots)
        w_tok = _score_finish(w_tok, a_sc.reshape(ns_score, t, slots), gate_slot)
    out = _peer_values(e_slot, w_tok, x2d, v_tab, g_final, final_norm, nt - ns_value)
    if ns_value:
        peer_sc = _peer_values_sc(idx_tok, w_tok.reshape(-1), v_tab_sc, ns_value * t, d, slots)
        out = _residual_norm(out, x2d, peer_sc, g_final, final_norm, math.gcd(TOKEN_TILE, ns_value * t))
    return out.reshape(b, s, d)


def _qkv_kernel(x_ref, g_ref, wqk_hi_ref, wqk_lo_ref, wv_ref, cos_ref, sin_ref,
                qt_ref, k0_ref, k1_ref, vt_ref, km_ref):
    tm, d = x_ref.shape[1], x_ref.shape[2]
    hn = _rms(x_ref[0], g_ref[...])
    hh, hl = _split(hn)
    qk = _dot3(hh, hl, wqk_hi_ref[...], wqk_lo_ref[...])
    v = _dot(hh, wv_ref[...])
    cos = jnp.concatenate([cos_ref[...]] * (d // LANES), axis=1)
    sin = jnp.concatenate([sin_ref[...]] * (d // LANES), axis=1)
    lane = lax.broadcasted_iota(I32, (tm, d), 1)
    first_half = (lane % HEAD_DIM) < (HEAD_DIM // 2)

    def rope(a):
        rot = jnp.where(first_half, pltpu.roll(a, d - HEAD_DIM // 2, 1), pltpu.roll(a, HEAD_DIM // 2, 1))
        return a * cos + rot * sin

    q = rope(qk[:, :d])
    k = rope(qk[:, d:])
    qt_ref[0] = q.T
    vt_ref[0] = v.T.astype(BF16)
    nb = tm // MOBA_BLOCK
    km_ref[0, 0] = jnp.mean(k.reshape(nb, MOBA_BLOCK, d), axis=1)
    row = lax.broadcasted_iota(I32, (tm, d), 0)
    block = (pl.program_id(1) * tm + row) // MOBA_BLOCK
    pair_lane = lane % LANES
    kb = k.astype(BF16)
    k0_ref[0] = jnp.where(pair_lane < HEAD_DIM, kb, jnp.where(pair_lane - HEAD_DIM == block, 1.0, 0.0).astype(BF16))
    k1_ref[0] = jnp.where(pair_lane >= HEAD_DIM, kb, jnp.where(pair_lane == block, 1.0, 0.0).astype(BF16))


def _qkv_rope(x, g, w_qkv, tm):
    b, s, d = x.shape
    half = HEAD_DIM // 2
    inv = ROPE_THETA ** (-jnp.arange(half, dtype=F32) / half)
    ang = jnp.arange(s).astype(F32)[:, None] * inv[None, :]
    cos, sin = jnp.cos(ang), jnp.sin(ang)
    cos128 = jnp.tile(jnp.concatenate([cos, cos], axis=1), (1, LANES // HEAD_DIM))
    sin128 = jnp.tile(jnp.concatenate([-sin, sin], axis=1), (1, LANES // HEAD_DIM))
    wqk_hi, wqk_lo = _split(w_qkv[:, :2 * d])
    wv = w_qkv[:, 2 * d:].astype(BF16)
    nb = tm // MOBA_BLOCK
    const = lambda i, j: (0, 0)
    qt, k0, k1, vt, km = pl.pallas_call(
        _qkv_kernel,
        grid=(b, s // tm),
        in_specs=[
            pl.BlockSpec((1, tm, d), lambda i, j: (i, j, 0)),
            pl.BlockSpec((1, d), const),
            pl.BlockSpec((d, 2 * d), const),
            pl.BlockSpec((d, 2 * d), const),
            pl.BlockSpec((d, d), const),
            pl.BlockSpec((tm, LANES), lambda i, j: (j, 0)),
            pl.BlockSpec((tm, LANES), lambda i, j: (j, 0)),
        ],
        out_specs=[
            pl.BlockSpec((1, d, tm), lambda i, j: (i, 0, j)),
            pl.BlockSpec((1, tm, d), lambda i, j: (i, j, 0)),
            pl.BlockSpec((1, tm, d), lambda i, j: (i, j, 0)),
            pl.BlockSpec((1, d, tm), lambda i, j: (i, 0, j)),
            pl.BlockSpec((1, 1, nb, d), lambda i, j: (i, j, 0, 0)),
        ],
        out_shape=[
            jax.ShapeDtypeStruct((b, d, s), F32),
            jax.ShapeDtypeStruct((b, s, d), BF16),
            jax.ShapeDtypeStruct((b, s, d), BF16),
            jax.ShapeDtypeStruct((b, d, s), BF16),
            jax.ShapeDtypeStruct((b, s // tm, nb, d), F32),
        ],
        compiler_params=_cparams(("arbitrary", "arbitrary")),
        name="qkv_rope",
    )(x, g.reshape(1, d), wqk_hi, wqk_lo, wv, cos128, sin128)
    return qt, (k0, k1), vt, km.reshape(b, s // MOBA_BLOCK, d)


def _moba_kernel(qt_ref, k0_ref, k1_ref, vt_ref, km_ref, o_ref, sa_ref, sb_ref):
    bs = MOBA_BLOCK
    nb = km_ref.shape[1]
    n_heads = LANES // HEAD_DIM
    k_refs = (k0_ref, k1_ref)
    j = pl.program_id(2)
    qt = qt_ref[0]
    km = km_ref[0]
    scale = HEAD_DIM ** -0.5 * 1.4426950408889634
    lane_km = lax.broadcasted_iota(I32, (nb, LANES), 1)
    blk = lax.broadcasted_iota(I32, (nb, bs), 0)
    zeros_pad = jnp.zeros((LANES - HEAD_DIM - nb, bs), F32)
    qh, ql = _split(qt)

    own = pl.ds(pl.multiple_of(j * bs, bs), bs)
    v_own = vt_ref[0, :, own]
    krow = lax.broadcasted_iota(I32, (bs, bs), 0)
    qcol = lax.broadcasted_iota(I32, (bs, bs), 1)

    q_augs, state = [], []
    for hh in range(n_heads):
        head_lo = hh * HEAD_DIM
        in_head_km = (lane_km >= head_lo) & (lane_km < head_lo + HEAD_DIM)
        kmh, kml = _split(jnp.where(in_head_km, km, 0.0))
        gate = _dot3(kmh, kml, qh, ql)
        valid = blk < j
        gate = jnp.where(valid, gate, -jnp.inf)
        sel = jnp.zeros((nb, bs), F32)
        for _ in range(MOBA_TOPK):
            m = jnp.max(gate, axis=0, keepdims=True)
            i = jnp.min(jnp.where(gate == m, blk, nb), axis=0, keepdims=True)
            pick = blk == i
            sel = jnp.where(pick, 1.0, sel)
            gate = jnp.where(pick, -jnp.inf, gate)
        bias_t = jnp.where((sel > 0.0) & valid, 0.0, MASK_NEG)
        q_head = qt[head_lo:head_lo + HEAD_DIM, :] * scale
        no_bias = jnp.zeros((LANES - HEAD_DIM, bs), F32)
        if hh == 0:
            q_aug = jnp.concatenate([q_head, bias_t, zeros_pad], axis=0)
            q_own = jnp.concatenate([q_head, no_bias], axis=0)
        else:
            q_aug = jnp.concatenate([bias_t, zeros_pad, q_head], axis=0)
            q_own = jnp.concatenate([no_bias, q_head], axis=0)
        q_augs.append(q_aug.astype(BF16))

        s_own = jnp.where(krow <= qcol, _dot(k_refs[hh][0, own, :], q_own.astype(BF16)), -1e30)
        m0 = jnp.max(s_own, axis=0, keepdims=True)
        p0 = jnp.exp2(s_own - m0)
        state += [m0, jnp.sum(p0, axis=0, keepdims=True), _dot(v_own, p0.astype(BF16))]

    chunk = KV_CHUNK * bs
    last_chunk = nb // KV_CHUNK - 1

    def score_chunk(c, s_ref):
        rows = pl.ds(pl.multiple_of(jnp.minimum(c, last_chunk) * chunk, chunk), chunk)
        for hh in range(n_heads):
            s_ref[hh] = _dot(k_refs[hh][0, rows, :], q_augs[hh])

    def attend(c, s_ref, state):
        vn = vt_ref[0, :, pl.ds(pl.multiple_of(c * chunk, chunk), chunk)]
        new_state = []
        for hh in range(n_heads):
            m, l, acc = state[3 * hh:3 * hh + 3]
            s = s_ref[hh]
            m_new = jnp.maximum(m, jnp.max(s, axis=0, keepdims=True))
            alpha = jnp.exp2(m - m_new)
            p = jnp.exp2(s - m_new)
            l = alpha * l + jnp.sum(p, axis=0, keepdims=True)
            acc = alpha * acc + _dot(vn, p.astype(BF16))
            new_state += [m_new, l, acc]
        return tuple(new_state)

    score_chunk(0, sa_ref)

    def body(i, state):
        score_chunk(2 * i + 1, sb_ref)
        state = attend(2 * i, sa_ref, state)
        score_chunk(2 * i + 2, sa_ref)
        return attend(2 * i + 1, sb_ref, state)

    state = lax.fori_loop(0, (j + 2 * KV_CHUNK - 1) // (2 * KV_CHUNK), body, tuple(state))
    halves = []
    for hh in range(n_heads):
        _, l, acc = state[3 * hh:3 * hh + 3]
        halves.append((acc / l)[hh * HEAD_DIM:(hh + 1) * HEAD_DIM, :])
    o_ref[0] = jnp.concatenate(halves, axis=0).T


def _moba_attention(qt, k01, vt, km):
    b, d, s = qt.shape
    nb = s // MOBA_BLOCK
    assert nb % (2 * KV_CHUNK) == 0
    return pl.pallas_call(
        _moba_kernel,
        grid=(b, d // LANES, nb),
        in_specs=[
            pl.BlockSpec((1, LANES, MOBA_BLOCK), lambda i, h, j: (i, h, j)),
            pl.BlockSpec((1, s, LANES), lambda i, h, j: (i, 0, h)),
            pl.BlockSpec((1, s, LANES), lambda i, h, j: (i, 0, h)),
            pl.BlockSpec((1, LANES, s), lambda i, h, j: (i, h, 0)),
            pl.BlockSpec((1, nb, LANES), lambda i, h, j: (i, 0, h)),
        ],
        out_specs=pl.BlockSpec((1, MOBA_BLOCK, LANES), lambda i, h, j: (i, j, h)),
        out_shape=jax.ShapeDtypeStruct((b, s, d), F32),
        scratch_shapes=[pltpu.VMEM((LANES // HEAD_DIM, KV_CHUNK * MOBA_BLOCK, MOBA_BLOCK), F32)] * 2,
        compiler_params=_cparams(("arbitrary", "arbitrary", "arbitrary")),
        name="moba_attention",
    )(qt, *k01, vt, km)


def _proj_residual_kernel(x_ref, a_ref, w_ref, o_ref):
    o_ref[...] = x_ref[...] + _dot(a_ref[...].astype(BF16), w_ref[...])


def _proj_residual(x2d, a2d, w, tm):
    t, d = x2d.shape
    return pl.pallas_call(
        _proj_residual_kernel,
        grid=(t // tm,),
        in_specs=[
            pl.BlockSpec((tm, d), lambda i: (i, 0)),
            pl.BlockSpec((tm, a2d.shape[1]), lambda i: (i, 0)),
            pl.BlockSpec(w.shape, lambda i: (0, 0)),
        ],
        out_specs=pl.BlockSpec((tm, d), lambda i: (i, 0)),
        out_shape=jax.ShapeDtypeStruct((t, d), F32),
        compiler_params=_cparams(("arbitrary",)),
        name="attn_out_proj",
    )(x2d, a2d, w.astype(BF16))


def _moba_mixer(x, g, w_qkv, w_o):
    b, s, d = x.shape
    tm = min(TOKEN_TILE, s)
    qt, k, vt, km = _qkv_rope(x, g, w_qkv, tm)
    attn = _moba_attention(qt, k, vt, km)
    return _proj_residual(x.reshape(b * s, d), attn.reshape(b * s, d), w_o, tm).reshape(b, s, d)


def kernel(x, norm_mix, norm_ffn, conv_w_in, conv_w, conv_w_out, attn_w_qkv, attn_w_o,
           peer_w_q, peer_k1, peer_k2, peer_u, peer_v, norm_final):
    depth = norm_mix.shape[0]
    tm = min(TOKEN_TILE, x.shape[1])
    for i in range(depth):
        j = i // 2
        if i % 2 == 0:
            x = _conv_mixer(x, norm_mix[i], conv_w_in[j], conv_w[j], conv_w_out[j], tm)
        else:
            x = _moba_mixer(x, norm_mix[i], attn_w_qkv[j], attn_w_o[j])
        x = _peer_ffn(x, norm_ffn[i], peer_w_q[i], peer_k1[i], peer_k2[i], peer_u[i], peer_v[i],
                      norm_final, final_norm=(i == depth - 1))
    return x
```

```python
import functools
import math

import jax
import jax.numpy as jnp
from jax import lax
from jax.experimental import pallas as pl
from jax.experimental.pallas import tpu as pltpu
from jax.experimental.pallas import tpu_sc as plsc

F32 = jnp.float32
BF16 = jnp.bfloat16
I32 = jnp.int32

RMS_EPS = 1e-6
N_HEADS = 16
HEAD_DIM = 64
MOBA_BLOCK = 256
MOBA_TOPK = 3
ROPE_THETA = 10000.0
PEER_HEADS = 8
PEER_NKEYS = 128
PEER_HALF = 64
PEER_TOPK = 16

LANES = 128
SUBLANES = 8
VMEM_LIMIT = 56 * 1024 * 1024
MASK_NEG = -1e9

TOKEN_TILE = 512
PEER_TILE = 128
PACK_TILE = 512
OCTET = 8
SC_CORES = 2
SC_SUBCORES = 16
SC_LANES = 16
SC_BATCH = 16
SC_GROUP = 8
SC_SCORE_PERCENT = 34
SC_VALUE_PERCENT = 47
KV_CHUNK = 2


def _cparams(sem):
    return pltpu.CompilerParams(dimension_semantics=sem, vmem_limit_bytes=VMEM_LIMIT)


def _rms(x, g):
    ms = jnp.mean(x * x, axis=-1, keepdims=True)
    return x * lax.rsqrt(ms + RMS_EPS) * g


def _split(a):
    hi = a.astype(BF16)
    lo = (a - hi.astype(F32)).astype(BF16)
    return hi, lo


def _dot(a, b):
    return lax.dot_general(a, b, (((1,), (0,)), ((), ())), preferred_element_type=F32)


def _dot_t(a, b):
    return lax.dot_general(a, b, (((1,), (1,)), ((), ())), preferred_element_type=F32)


def _dot3(a_hi, a_lo, b_hi, b_lo):
    return _dot(a_hi, b_hi) + _dot(a_lo, b_hi) + _dot(a_hi, b_lo)


def _dot3_t(a_hi, a_lo, b_hi, b_lo):
    return _dot_t(a_hi, b_hi) + _dot_t(a_lo, b_hi) + _dot_t(a_hi, b_lo)


def _conv_mixer_kernel(x_ref, g_ref, win_ref, cw_ref, wout_ref, o_ref, ubuf_ref):
    tm, d = x_ref.shape[1], x_ref.shape[2]

    @pl.when(pl.program_id(1) == 0)
    def _():
        ubuf_ref[0:SUBLANES, :] = jnp.zeros((SUBLANES, d), F32)

    x = x_ref[0]
    hn = _rms(x, g_ref[...]).astype(BF16)
    bcz = _dot(hn, win_ref[...])
    b_gate, c_gate, z = bcz[:, :d], bcz[:, d:2 * d], bcz[:, 2 * d:]
    u = c_gate * z
    ubuf_ref[SUBLANES:SUBLANES + tm, :] = u
    u1 = ubuf_ref[SUBLANES - 1:SUBLANES - 1 + tm, :]
    u2 = ubuf_ref[SUBLANES - 2:SUBLANES - 2 + tm, :]
    cw = cw_ref[...]
    u_conv = cw[0:1, :] * u2 + cw[1:2, :] * u1 + cw[2:3, :] * u
    ubuf_ref[0:SUBLANES, :] = u[tm - SUBLANES:tm, :]
    y = (b_gate * u_conv).astype(BF16)
    o_ref[0] = x + _dot(y, wout_ref[...])


def _conv_mixer(x, g, w_in, conv_w, w_out, tm):
    b, s, d = x.shape
    return pl.pallas_call(
        _conv_mixer_kernel,
        grid=(b, s // tm),
        in_specs=[
            pl.BlockSpec((1, tm, d), lambda i, j: (i, j, 0)),
            pl.BlockSpec((1, d), lambda i, j: (0, 0)),
            pl.BlockSpec((d, 3 * d), lambda i, j: (0, 0)),
            pl.BlockSpec((3, d), lambda i, j: (0, 0)),
            pl.BlockSpec((d, d), lambda i, j: (0, 0)),
        ],
        out_specs=pl.BlockSpec((1, tm, d), lambda i, j: (i, j, 0)),
        out_shape=jax.ShapeDtypeStruct((b, s, d), F32),
        scratch_shapes=[pltpu.VMEM((tm + SUBLANES, d), F32)],
        compiler_params=_cparams(("arbitrary", "arbitrary")),
        name="conv_mixer",
    )(x, g.reshape(1, d), w_in.astype(BF16), conv_w, w_out.astype(BF16))


def _topk_rows(s, k, order=None, payload=None):
    if order is None:
        order = lax.broadcasted_iota(I32, s.shape, 0)
    big = jnp.iinfo(jnp.int32).max
    vals, outs = [], []
    for _ in range(k):
        m = jnp.max(s, axis=0, keepdims=True)
        i = jnp.min(jnp.where(s == m, order, big), axis=0, keepdims=True)
        pick = order == i
        vals.append(m)
        if payload is None:
            outs.append(i)
        else:
            outs.append(jnp.max(jnp.where(pick, payload, -1), axis=0, keepdims=True))
        s = jnp.where(pick, -jnp.inf, s)
    return jnp.concatenate(vals, axis=0), jnp.concatenate(outs, axis=0)


def _staircase(kk):
    groups = []
    for a in range(2):
        for b0 in range(0, kk // (a + 1), SUBLANES):
            groups.append((a, 0, b0, 1, lambda j, a=a, b0=b0: (a + 1) * (b0 + j + 1) <= kk))
    for b in range(kk // 3):
        for a0 in range(0, kk // (b + 1), SUBLANES):
            groups.append((a0, 1, b, 0, lambda j, a0=a0, b=b: (a0 + j >= 2) & ((a0 + j + 1) * (b + 1) <= kk)))
    return groups


def _staircase_topk(v1, i1, v2, i2, nk):
    kk, t = v1.shape
    j = lax.broadcasted_iota(I32, (SUBLANES, t), 0)

    def rows(x, x0, step):
        if step == 0:
            return jnp.broadcast_to(x[x0:x0 + 1, :], (SUBLANES, t))
        return x[x0:x0 + SUBLANES, :]

    cand, order, cidx = [], [], []
    for a0, a_step, b0, b_step, valid in _staircase(kk):
        ok = valid(j)
        cand.append(jnp.where(ok, rows(v1, a0, a_step) + rows(v2, b0, b_step), -jnp.inf))
        order.append(jnp.where(ok, (a0 + j * a_step) * kk + (b0 + j * b_step), jnp.iinfo(jnp.int32).max - 1))
        cidx.append(rows(i1, a0, a_step) * nk + rows(i2, b0, b_step))
    cat = lambda xs: jnp.concatenate(xs, axis=0)
    return _topk_rows(cat(cand), kk, order=cat(order), payload=cat(cidx))


def _router_kernel(x_ref, g_ref, wq_hi_ref, wq_lo_ref, k1_hi_ref, k1_lo_ref, k2_hi_ref, k2_lo_ref,
                   e_ref, gate_ref, e_tok_ref):
    table_rows = x_ref.shape[1] // (2 * LANES)
    ids = []
    xn = _rms(x_ref[...], g_ref[...])
    xh, xl = _split(xn)
    q = _dot3(xh, xl, wq_hi_ref[...], wq_lo_ref[...])
    nk = k1_hi_ref.shape[0]
    for h in range(PEER_HEADS):
        qh, ql = _split(q[:, h * LANES:(h + 1) * LANES])
        s1 = _dot3_t(k1_hi_ref[...], k1_lo_ref[...], qh, ql)
        s2 = _dot3_t(k2_hi_ref[...], k2_lo_ref[...], qh, ql)
        v1, i1 = _topk_rows(s1, PEER_TOPK)
        v2, i2 = _topk_rows(s2, PEER_TOPK)
        kk = PEER_TOPK
        top_s, e_idx = _staircase_topk(v1, i1, v2, i2, nk)
        p = jnp.exp(top_s - top_s[0:1, :])
        gate = p / jnp.sum(p, axis=0, keepdims=True)
        e_ref[0, h * kk:(h + 1) * kk, :] = e_idx * table_rows
        gate_ref[0, h * kk:(h + 1) * kk, :] = gate
        ids.append(e_idx)
    e_tok_ref[0] = jnp.concatenate(ids, axis=0).T


def _pad_keys(k, lo):
    nk, half = k.shape
    out = jnp.zeros((nk, LANES), F32)
    return out.at[:, lo:lo + half].set(k)


def _peer_router(x2d, g, w_q, k1, k2):
    t, d = x2d.shape
    nt = t // PEER_TILE
    slots = PEER_HEADS * PEER_TOPK
    wq_hi, wq_lo = _split(w_q)
    k1_hi, k1_lo = _split(_pad_keys(k1, 0))
    k2_hi, k2_lo = _split(_pad_keys(k2, PEER_HALF))
    const = lambda i: (0, 0)
    kspec = pl.BlockSpec(k1_hi.shape, const)
    return pl.pallas_call(
        _router_kernel,
        grid=(nt,),
        in_specs=[
            pl.BlockSpec((PEER_TILE, d), lambda i: (i, 0)),
            pl.BlockSpec((1, d), const),
            pl.BlockSpec(wq_hi.shape, const),
            pl.BlockSpec(wq_lo.shape, const),
            kspec, kspec, kspec, kspec,
        ],
        out_specs=[
            pl.BlockSpec((1, slots, PEER_TILE), lambda i: (i, 0, 0)),
            pl.BlockSpec((1, slots, PEER_TILE), lambda i: (i, 0, 0)),
            pl.BlockSpec((1, PEER_TILE, slots), lambda i: (i, 0, 0)),
        ],
        out_shape=[
            jax.ShapeDtypeStruct((nt, slots, PEER_TILE), I32),
            jax.ShapeDtypeStruct((nt, slots, PEER_TILE), F32),
            jax.ShapeDtypeStruct((nt, PEER_TILE, slots), I32),
        ],
        compiler_params=_cparams(("arbitrary",)),
        name="peer_router",
    )(x2d, g.reshape(1, d), wq_hi, wq_lo, k1_hi, k1_lo, k2_hi, k2_lo)


def _gather_octet(e_ref, tab_ref, stage_ref, octet, nsub):
    t = e_ref.shape[2]
    for k in range(OCTET):
        for tok in range(t):
            off = e_ref[0, octet * OCTET + k, tok]
            stage_ref[k * (t // SUBLANES) + tok // SUBLANES,
                      pl.ds(tok % SUBLANES, nsub, stride=SUBLANES), :] = tab_ref[pl.ds(off, nsub), :]


def _octet_pipeline(n_octets, gather, consume, stage_a, stage_b):
    gather(0, stage_a)

    def pair(p, carry):
        gather(2 * p + 1, stage_b)
        consume(2 * p, stage_a)
        gather(jnp.minimum(2 * p + 2, n_octets - 1), stage_a)
        consume(2 * p + 1, stage_b)
        return carry
    lax.fori_loop(0, n_octets // 2, pair, 0)


def _score_kernel(e_ref, x_ref, g_ref, gate_ref, tab_ref, w_ref, stage_a, stage_b, a_ref):
    t, d = x_ref.shape
    nchunk = d // (2 * LANES)
    slots = e_ref.shape[1]
    rows = OCTET * t

    xn = _rms(x_ref[...], g_ref[...]).astype(BF16)
    x_even = jnp.concatenate([xn[:, (2 * c) * LANES:(2 * c + 1) * LANES] for c in range(nchunk)], axis=1)
    x_odd = jnp.concatenate([xn[:, (2 * c + 1) * LANES:(2 * c + 2) * LANES] for c in range(nchunk)], axis=1)
    rhs = jnp.concatenate([x_even, x_odd], axis=0)

    ri = lax.broadcasted_iota(I32, (2 * t, 2 * t), 0)
    ci = lax.broadcasted_iota(I32, (2 * t, 2 * t), 1)
    diag = (ci == (ri % 2) * t + ri // 2).astype(F32)

    def consume(o, stage_ref):
        planes = []
        for c in range(nchunk):
            plane = stage_ref[:, c * SUBLANES:(c + 1) * SUBLANES, :].reshape(rows, LANES)
            planes.append(pltpu.bitcast(plane, BF16))
        lhs = jnp.concatenate(planes, axis=1)
        prod = _dot_t(lhs, rhs)
        prod = prod.reshape(OCTET, 2 * t, 2 * t) * diag[None]
        a2 = jnp.sum(prod, axis=1)
        a_ref[pl.ds(pl.multiple_of(o * OCTET, OCTET), OCTET), :] = a2[:, :t] + a2[:, t:]

    gather = lambda o, stage_ref: _gather_octet(e_ref, tab_ref, stage_ref, o, nchunk)
    _octet_pipeline(slots // OCTET, gather, consume, stage_a, stage_b)

    a = a_ref[...]
    gelu = 0.5 * a * (1.0 + lax.erf(a * (2.0 ** -0.5)))
    w_ref[0] = (gate_ref[0] * gelu).T


def _peer_scores(e_t, gate_t, x2d, g, table, n_tiles):
    nt, slots, t = e_t.shape
    d = x2d.shape[1]
    stage = pltpu.VMEM((OCTET * t // SUBLANES, d // (2 * LANES) * SUBLANES, LANES), I32)
    return pl.pallas_call(
        _score_kernel,
        grid=(n_tiles,),
        in_specs=[
            pl.BlockSpec((1, slots, t), lambda i: (i, 0, 0), memory_space=pltpu.SMEM),
            pl.BlockSpec((t, d), lambda i: (i, 0)),
            pl.BlockSpec((1, d), lambda i: (0, 0)),
            pl.BlockSpec((1, slots, t), lambda i: (i, 0, 0)),
            pl.BlockSpec(table.shape, lambda i: (0, 0), pipeline_mode=pl.Buffered(1)),
        ],
        out_specs=pl.BlockSpec((1, t, slots), lambda i: (i, 0, 0)),
        out_shape=jax.ShapeDtypeStruct((nt, t, slots), F32),
        scratch_shapes=[stage, stage, pltpu.VMEM((slots, t), F32)],
        compiler_params=_cparams(("arbitrary",)),
        name="peer_scores",
    )(e_t, x2d, g.reshape(1, d), gate_t, table)


def _value_kernel(e_ref, w_ref, x_ref, gf_ref, tab_ref, o_ref, stage_a, stage_b, wb_ref, acc_ref, *, final_norm):
    t, d = x_ref.shape
    nchunk = d // (2 * LANES)
    slots = e_ref.shape[1]
    groups = t // SUBLANES

    acc_ref[...] = x_ref[...]

    def consume(o, stage_ref):
        w_oct = pltpu.roll(w_ref[0], lax.rem(slots - o * OCTET, slots), 1)
        for k in range(OCTET):
            wb_ref[k] = jnp.broadcast_to(w_oct[:, k:k + 1], (t, LANES))
        for c in range(nchunk):
            lo_cols = slice(2 * c * LANES, (2 * c + 1) * LANES)
            hi_cols = slice((2 * c + 1) * LANES, (2 * c + 2) * LANES)
            acc_lo, acc_hi = acc_ref[:, lo_cols], acc_ref[:, hi_cols]
            for k in range(OCTET):
                words = stage_ref[k * groups:(k + 1) * groups, c * SUBLANES:(c + 1) * SUBLANES, :]
                words = words.reshape(t, LANES)
                lo = pltpu.bitcast(words << 16, F32)
                hi = pltpu.bitcast(words & jnp.int32(-65536), F32)
                acc_lo = acc_lo + wb_ref[k] * lo
                acc_hi = acc_hi + wb_ref[k] * hi
            acc_ref[:, lo_cols] = acc_lo
            acc_ref[:, hi_cols] = acc_hi

    gather = lambda o, stage_ref: _gather_octet(e_ref, tab_ref, stage_ref, o, nchunk)
    _octet_pipeline(slots // OCTET, gather, consume, stage_a, stage_b)

    y = acc_ref[...]
    if final_norm:
        y = _rms(y, gf_ref[...])
    o_ref[...] = y


def _peer_values(e_t, w_t, x2d, table, g_final, final_norm, n_tiles):
    nt, slots, t = e_t.shape
    d = x2d.shape[1]
    stage = pltpu.VMEM((OCTET * t // SUBLANES, d // (2 * LANES) * SUBLANES, LANES), I32)
    return pl.pallas_call(
        functools.partial(_value_kernel, final_norm=final_norm),
        grid=(n_tiles,),
        in_specs=[
            pl.BlockSpec((1, slots, t), lambda i: (i, 0, 0), memory_space=pltpu.SMEM),
            pl.BlockSpec((1, t, slots), lambda i: (i, 0, 0)),
            pl.BlockSpec((t, d), lambda i: (i, 0)),
            pl.BlockSpec((1, d), lambda i: (0, 0)),
            pl.BlockSpec(table.shape, lambda i: (0, 0), pipeline_mode=pl.Buffered(1)),
        ],
        out_specs=pl.BlockSpec((t, d), lambda i: (i, 0)),
        out_shape=jax.ShapeDtypeStruct(x2d.shape, F32),
        scratch_shapes=[stage, stage, pltpu.VMEM((OCTET, t, LANES), F32), pltpu.VMEM((t, d), F32)],
        compiler_params=_cparams(("arbitrary",)),
        name="peer_values",
    )(e_t, w_t, x2d, g_final.reshape(1, d), table)


def _pack_kernel(w_ref, tc_ref, sc_ref):
    te, d = w_ref.shape
    nchunk = d // (2 * LANES)
    words = []
    for c in range(nchunk):
        lo = w_ref[:, (2 * c) * LANES:(2 * c + 1) * LANES].astype(BF16).astype(F32)
        hi = w_ref[:, (2 * c + 1) * LANES:(2 * c + 2) * LANES].astype(BF16).astype(F32)
        word = (pltpu.bitcast(hi, I32) & jnp.int32(-65536)) | lax.shift_right_logical(pltpu.bitcast(lo, I32), 16)
        tc_ref[pl.ds(c, te, stride=nchunk), :] = word
        words.append(word)
    sc_ref[...] = jnp.concatenate(words, axis=1)


def _pack_pairs(w):
    e, d = w.shape
    te = math.gcd(e, PACK_TILE)
    nchunk = d // (2 * LANES)
    return pl.pallas_call(
        _pack_kernel,
        grid=(e // te,),
        in_specs=[pl.BlockSpec((te, d), lambda i: (i, 0))],
        out_specs=[pl.BlockSpec((te * nchunk, LANES), lambda i: (i, 0)), pl.BlockSpec((te, d // 2), lambda i: (i, 0))],
        out_shape=[jax.ShapeDtypeStruct((e * nchunk, LANES), I32), jax.ShapeDtypeStruct((e, d // 2), I32)],
        compiler_params=_cparams(("arbitrary",)),
        name="peer_pack_table",
    )(w)


def _sc_value_kernel(tab_hbm, idx_hbm, w_hbm, out_hbm, idx_v, w_v, rows_a, rows_b, acc_a, acc_b, row_sem, out_sem,
                     *, slots, first_tok):
    n_tok = out_hbm.shape[0] // (SC_CORES * SC_SUBCORES)
    d = out_hbm.shape[1]
    words = d // 2
    win = slots // 2
    wid = lax.axis_index("s") * SC_CORES + lax.axis_index("c")
    base_tok = wid * n_tok

    def gather(tl, half, buf, sem):
        rows = idx_v.at[pl.ds(tl * slots + half * win, win)]
        return pltpu.make_async_copy(tab_hbm.at[rows], buf, sem)

    def accumulate(tl, half, buf, acc):
        for g in range(words // SC_LANES // SC_GROUP):
            def row(r, sums, g=g):
                slot = jnp.full((SC_LANES,), tl * slots + half * win + r, I32)
                wv = plsc.load_gather(w_v, [slot])
                out = []
                for j in range(SC_GROUP):
                    x = buf[r, pl.ds((g * SC_GROUP + j) * SC_LANES, SC_LANES)]
                    lo = lax.bitcast_convert_type(x << 16, F32)
                    hi = lax.bitcast_convert_type(x & jnp.int32(-65536), F32)
                    out += [sums[2 * j] + wv * lo, sums[2 * j + 1] + wv * hi]
                return tuple(out)
            zeros = tuple(jnp.zeros((SC_LANES,), F32) for _ in range(2 * SC_GROUP))
            sums = lax.fori_loop(0, win, row, zeros)
            for j in range(SC_GROUP):
                chunk, lane = divmod((g * SC_GROUP + j) * SC_LANES, LANES)
                plsc.addupdate(acc.at[pl.ds(2 * chunk * LANES + lane, SC_LANES)], sums[2 * j])
                plsc.addupdate(acc.at[pl.ds((2 * chunk + 1) * LANES + lane, SC_LANES)], sums[2 * j + 1])

    def write_out(acc, tok, sem):
        return pltpu.make_async_copy(acc, out_hbm.at[tok], sem)

    @pl.loop(0, n_tok // SC_BATCH)
    def _(bi):
        tok0 = base_tok + bi * SC_BATCH
        pltpu.sync_copy(idx_hbm.at[pl.ds((first_tok + tok0) * slots, SC_BATCH * slots)], idx_v)
        pltpu.sync_copy(w_hbm.at[pl.ds((first_tok + tok0) * slots, SC_BATCH * slots)], w_v)
        gather(0, 0, rows_a, row_sem.at[0]).start()

        @pl.loop(0, SC_BATCH // 2)
        def _(tp):
            for parity, acc in ((0, acc_a), (1, acc_b)):
                tl = 2 * tp + parity

                @pl.when(bi * SC_BATCH + tl >= 2)
                def _():
                    write_out(acc, tok0, out_sem.at[parity]).wait()
                for k in range(d // SC_LANES):
                    acc[pl.ds(k * SC_LANES, SC_LANES)] = jnp.zeros((SC_LANES,), F32)
                gather(tl, 1, rows_b, row_sem.at[1]).start()
                gather(tl, 0, rows_a, row_sem.at[0]).wait()
                accumulate(tl, 0, rows_a, acc)
                gather(jnp.minimum(tl + 1, SC_BATCH - 1), 0, rows_a, row_sem.at[0]).start()
                gather(tl, 1, rows_b, row_sem.at[1]).wait()
                accumulate(tl, 1, rows_b, acc)
                write_out(acc, tok0 + tl, out_sem.at[parity]).start()

        gather(SC_BATCH - 1, 0, rows_a, row_sem.at[0]).wait()

    write_out(acc_a, base_tok, out_sem.at[0]).wait()
    write_out(acc_b, base_tok, out_sem.at[1]).wait()


def _peer_values_sc(idx, w, table, n_tok, d, slots):
    assert n_tok % (SC_CORES * SC_SUBCORES * SC_BATCH) == 0 and SC_BATCH % 2 == 0
    mesh = plsc.VectorSubcoreMesh(core_axis_name="c", subcore_axis_name="s",
                                  num_cores=SC_CORES, num_subcores=SC_SUBCORES)
    return pl.kernel(
        functools.partial(_sc_value_kernel, slots=slots, first_tok=idx.shape[0] // slots - n_tok),
        out_type=jax.ShapeDtypeStruct((n_tok, d), F32),
        mesh=mesh,
        scratch_types=[
            pltpu.VMEM((SC_BATCH * slots,), I32),
            pltpu.VMEM((SC_BATCH * slots,), F32),
            pltpu.VMEM((slots // 2, d // 2), I32),
            pltpu.VMEM((slots // 2, d // 2), I32),
            pltpu.VMEM((d,), F32),
            pltpu.VMEM((d,), F32),
            pltpu.SemaphoreType.DMA((2,)),
            pltpu.SemaphoreType.DMA((2,)),
        ],
        compiler_params=pltpu.CompilerParams(needs_layout_passes=False),
        name="peer_values_sc",
    )(table, idx, w)


def _sc_score_kernel(tab_hbm, idx_hbm, xe_hbm, xo_hbm, out_hbm, idx_v, xe_v, xo_v, rows_a, rows_b, part_v, a_v, row_sem,
                     *, slots, first_tok):
    n_tok = out_hbm.shape[0] // slots // (SC_CORES * SC_SUBCORES)
    words = rows_a.shape[1]
    win = slots // 2
    n_groups = words // SC_LANES // SC_GROUP
    wid = lax.axis_index("s") * SC_CORES + lax.axis_index("c")
    base_tok = wid * n_tok
    lane_id = lax.iota(I32, SC_LANES)

    def gather(tl, half, buf, sem):
        rows = idx_v.at[pl.ds(tl * slots + half * win, win)]
        return pltpu.make_async_copy(tab_hbm.at[rows], buf, sem)

    def scores(tl, half, buf):
        for r in range(win):
            part_v[pl.ds(r * SC_LANES, SC_LANES)] = jnp.zeros((SC_LANES,), F32)
        for g in range(n_groups):
            xs = []
            for j in range(SC_GROUP):
                at = pl.ds(tl * words + (g * SC_GROUP + j) * SC_LANES, SC_LANES)
                xs += [xe_v[at], xo_v[at]]

            @pl.loop(0, win)
            def _(r, g=g, xs=xs):
                sums = [jnp.zeros((SC_LANES,), F32) for _ in range(4)]
                for j in range(SC_GROUP):
                    x = buf[r, pl.ds((g * SC_GROUP + j) * SC_LANES, SC_LANES)]
                    lo = lax.bitcast_convert_type(x << 16, F32)
                    hi = lax.bitcast_convert_type(x & jnp.int32(-65536), F32)
                    sums[(2 * j) % 4] += lo * xs[2 * j]
                    sums[(2 * j + 1) % 4] += hi * xs[2 * j + 1]
                plsc.addupdate(part_v.at[pl.ds(r * SC_LANES, SC_LANES)], (sums[0] + sums[1]) + (sums[2] + sums[3]))
        for rb in range(win // SC_LANES):
            first = (lane_id + rb * SC_LANES) * SC_LANES
            total = jnp.zeros((SC_LANES,), F32)
            for lane in range(SC_LANES):
                total += plsc.load_gather(part_v, [first + lane])
            a_v[pl.ds(tl * slots + half * win + rb * SC_LANES, SC_LANES)] = total

    @pl.loop(0, n_tok // SC_BATCH)
    def _(bi):
        tok0 = base_tok + bi * SC_BATCH
        pltpu.sync_copy(idx_hbm.at[pl.ds((first_tok + tok0) * slots, SC_BATCH * slots)], idx_v)
        pltpu.sync_copy(xe_hbm.at[pl.ds(tok0 * words, SC_BATCH * words)], xe_v)
        pltpu.sync_copy(xo_hbm.at[pl.ds(tok0 * words, SC_BATCH * words)], xo_v)
        gather(0, 0, rows_a, row_sem.at[0]).start()

        @pl.loop(0, SC_BATCH)
        def _(tl):
            gather(tl, 1, rows_b, row_sem.at[1]).start()
            gather(tl, 0, rows_a, row_sem.at[0]).wait()
            scores(tl, 0, rows_a)
            gather(jnp.minimum(tl + 1, SC_BATCH - 1), 0, rows_a, row_sem.at[0]).start()
            gather(tl, 1, rows_b, row_sem.at[1]).wait()
            scores(tl, 1, rows_b)

        gather(SC_BATCH - 1, 0, rows_a, row_sem.at[0]).wait()
        pltpu.sync_copy(a_v, out_hbm.at[pl.ds(tok0 * slots, SC_BATCH * slots)])


def _peer_scores_sc(idx, xe, xo, table, n_tok, slots):
    words = table.shape[1]
    assert n_tok % (SC_CORES * SC_SUBCORES * SC_BATCH) == 0
    mesh = plsc.VectorSubcoreMesh(core_axis_name="c", subcore_axis_name="s",
                                  num_cores=SC_CORES, num_subcores=SC_SUBCORES)
    return pl.kernel(
        functools.partial(_sc_score_kernel, slots=slots, first_tok=idx.shape[0] // slots - n_tok),
        out_type=jax.ShapeDtypeStruct((n_tok * slots,), F32),
        mesh=mesh,
        scratch_types=[
            pltpu.VMEM((SC_BATCH * slots,), I32),
            pltpu.VMEM((SC_BATCH * words,), F32),
            pltpu.VMEM((SC_BATCH * words,), F32),
            pltpu.VMEM((slots // 2, words), I32),
            pltpu.VMEM((slots // 2, words), I32),
            pltpu.VMEM((slots // 2 * SC_LANES,), F32),
            pltpu.VMEM((SC_BATCH * slots,), F32),
            pltpu.SemaphoreType.DMA((2,)),
        ],
        compiler_params=pltpu.CompilerParams(needs_layout_passes=False),
        name="peer_scores_sc",
    )(table, idx, xe, xo)


def _split_norm_kernel(x_ref, g_ref, xe_ref, xo_ref):
    d = x_ref.shape[1]
    xn = _rms(x_ref[...], g_ref[...])
    nchunk = d // (2 * LANES)
    xe_ref[...] = jnp.concatenate([xn[:, (2 * c) * LANES:(2 * c + 1) * LANES] for c in range(nchunk)], axis=1)
    xo_ref[...] = jnp.concatenate([xn[:, (2 * c + 1) * LANES:(2 * c + 2) * LANES] for c in range(nchunk)], axis=1)


def _split_norm(x2d, g, n_rows, tm):
    d = x2d.shape[1]
    first = (x2d.shape[0] - n_rows) // tm
    half = jax.ShapeDtypeStruct((n_rows, d // 2), F32)
    return pl.pallas_call(
        _split_norm_kernel,
        grid=(n_rows // tm,),
        in_specs=[pl.BlockSpec((tm, d), lambda i: (first + i, 0)), pl.BlockSpec((1, d), lambda i: (0, 0))],
        out_specs=[pl.BlockSpec((tm, d // 2), lambda i: (i, 0))] * 2,
        out_shape=[half, half],
        compiler_params=_cparams(("arbitrary",)),
        name="peer_split_norm",
    )(x2d, g.reshape(1, d))


def _score_finish_kernel(w_hbm, a_ref, gate_ref, w_ref):
    del w_hbm
    a = a_ref[0]
    gelu = 0.5 * a * (1.0 + lax.erf(a * (2.0 ** -0.5)))
    w_ref[0] = gate_ref[0].T * gelu


def _score_finish(w_full, a_sc, gate_t):
    nt, t, slots = w_full.shape
    n_sc = a_sc.shape[0]
    first = nt - n_sc
    return pl.pallas_call(
        _score_finish_kernel,
        grid=(n_sc,),
        in_specs=[
            pl.BlockSpec(memory_space=pl.ANY),
            pl.BlockSpec((1, t, slots), lambda i: (i, 0, 0)),
            pl.BlockSpec((1, slots, t), lambda i: (first + i, 0, 0)),
        ],
        out_specs=pl.BlockSpec((1, t, slots), lambda i: (first + i, 0, 0)),
        out_shape=jax.ShapeDtypeStruct(w_full.shape, F32),
        input_output_aliases={0: 0},
        compiler_params=_cparams(("arbitrary",)),
        name="peer_score_finish",
    )(w_full, a_sc, gate_t)


def _residual_norm_kernel(out_hbm, x_ref, p_ref, gf_ref, o_ref, *, final_norm):
    del out_hbm
    y = x_ref[...] + p_ref[...]
    if final_norm:
        y = _rms(y, gf_ref[...])
    o_ref[...] = y


def _residual_norm(out_full, x2d, p2d, g_final, final_norm, tm):
    t, d = p2d.shape
    first = (x2d.shape[0] - t) // tm
    return pl.pallas_call(
        functools.partial(_residual_norm_kernel, final_norm=final_norm),
        grid=(t // tm,),
        in_specs=[
            pl.BlockSpec(memory_space=pl.ANY),
            pl.BlockSpec((tm, d), lambda i: (first + i, 0)),
            pl.BlockSpec((tm, d), lambda i: (i, 0)),
            pl.BlockSpec((1, d), lambda i: (0, 0)),
        ],
        out_specs=pl.BlockSpec((tm, d), lambda i: (first + i, 0)),
        out_shape=jax.ShapeDtypeStruct(x2d.shape, F32),
        input_output_aliases={0: 0},
        compiler_params=_cparams(("arbitrary",)),
        name="peer_residual",
    )(out_full, x2d, p2d, g_final.reshape(1, d))


def _peer_ffn(x, g, w_q, k1, k2, u_emb, v_emb, g_final, final_norm):
    b, s, d = x.shape
    x2d = x.reshape(b * s, d)
    e_slot, gate_slot, e_tok = _peer_router(x2d, g, w_q, k1, k2)
    (u_tab, u_tab_sc), (v_tab, v_tab_sc) = _pack_pairs(u_emb), _pack_pairs(v_emb)
    nt, slots, t = e_slot.shape
    sc_quantum = SC_CORES * SC_SUBCORES * SC_BATCH // t
    ns_score = nt * SC_SCORE_PERCENT // 100 // sc_quantum * sc_quantum
    ns_value = nt * SC_VALUE_PERCENT // 100 // sc_quantum * sc_quantum
    idx_tok = e_tok.reshape(-1)

    w_tok = _peer_scores(e_slot, gate_slot, x2d, g, u_tab, nt - ns_score)
    if ns_score:
        xe, xo = _split_norm(x2d, g, ns_score * t, math.gcd(TOKEN_TILE, ns_score * t))
        a_sc = _peer_scores_sc(idx_tok, xe.reshape(-1), xo.reshape(-1), u_tab_sc, ns_score * t, slots)
        w_tok = _score_finish(w_tok, a_sc.reshape(ns_score, t, slots), gate_slot)
    out = _peer_values(e_slot, w_tok, x2d, v_tab, g_final, final_norm, nt - ns_value)
    if ns_value:
        peer_sc = _peer_values_sc(idx_tok, w_tok.reshape(-1), v_tab_sc, ns_value * t, d, slots)
        out = _residual_norm(out, x2d, peer_sc, g_final, final_norm, math.gcd(TOKEN_TILE, ns_value * t))
    return out.reshape(b, s, d)


def _qkv_kernel(x_ref, g_ref, wqk_hi_ref, wqk_lo_ref, wv_ref, cos_ref, sin_ref,
                qt_ref, k0_ref, k1_ref, vt_ref, km_ref):
    tm, d = x_ref.shape[1], x_ref.shape[2]
    hn = _rms(x_ref[0], g_ref[...])
    hh, hl = _split(hn)
    qk = _dot3(hh, hl, wqk_hi_ref[...], wqk_lo_ref[...])
    v = _dot(hh, wv_ref[...])
    cos = jnp.concatenate([cos_ref[...]] * (d // LANES), axis=1)
    sin = jnp.concatenate([sin_ref[...]] * (d // LANES), axis=1)
    lane = lax.broadcasted_iota(I32, (tm, d), 1)
    first_half = (lane % HEAD_DIM) < (HEAD_DIM // 2)

    def rope(a):
        rot = jnp.where(first_half, pltpu.roll(a, d - HEAD_DIM // 2, 1), pltpu.roll(a, HEAD_DIM // 2, 1))
        return a * cos + rot * sin

    q = rope(qk[:, :d])
    k = rope(qk[:, d:])
    qt_ref[0] = q.T
    vt_ref[0] = v.T.astype(BF16)
    nb = tm // MOBA_BLOCK
    km_ref[0, 0] = jnp.mean(k.reshape(nb, MOBA_BLOCK, d), axis=1)
    row = lax.broadcasted_iota(I32, (tm, d), 0)
    block = (pl.program_id(1) * tm + row) // MOBA_BLOCK
    pair_lane = lane % LANES
    kb = k.astype(BF16)
    k0_ref[0] = jnp.where(pair_lane < HEAD_DIM, kb, jnp.where(pair_lane - HEAD_DIM == block, 1.0, 0.0).astype(BF16))
    k1_ref[0] = jnp.where(pair_lane >= HEAD_DIM, kb, jnp.where(pair_lane == block, 1.0, 0.0).astype(BF16))


def _qkv_rope(x, g, w_qkv, tm):
    b, s, d = x.shape
    half = HEAD_DIM // 2
    inv = ROPE_THETA ** (-jnp.arange(half, dtype=F32) / half)
    ang = jnp.arange(s).astype(F32)[:, None] * inv[None, :]
    cos, sin = jnp.cos(ang), jnp.sin(ang)
    cos128 = jnp.tile(jnp.concatenate([cos, cos], axis=1), (1, LANES // HEAD_DIM))
    sin128 = jnp.tile(jnp.concatenate([-sin, sin], axis=1), (1, LANES // HEAD_DIM))
    wqk_hi, wqk_lo = _split(w_qkv[:, :2 * d])
    wv = w_qkv[:, 2 * d:].astype(BF16)
    nb = tm // MOBA_BLOCK
    const = lambda i, j: (0, 0)
    qt, k0, k1, vt, km = pl.pallas_call(
        _qkv_kernel,
        grid=(b, s // tm),
        in_specs=[
            pl.BlockSpec((1, tm, d), lambda i, j: (i, j, 0)),
            pl.BlockSpec((1, d), const),
            pl.BlockSpec((d, 2 * d), const),
            pl.BlockSpec((d, 2 * d), const),
            pl.BlockSpec((d, d), const),
            pl.BlockSpec((tm, LANES), lambda i, j: (j, 0)),
            pl.BlockSpec((tm, LANES), lambda i, j: (j, 0)),
        ],
        out_specs=[
            pl.BlockSpec((1, d, tm), lambda i, j: (i, 0, j)),
            pl.BlockSpec((1, tm, d), lambda i, j: (i, j, 0)),
            pl.BlockSpec((1, tm, d), lambda i, j: (i, j, 0)),
            pl.BlockSpec((1, d, tm), lambda i, j: (i, 0, j)),
            pl.BlockSpec((1, 1, nb, d), lambda i, j: (i, j, 0, 0)),
        ],
        out_shape=[
            jax.ShapeDtypeStruct((b, d, s), F32),
            jax.ShapeDtypeStruct((b, s, d), BF16),
            jax.ShapeDtypeStruct((b, s, d), BF16),
            jax.ShapeDtypeStruct((b, d, s), BF16),
            jax.ShapeDtypeStruct((b, s // tm, nb, d), F32),
        ],
        compiler_params=_cparams(("arbitrary", "arbitrary")),
        name="qkv_rope",
    )(x, g.reshape(1, d), wqk_hi, wqk_lo, wv, cos128, sin128)
    return qt, (k0, k1), vt, km.reshape(b, s // MOBA_BLOCK, d)


def _moba_kernel(qt_ref, k0_ref, k1_ref, vt_ref, km_ref, o_ref, sa_ref, sb_ref):
    bs = MOBA_BLOCK
    nb = km_ref.shape[1]
    n_heads = LANES // HEAD_DIM
    k_refs = (k0_ref, k1_ref)
    j = pl.program_id(2)
    qt = qt_ref[0]
    km = km_ref[0]
    scale = HEAD_DIM ** -0.5 * 1.4426950408889634
    lane_km = lax.broadcasted_iota(I32, (nb, LANES), 1)
    blk = lax.broadcasted_iota(I32, (nb, bs), 0)
    zeros_pad = jnp.zeros((LANES - HEAD_DIM - nb, bs), F32)
    qh, ql = _split(qt)

    own = pl.ds(pl.multiple_of(j * bs, bs), bs)
    v_own = vt_ref[0, :, own]
    krow = lax.broadcasted_iota(I32, (bs, bs), 0)
    qcol = lax.broadcasted_iota(I32, (bs, bs), 1)

    q_augs, state = [], []
    for hh in range(n_heads):
        head_lo = hh * HEAD_DIM
        in_head_km = (lane_km >= head_lo) & (lane_km < head_lo + HEAD_DIM)
        kmh, kml = _split(jnp.where(in_head_km, km, 0.0))
        gate = _dot3(kmh, kml, qh, ql)
        valid = blk < j
        gate = jnp.where(valid, gate, -jnp.inf)
        sel = jnp.zeros((nb, bs), F32)
        for _ in range(MOBA_TOPK):
            m = jnp.max(gate, axis=0, keepdims=True)
            i = jnp.min(jnp.where(gate == m, blk, nb), axis=0, keepdims=True)
            pick = blk == i
            sel = jnp.where(pick, 1.0, sel)
            gate = jnp.where(pick, -jnp.inf, gate)
        bias_t = jnp.where((sel > 0.0) & valid, 0.0, MASK_NEG)
        q_head = qt[head_lo:head_lo + HEAD_DIM, :] * scale
        no_bias = jnp.zeros((LANES - HEAD_DIM, bs), F32)
        if hh == 0:
            q_aug = jnp.concatenate([q_head, bias_t, zeros_pad], axis=0)
            q_own = jnp.concatenate([q_head, no_bias], axis=0)
        else:
            q_aug = jnp.concatenate([bias_t, zeros_pad, q_head], axis=0)
            q_own = jnp.concatenate([no_bias, q_head], axis=0)
        q_augs.append(q_aug.astype(BF16))

        s_own = jnp.where(krow <= qcol, _dot(k_refs[hh][0, own, :], q_own.astype(BF16)), -1e30)
        m0 = jnp.max(s_own, axis=0, keepdims=True)
        p0 = jnp.exp2(s_own - m0)
        state += [m0, jnp.sum(p0, axis=0, keepdims=True), _dot(v_own, p0.astype(BF16))]

    chunk = KV_CHUNK * bs
    last_chunk = nb // KV_CHUNK - 1

    def score_chunk(c, s_ref):
        rows = pl.ds(pl.multiple_of(jnp.minimum(c, last_chunk) * chunk, chunk), chunk)
        for hh in range(n_heads):
            s_ref[hh] = _dot(k_refs[hh][0, rows, :], q_augs[hh])

    def attend(c, s_ref, state):
        vn = vt_ref[0, :, pl.ds(pl.multiple_of(c * chunk, chunk), chunk)]
        new_state = []
        for hh in range(n_heads):
            m, l, acc = state[3 * hh:3 * hh + 3]
            s = s_ref[hh]
            m_new = jnp.maximum(m, jnp.max(s, axis=0, keepdims=True))
            alpha = jnp.exp2(m - m_new)
            p = jnp.exp2(s - m_new)
            l = alpha * l + jnp.sum(p, axis=0, keepdims=True)
            acc = alpha * acc + _dot(vn, p.astype(BF16))
            new_state += [m_new, l, acc]
        return tuple(new_state)

    score_chunk(0, sa_ref)

    def body(i, state):
        score_chunk(2 * i + 1, sb_ref)
        state = attend(2 * i, sa_ref, state)
        score_chunk(2 * i + 2, sa_ref)
        return attend(2 * i + 1, sb_ref, state)

    state = lax.fori_loop(0, (j + 2 * KV_CHUNK - 1) // (2 * KV_CHUNK), body, tuple(state))
    halves = []
    for hh in range(n_heads):
        _, l, acc = state[3 * hh:3 * hh + 3]
        halves.append((acc / l)[hh * HEAD_DIM:(hh + 1) * HEAD_DIM, :])
    o_ref[0] = jnp.concatenate(halves, axis=0).T


def _moba_attention(qt, k01, vt, km):
    b, d, s = qt.shape
    nb = s // MOBA_BLOCK
    assert nb % (2 * KV_CHUNK) == 0
    return pl.pallas_call(
        _moba_kernel,
        grid=(b, d // LANES, nb),
        in_specs=[
            pl.BlockSpec((1, LANES, MOBA_BLOCK), lambda i, h, j: (i, h, j)),
            pl.BlockSpec((1, s, LANES), lambda i, h, j: (i, 0, h)),
            pl.BlockSpec((1, s, LANES), lambda i, h, j: (i, 0, h)),
            pl.BlockSpec((1, LANES, s), lambda i, h, j: (i, h, 0)),
            pl.BlockSpec((1, nb, LANES), lambda i, h, j: (i, 0, h)),
        ],
        out_specs=pl.BlockSpec((1, MOBA_BLOCK, LANES), lambda i, h, j: (i, j, h)),
        out_shape=jax.ShapeDtypeStruct((b, s, d), F32),
        scratch_shapes=[pltpu.VMEM((LANES // HEAD_DIM, KV_CHUNK * MOBA_BLOCK, MOBA_BLOCK), F32)] * 2,
        compiler_params=_cparams(("arbitrary", "arbitrary", "arbitrary")),
        name="moba_attention",
    )(qt, *k01, vt, km)


def _proj_residual_kernel(x_ref, a_ref, w_ref, o_ref):
    o_ref[...] = x_ref[...] + _dot(a_ref[...].astype(BF16), w_ref[...])


def _proj_residual(x2d, a2d, w, tm):
    t, d = x2d.shape
    return pl.pallas_call(
        _proj_residual_kernel,
        grid=(t // tm,),
        in_specs=[
            pl.BlockSpec((tm, d), lambda i: (i, 0)),
            pl.BlockSpec((tm, a2d.shape[1]), lambda i: (i, 0)),
            pl.BlockSpec(w.shape, lambda i: (0, 0)),
        ],
        out_specs=pl.BlockSpec((tm, d), lambda i: (i, 0)),
        out_shape=jax.ShapeDtypeStruct((t, d), F32),
        compiler_params=_cparams(("arbitrary",)),
        name="attn_out_proj",
    )(x2d, a2d, w.astype(BF16))


def _moba_mixer(x, g, w_qkv, w_o):
    b, s, d = x.shape
    tm = min(TOKEN_TILE, s)
    qt, k, vt, km = _qkv_rope(x, g, w_qkv, tm)
    attn = _moba_attention(qt, k, vt, km)
    return _proj_residual(x.reshape(b * s, d), attn.reshape(b * s, d), w_o, tm).reshape(b, s, d)


def kernel(x, norm_mix, norm_ffn, conv_w_in, conv_w, conv_w_out, attn_w_qkv, attn_w_o,
           peer_w_q, peer_k1, peer_k2, peer_u, peer_v, norm_final):
    depth = norm_mix.shape[0]
    tm = min(TOKEN_TILE, x.shape[1])
    for i in range(depth):
        j = i // 2
        if i % 2 == 0:
            x = _conv_mixer(x, norm_mix[i], conv_w_in[j], conv_w[j], conv_w_out[j], tm)
        else:
            x = _moba_mixer(x, norm_mix[i], attn_w_qkv[j], attn_w_o[j])
        x = _peer_ffn(x, norm_ffn[i], peer_w_q[i], peer_k1[i], peer_k2[i], peer_u[i], peer_v[i],
                      norm_final, final_norm=(i == depth - 1))
    return x
```

```python
import functools
import math

import jax
import jax.numpy as jnp
from jax import lax
from jax.experimental import pallas as pl
from jax.experimental.pallas import tpu as pltpu
from jax.experimental.pallas import tpu_sc as plsc

F32 = jnp.float32
BF16 = jnp.bfloat16
I32 = jnp.int32

RMS_EPS = 1e-6
N_HEADS = 16
HEAD_DIM = 64
MOBA_BLOCK = 256
MOBA_TOPK = 3
ROPE_THETA = 10000.0
PEER_HEADS = 8
PEER_NKEYS = 128
PEER_HALF = 64
PEER_TOPK = 16

LANES = 128
SUBLANES = 8
VMEM_LIMIT = 56 * 1024 * 1024
MASK_NEG = -1e9

TOKEN_TILE = 512
PEER_TILE = 128
ROUTER_TILES = 2
PACK_TILE = 512
OCTET = 8
SC_CORES = 2
SC_SUBCORES = 16
SC_LANES = 16
SC_BATCH = 16
SC_GROUP = 8
SC_SCORE_PERCENT = 34
SC_VALUE_PERCENT = 47
KV_CHUNK = 2


def _cparams(sem):
    return pltpu.CompilerParams(dimension_semantics=sem, vmem_limit_bytes=VMEM_LIMIT)


def _rms(x, g):
    ms = jnp.mean(x * x, axis=-1, keepdims=True)
    return x * lax.rsqrt(ms + RMS_EPS) * g


def _split(a):
    hi = a.astype(BF16)
    lo = (a - hi.astype(F32)).astype(BF16)
    return hi, lo


def _dot(a, b):
    return lax.dot_general(a, b, (((1,), (0,)), ((), ())), preferred_element_type=F32)


def _dot_t(a, b):
    return lax.dot_general(a, b, (((1,), (1,)), ((), ())), preferred_element_type=F32)


def _dot3(a_hi, a_lo, b_hi, b_lo):
    return _dot(a_hi, b_hi) + _dot(a_lo, b_hi) + _dot(a_hi, b_lo)


def _dot3_t(a_hi, a_lo, b_hi, b_lo):
    return _dot_t(a_hi, b_hi) + _dot_t(a_lo, b_hi) + _dot_t(a_hi, b_lo)


def _conv_mixer_kernel(x_ref, g_ref, win_ref, cw_ref, wout_ref, o_ref, ubuf_ref):
    tm, d = x_ref.shape[1], x_ref.shape[2]

    @pl.when(pl.program_id(1) == 0)
    def _():
        ubuf_ref[0:SUBLANES, :] = jnp.zeros((SUBLANES, d), F32)

    x = x_ref[0]
    hn = _rms(x, g_ref[...]).astype(BF16)
    bcz = _dot(hn, win_ref[...])
    b_gate, c_gate, z = bcz[:, :d], bcz[:, d:2 * d], bcz[:, 2 * d:]
    u = c_gate * z
    ubuf_ref[SUBLANES:SUBLANES + tm, :] = u
    u1 = ubuf_ref[SUBLANES - 1:SUBLANES - 1 + tm, :]
    u2 = ubuf_ref[SUBLANES - 2:SUBLANES - 2 + tm, :]
    cw = cw_ref[...]
    u_conv = cw[0:1, :] * u2 + cw[1:2, :] * u1 + cw[2:3, :] * u
    ubuf_ref[0:SUBLANES, :] = u[tm - SUBLANES:tm, :]
    y = (b_gate * u_conv).astype(BF16)
    o_ref[0] = x + _dot(y, wout_ref[...])


def _conv_mixer(x, g, w_in, conv_w, w_out, tm):
    b, s, d = x.shape
    return pl.pallas_call(
        _conv_mixer_kernel,
        grid=(b, s // tm),
        in_specs=[
            pl.BlockSpec((1, tm, d), lambda i, j: (i, j, 0)),
            pl.BlockSpec((1, d), lambda i, j: (0, 0)),
            pl.BlockSpec((d, 3 * d), lambda i, j: (0, 0)),
            pl.BlockSpec((3, d), lambda i, j: (0, 0)),
            pl.BlockSpec((d, d), lambda i, j: (0, 0)),
        ],
        out_specs=pl.BlockSpec((1, tm, d), lambda i, j: (i, j, 0)),
        out_shape=jax.ShapeDtypeStruct((b, s, d), F32),
        scratch_shapes=[pltpu.VMEM((tm + SUBLANES, d), F32)],
        compiler_params=_cparams(("arbitrary", "arbitrary")),
        name="conv_mixer",
    )(x, g.reshape(1, d), w_in.astype(BF16), conv_w, w_out.astype(BF16))


def _topk_rows(s, k, order=None, payload=None):
    if order is None:
        order = lax.broadcasted_iota(I32, s.shape, 0)
    big = jnp.iinfo(jnp.int32).max
    vals, outs = [], []
    for _ in range(k):
        m = jnp.max(s, axis=0, keepdims=True)
        i = jnp.min(jnp.where(s == m, order, big), axis=0, keepdims=True)
        pick = order == i
        vals.append(m)
        if payload is None:
            outs.append(i)
        else:
            outs.append(jnp.max(jnp.where(pick, payload, -1), axis=0, keepdims=True))
        s = jnp.where(pick, -jnp.inf, s)
    return jnp.concatenate(vals, axis=0), jnp.concatenate(outs, axis=0)


def _staircase(kk):
    groups = []
    for a in range(2):
        for b0 in range(0, kk // (a + 1), SUBLANES):
            groups.append((a, 0, b0, 1, lambda j, a=a, b0=b0: (a + 1) * (b0 + j + 1) <= kk))
    for b in range(kk // 3):
        for a0 in range(0, kk // (b + 1), SUBLANES):
            groups.append((a0, 1, b, 0, lambda j, a0=a0, b=b: (a0 + j >= 2) & ((a0 + j + 1) * (b + 1) <= kk)))
    return groups


def _staircase_topk(v1, i1, v2, i2, nk):
    kk, t = v1.shape
    j = lax.broadcasted_iota(I32, (SUBLANES, t), 0)

    def rows(x, x0, step):
        if step == 0:
            return jnp.broadcast_to(x[x0:x0 + 1, :], (SUBLANES, t))
        return x[x0:x0 + SUBLANES, :]

    cand, order, cidx = [], [], []
    for a0, a_step, b0, b_step, valid in _staircase(kk):
        ok = valid(j)
        cand.append(jnp.where(ok, rows(v1, a0, a_step) + rows(v2, b0, b_step), -jnp.inf))
        order.append(jnp.where(ok, (a0 + j * a_step) * kk + (b0 + j * b_step), jnp.iinfo(jnp.int32).max - 1))
        cidx.append(rows(i1, a0, a_step) * nk + rows(i2, b0, b_step))
    cat = lambda xs: jnp.concatenate(xs, axis=0)
    return _topk_rows(cat(cand), kk, order=cat(order), payload=cat(cidx))


def _router_kernel(x_ref, g_ref, wq_hi_ref, wq_lo_ref, k1_hi_ref, k1_lo_ref, k2_hi_ref, k2_lo_ref,
                   e_ref, gate_ref, e_tok_ref):
    table_rows = x_ref.shape[1] // (2 * LANES)
    tile = e_ref.shape[2]
    xn = _rms(x_ref[...], g_ref[...])
    xh, xl = _split(xn)
    q = _dot3(xh, xl, wq_hi_ref[...], wq_lo_ref[...])
    nk = k1_hi_ref.shape[0]
    kk = PEER_TOPK
    ids = [[] for _ in range(ROUTER_TILES)]
    for h in range(PEER_HEADS):
        for sub in range(ROUTER_TILES):
            qh, ql = _split(q[sub * tile:(sub + 1) * tile, h * LANES:(h + 1) * LANES])
            s1 = _dot3_t(k1_hi_ref[...], k1_lo_ref[...], qh, ql)
            s2 = _dot3_t(k2_hi_ref[...], k2_lo_ref[...], qh, ql)
            v1, i1 = _topk_rows(s1, PEER_TOPK)
            v2, i2 = _topk_rows(s2, PEER_TOPK)
            top_s, e_idx = _staircase_topk(v1, i1, v2, i2, nk)
            p = jnp.exp(top_s - top_s[0:1, :])
            gate = p / jnp.sum(p, axis=0, keepdims=True)
            e_ref[sub, h * kk:(h + 1) * kk, :] = e_idx * table_rows
            gate_ref[sub, h * kk:(h + 1) * kk, :] = gate
            ids[sub].append(e_idx)
    for sub in range(ROUTER_TILES):
        e_tok_ref[sub] = jnp.concatenate(ids[sub], axis=0).T


def _pad_keys(k, lo):
    nk, half = k.shape
    out = jnp.zeros((nk, LANES), F32)
    return out.at[:, lo:lo + half].set(k)


def _peer_router(x2d, g, w_q, k1, k2):
    t, d = x2d.shape
    nt = t // PEER_TILE
    slots = PEER_HEADS * PEER_TOPK
    wq_hi, wq_lo = _split(w_q)
    k1_hi, k1_lo = _split(_pad_keys(k1, 0))
    k2_hi, k2_lo = _split(_pad_keys(k2, PEER_HALF))
    const = lambda i: (0, 0)
    kspec = pl.BlockSpec(k1_hi.shape, const)
    return pl.pallas_call(
        _router_kernel,
        grid=(nt // ROUTER_TILES,),
        in_specs=[
            pl.BlockSpec((ROUTER_TILES * PEER_TILE, d), lambda i: (i, 0)),
            pl.BlockSpec((1, d), const),
            pl.BlockSpec(wq_hi.shape, const),
            pl.BlockSpec(wq_lo.shape, const),
            kspec, kspec, kspec, kspec,
        ],
        out_specs=[
            pl.BlockSpec((ROUTER_TILES, slots, PEER_TILE), lambda i: (i, 0, 0)),
            pl.BlockSpec((ROUTER_TILES, slots, PEER_TILE), lambda i: (i, 0, 0)),
            pl.BlockSpec((ROUTER_TILES, PEER_TILE, slots), lambda i: (i, 0, 0)),
        ],
        out_shape=[
            jax.ShapeDtypeStruct((nt, slots, PEER_TILE), I32),
            jax.ShapeDtypeStruct((nt, slots, PEER_TILE), F32),
            jax.ShapeDtypeStruct((nt, PEER_TILE, slots), I32),
        ],
        compiler_params=_cparams(("arbitrary",)),
        name="peer_router",
    )(x2d, g.reshape(1, d), wq_hi, wq_lo, k1_hi, k1_lo, k2_hi, k2_lo)


def _gather_octet(e_ref, tab_ref, stage_ref, octet, nsub):
    t = e_ref.shape[2]
    for k in range(OCTET):
        for tok in range(t):
            off = e_ref[0, octet * OCTET + k, tok]
            stage_ref[k * (t // SUBLANES) + tok // SUBLANES,
                      pl.ds(tok % SUBLANES, nsub, stride=SUBLANES), :] = tab_ref[pl.ds(off, nsub), :]


def _octet_pipeline(n_octets, gather, consume, stage_a, stage_b):
    gather(0, stage_a)

    def pair(p, carry):
        gather(2 * p + 1, stage_b)
        consume(2 * p, stage_a)
        gather(jnp.minimum(2 * p + 2, n_octets - 1), stage_a)
        consume(2 * p + 1, stage_b)
        return carry
    lax.fori_loop(0, n_octets // 2, pair, 0)


def _score_kernel(e_ref, x_ref, g_ref, gate_ref, tab_ref, w_ref, stage_a, stage_b, a_ref):
    t, d = x_ref.shape
    nchunk = d // (2 * LANES)
    slots = e_ref.shape[1]
    rows = OCTET * t

    xn = _rms(x_ref[...], g_ref[...]).astype(BF16)
    x_even = jnp.concatenate([xn[:, (2 * c) * LANES:(2 * c + 1) * LANES] for c in range(nchunk)], axis=1)
    x_odd = jnp.concatenate([xn[:, (2 * c + 1) * LANES:(2 * c + 2) * LANES] for c in range(nchunk)], axis=1)
    rhs = jnp.concatenate([x_even, x_odd], axis=0)

    ri = lax.broadcasted_iota(I32, (2 * t, 2 * t), 0)
    ci = lax.broadcasted_iota(I32, (2 * t, 2 * t), 1)
    diag = (ci == (ri % 2) * t + ri // 2).astype(F32)

    def consume(o, stage_ref):
        planes = []
        for c in range(nchunk):
            plane = stage_ref[:, c * SUBLANES:(c + 1) * SUBLANES, :].reshape(rows, LANES)
            planes.append(pltpu.bitcast(plane, BF16))
        lhs = jnp.concatenate(planes, axis=1)
        prod = _dot_t(lhs, rhs)
        prod = prod.reshape(OCTET, 2 * t, 2 * t) * diag[None]
        a2 = jnp.sum(prod, axis=1)
        a_ref[pl.ds(pl.multiple_of(o * OCTET, OCTET), OCTET), :] = a2[:, :t] + a2[:, t:]

    gather = lambda o, stage_ref: _gather_octet(e_ref, tab_ref, stage_ref, o, nchunk)
    _octet_pipeline(slots // OCTET, gather, consume, stage_a, stage_b)

    a = a_ref[...]
    gelu = 0.5 * a * (1.0 + lax.erf(a * (2.0 ** -0.5)))
    w_ref[0] = (gate_ref[0] * gelu).T


def _peer_scores(e_t, gate_t, x2d, g, table, n_tiles):
    nt, slots, t = e_t.shape
    d = x2d.shape[1]
    stage = pltpu.VMEM((OCTET * t // SUBLANES, d // (2 * LANES) * SUBLANES, LANES), I32)
    return pl.pallas_call(
        _score_kernel,
        grid=(n_tiles,),
        in_specs=[
            pl.BlockSpec((1, slots, t), lambda i: (i, 0, 0), memory_space=pltpu.SMEM),
            pl.BlockSpec((t, d), lambda i: (i, 0)),
            pl.BlockSpec((1, d), lambda i: (0, 0)),
            pl.BlockSpec((1, slots, t), lambda i: (i, 0, 0)),
            pl.BlockSpec(table.shape, lambda i: (0, 0), pipeline_mode=pl.Buffered(1)),
        ],
        out_specs=pl.BlockSpec((1, t, slots), lambda i: (i, 0, 0)),
        out_shape=jax.ShapeDtypeStruct((nt, t, slots), F32),
        scratch_shapes=[stage, stage, pltpu.VMEM((slots, t), F32)],
        compiler_params=_cparams(("arbitrary",)),
        name="peer_scores",
    )(e_t, x2d, g.reshape(1, d), gate_t, table)


def _value_kernel(e_ref, w_ref, x_ref, gf_ref, tab_ref, o_ref, stage_a, stage_b, wb_ref, acc_ref, *, final_norm):
    t, d = x_ref.shape
    nchunk = d // (2 * LANES)
    slots = e_ref.shape[1]
    groups = t // SUBLANES

    acc_ref[...] = x_ref[...]

    def consume(o, stage_ref):
        w_oct = pltpu.roll(w_ref[0], lax.rem(slots - o * OCTET, slots), 1)
        for k in range(OCTET):
            wb_ref[k] = jnp.broadcast_to(w_oct[:, k:k + 1], (t, LANES))
        for c in range(nchunk):
            lo_cols = slice(2 * c * LANES, (2 * c + 1) * LANES)
            hi_cols = slice((2 * c + 1) * LANES, (2 * c + 2) * LANES)
            acc_lo, acc_hi = acc_ref[:, lo_cols], acc_ref[:, hi_cols]
            for k in range(OCTET):
                words = stage_ref[k * groups:(k + 1) * groups, c * SUBLANES:(c + 1) * SUBLANES, :]
                words = words.reshape(t, LANES)
                lo = pltpu.bitcast(words << 16, F32)
                hi = pltpu.bitcast(words & jnp.int32(-65536), F32)
                acc_lo = acc_lo + wb_ref[k] * lo
                acc_hi = acc_hi + wb_ref[k] * hi
            acc_ref[:, lo_cols] = acc_lo
            acc_ref[:, hi_cols] = acc_hi

    gather = lambda o, stage_ref: _gather_octet(e_ref, tab_ref, stage_ref, o, nchunk)
    _octet_pipeline(slots // OCTET, gather, consume, stage_a, stage_b)

    y = acc_ref[...]
    if final_norm:
        y = _rms(y, gf_ref[...])
    o_ref[...] = y


def _peer_values(e_t, w_t, x2d, table, g_final, final_norm, n_tiles):
    nt, slots, t = e_t.shape
    d = x2d.shape[1]
    stage = pltpu.VMEM((OCTET * t // SUBLANES, d // (2 * LANES) * SUBLANES, LANES), I32)
    return pl.pallas_call(
        functools.partial(_value_kernel, final_norm=final_norm),
        grid=(n_tiles,),
        in_specs=[
            pl.BlockSpec((1, slots, t), lambda i: (i, 0, 0), memory_space=pltpu.SMEM),
            pl.BlockSpec((1, t, slots), lambda i: (i, 0, 0)),
            pl.BlockSpec((t, d), lambda i: (i, 0)),
            pl.BlockSpec((1, d), lambda i: (0, 0)),
            pl.BlockSpec(table.shape, lambda i: (0, 0), pipeline_mode=pl.Buffered(1)),
        ],
        out_specs=pl.BlockSpec((t, d), lambda i: (i, 0)),
        out_shape=jax.ShapeDtypeStruct(x2d.shape, F32),
        scratch_shapes=[stage, stage, pltpu.VMEM((OCTET, t, LANES), F32), pltpu.VMEM((t, d), F32)],
        compiler_params=_cparams(("arbitrary",)),
        name="peer_values",
    )(e_t, w_t, x2d, g_final.reshape(1, d), table)


def _pack_kernel(w_ref, tc_ref, sc_ref):
    te, d = w_ref.shape
    nchunk = d // (2 * LANES)
    words = []
    for c in range(nchunk):
        lo = w_ref[:, (2 * c) * LANES:(2 * c + 1) * LANES].astype(BF16).astype(F32)
        hi = w_ref[:, (2 * c + 1) * LANES:(2 * c + 2) * LANES].astype(BF16).astype(F32)
        word = (pltpu.bitcast(hi, I32) & jnp.int32(-65536)) | lax.shift_right_logical(pltpu.bitcast(lo, I32), 16)
        tc_ref[pl.ds(c, te, stride=nchunk), :] = word
        words.append(word)
    sc_ref[...] = jnp.concatenate(words, axis=1)


def _pack_pairs(w):
    e, d = w.shape
    te = math.gcd(e, PACK_TILE)
    nchunk = d // (2 * LANES)
    return pl.pallas_call(
        _pack_kernel,
        grid=(e // te,),
        in_specs=[pl.BlockSpec((te, d), lambda i: (i, 0))],
        out_specs=[pl.BlockSpec((te * nchunk, LANES), lambda i: (i, 0)), pl.BlockSpec((te, d // 2), lambda i: (i, 0))],
        out_shape=[jax.ShapeDtypeStruct((e * nchunk, LANES), I32), jax.ShapeDtypeStruct((e, d // 2), I32)],
        compiler_params=_cparams(("arbitrary",)),
        name="peer_pack_table",
    )(w)


def _sc_value_kernel(tab_hbm, idx_hbm, w_hbm, out_hbm, idx_v, w_v, rows_a, rows_b, acc_a, acc_b, row_sem, out_sem,
                     *, slots, first_tok):
    n_tok = out_hbm.shape[0] // (SC_CORES * SC_SUBCORES)
    d = out_hbm.shape[1]
    words = d // 2
    win = slots // 2
    wid = lax.axis_index("s") * SC_CORES + lax.axis_index("c")
    base_tok = wid * n_tok

    def gather(tl, half, buf, sem):
        rows = idx_v.at[pl.ds(tl * slots + half * win, win)]
        return pltpu.make_async_copy(tab_hbm.at[rows], buf, sem)

    def accumulate(tl, half, buf, acc):
        for g in range(words // SC_LANES // SC_GROUP):
            def row(r, sums, g=g):
                slot = jnp.full((SC_LANES,), tl * slots + half * win + r, I32)
                wv = plsc.load_gather(w_v, [slot])
                out = []
                for j in range(SC_GROUP):
                    x = buf[r, pl.ds((g * SC_GROUP + j) * SC_LANES, SC_LANES)]
                    lo = lax.bitcast_convert_type(x << 16, F32)
                    hi = lax.bitcast_convert_type(x & jnp.int32(-65536), F32)
                    out += [sums[2 * j] + wv * lo, sums[2 * j + 1] + wv * hi]
                return tuple(out)
            zeros = tuple(jnp.zeros((SC_LANES,), F32) for _ in range(2 * SC_GROUP))
            sums = lax.fori_loop(0, win, row, zeros)
            for j in range(SC_GROUP):
                chunk, lane = divmod((g * SC_GROUP + j) * SC_LANES, LANES)
                plsc.addupdate(acc.at[pl.ds(2 * chunk * LANES + lane, SC_LANES)], sums[2 * j])
                plsc.addupdate(acc.at[pl.ds((2 * chunk + 1) * LANES + lane, SC_LANES)], sums[2 * j + 1])

    def write_out(acc, tok, sem):
        return pltpu.make_async_copy(acc, out_hbm.at[tok], sem)

    @pl.loop(0, n_tok // SC_BATCH)
    def _(bi):
        tok0 = base_tok + bi * SC_BATCH
        pltpu.sync_copy(idx_hbm.at[pl.ds((first_tok + tok0) * slots, SC_BATCH * slots)], idx_v)
        pltpu.sync_copy(w_hbm.at[pl.ds((first_tok + tok0) * slots, SC_BATCH * slots)], w_v)
        gather(0, 0, rows_a, row_sem.at[0]).start()

        @pl.loop(0, SC_BATCH // 2)
        def _(tp):
            for parity, acc in ((0, acc_a), (1, acc_b)):
                tl = 2 * tp + parity

                @pl.when(bi * SC_BATCH + tl >= 2)
                def _():
                    write_out(acc, tok0, out_sem.at[parity]).wait()
                for k in range(d // SC_LANES):
                    acc[pl.ds(k * SC_LANES, SC_LANES)] = jnp.zeros((SC_LANES,), F32)
                gather(tl, 1, rows_b, row_sem.at[1]).start()
                gather(tl, 0, rows_a, row_sem.at[0]).wait()
                accumulate(tl, 0, rows_a, acc)
                gather(jnp.minimum(tl + 1, SC_BATCH - 1), 0, rows_a, row_sem.at[0]).start()
                gather(tl, 1, rows_b, row_sem.at[1]).wait()
                accumulate(tl, 1, rows_b, acc)
                write_out(acc, tok0 + tl, out_sem.at[parity]).start()

        gather(SC_BATCH - 1, 0, rows_a, row_sem.at[0]).wait()

    write_out(acc_a, base_tok, out_sem.at[0]).wait()
    write_out(acc_b, base_tok, out_sem.at[1]).wait()


def _peer_values_sc(idx, w, table, n_tok, d, slots):
    assert n_tok % (SC_CORES * SC_SUBCORES * SC_BATCH) == 0 and SC_BATCH % 2 == 0
    mesh = plsc.VectorSubcoreMesh(core_axis_name="c", subcore_axis_name="s",
                                  num_cores=SC_CORES, num_subcores=SC_SUBCORES)
    return pl.kernel(
        functools.partial(_sc_value_kernel, slots=slots, first_tok=idx.shape[0] // slots - n_tok),
        out_type=jax.ShapeDtypeStruct((n_tok, d), F32),
        mesh=mesh,
        scratch_types=[
            pltpu.VMEM((SC_BATCH * slots,), I32),
            pltpu.VMEM((SC_BATCH * slots,), F32),
            pltpu.VMEM((slots // 2, d // 2), I32),
            pltpu.VMEM((slots // 2, d // 2), I32),
            pltpu.VMEM((d,), F32),
            pltpu.VMEM((d,), F32),
            pltpu.SemaphoreType.DMA((2,)),
            pltpu.SemaphoreType.DMA((2,)),
        ],
        compiler_params=pltpu.CompilerParams(needs_layout_passes=False),
        name="peer_values_sc",
    )(table, idx, w)


def _sc_score_kernel(tab_hbm, idx_hbm, xe_hbm, xo_hbm, out_hbm, idx_v, xe_v, xo_v, rows_a, rows_b, part_v, a_v, row_sem,
                     *, slots, first_tok):
    n_tok = out_hbm.shape[0] // slots // (SC_CORES * SC_SUBCORES)
    words = rows_a.shape[1]
    win = slots // 2
    n_groups = words // SC_LANES // SC_GROUP
    wid = lax.axis_index("s") * SC_CORES + lax.axis_index("c")
    base_tok = wid * n_tok
    lane_id = lax.iota(I32, SC_LANES)

    def gather(tl, half, buf, sem):
        rows = idx_v.at[pl.ds(tl * slots + half * win, win)]
        return pltpu.make_async_copy(tab_hbm.at[rows], buf, sem)

    def scores(tl, half, buf):
        for r in range(win):
            part_v[pl.ds(r * SC_LANES, SC_LANES)] = jnp.zeros((SC_LANES,), F32)
        for g in range(n_groups):
            xs = []
            for j in range(SC_GROUP):
                at = pl.ds(tl * words + (g * SC_GROUP + j) * SC_LANES, SC_LANES)
                xs += [xe_v[at], xo_v[at]]

            @pl.loop(0, win)
            def _(r, g=g, xs=xs):
                sums = [jnp.zeros((SC_LANES,), F32) for _ in range(4)]
                for j in range(SC_GROUP):
                    x = buf[r, pl.ds((g * SC_GROUP + j) * SC_LANES, SC_LANES)]
                    lo = lax.bitcast_convert_type(x << 16, F32)
                    hi = lax.bitcast_convert_type(x & jnp.int32(-65536), F32)
                    sums[(2 * j) % 4] += lo * xs[2 * j]
                    sums[(2 * j + 1) % 4] += hi * xs[2 * j + 1]
                plsc.addupdate(part_v.at[pl.ds(r * SC_LANES, SC_LANES)], (sums[0] + sums[1]) + (sums[2] + sums[3]))
        for rb in range(win // SC_LANES):
            first = (lane_id + rb * SC_LANES) * SC_LANES
            total = jnp.zeros((SC_LANES,), F32)
            for lane in range(SC_LANES):
                total += plsc.load_gather(part_v, [first + lane])
            a_v[pl.ds(tl * slots + half * win + rb * SC_LANES, SC_LANES)] = total

    @pl.loop(0, n_tok // SC_BATCH)
    def _(bi):
        tok0 = base_tok + bi * SC_BATCH
        pltpu.sync_copy(idx_hbm.at[pl.ds((first_tok + tok0) * slots, SC_BATCH * slots)], idx_v)
        pltpu.sync_copy(xe_hbm.at[pl.ds(tok0 * words, SC_BATCH * words)], xe_v)
        pltpu.sync_copy(xo_hbm.at[pl.ds(tok0 * words, SC_BATCH * words)], xo_v)
        gather(0, 0, rows_a, row_sem.at[0]).start()

        @pl.loop(0, SC_BATCH)
        def _(tl):
            gather(tl, 1, rows_b, row_sem.at[1]).start()
            gather(tl, 0, rows_a, row_sem.at[0]).wait()
            scores(tl, 0, rows_a)
            gather(jnp.minimum(tl + 1, SC_BATCH - 1), 0, rows_a, row_sem.at[0]).start()
            gather(tl, 1, rows_b, row_sem.at[1]).wait()
            scores(tl, 1, rows_b)

        gather(SC_BATCH - 1, 0, rows_a, row_sem.at[0]).wait()
        pltpu.sync_copy(a_v, out_hbm.at[pl.ds(tok0 * slots, SC_BATCH * slots)])


def _peer_scores_sc(idx, xe, xo, table, n_tok, slots):
    words = table.shape[1]
    assert n_tok % (SC_CORES * SC_SUBCORES * SC_BATCH) == 0
    mesh = plsc.VectorSubcoreMesh(core_axis_name="c", subcore_axis_name="s",
                                  num_cores=SC_CORES, num_subcores=SC_SUBCORES)
    return pl.kernel(
        functools.partial(_sc_score_kernel, slots=slots, first_tok=idx.shape[0] // slots - n_tok),
        out_type=jax.ShapeDtypeStruct((n_tok * slots,), F32),
        mesh=mesh,
        scratch_types=[
            pltpu.VMEM((SC_BATCH * slots,), I32),
            pltpu.VMEM((SC_BATCH * words,), F32),
            pltpu.VMEM((SC_BATCH * words,), F32),
            pltpu.VMEM((slots // 2, words), I32),
            pltpu.VMEM((slots // 2, words), I32),
            pltpu.VMEM((slots // 2 * SC_LANES,), F32),
            pltpu.VMEM((SC_BATCH * slots,), F32),
            pltpu.SemaphoreType.DMA((2,)),
        ],
        compiler_params=pltpu.CompilerParams(needs_layout_passes=False),
        name="peer_scores_sc",
    )(table, idx, xe, xo)


def _split_norm_kernel(x_ref, g_ref, xe_ref, xo_ref):
    d = x_ref.shape[1]
    xn = _rms(x_ref[...], g_ref[...])
    nchunk = d // (2 * LANES)
    xe_ref[...] = jnp.concatenate([xn[:, (2 * c) * LANES:(2 * c + 1) * LANES] for c in range(nchunk)], axis=1)
    xo_ref[...] = jnp.concatenate([xn[:, (2 * c + 1) * LANES:(2 * c + 2) * LANES] for c in range(nchunk)], axis=1)


def _split_norm(x2d, g, n_rows, tm):
    d = x2d.shape[1]
    first = (x2d.shape[0] - n_rows) // tm
    half = jax.ShapeDtypeStruct((n_rows, d // 2), F32)
    return pl.pallas_call(
        _split_norm_kernel,
        grid=(n_rows // tm,),
        in_specs=[pl.BlockSpec((tm, d), lambda i: (first + i, 0)), pl.BlockSpec((1, d), lambda i: (0, 0))],
        out_specs=[pl.BlockSpec((tm, d // 2), lambda i: (i, 0))] * 2,
        out_shape=[half, half],
        compiler_params=_cparams(("arbitrary",)),
        name="peer_split_norm",
    )(x2d, g.reshape(1, d))


def _score_finish_kernel(w_hbm, a_ref, gate_ref, w_ref):
    del w_hbm
    a = a_ref[0]
    gelu = 0.5 * a * (1.0 + lax.erf(a * (2.0 ** -0.5)))
    w_ref[0] = gate_ref[0].T * gelu


def _score_finish(w_full, a_sc, gate_t):
    nt, t, slots = w_full.shape
    n_sc = a_sc.shape[0]
    first = nt - n_sc
    return pl.pallas_call(
        _score_finish_kernel,
        grid=(n_sc,),
        in_specs=[
            pl.BlockSpec(memory_space=pl.ANY),
            pl.BlockSpec((1, t, slots), lambda i: (i, 0, 0)),
            pl.BlockSpec((1, slots, t), lambda i: (first + i, 0, 0)),
        ],
        out_specs=pl.BlockSpec((1, t, slots), lambda i: (first + i, 0, 0)),
        out_shape=jax.ShapeDtypeStruct(w_full.shape, F32),
        input_output_aliases={0: 0},
        compiler_params=_cparams(("arbitrary",)),
        name="peer_score_finish",
    )(w_full, a_sc, gate_t)


def _residual_norm_kernel(out_hbm, x_ref, p_ref, gf_ref, o_ref, *, final_norm):
    del out_hbm
    y = x_ref[...] + p_ref[...]
    if final_norm:
        y = _rms(y, gf_ref[...])
    o_ref[...] = y


def _residual_norm(out_full, x2d, p2d, g_final, final_norm, tm):
    t, d = p2d.shape
    first = (x2d.shape[0] - t) // tm
    return pl.pallas_call(
        functools.partial(_residual_norm_kernel, final_norm=final_norm),
        grid=(t // tm,),
        in_specs=[
            pl.BlockSpec(memory_space=pl.ANY),
            pl.BlockSpec((tm, d), lambda i: (first + i, 0)),
            pl.BlockSpec((tm, d), lambda i: (i, 0)),
            pl.BlockSpec((1, d), lambda i: (0, 0)),
        ],
        out_specs=pl.BlockSpec((tm, d), lambda i: (first + i, 0)),
        out_shape=jax.ShapeDtypeStruct(x2d.shape, F32),
        input_output_aliases={0: 0},
        compiler_params=_cparams(("arbitrary",)),
        name="peer_residual",
    )(out_full, x2d, p2d, g_final.reshape(1, d))


def _peer_ffn(x, g, w_q, k1, k2, u_emb, v_emb, g_final, final_norm):
    b, s, d = x.shape
    x2d = x.reshape(b * s, d)
    e_slot, gate_slot, e_tok = _peer_router(x2d, g, w_q, k1, k2)
    (u_tab, u_tab_sc), (v_tab, v_tab_sc) = _pack_pairs(u_emb), _pack_pairs(v_emb)
    nt, slots, t = e_slot.shape
    sc_quantum = SC_CORES * SC_SUBCORES * SC_BATCH // t
    ns_score = nt * SC_SCORE_PERCENT // 100 // sc_quantum * sc_quantum
    ns_value = nt * SC_VALUE_PERCENT // 100 // sc_quantum * sc_quantum
    idx_tok = e_tok.reshape(-1)

    w_tok = _peer_scores(e_slot, gate_slot, x2d, g, u_tab, nt - ns_score)
    if ns_score:
        xe, xo = _split_norm(x2d, g, ns_score * t, math.gcd(TOKEN_TILE, ns_score * t))
        a_sc = _peer_scores_sc(idx_tok, xe.reshape(-1), xo.reshape(-1), u_tab_sc, ns_score * t, slots)
        w_tok = _score_finish(w_tok, a_sc.reshape(ns_score, t, slots), gate_slot)
    out = _peer_values(e_slot, w_tok, x2d, v_tab, g_final, final_norm, nt - ns_value)
    if ns_value:
        peer_sc = _peer_values_sc(idx_tok, w_tok.reshape(-1), v_tab_sc, ns_value * t, d, slots)
        out = _residual_norm(out, x2d, peer_sc, g_final, final_norm, math.gcd(TOKEN_TILE, ns_value * t))
    return out.reshape(b, s, d)


def _qkv_kernel(x_ref, g_ref, wqk_hi_ref, wqk_lo_ref, wv_ref, cos_ref, sin_ref,
                qt_ref, k0_ref, k1_ref, vt_ref, km_ref):
    tm, d = x_ref.shape[1], x_ref.shape[2]
    hn = _rms(x_ref[0], g_ref[...])
    hh, hl = _split(hn)
    qk = _dot3(hh, hl, wqk_hi_ref[...], wqk_lo_ref[...])
    v = _dot(hh, wv_ref[...])
    cos = jnp.concatenate([cos_ref[...]] * (d // LANES), axis=1)
    sin = jnp.concatenate([sin_ref[...]] * (d // LANES), axis=1)
    lane = lax.broadcasted_iota(I32, (tm, d), 1)
    first_half = (lane % HEAD_DIM) < (HEAD_DIM // 2)

    def rope(a):
        rot = jnp.where(first_half, pltpu.roll(a, d - HEAD_DIM // 2, 1), pltpu.roll(a, HEAD_DIM // 2, 1))
        return a * cos + rot * sin

    q = rope(qk[:, :d])
    k = rope(qk[:, d:])
    qt_ref[0] = q.T
    vt_ref[0] = v.T.astype(BF16)
    nb = tm // MOBA_BLOCK
    km_ref[0, 0] = jnp.mean(k.reshape(nb, MOBA_BLOCK, d), axis=1)
    row = lax.broadcasted_iota(I32, (tm, d), 0)
    block = (pl.program_id(1) * tm + row) // MOBA_BLOCK
    pair_lane = lane % LANES
    kb = k.astype(BF16)
    k0_ref[0] = jnp.where(pair_lane < HEAD_DIM, kb, jnp.where(pair_lane - HEAD_DIM == block, 1.0, 0.0).astype(BF16))
    k1_ref[0] = jnp.where(pair_lane >= HEAD_DIM, kb, jnp.where(pair_lane == block, 1.0, 0.0).astype(BF16))


def _qkv_rope(x, g, w_qkv, tm):
    b, s, d = x.shape
    half = HEAD_DIM // 2
    inv = ROPE_THETA ** (-jnp.arange(half, dtype=F32) / half)
    ang = jnp.arange(s).astype(F32)[:, None] * inv[None, :]
    cos, sin = jnp.cos(ang), jnp.sin(ang)
    cos128 = jnp.tile(jnp.concatenate([cos, cos], axis=1), (1, LANES // HEAD_DIM))
    sin128 = jnp.tile(jnp.concatenate([-sin, sin], axis=1), (1, LANES // HEAD_DIM))
    wqk_hi, wqk_lo = _split(w_qkv[:, :2 * d])
    wv = w_qkv[:, 2 * d:].astype(BF16)
    nb = tm // MOBA_BLOCK
    const = lambda i, j: (0, 0)
    qt, k0, k1, vt, km = pl.pallas_call(
        _qkv_kernel,
        grid=(b, s // tm),
        in_specs=[
            pl.BlockSpec((1, tm, d), lambda i, j: (i, j, 0)),
            pl.BlockSpec((1, d), const),
            pl.BlockSpec((d, 2 * d), const),
            pl.BlockSpec((d, 2 * d), const),
            pl.BlockSpec((d, d), const),
            pl.BlockSpec((tm, LANES), lambda i, j: (j, 0)),
            pl.BlockSpec((tm, LANES), lambda i, j: (j, 0)),
        ],
        out_specs=[
            pl.BlockSpec((1, d, tm), lambda i, j: (i, 0, j)),
            pl.BlockSpec((1, tm, d), lambda i, j: (i, j, 0)),
            pl.BlockSpec((1, tm, d), lambda i, j: (i, j, 0)),
            pl.BlockSpec((1, d, tm), lambda i, j: (i, 0, j)),
            pl.BlockSpec((1, 1, nb, d), lambda i, j: (i, j, 0, 0)),
        ],
        out_shape=[
            jax.ShapeDtypeStruct((b, d, s), F32),
            jax.ShapeDtypeStruct((b, s, d), BF16),
            jax.ShapeDtypeStruct((b, s, d), BF16),
            jax.ShapeDtypeStruct((b, d, s), BF16),
            jax.ShapeDtypeStruct((b, s // tm, nb, d), F32),
        ],
        compiler_params=_cparams(("arbitrary", "arbitrary")),
        name="qkv_rope",
    )(x, g.reshape(1, d), wqk_hi, wqk_lo, wv, cos128, sin128)
    return qt, (k0, k1), vt, km.reshape(b, s // MOBA_BLOCK, d)


def _moba_kernel(qt_ref, k0_ref, k1_ref, vt_ref, km_ref, o_ref, sa_ref, sb_ref):
    bs = MOBA_BLOCK
    nb = km_ref.shape[1]
    n_heads = LANES // HEAD_DIM
    k_refs = (k0_ref, k1_ref)
    j = pl.program_id(2)
    qt = qt_ref[0]
    km = km_ref[0]
    scale = HEAD_DIM ** -0.5 * 1.4426950408889634
    lane_km = lax.broadcasted_iota(I32, (nb, LANES), 1)
    blk = lax.broadcasted_iota(I32, (nb, bs), 0)
    zeros_pad = jnp.zeros((LANES - HEAD_DIM - nb, bs), F32)
    qh, ql = _split(qt)

    own = pl.ds(pl.multiple_of(j * bs, bs), bs)
    v_own = vt_ref[0, :, own]
    krow = lax.broadcasted_iota(I32, (bs, bs), 0)
    qcol = lax.broadcasted_iota(I32, (bs, bs), 1)

    q_augs, state = [], []
    for hh in range(n_heads):
        head_lo = hh * HEAD_DIM
        in_head_km = (lane_km >= head_lo) & (lane_km < head_lo + HEAD_DIM)
        kmh, kml = _split(jnp.where(in_head_km, km, 0.0))
        gate = _dot3(kmh, kml, qh, ql)
        valid = blk < j
        gate = jnp.where(valid, gate, -jnp.inf)
        sel = jnp.zeros((nb, bs), F32)
        for _ in range(MOBA_TOPK):
            m = jnp.max(gate, axis=0, keepdims=True)
            i = jnp.min(jnp.where(gate == m, blk, nb), axis=0, keepdims=True)
            pick = blk == i
            sel = jnp.where(pick, 1.0, sel)
            gate = jnp.where(pick, -jnp.inf, gate)
        bias_t = jnp.where((sel > 0.0) & valid, 0.0, MASK_NEG)
        q_head = qt[head_lo:head_lo + HEAD_DIM, :] * scale
        no_bias = jnp.zeros((LANES - HEAD_DIM, bs), F32)
        if hh == 0:
            q_aug = jnp.concatenate([q_head, bias_t, zeros_pad], axis=0)
            q_own = jnp.concatenate([q_head, no_bias], axis=0)
        else:
            q_aug = jnp.concatenate([bias_t, zeros_pad, q_head], axis=0)
            q_own = jnp.concatenate([no_bias, q_head], axis=0)
        q_augs.append(q_aug.astype(BF16))

        s_own = jnp.where(krow <= qcol, _dot(k_refs[hh][0, own, :], q_own.astype(BF16)), -1e30)
        m0 = jnp.max(s_own, axis=0, keepdims=True)
        p0 = jnp.exp2(s_own - m0)
        state += [m0, jnp.sum(p0, axis=0, keepdims=True), _dot(v_own, p0.astype(BF16))]

    chunk = KV_CHUNK * bs
    last_chunk = nb // KV_CHUNK - 1

    def score_chunk(c, s_ref):
        rows = pl.ds(pl.multiple_of(jnp.minimum(c, last_chunk) * chunk, chunk), chunk)
        for hh in range(n_heads):
            s_ref[hh] = _dot(k_refs[hh][0, rows, :], q_augs[hh])

    def attend(c, s_ref, state):
        vn = vt_ref[0, :, pl.ds(pl.multiple_of(c * chunk, chunk), chunk)]
        new_state = []
        for hh in range(n_heads):
            m, l, acc = state[3 * hh:3 * hh + 3]
            s = s_ref[hh]
            m_new = jnp.maximum(m, jnp.max(s, axis=0, keepdims=True))
            alpha = jnp.exp2(m - m_new)
            p = jnp.exp2(s - m_new)
            l = alpha * l + jnp.sum(p, axis=0, keepdims=True)
            acc = alpha * acc + _dot(vn, p.astype(BF16))
            new_state += [m_new, l, acc]
        return tuple(new_state)

    score_chunk(0, sa_ref)

    def body(i, state):
        score_chunk(2 * i + 1, sb_ref)
        state = attend(2 * i, sa_ref, state)
        score_chunk(2 * i + 2, sa_ref)
        return attend(2 * i + 1, sb_ref, state)

    state = lax.fori_loop(0, (j + 2 * KV_CHUNK - 1) // (2 * KV_CHUNK), body, tuple(state))
    halves = []
    for hh in range(n_heads):
        _, l, acc = state[3 * hh:3 * hh + 3]
        halves.append((acc / l)[hh * HEAD_DIM:(hh + 1) * HEAD_DIM, :])
    o_ref[0] = jnp.concatenate(halves, axis=0).T


def _moba_attention(qt, k01, vt, km):
    b, d, s = qt.shape
    nb = s // MOBA_BLOCK
    assert nb % (2 * KV_CHUNK) == 0
    return pl.pallas_call(
        _moba_kernel,
        grid=(b, d // LANES, nb),
        in_specs=[
            pl.BlockSpec((1, LANES, MOBA_BLOCK), lambda i, h, j: (i, h, j)),
            pl.BlockSpec((1, s, LANES), lambda i, h, j: (i, 0, h)),
            pl.BlockSpec((1, s, LANES), lambda i, h, j: (i, 0, h)),
            pl.BlockSpec((1, LANES, s), lambda i, h, j: (i, h, 0)),
            pl.BlockSpec((1, nb, LANES), lambda i, h, j: (i, 0, h)),
        ],
        out_specs=pl.BlockSpec((1, MOBA_BLOCK, LANES), lambda i, h, j: (i, j, h)),
        out_shape=jax.ShapeDtypeStruct((b, s, d), F32),
        scratch_shapes=[pltpu.VMEM((LANES // HEAD_DIM, KV_CHUNK * MOBA_BLOCK, MOBA_BLOCK), F32)] * 2,
        compiler_params=_cparams(("arbitrary", "arbitrary", "arbitrary")),
        name="moba_attention",
    )(qt, *k01, vt, km)


def _proj_residual_kernel(x_ref, a_ref, w_ref, o_ref):
    o_ref[...] = x_ref[...] + _dot(a_ref[...].astype(BF16), w_ref[...])


def _proj_residual(x2d, a2d, w, tm):
    t, d = x2d.shape
    return pl.pallas_call(
        _proj_residual_kernel,
        grid=(t // tm,),
        in_specs=[
            pl.BlockSpec((tm, d), lambda i: (i, 0)),
            pl.BlockSpec((tm, a2d.shape[1]), lambda i: (i, 0)),
            pl.BlockSpec(w.shape, lambda i: (0, 0)),
        ],
        out_specs=pl.BlockSpec((tm, d), lambda i: (i, 0)),
        out_shape=jax.ShapeDtypeStruct((t, d), F32),
        compiler_params=_cparams(("arbitrary",)),
        name="attn_out_proj",
    )(x2d, a2d, w.astype(BF16))


def _moba_mixer(x, g, w_qkv, w_o):
    b, s, d = x.shape
    tm = min(TOKEN_TILE, s)
    qt, k, vt, km = _qkv_rope(x, g, w_qkv, tm)
    attn = _moba_attention(qt, k, vt, km)
    return _proj_residual(x.reshape(b * s, d), attn.reshape(b * s, d), w_o, tm).reshape(b, s, d)


def kernel(x, norm_mix, norm_ffn, conv_w_in, conv_w, conv_w_out, attn_w_qkv, attn_w_o,
           peer_w_q, peer_k1, peer_k2, peer_u, peer_v, norm_final):
    depth = norm_mix.shape[0]
    tm = min(TOKEN_TILE, x.shape[1])
    for i in range(depth):
        j = i // 2
        if i % 2 == 0:
            x = _conv_mixer(x, norm_mix[i], conv_w_in[j], conv_w[j], conv_w_out[j], tm)
        else:
            x = _moba_mixer(x, norm_mix[i], attn_w_qkv[j], attn_w_o[j])
        x = _peer_ffn(x, norm_ffn[i], peer_w_q[i], peer_k1[i], peer_k2[i], peer_u[i], peer_v[i],
                      norm_final, final_norm=(i == depth - 1))
    return x
```
